```python
import math
import jax
import jax.numpy as jnp
from jax import lax
import numpy as np

D_MODEL = 1024
BATCH = 16
SEQ = 256
DEPTH = 2
DEC_BATCH = 4
DEC_SEQ = 2048
PAST_LEN = 512

GRID_W = 64
MIX_WIDTH = D_MODEL
S5_WIDTH = MIX_WIDTH // 2
S5_GROUP_CH = 16
S5_GROUPS = S5_WIDTH // S5_GROUP_CH
S5_STATE = 64
HGRN_WIDTH = MIX_WIDTH - S5_WIDTH
HGRN_DK = 128
HGRN_HEADS = HGRN_WIDTH // HGRN_DK
HGRN_DV = HGRN_WIDTH // HGRN_HEADS
HGRN_CHUNK = 16
IN_COLS = S5_WIDTH + 5 * HGRN_WIDTH
N_EXPERTS = 32
TOP_K = 4
D_FF = D_MODEL
SWIGLU_LIMIT = 7.0
SWIGLU_ALPHA = 1.702
MOE_BLOCK = 128
NORM_EPS = 1e-6
POS_BASE = 10000.0

kernel_name = 'hybrid_s5_hgrn2_moe_diffusion_step'


def _rmsnorm(x, w):
    xf = x.astype(jnp.float32)
    y = xf * lax.rsqrt(jnp.mean(xf * xf, axis=-1, keepdims=True) + NORM_EPS)
    return (y * w.astype(jnp.float32)).astype(x.dtype)


def _pos_embed_2d(rows, dim):
    r = jnp.repeat(jnp.arange(rows, dtype=jnp.float32), GRID_W)
    col = jnp.tile(jnp.arange(GRID_W, dtype=jnp.float32), rows)
    quarter = dim // 4
    omega = 1.0 / (POS_BASE ** (jnp.arange(quarter, dtype=jnp.float32) / quarter))
    def emb(pos):
        ang = pos[:, None] * omega[None, :]
        return jnp.concatenate([jnp.sin(ang), jnp.cos(ang)], axis=-1)
    return jnp.concatenate([emb(r), emb(col)], axis=-1)


def _complex_scan_op(e1, e2):
    ar1, ai1, br1, bi1 = e1
    ar2, ai2, br2, bi2 = e2
    return (ar1 * ar2 - ai1 * ai2,
            ar1 * ai2 + ai1 * ar2,
            ar2 * br1 - ai2 * bi1 + br2,
            ar2 * bi1 + ai2 * br1 + bi2)


def _s5_direction(u, a_re, a_im, log_dt, b_re, b_im, h0_re, h0_im):
    dt = jnp.exp(log_dt)[:, None]
    lam_re = jnp.minimum(a_re, -1e-4)
    lam_im = a_im
    mag = jnp.exp(dt * lam_re)
    ang = dt * lam_im
    ab_re = mag * jnp.cos(ang)
    ab_im = mag * jnp.sin(ang)
    den = lam_re * lam_re + lam_im * lam_im
    nr = ab_re - 1.0
    ni = ab_im
    co_re = (nr * lam_re + ni * lam_im) / den
    co_im = (ni * lam_re - nr * lam_im) / den
    bb_re = co_re[..., None] * b_re - co_im[..., None] * b_im
    bb_im = co_re[..., None] * b_im + co_im[..., None] * b_re
    bu_re = jnp.einsum('blgp,gnp->blgn', u, bb_re)
    bu_im = jnp.einsum('blgp,gnp->blgn', u, bb_im)
    a_r = jnp.broadcast_to(ab_re, bu_re.shape)
    a_i = jnp.broadcast_to(ab_im, bu_im.shape)
    acr, aci, bcr, bci = lax.associative_scan(_complex_scan_op, (a_r, a_i, bu_re, bu_im), axis=1)
    h_re = acr * h0_re[:, None] - aci * h0_im[:, None] + bcr
    h_im = acr * h0_im[:, None] + aci * h0_re[:, None] + bci
    return h_re, h_im


def _hgrn_chunked(q, k, v, logf, s0):
    bn, nh, seq, dk = q.shape
    dv = v.shape[-1]
    nc = seq // HGRN_CHUNK
    q, k, v, logf = (t.reshape(bn, nh, nc, HGRN_CHUNK, t.shape[-1]) for t in (q, k, v, logf))
    b = jnp.cumsum(logf, axis=3)
    b_last = b[:, :, :, -1:, :]
    causal = jnp.tril(jnp.ones((HGRN_CHUNK, HGRN_CHUNK), dtype=bool))[:, :, None]
    diff = b[:, :, :, :, None, :] - b[:, :, :, None, :, :]
    decay = jnp.where(causal, jnp.exp(jnp.where(causal, diff, 0.0)), 0.0)
    scores = jnp.einsum('bhntd,bhnsd,bhntsd->bhnts', q, k, decay)
    o_intra = jnp.einsum('bhnts,bhnsv->bhntv', scores, v)
    q_in = q * jnp.exp(b)
    k_in = k * jnp.exp(b_last - b)
    chunk_decay = jnp.exp(b_last[:, :, :, 0, :])

    def step(state, inp):
        qc, kc, vc, dc = inp
        o = jnp.einsum('bhtd,bhdv->bhtv', qc, state)
        state = dc[..., None] * state + jnp.einsum('bhtd,bhtv->bhdv', kc, vc)
        return state, o

    xs = tuple(jnp.moveaxis(t, 2, 0) for t in (q_in, k_in, v, chunk_decay))
    s_fin, o_inter = lax.scan(step, s0, xs)
    o = o_intra + jnp.moveaxis(o_inter, 0, 2)
    return o.reshape(bn, nh, seq, dv), s_fin


def _clamped_swiglu(h):
    glu = jnp.minimum(h[..., :D_FF], SWIGLU_LIMIT)
    lin = jnp.clip(h[..., D_FF:], -SWIGLU_LIMIT, SWIGLU_LIMIT)
    return glu * jax.nn.sigmoid(SWIGLU_ALPHA * glu) * (lin + 1.0)


def _moe(x, router_w, router_b, w1, b1, w2, b2):
    n_tok, dm = x.shape
    n_assign = n_tok * TOP_K
    logits = x.astype(jnp.float32) @ router_w.astype(jnp.float32) + router_b.astype(jnp.float32)
    top_v, top_e = lax.top_k(logits, TOP_K)
    gate = jax.nn.softmax(top_v, axis=-1)
    flat_e = top_e.reshape(-1)
    flat_tok = jnp.repeat(jnp.arange(n_tok, dtype=jnp.int32), TOP_K)
    flat_g = gate.reshape(-1)
    order = jnp.argsort(flat_e)
    se, stok, sg = flat_e[order], flat_tok[order], flat_g[order]
    counts = jnp.bincount(flat_e, length=N_EXPERTS)
    padded = (counts + MOE_BLOCK - 1) // MOE_BLOCK * MOE_BLOCK
    start = jnp.cumsum(counts) - counts
    pend = jnp.cumsum(padded)
    pstart = pend - padded
    dest = pstart[se] + jnp.arange(n_assign, dtype=jnp.int32) - start[se]
    n_blocks = -(-n_assign // MOE_BLOCK) + N_EXPERTS
    buf_tok = jnp.full((n_blocks * MOE_BLOCK,), n_tok, jnp.int32).at[dest].set(stok)
    buf_g = jnp.zeros((n_blocks * MOE_BLOCK,), jnp.float32).at[dest].set(sg)
    block_e = jnp.minimum(jnp.searchsorted(pend, jnp.arange(n_blocks, dtype=jnp.int32) * MOE_BLOCK, side='right'), N_EXPERTS - 1)
    x_pad = jnp.concatenate([x, jnp.zeros((1, dm), x.dtype)], axis=0)

    def expert_block(args):
        tok, e = args
        h = x_pad[tok] @ w1[e] + b1[e]
        return _clamped_swiglu(h) @ w2[e] + b2[e]

    out = lax.map(expert_block, (buf_tok.reshape(n_blocks, MOE_BLOCK), block_e))
    out = out.reshape(-1, dm).astype(jnp.float32) * buf_g[:, None]
    y = jnp.zeros((n_tok + 1, dm), jnp.float32).at[buf_tok].add(out)
    return y[:n_tok].astype(x.dtype)


def _layer(l, x, cond, s5_h0_re, s5_h0_im, hg_s0, p):
    f32 = jnp.float32
    bn, seq, _ = x.shape
    mod = jax.nn.silu(cond) @ p['ada_w'][l] + p['ada_b'][l]
    sh1, sc1, g1, sh2, sc2, g2 = jnp.split(mod[:, None, :], 6, axis=-1)

    h = _rmsnorm(x, p['norm_w'][l, 0]) * (1.0 + sc1) + sh1
    z = (h @ p['w_in'][l]).astype(f32)
    cuts = [S5_WIDTH, S5_WIDTH + HGRN_WIDTH, S5_WIDTH + 2 * HGRN_WIDTH, S5_WIDTH + 3 * HGRN_WIDTH, S5_WIDTH + 4 * HGRN_WIDTH]
    u, q, f_fw, f_bw, v, g = jnp.split(z, cuts, axis=-1)

    def s5p(name, d):
        return p[name][l, d].astype(f32)
    uu = u.reshape(bn, seq, S5_GROUPS, S5_GROUP_CH)
    hf_re, hf_im = _s5_direction(uu, s5p('s5_a_re', 0), s5p('s5_a_im', 0), s5p('s5_log_dt', 0), s5p('s5_b_re', 0), s5p('s5_b_im', 0), s5_h0_re[:, 0].astype(f32), s5_h0_im[:, 0].astype(f32))
    hb_re, hb_im = _s5_direction(uu[:, ::-1], s5p('s5_a_re', 1), s5p('s5_a_im', 1), s5p('s5_log_dt', 1), s5p('s5_b_re', 1), s5p('s5_b_im', 1), s5_h0_re[:, 1].astype(f32), s5_h0_im[:, 1].astype(f32))
    s5_fin_re = jnp.stack([hf_re[:, -1], hb_re[:, -1]], axis=1)
    s5_fin_im = jnp.stack([hf_im[:, -1], hb_im[:, -1]], axis=1)
    hb_re, hb_im = hb_re[:, ::-1], hb_im[:, ::-1]
    y = (jnp.einsum('blgn,gpn->blgp', hf_re, s5p('s5_c_re', 0)) - jnp.einsum('blgn,gpn->blgp', hf_im, s5p('s5_c_im', 0))
         + jnp.einsum('blgn,gpn->blgp', hb_re, s5p('s5_c_re', 1)) - jnp.einsum('blgn,gpn->blgp', hb_im, s5p('s5_c_im', 1)))
    y = y.reshape(bn, seq, S5_WIDTH) + p['s5_d'][l].astype(f32) * u
    y = jax.nn.gelu(y)
    y_s5 = y * jax.nn.sigmoid(y @ p['s5_w_glu'][l].astype(f32) + p['s5_b_glu'][l].astype(f32))

    probs = jax.nn.softmax(p['hgrn_lb'].astype(f32), axis=0)
    lb = (jnp.cumsum(probs, axis=0) - probs[0])[l]
    def heads(t):
        return t.reshape(bn, seq, HGRN_HEADS, -1).transpose(0, 2, 1, 3)
    qh = heads(jax.nn.silu(q)) * (HGRN_DK ** -0.5)
    vh = heads(v)
    def forget(f, lbd):
        fg = lbd + (1.0 - lbd) * jax.nn.sigmoid(f)
        return heads(jnp.log(fg)), heads(1.0 - fg)
    lf_f, k_f = forget(f_fw, lb[0])
    lf_b, k_b = forget(f_bw, lb[1])
    def rev(t):
        return t[:, :, ::-1]
    o_f, s_f = _hgrn_chunked(qh, k_f, vh, lf_f, hg_s0[:, 0].astype(f32))
    o_b, s_b = _hgrn_chunked(rev(qh), rev(k_b), rev(vh), rev(lf_b), hg_s0[:, 1].astype(f32))
    o = _rmsnorm(o_f + rev(o_b), p['hgrn_norm_w'][l])
    y_hg = o.transpose(0, 2, 1, 3).reshape(bn, seq, HGRN_WIDTH) * jax.nn.silu(g)
    hg_fin = jnp.stack([s_f, s_b], axis=1)

    mix = jnp.concatenate([y_s5, y_hg], axis=-1).astype(x.dtype) @ p['w_out'][l]
    x = x + g1 * _rmsnorm(mix, p['norm_w'][l, 1])

    h2 = _rmsnorm(x, p['norm_w'][l, 2]) * (1.0 + sc2) + sh2
    ffn = _moe(h2.reshape(bn * seq, D_MODEL), p['router_w'][l], p['router_b'][l], p['exp_w1'][l], p['exp_b1'][l], p['exp_w2'][l], p['exp_b2'][l])
    x = x + g2 * _rmsnorm(ffn.reshape(bn, seq, D_MODEL), p['norm_w'][l, 3])
    return x, s5_fin_re, s5_fin_im, hg_fin


def setup_inputs(seed: int = 0) -> dict:
    key = jax.random.key(seed)
    keys = list(jax.random.split(key, 32))
    def nrm(shape, std):
        return std * jax.random.normal(keys.pop(), shape, jnp.float32)
    n_idx = jnp.arange(S5_STATE, dtype=jnp.float32)
    return {
        'x_prompt': nrm((BATCH, SEQ, D_MODEL), 1.0),
        'x_sample': nrm((DEC_BATCH, DEC_SEQ, D_MODEL), 1.0),
        'state_s5_re': nrm((DEC_BATCH, DEPTH, 2, S5_GROUPS, S5_STATE), 0.3),
        'state_s5_im': nrm((DEC_BATCH, DEPTH, 2, S5_GROUPS, S5_STATE), 0.3),
        'state_hgrn': nrm((DEC_BATCH, DEPTH, 2, HGRN_HEADS, HGRN_DK, HGRN_DV), 0.3),
        'c': nrm((DEC_BATCH, D_MODEL), 1.0),
        'c_ctx': nrm((D_MODEL,), 1.0),
        'ada_w': nrm((DEPTH, D_MODEL, 6 * D_MODEL), 0.5 * D_MODEL ** -0.5),
        'ada_b': nrm((DEPTH, 6 * D_MODEL), 0.02),
        'norm_w': 1.0 + nrm((DEPTH, 4, D_MODEL), 0.05),
        'w_in': nrm((DEPTH, D_MODEL, IN_COLS), D_MODEL ** -0.5),
        's5_a_re': -0.5 + nrm((DEPTH, 2, S5_GROUPS, S5_STATE), 0.01),
        's5_a_im': math.pi * n_idx + nrm((DEPTH, 2, S5_GROUPS, S5_STATE), 0.01),
        's5_log_dt': jax.random.uniform(keys.pop(), (DEPTH, 2, S5_GROUPS), jnp.float32, math.log(0.001), math.log(0.1)),
        's5_b_re': nrm((DEPTH, 2, S5_GROUPS, S5_STATE, S5_GROUP_CH), (2 * S5_GROUP_CH) ** -0.5),
        's5_b_im': nrm((DEPTH, 2, S5_GROUPS, S5_STATE, S5_GROUP_CH), (2 * S5_GROUP_CH) ** -0.5),
        's5_c_re': nrm((DEPTH, 2, S5_GROUPS, S5_GROUP_CH, S5_STATE), (2 * S5_STATE) ** -0.5),
        's5_c_im': nrm((DEPTH, 2, S5_GROUPS, S5_GROUP_CH, S5_STATE), (2 * S5_STATE) ** -0.5),
        's5_d': nrm((DEPTH, S5_WIDTH), 1.0),
        's5_w_glu': nrm((DEPTH, S5_WIDTH, S5_WIDTH), S5_WIDTH ** -0.5),
        's5_b_glu': nrm((DEPTH, S5_WIDTH), 0.02),
        'hgrn_lb': nrm((DEPTH, 2, HGRN_WIDTH), 0.5),
        'hgrn_norm_w': 1.0 + nrm((DEPTH, HGRN_DV), 0.05),
        'w_out': nrm((DEPTH, MIX_WIDTH, D_MODEL), MIX_WIDTH ** -0.5),
        'router_w': nrm((DEPTH, D_MODEL, N_EXPERTS), D_MODEL ** -0.5),
        'router_b': nrm((DEPTH, N_EXPERTS), 0.01),
        'exp_w1': nrm((DEPTH, N_EXPERTS, D_MODEL, 2 * D_FF), D_MODEL ** -0.5),
        'exp_b1': nrm((DEPTH, N_EXPERTS, 2 * D_FF), 0.02),
        'exp_w2': nrm((DEPTH, N_EXPERTS, D_FF, D_MODEL), D_FF ** -0.5),
        'exp_b2': nrm((DEPTH, N_EXPERTS, D_MODEL), 0.02),
    }


def reference(x_prompt, x_sample, state_s5_re, state_s5_im, state_hgrn, c, c_ctx, ada_w, ada_b, norm_w, w_in,
              s5_a_re, s5_a_im, s5_log_dt, s5_b_re, s5_b_im, s5_c_re, s5_c_im, s5_d, s5_w_glu, s5_b_glu,
              hgrn_lb, hgrn_norm_w, w_out, router_w, router_b, exp_w1, exp_b1, exp_w2, exp_b2):
    p = {'ada_w': ada_w, 'ada_b': ada_b, 'norm_w': norm_w, 'w_in': w_in,
         's5_a_re': s5_a_re, 's5_a_im': s5_a_im, 's5_log_dt': s5_log_dt, 's5_b_re': s5_b_re, 's5_b_im': s5_b_im,
         's5_c_re': s5_c_re, 's5_c_im': s5_c_im, 's5_d': s5_d, 's5_w_glu': s5_w_glu, 's5_b_glu': s5_b_glu,
         'hgrn_lb': hgrn_lb, 'hgrn_norm_w': hgrn_norm_w, 'w_out': w_out,
         'router_w': router_w, 'router_b': router_b,
         'exp_w1': exp_w1, 'exp_b1': exp_b1, 'exp_w2': exp_w2, 'exp_b2': exp_b2}

    bp = x_prompt.shape[0]
    z_re = jnp.zeros((bp, 2, S5_GROUPS, S5_STATE), jnp.float32)
    z_im = jnp.zeros((bp, 2, S5_GROUPS, S5_STATE), jnp.float32)
    z_hg = jnp.zeros((bp, 2, HGRN_HEADS, HGRN_DK, HGRN_DV), jnp.float32)
    xp = x_prompt
    fin_re, fin_im, fin_hg = [], [], []
    for l in range(DEPTH):
        xp, fr, fi, fh = _layer(l, xp, c_ctx[None, :], z_re, z_im, z_hg, p)
        fin_re.append(fr)
        fin_im.append(fi)
        fin_hg.append(fh)
    y_prompt = xp
    new_s5_re = jnp.stack(fin_re, axis=1)
    new_s5_im = jnp.stack(fin_im, axis=1)
    new_hgrn = jnp.stack(fin_hg, axis=1)

    rows = x_sample.shape[1] // GRID_W
    xs = x_sample + _pos_embed_2d(rows, D_MODEL).astype(x_sample.dtype)[None]
    for l in range(DEPTH):
        xs, _, _, _ = _layer(l, xs, c, state_s5_re[:, l], state_s5_im[:, l], state_hgrn[:, l], p)
    y_sample = xs

    return (y_prompt, y_sample, new_s5_re, new_s5_im, new_hgrn)
```

```python
import functools
import math

import jax
import jax.numpy as jnp
from jax import lax
from jax.experimental import pallas as pl
from jax.experimental.pallas import tpu as pltpu

F32 = jnp.float32
BF16 = jnp.bfloat16

D_MODEL = 1024
BATCH = 16
SEQ = 256
DEPTH = 2
DEC_BATCH = 4
DEC_SEQ = 2048
GRID_W = 64
S5_WIDTH = 512
S5_GROUP_CH = 16
S5_GROUPS = 32
S5_STATE = 64
HGRN_WIDTH = 512
HGRN_DK = 128
HGRN_HEADS = 4
HGRN_DV = 128
IN_COLS = S5_WIDTH + 5 * HGRN_WIDTH
N_EXPERTS = 32
TOP_K = 4
D_FF = D_MODEL
SWIGLU_LIMIT = 7.0
SWIGLU_ALPHA = 1.702
NORM_EPS = 1e-6
POS_BASE = 10000.0

N_PROMPT = BATCH * SEQ
N_SAMPLE = DEC_BATCH * DEC_SEQ
N_TOK = N_PROMPT + N_SAMPLE
N_SEQ = BATCH + DEC_BATCH
N_COND = 8

TM = 256
N_TILES = N_TOK // TM
S5_T = 16
S5_LANES = S5_T * S5_GROUP_CH
S5_GB = 4
S5_BPAD = 8
S5_ROWS_P = (SEQ // S5_T) * BATCH
S5_ROWS_S = (DEC_SEQ // S5_T) * S5_BPAD
S5_ROWS = S5_ROWS_P + S5_ROWS_S
HC = 128
N_HCHUNK = N_TOK // HC
EXP_CLAMP = 80.0
MOE_BM = 256
N_ASSIGN = N_TOK * TOP_K
SEG_ALIGN = 8
TILE_SLOTS = -(-(TM * TOP_K + N_EXPERTS * (SEG_ALIGN - 1)) // 128) * 128
MOE_BLOCKS = -(-(N_ASSIGN + N_TILES * N_EXPERTS * (SEG_ALIGN - 1) + N_EXPERTS * (MOE_BM - 1)) // MOE_BM)
MOE_ROWS = MOE_BLOCKS * MOE_BM
VMEM_LIMIT = 56 * 1024 * 1024


def _rms(x, w):
    return x * lax.rsqrt(jnp.mean(x * x, axis=-1, keepdims=True) + NORM_EPS) * w


def _silu(x):
    return x * jax.nn.sigmoid(x)


MOD_TN = 1536


def _mod_kernel(cond_ref, w_ref, b_ref, o_ref):
    s = _silu(cond_ref[...]).astype(BF16)
    o_ref[0] = jnp.dot(s, w_ref[0].astype(BF16), preferred_element_type=F32) + b_ref[0]


def _modulation(cond, ada_w, ada_b):
    return pl.pallas_call(
        _mod_kernel,
        grid=(DEPTH, 6 * D_MODEL // MOD_TN),
        in_specs=[
            pl.BlockSpec((N_COND, D_MODEL), lambda l, j: (0, 0)),
            pl.BlockSpec((1, D_MODEL, MOD_TN), lambda l, j: (l, 0, j)),
            pl.BlockSpec((1, 1, MOD_TN), lambda l, j: (l, 0, j)),
        ],
        out_specs=pl.BlockSpec((1, N_COND, MOD_TN), lambda l, j: (l, 0, j)),
        out_shape=jax.ShapeDtypeStruct((DEPTH, N_COND, 6 * D_MODEL), F32),
        compiler_params=pltpu.CompilerParams(vmem_limit_bytes=VMEM_LIMIT),
        name="adaln_mod",
    )(cond, ada_w, ada_b.reshape(DEPTH, 1, 6 * D_MODEL))


def _in_kernel(layer, cond_ref, x_ref, mod_ref, nw_ref, w_ref, lb_ref,
               u_ref, q_ref, v_ref, g_ref, lff_ref, kf_ref, lfb_ref, kb_ref):
    del cond_ref
    mod = mod_ref[0]
    sh1 = mod[:, 0:D_MODEL]
    sc1 = mod[:, D_MODEL:2 * D_MODEL]
    h = _rms(x_ref[...], nw_ref[0:1, :]) * (1.0 + sc1) + sh1
    z = jnp.dot(h.astype(BF16), w_ref[...], preferred_element_type=F32)
    w = HGRN_WIDTH
    c0 = S5_WIDTH
    u_ref[...] = z[:, 0:c0]
    q_ref[...] = _silu(z[:, c0:c0 + w]) * (HGRN_DK ** -0.5)
    v_ref[...] = z[:, c0 + 3 * w:c0 + 4 * w]
    g_ref[...] = _silu(z[:, c0 + 4 * w:c0 + 5 * w])
    lbp = lb_ref[...]
    e = jnp.exp(lbp - jnp.max(lbp, axis=0, keepdims=True))
    probs = e / jnp.sum(e, axis=0, keepdims=True)
    lb = jnp.sum(probs[0:layer + 1], axis=0) - probs[0]
    for d, (lf_ref, k_ref) in enumerate(((lff_ref, kf_ref), (lfb_ref, kb_ref))):
        f = z[:, c0 + (1 + d) * w:c0 + (2 + d) * w]
        lbd = lb[d:d + 1, :]
        fg = lbd + (1.0 - lbd) * jax.nn.sigmoid(f)
        lf_ref[...] = jnp.log(fg)
        k_ref[...] = 1.0 - fg


def _in_proj(layer, tile_cond, x, mod_l, norm_w_l, w_in_l, hgrn_lb):
    tok = lambda i, c: (i, 0)
    out = jax.ShapeDtypeStruct((N_TOK, HGRN_WIDTH), F32)
    return pl.pallas_call(
        functools.partial(_in_kernel, layer),
        grid_spec=pltpu.PrefetchScalarGridSpec(
            num_scalar_prefetch=1,
            grid=(N_TILES,),
            in_specs=[
                pl.BlockSpec((TM, D_MODEL), tok),
                pl.BlockSpec((1, 1, 6 * D_MODEL), lambda i, c: (c[i], 0, 0)),
                pl.BlockSpec((4, D_MODEL), lambda i, c: (0, 0)),
                pl.BlockSpec((D_MODEL, IN_COLS), lambda i, c: (0, 0)),
                pl.BlockSpec((DEPTH, 2, HGRN_WIDTH), lambda i, c: (0, 0, 0)),
            ],
            out_specs=[pl.BlockSpec((TM, HGRN_WIDTH), tok)] * 8,
        ),
        out_shape=[out] * 8,
        compiler_params=pltpu.CompilerParams(
            dimension_semantics=("arbitrary",), vmem_limit_bytes=VMEM_LIMIT),
        name="in_proj",
    )(tile_cond, x, mod_l, norm_w_l, w_in_l, hgrn_lb)


def _s5_weights(l, a_re, a_im, log_dt, b_re, b_im, c_re, c_im):
    hp = lax.Precision.HIGHEST
    t = S5_T
    pows, abs_, kern = [], [], []
    for d in range(2):
        dt = jnp.exp(log_dt[l, d])[:, None]
        lam_re = jnp.minimum(a_re[l, d], -1e-4)
        lam_im = a_im[l, d]
        mag = jnp.exp(dt * lam_re)
        ang = dt * lam_im
        ab_re = mag * jnp.cos(ang)
        ab_im = mag * jnp.sin(ang)
        den = lam_re * lam_re + lam_im * lam_im
        nr = ab_re - 1.0
        ni = ab_im
        co_re = (nr * lam_re + ni * lam_im) / den
        co_im = (ni * lam_re - nr * lam_im) / den
        bb_re = co_re[..., None] * b_re[l, d] - co_im[..., None] * b_im[l, d]
        bb_im = co_re[..., None] * b_im[l, d] + co_im[..., None] * b_re[l, d]
        pr = [jnp.ones_like(ab_re)]
        pi = [jnp.zeros_like(ab_im)]
        for _ in range(t):
            pr.append(pr[-1] * ab_re - pi[-1] * ab_im)
            pi.append(pr[-2] * ab_im + pi[-1] * ab_re)
        pr = jnp.stack(pr)
        pi = jnp.stack(pi)
        abr = pr[..., None] * bb_re[None] - pi[..., None] * bb_im[None]
        abi = pr[..., None] * bb_im[None] + pi[..., None] * bb_re[None]
        kk = (jnp.einsum('gon,kgni->kgoi', c_re[l, d], abr, precision=hp)
              - jnp.einsum('gon,kgni->kgoi', c_im[l, d], abi, precision=hp))
        pows.append((pr, pi))
        abs_.append((abr, abi))
        kern.append(kk)
    sig = jnp.arange(t)[:, None]
    tau = jnp.arange(t)[None, :]
    lag = tau - sig
    kf = jnp.where((lag >= 0)[:, :, None, None, None], kern[0][jnp.clip(lag, 0, t)], 0.0)
    kb = jnp.where((lag <= 0)[:, :, None, None, None], kern[1][jnp.clip(-lag, 0, t)], 0.0)
    toep = (kf + kb).transpose(2, 0, 4, 1, 3).reshape(S5_GROUPS, S5_LANES, S5_LANES)
    (abr_f, abi_f), (abr_b, abi_b) = abs_
    fr = abr_f[t - 1::-1] if t > 1 else abr_f[:1]
    fi = abi_f[t - 1::-1] if t > 1 else abi_f[:1]
    br = abr_b[:t]
    bi = abi_b[:t]
    wst = jnp.concatenate([x.transpose(1, 0, 3, 2) for x in (fr, br, fi, bi)], axis=-1)
    wst = wst.reshape(S5_GROUPS, S5_LANES, 4 * S5_STATE)
    w1 = jnp.concatenate([toep, wst], axis=-1).astype(BF16)
    (pr_f, pi_f), (pr_b, pi_b) = pows
    def out_rows(cr, ci, pr_, pi_):
        re_rows = cr[None] * pr_[:, :, None, :] - ci[None] * pi_[:, :, None, :]
        im_rows = -(cr[None] * pi_[:, :, None, :] + ci[None] * pr_[:, :, None, :])
        return re_rows.transpose(1, 3, 0, 2), im_rows.transpose(1, 3, 0, 2)
    fre, fim = out_rows(c_re[l, 0], c_im[l, 0], pr_f[1:t + 1], pi_f[1:t + 1])
    bre, bim = out_rows(c_re[l, 1], c_im[l, 1], pr_b[t:0:-1], pi_b[t:0:-1])
    wout = jnp.concatenate([fre, bre, fim, bim], axis=1).reshape(S5_GROUPS, 4 * S5_STATE, S5_LANES)
    at = jnp.stack([jnp.concatenate([pr_f[t], pr_b[t]], axis=-1),
                    jnp.concatenate([pi_f[t], pi_b[t]], axis=-1)], axis=1)
    return w1, wout.astype(BF16), at


def _s5_kernel(u_ref, w1_ref, wout_ref, at_ref, h0_ref, y_ref, hfin_ref, dh_scr, hf_scr, hb_scr):
    n2 = 2 * S5_STATE
    for gi in range(S5_GB):
        p = jnp.dot(u_ref[gi], w1_ref[gi], preferred_element_type=F32)
        y_ref[gi] = p[:, 0:S5_LANES]
        dh_scr[gi] = p[:, S5_LANES:]
    lane = lax.broadcasted_iota(jnp.int32, (1, n2), 1)
    fwd_lane = lane < S5_STATE
    are = [at_ref[gi, 0:1, :] for gi in range(S5_GB)]
    aim = [at_ref[gi, 1:2, :] for gi in range(S5_GB)]

    def scan(base, nchunk, bsz, init):
        def step(s, carry):
            rf = pl.multiple_of(base + s * bsz, bsz)
            rb = pl.multiple_of(base + (nchunk - 1 - s) * bsz, bsz)
            new = []
            for gi in range(S5_GB):
                hre, him = carry[gi]
                hcat = jnp.concatenate([hre, him], axis=-1)
                hf_scr[gi, pl.ds(rf, bsz), :] = hcat
                hb_scr[gi, pl.ds(rb, bsz), :] = hcat
                df = dh_scr[gi, pl.ds(rf, bsz), :]
                db = dh_scr[gi, pl.ds(rb, bsz), :]
                dre = jnp.where(fwd_lane, df[:, 0:n2], db[:, 0:n2])
                dim = jnp.where(fwd_lane, df[:, n2:], db[:, n2:])
                new.append((are[gi] * hre - aim[gi] * him + dre,
                            are[gi] * him + aim[gi] * hre + dim))
            return tuple(new)
        return lax.fori_loop(0, nchunk, step, init)

    zero = jnp.zeros((BATCH, n2), F32)
    fin = scan(0, SEQ // S5_T, BATCH, tuple((zero, zero) for _ in range(S5_GB)))
    for gi in range(S5_GB):
        hfin_ref[gi] = jnp.concatenate(fin[gi], axis=-1)
    init = tuple((h0_ref[gi, :, 0:n2], h0_ref[gi, :, n2:]) for gi in range(S5_GB))
    scan(S5_ROWS_P, DEC_SEQ // S5_T, S5_BPAD, init)

    lane2 = lax.broadcasted_iota(jnp.int32, (1, 2 * n2), 1)
    fwd2 = (lane2 % n2) < S5_STATE
    for gi in range(S5_GB):
        hent = jnp.where(fwd2, hf_scr[gi], hb_scr[gi]).astype(BF16)
        y_ref[gi] = y_ref[gi] + jnp.dot(hent, wout_ref[gi], preferred_element_type=F32)


def _s5_scan(u_rows, w1, wout, at, h0):
    grp = lambda i: (i, 0, 0)
    return pl.pallas_call(
        _s5_kernel,
        grid=(S5_GROUPS // S5_GB,),
        in_specs=[
            pl.BlockSpec((S5_GB, S5_ROWS, S5_LANES), grp),
            pl.BlockSpec((S5_GB, S5_LANES, 2 * S5_LANES), grp),
            pl.BlockSpec((S5_GB, 4 * S5_STATE, S5_LANES), grp),
            pl.BlockSpec((S5_GB, 2, 2 * S5_STATE), grp),
            pl.BlockSpec((S5_GB, S5_BPAD, 4 * S5_STATE), grp),
        ],
        out_specs=[
            pl.BlockSpec((S5_GB, S5_ROWS, S5_LANES), grp),
            pl.BlockSpec((S5_GB, BATCH, 4 * S5_STATE), grp),
        ],
        out_shape=[
            jax.ShapeDtypeStruct((S5_GROUPS, S5_ROWS, S5_LANES), F32),
            jax.ShapeDtypeStruct((S5_GROUPS, BATCH, 4 * S5_STATE), F32),
        ],
        scratch_shapes=[pltpu.VMEM((S5_GB, S5_ROWS, 4 * S5_STATE), F32)] * 3,
        compiler_params=pltpu.CompilerParams(
            dimension_semantics=("arbitrary",), vmem_limit_bytes=VMEM_LIMIT),
        name="s5_scan",
    )(u_rows, w1, wout, at, h0)


def _s5_to_rows(u):
    g, p, t = S5_GROUPS, S5_GROUP_CH, S5_T
    up = u[:N_PROMPT].reshape(BATCH, SEQ // t, t, g, p).transpose(3, 1, 0, 2, 4)
    us = u[N_PROMPT:].reshape(DEC_BATCH, DEC_SEQ // t, t, g, p).transpose(3, 1, 0, 2, 4)
    us = jnp.pad(us, ((0, 0), (0, 0), (0, S5_BPAD - DEC_BATCH), (0, 0), (0, 0)))
    rows = jnp.concatenate([up.reshape(g, S5_ROWS_P, t * p), us.reshape(g, S5_ROWS_S, t * p)], axis=1)
    return rows.astype(BF16)


def _s5_from_rows(y):
    g, p, t = S5_GROUPS, S5_GROUP_CH, S5_T
    yp = y[:, :S5_ROWS_P].reshape(g, SEQ // t, BATCH, t, p).transpose(2, 1, 3, 0, 4)
    ys = y[:, S5_ROWS_P:].reshape(g, DEC_SEQ // t, S5_BPAD, t, p)[:, :, :DEC_BATCH].transpose(2, 1, 3, 0, 4)
    return jnp.concatenate([yp.reshape(N_PROMPT, g * p), ys.reshape(N_SAMPLE, g * p)], axis=0)


def _split3(x):
    hi = x.astype(BF16)
    r1 = x - hi.astype(F32)
    mid = r1.astype(BF16)
    lo = (r1 - mid.astype(F32)).astype(BF16)
    return hi, mid, lo


def _piecewise_rows(b, blk, row_in_blk):
    parts = []
    for j in range(HC // blk):
        r = j * blk + row_in_blk
        parts.append(jnp.broadcast_to(b[r:r + 1, :], (blk, b.shape[1])))
    return parts[0] if len(parts) == 1 else jnp.concatenate(parts, axis=0)


def _nt(a, b):
    return lax.dot_general(a, b, (((1,), (1,)), ((), ())), preferred_element_type=F32)


def _tn(a, b):
    return lax.dot_general(a, b, (((0,), (0,)), ((), ())), preferred_element_type=F32)


def _hgrn_dir(reverse, q_ref, v_ref, lf_ref, k_ref, o_ref, st_ref):
    row = lax.broadcasted_iota(jnp.int32, (HC, HC), 0)
    col = lax.broadcasted_iota(jnp.int32, (HC, HC), 1)
    causal = (col >= row) if reverse else (col <= row)
    tri = jnp.where(causal, 1.0, 0.0).astype(BF16)
    lf = lf_ref[...]
    hi, mid, lo = _split3(lf)
    ball = (jnp.dot(tri, hi, preferred_element_type=F32) + jnp.dot(tri, mid, preferred_element_type=F32)
            + jnp.dot(tri, lo, preferred_element_type=F32))
    last = 0 if reverse else HC - 1
    masks = []
    for blk in (128, 64, 32):
        half = blk // 2
        same = (row // blk) == (col // blk)
        t_hi = (row % blk) >= half
        s_hi = (col % blk) >= half
        if reverse:
            masks.append(same & jnp.logical_not(t_hi) & s_hi)
        else:
            masks.append(same & t_hi & jnp.logical_not(s_hi))
    diag_mask = ((row // 16) == (col // 16)) & causal
    for h in range(HGRN_HEADS):
        sl = slice(h * HGRN_DK, (h + 1) * HGRN_DK)
        b = ball[:, sl]
        q = q_ref[:, sl]
        k = k_ref[:, sl]
        v = v_ref[:, sl].astype(BF16)
        st = st_ref[h]
        b_last = b[last:last + 1, :]
        q_in = (q * jnp.exp(b)).astype(BF16)
        k_in = (k * jnp.exp(b_last - b)).astype(BF16)
        o = _nt(q_in, st.astype(BF16))
        st_ref[h] = jnp.exp(b_last) * st + _tn(v, k_in)
        scores = jnp.zeros((HC, HC), F32)
        for blk, mask in zip((128, 64, 32), masks):
            half = blk // 2
            m = _piecewise_rows(b, blk, half if reverse else half - 1)
            qj = (q * jnp.exp(jnp.minimum(b - m, 0.0))).astype(BF16)
            kj = (k * jnp.exp(jnp.minimum(m - b, 0.0))).astype(BF16)
            scores = scores + jnp.where(mask, _nt(qj, kj), 0.0)
        m = _piecewise_rows(b, 16, 8 if reverse else 7)
        qd = (q * jnp.exp(jnp.minimum(b - m, EXP_CLAMP))).astype(BF16)
        kd = (k * jnp.exp(jnp.minimum(m - b, EXP_CLAMP))).astype(BF16)
        scores = scores + jnp.where(diag_mask, _nt(qd, kd), 0.0)
        o_ref[:, sl] = o + jnp.dot(scores.astype(BF16), v, preferred_element_type=F32)


def _hgrn_kernel(cf_ref, cb_ref, seq_ref, first_ref, last_ref,
                 qf_ref, vf_ref, lff_ref, kf_ref, qb_ref, vb_ref, lfb_ref, kb_ref, s0_ref,
                 of_ref, ob_ref, sfin_ref, st_scr):
    del cf_ref, cb_ref, seq_ref
    j = pl.program_id(0)

    @pl.when(first_ref[j] == 1)
    def _():
        st_scr[...] = s0_ref[0]

    _hgrn_dir(False, qf_ref, vf_ref, lff_ref, kf_ref, of_ref, st_scr.at[0])
    _hgrn_dir(True, qb_ref, vb_ref, lfb_ref, kb_ref, ob_ref, st_scr.at[1])

    @pl.when(last_ref[j] == 1)
    def _():
        sfin_ref[0] = st_scr[...]


def _hgrn_tables():
    cf, cb, sq, first, last = [], [], [], [], []
    base = 0
    for s in range(N_SEQ):
        nc = (SEQ if s < BATCH else DEC_SEQ) // HC
        for t in range(nc):
            cf.append(base + t)
            cb.append(base + nc - 1 - t)
            sq.append(s)
            first.append(int(t == 0))
            last.append(int(t == nc - 1))
        base += nc
    return tuple(jnp.asarray(x, jnp.int32) for x in (cf, cb, sq, first, last))


def _hgrn_scan(qs, v, lf_f, k_f, lf_b, k_b, s0):
    fwd = lambda j, cf, cb, sq, fi, la: (cf[j], 0)
    bwd = lambda j, cf, cb, sq, fi, la: (cb[j], 0)
    seq = lambda j, cf, cb, sq, fi, la: (sq[j], 0, 0, 0, 0)
    tile = (HC, HGRN_WIDTH)
    sblk = (1, 2, HGRN_HEADS, HGRN_DV, HGRN_DK)
    out = jax.ShapeDtypeStruct((N_TOK, HGRN_WIDTH), F32)
    return pl.pallas_call(
        _hgrn_kernel,
        grid_spec=pltpu.PrefetchScalarGridSpec(
            num_scalar_prefetch=5,
            grid=(N_HCHUNK,),
            in_specs=[pl.BlockSpec(tile, fwd)] * 4 + [pl.BlockSpec(tile, bwd)] * 4
                     + [pl.BlockSpec(sblk, seq)],
            out_specs=[pl.BlockSpec(tile, fwd), pl.BlockSpec(tile, bwd), pl.BlockSpec(sblk, seq)],
            scratch_shapes=[pltpu.VMEM(sblk[1:], F32)],
        ),
        out_shape=[out, out, jax.ShapeDtypeStruct((N_SEQ,) + sblk[1:], F32)],
        compiler_params=pltpu.CompilerParams(
            dimension_semantics=("arbitrary",), vmem_limit_bytes=VMEM_LIMIT),
        name="hgrn_scan",
    )(*_hgrn_tables(), qs, v, lf_f, k_f, qs, v, lf_b, k_b, s0)


def _gelu_tanh(x):
    return 0.5 * x * (1.0 + jnp.tanh(math.sqrt(2.0 / math.pi) * (x + 0.044715 * (x * x * x))))


def _out_kernel(cond_ref, x_ref, yc_ref, u_ref, of_ref, ob_ref, g_ref, mod_ref, nw_ref, d_ref,
                wglu_ref, bglu_ref, hnw_ref, wout_ref, rw_ref, rb_ref,
                x1_ref, h2_ref, ri_ref, rg_ref, cnt_ref):
    del cond_ref
    mod = mod_ref[0]
    g1 = mod[:, 2 * D_MODEL:3 * D_MODEL]
    sh2 = mod[:, 3 * D_MODEL:4 * D_MODEL]
    sc2 = mod[:, 4 * D_MODEL:5 * D_MODEL]
    y = _gelu_tanh(yc_ref[...] + d_ref[...] * u_ref[...])
    y_s5 = y * jax.nn.sigmoid(jnp.dot(y.astype(BF16), wglu_ref[...], preferred_element_type=F32) + bglu_ref[...])
    o = of_ref[...] + ob_ref[...]
    gs = g_ref[...]
    heads = []
    for h in range(HGRN_HEADS):
        sl = slice(h * HGRN_DV, (h + 1) * HGRN_DV)
        heads.append(_rms(o[:, sl], hnw_ref[...]) * gs[:, sl])
    y_hg = jnp.concatenate(heads, axis=-1)
    mix = (jnp.dot(y_s5.astype(BF16), wout_ref[0:S5_WIDTH, :], preferred_element_type=F32)
           + jnp.dot(y_hg.astype(BF16), wout_ref[S5_WIDTH:, :], preferred_element_type=F32))
    x1 = x_ref[...] + g1 * _rms(mix, nw_ref[1:2, :])
    x1_ref[...] = x1
    h2 = _rms(x1, nw_ref[2:3, :]) * (1.0 + sc2) + sh2
    h2_ref[...] = h2.astype(BF16)
    logits = jnp.dot(h2, rw_ref[...], precision=lax.Precision.HIGHEST, preferred_element_type=F32) + rb_ref[...]
    eidx = lax.broadcasted_iota(jnp.int32, (TM, N_EXPERTS), 1).astype(F32)
    vals = logits
    top_v, top_i, onehots = [], [], []
    for _ in range(TOP_K):
        mx = jnp.max(vals, axis=-1, keepdims=True)
        ix = jnp.min(jnp.where(vals == mx, eidx, float(N_EXPERTS)), axis=-1, keepdims=True)
        sel = eidx == ix
        top_v.append(mx)
        top_i.append(ix)
        onehots.append(sel)
        vals = jnp.where(sel, -jnp.inf, vals)
    ex = [jnp.exp(tv - top_v[0]) for tv in top_v]
    den = ex[0] + ex[1] + ex[2] + ex[3]
    tot = jnp.zeros((TM, N_EXPERTS), F32)
    for sel in onehots:
        tot = tot + jnp.where(sel, 1.0, 0.0)
    r_t = lax.broadcasted_iota(jnp.int32, (TM, TM), 0)
    r_s = lax.broadcasted_iota(jnp.int32, (TM, TM), 1)
    strict = jnp.where(r_s < r_t, 1.0, 0.0).astype(BF16)
    before = jnp.dot(strict, tot.astype(BF16), preferred_element_type=F32)
    lane = lax.broadcasted_iota(jnp.int32, (TM, 128), 1)
    ri = jnp.zeros((TM, 128), F32)
    rg = jnp.zeros((TM, 128), F32)
    for kk in range(TOP_K):
        rank = jnp.sum(jnp.where(onehots[kk], before, 0.0), axis=-1, keepdims=True)
        ri = jnp.where(lane == kk, top_i[kk], ri)
        ri = jnp.where(lane == TOP_K + kk, rank, ri)
        rg = jnp.where(lane == kk, ex[kk] / den, rg)
    ri_ref[...] = ri.astype(jnp.int32)
    rg_ref[...] = rg
    cnt_ref[0] = jnp.sum(tot, axis=0, keepdims=True).astype(jnp.int32)


def _out_proj(tile_cond, x, ycore, u, o_f, o_b, gs, mod_l, norm_w_l, s5_d_l, wglu_l, bglu_l, hnw_l,
              wout_l, rw_l, rb_l):
    tok = lambda i, c: (i, 0)
    full2 = lambda i, c: (0, 0)
    half = pl.BlockSpec((TM, HGRN_WIDTH), tok)
    wide = pl.BlockSpec((TM, D_MODEL), tok)
    return pl.pallas_call(
        _out_kernel,
        grid_spec=pltpu.PrefetchScalarGridSpec(
            num_scalar_prefetch=1,
            grid=(N_TILES,),
            in_specs=[
                wide, half, half, half, half, half,
                pl.BlockSpec((1, 1, 6 * D_MODEL), lambda i, c: (c[i], 0, 0)),
                pl.BlockSpec((4, D_MODEL), full2),
                pl.BlockSpec((1, S5_WIDTH), full2),
                pl.BlockSpec((S5_WIDTH, S5_WIDTH), full2),
                pl.BlockSpec((1, S5_WIDTH), full2),
                pl.BlockSpec((1, HGRN_DV), full2),
                pl.BlockSpec((D_MODEL, D_MODEL), full2),
                pl.BlockSpec((D_MODEL, N_EXPERTS), full2),
                pl.BlockSpec((1, N_EXPERTS), full2),
            ],
            out_specs=[wide, wide, pl.BlockSpec((TM, 128), tok), pl.BlockSpec((TM, 128), tok),
                       pl.BlockSpec((1, 1, N_EXPERTS), lambda i, c: (i, 0, 0))],
        ),
        out_shape=[
            jax.ShapeDtypeStruct((N_TOK, D_MODEL), F32),
            jax.ShapeDtypeStruct((N_TOK, D_MODEL), BF16),
            jax.ShapeDtypeStruct((N_TOK, 128), jnp.int32),
            jax.ShapeDtypeStruct((N_TOK, 128), F32),
            jax.ShapeDtypeStruct((N_TILES, 1, N_EXPERTS), jnp.int32),
        ],
        compiler_params=pltpu.CompilerParams(
            dimension_semantics=("arbitrary",), vmem_limit_bytes=VMEM_LIMIT),
        name="out_proj_router",
    )(tile_cond, x, ycore, u, o_f, o_b, gs, mod_l, norm_w_l, s5_d_l, wglu_l, bglu_l, hnw_l,
      wout_l, rw_l, rb_l)


def _segment_copies(i_tile, lstart_ref, gstart_ref, ngrp_ref, local_ref, global_ref, sem, to_global, wait):
    del i_tile

    def per_expert(e, c):
        ls = lstart_ref[0, 0, e]
        gs = gstart_ref[0, 0, e]

        def per_group(g, c2):
            lo = pl.multiple_of(ls + g * SEG_ALIGN, SEG_ALIGN)
            go = pl.multiple_of(gs + g * SEG_ALIGN, SEG_ALIGN)
            loc = local_ref.at[pl.ds(lo, SEG_ALIGN)]
            glo = global_ref.at[pl.ds(go, SEG_ALIGN)]
            cp = pltpu.make_async_copy(loc, glo, sem) if to_global else pltpu.make_async_copy(glo, loc, sem)
            if wait:
                cp.wait()
            else:
                cp.start()
            return c2

        return lax.fori_loop(0, ngrp_ref[0, 0, e], per_group, c)

    lax.fori_loop(0, N_EXPERTS, per_expert, 0)


def _dispatch_kernel(zpos_ref, slot_ref, lstart_ref, gstart_ref, ngrp_ref, h_ref, xs_ref,
                     sbuf, zbuf, sem, zsem):
    i = pl.program_id(0)

    @pl.when(i == 0)
    def _():
        zbuf[...] = jnp.zeros_like(zbuf)
        for wait in (False, True):
            for e in range(N_EXPERTS):
                @pl.when(zpos_ref[e] >= 0)
                def _():
                    z0 = pl.multiple_of(zpos_ref[e], MOE_BM)
                    cp = pltpu.make_async_copy(zbuf, xs_ref.at[pl.ds(z0, MOE_BM)], zsem)
                    if wait:
                        cp.wait()
                    else:
                        cp.start()

    srow = lax.broadcasted_iota(jnp.int32, (TILE_SLOTS, TM), 0)
    perm = jnp.zeros((TILE_SLOTS, TM), F32)
    for kk in range(TOP_K):
        perm = perm + jnp.where(srow == slot_ref[0, kk:kk + 1, :], 1.0, 0.0)
    rows = jnp.dot(perm.astype(BF16), h_ref[...], preferred_element_type=F32)
    sbuf[...] = rows
    for wait in (False, True):
        _segment_copies(i, lstart_ref, gstart_ref, ngrp_ref, sbuf, xs_ref, sem, True, wait)


def _seg_specs():
    tab = lambda i, *_: (i, 0, 0)
    return [pl.BlockSpec((1, 1, N_EXPERTS), tab, memory_space=pltpu.SMEM)] * 3


def _dispatch(zpos, slot_t, lstart, gstart, ngrp, h2):
    return pl.pallas_call(
        _dispatch_kernel,
        grid_spec=pltpu.PrefetchScalarGridSpec(
            num_scalar_prefetch=1,
            grid=(N_TILES,),
            in_specs=[pl.BlockSpec((1, TOP_K, TM), lambda i, z: (i, 0, 0))] + _seg_specs()
                     + [pl.BlockSpec((TM, D_MODEL), lambda i, z: (i, 0))],
            out_specs=pl.BlockSpec(memory_space=pl.ANY),
            scratch_shapes=[pltpu.VMEM((TILE_SLOTS, D_MODEL), F32),
                            pltpu.VMEM((MOE_BM, D_MODEL), F32),
                            pltpu.SemaphoreType.DMA, pltpu.SemaphoreType.DMA],
        ),
        out_shape=jax.ShapeDtypeStruct((MOE_ROWS, D_MODEL), F32),
        compiler_params=pltpu.CompilerParams(
            dimension_semantics=("arbitrary",), vmem_limit_bytes=VMEM_LIMIT),
        name="moe_dispatch",
    )(zpos, slot_t, lstart, gstart, ngrp, h2)


def _moe_kernel(be_ref, nu_ref, x_ref, w1_ref, b1_ref, w2_ref, b2_ref, o_ref, w1b, w2b):
    i = pl.program_id(0)
    prev = be_ref[jnp.maximum(i - 1, 0)]
    fresh = (i == 0) | (be_ref[i] != prev)

    @pl.when(fresh & (i < nu_ref[0]))
    def _():
        def cast(j, c):
            r = pl.multiple_of(j * 128, 128)
            w1b[pl.ds(r, 128), :] = w1_ref[0, pl.ds(r, 128), :].astype(BF16)
            w2b[pl.ds(r, 128), :] = w2_ref[0, pl.ds(r, 128), :].astype(BF16)
            return c

        lax.fori_loop(0, D_MODEL // 128, cast, 0)

    @pl.when(i < nu_ref[0])
    def _():
        h = jnp.dot(x_ref[...].astype(BF16), w1b[...], preferred_element_type=F32) + b1_ref[0]
        glu = jnp.minimum(h[:, :D_FF], SWIGLU_LIMIT)
        lin = jnp.clip(h[:, D_FF:], -SWIGLU_LIMIT, SWIGLU_LIMIT)
        act = glu * jax.nn.sigmoid(SWIGLU_ALPHA * glu) * (lin + 1.0)
        o_ref[...] = jnp.dot(act.astype(BF16), w2b[...], preferred_element_type=F32) + b2_ref[0]


def _moe_experts(block_e, n_used, xs, w1, b1, w2, b2):
    blk = lambda i, be, nu: (jnp.minimum(i, nu[0] - 1), 0)
    exp3 = lambda i, be, nu: (be[i], 0, 0)
    return pl.pallas_call(
        _moe_kernel,
        grid_spec=pltpu.PrefetchScalarGridSpec(
            num_scalar_prefetch=2,
            grid=(MOE_BLOCKS,),
            in_specs=[
                pl.BlockSpec((MOE_BM, D_MODEL), blk),
                pl.BlockSpec((1, D_MODEL, 2 * D_FF), exp3),
                pl.BlockSpec((1, 1, 2 * D_FF), exp3),
                pl.BlockSpec((1, D_FF, D_MODEL), exp3),
                pl.BlockSpec((1, 1, D_MODEL), exp3),
            ],
            out_specs=pl.BlockSpec((MOE_BM, D_MODEL), blk),
            scratch_shapes=[pltpu.VMEM((D_MODEL, 2 * D_FF), BF16), pltpu.VMEM((D_FF, D_MODEL), BF16)],
        ),
        out_shape=jax.ShapeDtypeStruct((MOE_ROWS, D_MODEL), F32),
        compiler_params=pltpu.CompilerParams(
            dimension_semantics=("arbitrary",), vmem_limit_bytes=VMEM_LIMIT),
        name="moe_experts",
    )(block_e, n_used, xs, w1, b1.reshape(N_EXPERTS, 1, 2 * D_FF), w2, b2.reshape(N_EXPERTS, 1, D_MODEL))


def _combine_kernel(cond_ref, lstart_ref, gstart_ref, ngrp_ref, x1_ref, slot_ref, gate_ref, mod_ref, nw_ref,
                    ys_ref, o_ref, buf, sem):
    del cond_ref
    i = pl.program_id(0)

    @pl.when(i == 0)
    def _():
        buf[...] = jnp.zeros_like(buf)

    for wait in (False, True):
        _segment_copies(i, lstart_ref, gstart_ref, ngrp_ref, buf, ys_ref, sem, False, wait)
    scol = lax.broadcasted_iota(jnp.int32, (TM, TILE_SLOTS), 1)
    slot = slot_ref[...]
    gate = gate_ref[...]
    gmat = jnp.zeros((TM, TILE_SLOTS), F32)
    for kk in range(TOP_K):
        gmat = gmat + jnp.where(scol == slot[:, kk:kk + 1], gate[:, kk:kk + 1], 0.0)
    gmat = gmat.astype(BF16)
    ffn = jnp.dot(gmat, buf[...].astype(BF16), preferred_element_type=F32)
    g2 = mod_ref[0][:, 5 * D_MODEL:6 * D_MODEL]
    o_ref[...] = x1_ref[...] + g2 * _rms(ffn, nw_ref[3:4, :])


def _combine(tile_cond, lstart, gstart, ngrp, x1, slot, gate, mod_l, norm_w_l, ys):
    tok = lambda i, c: (i, 0)
    return pl.pallas_call(
        _combine_kernel,
        grid_spec=pltpu.PrefetchScalarGridSpec(
            num_scalar_prefetch=1,
            grid=(N_TILES,),
            in_specs=_seg_specs() + [
                pl.BlockSpec((TM, D_MODEL), tok),
                pl.BlockSpec((TM, TOP_K), tok),
                pl.BlockSpec((TM, 128), tok),
                pl.BlockSpec((1, 1, 6 * D_MODEL), lambda i, c: (c[i], 0, 0)),
                pl.BlockSpec((4, D_MODEL), lambda i, c: (0, 0)),
                pl.BlockSpec(memory_space=pl.ANY),
            ],
            out_specs=pl.BlockSpec((TM, D_MODEL), tok),
            scratch_shapes=[pltpu.VMEM((TILE_SLOTS, D_MODEL), F32), pltpu.SemaphoreType.DMA],
        ),
        out_shape=jax.ShapeDtypeStruct((N_TOK, D_MODEL), F32),
        compiler_params=pltpu.CompilerParams(
            dimension_semantics=("arbitrary",), vmem_limit_bytes=VMEM_LIMIT),
        name="moe_combine",
    )(tile_cond, lstart, gstart, ngrp, x1, slot, gate, mod_l, norm_w_l, ys)


def _pos_embed_2d(rows, dim):
    r = jnp.repeat(jnp.arange(rows, dtype=F32), GRID_W)
    col = jnp.tile(jnp.arange(GRID_W, dtype=F32), rows)
    quarter = dim // 4
    omega = 1.0 / (POS_BASE ** (jnp.arange(quarter, dtype=F32) / quarter))

    def emb(pos):
        ang = pos[:, None] * omega[None, :]
        return jnp.concatenate([jnp.sin(ang), jnp.cos(ang)], axis=-1)

    return jnp.concatenate([emb(r), emb(col)], axis=-1)


def _routing_tables(ri, counts):
    i32 = jnp.int32
    e = ri[:, 0:TOP_K].reshape(N_TILES, TM, TOP_K)
    lrank = ri[:, TOP_K:2 * TOP_K].reshape(N_TILES, TM, TOP_K)
    cnt = counts.reshape(N_TILES, N_EXPERTS)
    seg = (cnt + SEG_ALIGN - 1) // SEG_ALIGN * SEG_ALIGN
    lstart = jnp.cumsum(seg, axis=1) - seg
    region = jnp.sum(seg, axis=0)
    padded = (region + MOE_BM - 1) // MOE_BM * MOE_BM
    pend = jnp.cumsum(padded)
    pstart = pend - padded
    gstart = pstart[None, :] + jnp.cumsum(seg, axis=0) - seg
    slot = jnp.take_along_axis(lstart[:, None, :], e, axis=2) + lrank
    slot_t = slot.transpose(0, 2, 1).astype(i32)
    blk_start = jnp.arange(MOE_BLOCKS, dtype=i32) * MOE_BM
    n_used = (pend[-1] // MOE_BM).astype(i32).reshape(1)
    block_e = jnp.minimum(jnp.sum(pend[None, :] <= blk_start[:, None], axis=1), N_EXPERTS - 1)
    last_e = block_e[jnp.maximum(n_used[0] - 1, 0)]
    block_e = jnp.where(blk_start < pend[-1], block_e, last_e).astype(i32)
    zpos = jnp.where(region > 0, pend - MOE_BM, -1).astype(i32)
    tab = lambda t: t.reshape(N_TILES, 1, N_EXPERTS).astype(i32)
    return (slot.reshape(N_TOK, TOP_K).astype(i32), slot_t, tab(lstart), tab(gstart),
            tab(seg // SEG_ALIGN), block_e, n_used, zpos)


def kernel(x_prompt, x_sample, state_s5_re, state_s5_im, state_hgrn, c, c_ctx, ada_w, ada_b, norm_w, w_in,
           s5_a_re, s5_a_im, s5_log_dt, s5_b_re, s5_b_im, s5_c_re, s5_c_im, s5_d, s5_w_glu, s5_b_glu,
           hgrn_lb, hgrn_norm_w, w_out, router_w, router_b, exp_w1, exp_b1, exp_w2, exp_b2):
    pos = _pos_embed_2d(DEC_SEQ // GRID_W, D_MODEL)
    x = jnp.concatenate([x_prompt.reshape(N_PROMPT, D_MODEL),
                         (x_sample + pos[None]).reshape(N_SAMPLE, D_MODEL)], axis=0)
    cond = jnp.concatenate([c_ctx[None, :], c, jnp.zeros((N_COND - 1 - DEC_BATCH, D_MODEL), F32)], axis=0)
    mod = _modulation(cond, ada_w, ada_b).reshape(DEPTH, N_COND, 1, 6 * D_MODEL)
    tiles = jnp.arange(N_TILES, dtype=jnp.int32)
    tile_cond = jnp.where(tiles < N_PROMPT // TM, 0, 1 + (tiles - N_PROMPT // TM) // (DEC_SEQ // TM)).astype(jnp.int32)
    w_in_b = w_in.astype(BF16)
    w_out_b = w_out.astype(BF16)
    w_glu_b = s5_w_glu.astype(BF16)

    fin_re, fin_im, fin_hg = [], [], []
    n = S5_STATE
    for l in range(DEPTH):
        u, qs, v, gs, lf_f, k_f, lf_b, k_b = _in_proj(l, tile_cond, x, mod[l], norm_w[l], w_in_b[l], hgrn_lb)

        w1, wout, at = _s5_weights(l, s5_a_re, s5_a_im, s5_log_dt, s5_b_re, s5_b_im, s5_c_re, s5_c_im)
        h0 = jnp.concatenate([
            state_s5_re[:, l].transpose(2, 0, 1, 3).reshape(S5_GROUPS, DEC_BATCH, 2 * n),
            state_s5_im[:, l].transpose(2, 0, 1, 3).reshape(S5_GROUPS, DEC_BATCH, 2 * n)], axis=-1)
        h0 = jnp.pad(h0, ((0, 0), (0, S5_BPAD - DEC_BATCH), (0, 0)))
        y_rows, hfin = _s5_scan(_s5_to_rows(u), w1, wout, at, h0)
        ycore = _s5_from_rows(y_rows)
        fin_re.append(hfin[:, :, 0:2 * n].reshape(S5_GROUPS, BATCH, 2, n).transpose(1, 2, 0, 3))
        fin_im.append(hfin[:, :, 2 * n:].reshape(S5_GROUPS, BATCH, 2, n).transpose(1, 2, 0, 3))

        s0 = jnp.concatenate([jnp.zeros((BATCH, 2, HGRN_HEADS, HGRN_DV, HGRN_DK), F32),
                              jnp.swapaxes(state_hgrn[:, l], -1, -2)], axis=0)
        o_f, o_b, sfin = _hgrn_scan(qs, v, lf_f, k_f, lf_b, k_b, s0)
        fin_hg.append(jnp.swapaxes(sfin[:BATCH], -1, -2))

        x1, h2, ri, gate, counts = _out_proj(
            tile_cond, x, ycore, u, o_f, o_b, gs, mod[l], norm_w[l], s5_d[l].reshape(1, S5_WIDTH),
            w_glu_b[l], s5_b_glu[l].reshape(1, S5_WIDTH), hgrn_norm_w[l].reshape(1, HGRN_DV),
            w_out_b[l], router_w[l], router_b[l].reshape(1, N_EXPERTS))

        slot, slot_t, lstart, gstart, ngrp, block_e, n_used, zpos = _routing_tables(ri, counts)
        xs = _dispatch(zpos, slot_t, lstart, gstart, ngrp, h2)
        ys = _moe_experts(block_e, n_used, xs, exp_w1[l], exp_b1[l], exp_w2[l], exp_b2[l])
        x = _combine(tile_cond, lstart, gstart, ngrp, x1, slot, gate, mod[l], norm_w[l], ys)

    y_prompt = x[:N_PROMPT].reshape(BATCH, SEQ, D_MODEL)
    y_sample = x[N_PROMPT:].reshape(DEC_BATCH, DEC_SEQ, D_MODEL)
    return (y_prompt, y_sample, jnp.stack(fin_re, axis=1), jnp.stack(fin_im, axis=1),
            jnp.stack(fin_hg, axis=1))
```

```python
import functools
import math

import jax
import jax.numpy as jnp
from jax import lax
from jax.experimental import pallas as pl
from jax.experimental.pallas import tpu as pltpu

F32 = jnp.float32
BF16 = jnp.bfloat16

D_MODEL = 1024
BATCH = 16
SEQ = 256
DEPTH = 2
DEC_BATCH = 4
DEC_SEQ = 2048
GRID_W = 64
S5_WIDTH = 512
S5_GROUP_CH = 16
S5_GROUPS = 32
S5_STATE = 64
HGRN_WIDTH = 512
HGRN_DK = 128
HGRN_HEADS = 4
HGRN_DV = 128
IN_COLS = S5_WIDTH + 5 * HGRN_WIDTH
N_EXPERTS = 32
TOP_K = 4
D_FF = D_MODEL
SWIGLU_LIMIT = 7.0
SWIGLU_ALPHA = 1.702
NORM_EPS = 1e-6
POS_BASE = 10000.0

N_PROMPT = BATCH * SEQ
N_SAMPLE = DEC_BATCH * DEC_SEQ
N_TOK = N_PROMPT + N_SAMPLE
N_SEQ = BATCH + DEC_BATCH
N_COND = 8

TM = 256
N_TILES = N_TOK // TM
S5_T = 16
S5_LT = 128
S5_GPT = S5_LT // S5_GROUP_CH
S5_NLT = S5_WIDTH // S5_LT
S5_RB = N_PROMPT
S5_CR = S5_RB // S5_T
S5_NRB = N_TOK // S5_RB
S5_SEQ_PER_RB = S5_RB // DEC_SEQ
S5_NLAG = 2 * S5_T - 1
S5_NS = 4 * S5_STATE
HC = 128
N_HCHUNK = N_TOK // HC
EXP_CLAMP = 80.0
MOE_BM = 256
N_ASSIGN = N_TOK * TOP_K
SEG_ALIGN = 8
TILE_SLOTS = -(-(TM * TOP_K + N_EXPERTS * (SEG_ALIGN - 1)) // 128) * 128
MOE_BLOCKS = -(-(N_ASSIGN + N_TILES * N_EXPERTS * (SEG_ALIGN - 1) + N_EXPERTS * (MOE_BM - 1)) // MOE_BM)
MOE_ROWS = MOE_BLOCKS * MOE_BM
VMEM_LIMIT = 56 * 1024 * 1024


def _rms(x, w):
    return x * lax.rsqrt(jnp.mean(x * x, axis=-1, keepdims=True) + NORM_EPS) * w


def _silu(x):
    return x * jax.nn.sigmoid(x)


MOD_TN = 1536


def _mod_kernel(cond_ref, w_ref, b_ref, o_ref):
    s = _silu(cond_ref[...]).astype(BF16)
    o_ref[0] = jnp.dot(s, w_ref[0].astype(BF16), preferred_element_type=F32) + b_ref[0]


def _modulation(cond, ada_w, ada_b):
    return pl.pallas_call(
        _mod_kernel,
        grid=(DEPTH, 6 * D_MODEL // MOD_TN),
        in_specs=[
            pl.BlockSpec((N_COND, D_MODEL), lambda l, j: (0, 0)),
            pl.BlockSpec((1, D_MODEL, MOD_TN), lambda l, j: (l, 0, j)),
            pl.BlockSpec((1, 1, MOD_TN), lambda l, j: (l, 0, j)),
        ],
        out_specs=pl.BlockSpec((1, N_COND, MOD_TN), lambda l, j: (l, 0, j)),
        out_shape=jax.ShapeDtypeStruct((DEPTH, N_COND, 6 * D_MODEL), F32),
        compiler_params=pltpu.CompilerParams(vmem_limit_bytes=VMEM_LIMIT),
        name="adaln_mod",
    )(cond, ada_w, ada_b.reshape(DEPTH, 1, 6 * D_MODEL))


def _in_kernel(layer, cond_ref, x_ref, mod_ref, nw_ref, w_ref, lb_ref,
               u_ref, q_ref, v_ref, g_ref, lff_ref, kf_ref, lfb_ref, kb_ref):
    del cond_ref
    mod = mod_ref[0]
    sh1 = mod[:, 0:D_MODEL]
    sc1 = mod[:, D_MODEL:2 * D_MODEL]
    h = _rms(x_ref[...], nw_ref[0:1, :]) * (1.0 + sc1) + sh1
    z = jnp.dot(h.astype(BF16), w_ref[...], preferred_element_type=F32)
    w = HGRN_WIDTH
    c0 = S5_WIDTH
    u_ref[...] = z[:, 0:c0]
    q_ref[...] = _silu(z[:, c0:c0 + w]) * (HGRN_DK ** -0.5)
    v_ref[...] = z[:, c0 + 3 * w:c0 + 4 * w]
    g_ref[...] = _silu(z[:, c0 + 4 * w:c0 + 5 * w])
    lbp = lb_ref[...]
    e = jnp.exp(lbp - jnp.max(lbp, axis=0, keepdims=True))
    probs = e / jnp.sum(e, axis=0, keepdims=True)
    lb = jnp.sum(probs[0:layer + 1], axis=0) - probs[0]
    for d, (lf_ref, k_ref) in enumerate(((lff_ref, kf_ref), (lfb_ref, kb_ref))):
        f = z[:, c0 + (1 + d) * w:c0 + (2 + d) * w]
        lbd = lb[d:d + 1, :]
        fg = lbd + (1.0 - lbd) * jax.nn.sigmoid(f)
        lf_ref[...] = jnp.log(fg)
        k_ref[...] = 1.0 - fg


def _in_proj(layer, tile_cond, x, mod_l, norm_w_l, w_in_l, hgrn_lb):
    tok = lambda i, c: (i, 0)
    out = jax.ShapeDtypeStruct((N_TOK, HGRN_WIDTH), F32)
    return pl.pallas_call(
        functools.partial(_in_kernel, layer),
        grid_spec=pltpu.PrefetchScalarGridSpec(
            num_scalar_prefetch=1,
            grid=(N_TILES,),
            in_specs=[
                pl.BlockSpec((TM, D_MODEL), tok),
                pl.BlockSpec((1, 1, 6 * D_MODEL), lambda i, c: (c[i], 0, 0)),
                pl.BlockSpec((4, D_MODEL), lambda i, c: (0, 0)),
                pl.BlockSpec((D_MODEL, IN_COLS), lambda i, c: (0, 0)),
                pl.BlockSpec((DEPTH, 2, HGRN_WIDTH), lambda i, c: (0, 0, 0)),
            ],
            out_specs=[pl.BlockSpec((TM, HGRN_WIDTH), tok)] * 8,
        ),
        out_shape=[out] * 8,
        compiler_params=pltpu.CompilerParams(
            dimension_semantics=("arbitrary",), vmem_limit_bytes=VMEM_LIMIT),
        name="in_proj",
    )(tile_cond, x, mod_l, norm_w_l, w_in_l, hgrn_lb)


def _s5_weights(l, a_re, a_im, log_dt, b_re, b_im, c_re, c_im):
    hp = lax.Precision.HIGHEST
    t = S5_T
    pows, abs_, kern = [], [], []
    for d in range(2):
        dt = jnp.exp(log_dt[l, d])[:, None]
        lam_re = jnp.minimum(a_re[l, d], -1e-4)
        lam_im = a_im[l, d]
        mag = jnp.exp(dt * lam_re)
        ang = dt * lam_im
        ab_re = mag * jnp.cos(ang)
        ab_im = mag * jnp.sin(ang)
        den = lam_re * lam_re + lam_im * lam_im
        nr = ab_re - 1.0
        ni = ab_im
        co_re = (nr * lam_re + ni * lam_im) / den
        co_im = (ni * lam_re - nr * lam_im) / den
        bb_re = co_re[..., None] * b_re[l, d] - co_im[..., None] * b_im[l, d]
        bb_im = co_re[..., None] * b_im[l, d] + co_im[..., None] * b_re[l, d]
        pr = [jnp.ones_like(ab_re)]
        pi = [jnp.zeros_like(ab_im)]
        for _ in range(t):
            pr.append(pr[-1] * ab_re - pi[-1] * ab_im)
            pi.append(pr[-2] * ab_im + pi[-1] * ab_re)
        pr = jnp.stack(pr)
        pi = jnp.stack(pi)
        abr = pr[..., None] * bb_re[None] - pi[..., None] * bb_im[None]
        abi = pr[..., None] * bb_im[None] + pi[..., None] * bb_re[None]
        kk = (jnp.einsum('gon,kgni->kgio', c_re[l, d], abr, precision=hp)
              - jnp.einsum('gon,kgni->kgio', c_im[l, d], abi, precision=hp))
        pows.append((pr, pi))
        abs_.append((abr, abi))
        kern.append(kk)
    p = S5_GROUP_CH
    lags = jnp.concatenate([kern[1][t - 1:0:-1], (kern[0][0] + kern[1][0])[None], kern[0][1:t]], axis=0)
    m = lags.reshape(S5_NLAG, S5_NLT, S5_GPT, p, p)
    bd = jnp.einsum('ltaio,ab->ltaibo', m, jnp.eye(S5_GPT, dtype=F32))
    wcat = bd.reshape(S5_NLAG, S5_NLT, S5_LT, S5_LT).transpose(1, 2, 0, 3).reshape(S5_NLT, S5_LT, S5_NLAG * S5_LT)
    (abr_f, abi_f), (abr_b, abi_b) = abs_
    parts = (abr_f[t - 1::-1], abr_b[:t], abi_f[t - 1::-1], abi_b[:t])
    wst = jnp.concatenate([x.transpose(1, 0, 3, 2) for x in parts], axis=-1)
    wst = wst.reshape(S5_NLT, S5_GPT, t, p, S5_NS).transpose(0, 2, 1, 3, 4).reshape(S5_NLT, t * S5_LT, S5_NS)
    (pr_f, pi_f), (pr_b, pi_b) = pows

    def out_rows(cr, ci, pr_, pi_):
        re_rows = cr[None] * pr_[:, :, None, :] - ci[None] * pi_[:, :, None, :]
        im_rows = -(cr[None] * pi_[:, :, None, :] + ci[None] * pr_[:, :, None, :])
        return re_rows.transpose(1, 3, 0, 2), im_rows.transpose(1, 3, 0, 2)

    fre, fim = out_rows(c_re[l, 0], c_im[l, 0], pr_f[1:t + 1], pi_f[1:t + 1])
    bre, bim = out_rows(c_re[l, 1], c_im[l, 1], pr_b[t:0:-1], pi_b[t:0:-1])
    wout = lax.optimization_barrier(jnp.concatenate([fre, bre, fim, bim], axis=1))
    wdense = (wout.reshape(S5_NLT, S5_GPT, S5_NS, t, p).transpose(0, 2, 3, 1, 4)
              .reshape(S5_NLT, S5_NS, t * S5_LT))
    at = jnp.stack([jnp.concatenate([pr_f[t], pr_b[t]], axis=-1),
                    jnp.concatenate([pi_f[t], pi_b[t]], axis=-1)], axis=1)
    return (wcat.astype(BF16), wst.astype(BF16), wdense.astype(BF16),
            at.reshape(S5_NLT, S5_GPT, 2, 2 * S5_STATE))


def _s5_kernel(u_ref, wcat_ref, wst_ref, wd_ref, at_ref, h0_ref, y_ref, hfin_ref,
               wbig, dh_scr, hf_scr, hb_scr):
    t = S5_T
    n2 = 2 * S5_STATE
    r = pl.program_id(1)

    @pl.when(r == 0)
    def _():
        for s in range(t):
            wbig[s * S5_LT:(s + 1) * S5_LT, :] = wcat_ref[0, :, (t - 1 - s) * S5_LT:(2 * t - 1 - s) * S5_LT]

    xcat = jnp.concatenate([u_ref[pl.ds(s, S5_CR, stride=t), :].astype(BF16) for s in range(t)], axis=-1)
    yacc = jnp.dot(xcat, wbig[...], preferred_element_type=F32)
    lane_k = lax.broadcasted_iota(jnp.int32, (1, t * S5_LT), 1)
    grp_k = (lane_k % S5_LT) // S5_GROUP_CH
    for gi in range(S5_GPT):
        xg = jnp.where(grp_k == gi, xcat, jnp.zeros_like(xcat))
        dh = jnp.dot(xg, wst_ref[0], preferred_element_type=F32)
        dh_scr[gi, 0] = dh[:, 0:n2]
        dh_scr[gi, 1] = dh[:, n2:]

    lane = lax.broadcasted_iota(jnp.int32, (1, n2), 1)
    fwd_lane = lane < S5_STATE
    are = [at_ref[0, gi, 0:1, :] for gi in range(S5_GPT)]
    aim = [at_ref[0, gi, 1:2, :] for gi in range(S5_GPT)]

    def advance(gi, hre, him, dre, dim):
        return (are[gi] * hre - aim[gi] * him + dre, are[gi] * him + aim[gi] * hre + dim)

    @pl.when(r == 0)
    def _():
        nc = SEQ // t
        for gi in range(S5_GPT):
            hre = jnp.zeros((BATCH, n2), F32)
            him = jnp.zeros((BATCH, n2), F32)
            for s in range(nc):
                rows_f = pl.ds(s, BATCH, stride=nc)
                rows_b = pl.ds(nc - 1 - s, BATCH, stride=nc)
                hf_scr[gi, 0, rows_f, :] = hre
                hf_scr[gi, 1, rows_f, :] = him
                hb_scr[gi, 0, rows_b, :] = hre
                hb_scr[gi, 1, rows_b, :] = him
                dre = jnp.where(fwd_lane, dh_scr[gi, 0, rows_f, :], dh_scr[gi, 0, rows_b, :])
                dim = jnp.where(fwd_lane, dh_scr[gi, 1, rows_f, :], dh_scr[gi, 1, rows_b, :])
                hre, him = advance(gi, hre, him, dre, dim)
            hfin_ref[0, gi] = jnp.concatenate([hre, him], axis=-1)

    @pl.when(r > 0)
    def _():
        nc = DEC_SEQ // t
        nb = S5_SEQ_PER_RB
        init = tuple((h0_ref[0, 0, gi, b:b + 1, 0:n2], h0_ref[0, 0, gi, b:b + 1, n2:])
                     for gi in range(S5_GPT) for b in range(nb))

        def step(o, carry):
            new = []
            for gi in range(S5_GPT):
                for b in range(nb):
                    hre, him = carry[gi * nb + b]
                    rf = pl.multiple_of(b * nc + o * 8, 8)
                    rb = pl.multiple_of(b * nc + nc - 8 - o * 8, 8)
                    dfr = dh_scr[gi, 0, pl.ds(rf, 8), :]
                    dfi = dh_scr[gi, 1, pl.ds(rf, 8), :]
                    dbr = dh_scr[gi, 0, pl.ds(rb, 8), :]
                    dbi = dh_scr[gi, 1, pl.ds(rb, 8), :]
                    ent_re, ent_im = [], []
                    for i in range(8):
                        ent_re.append(hre)
                        ent_im.append(him)
                        dre = jnp.where(fwd_lane, dfr[i:i + 1], dbr[7 - i:8 - i])
                        dim = jnp.where(fwd_lane, dfi[i:i + 1], dbi[7 - i:8 - i])
                        hre, him = advance(gi, hre, him, dre, dim)
                    hf_scr[gi, 0, pl.ds(rf, 8), :] = jnp.concatenate(ent_re, axis=0)
                    hf_scr[gi, 1, pl.ds(rf, 8), :] = jnp.concatenate(ent_im, axis=0)
                    hb_scr[gi, 0, pl.ds(rb, 8), :] = jnp.concatenate(ent_re[::-1], axis=0)
                    hb_scr[gi, 1, pl.ds(rb, 8), :] = jnp.concatenate(ent_im[::-1], axis=0)
                    new.append((hre, him))
            return tuple(new)

        lax.fori_loop(0, nc // 8, step, init)

    for gi in range(S5_GPT):
        hent = jnp.concatenate([jnp.where(fwd_lane, hf_scr[gi, 0], hb_scr[gi, 0]),
                                jnp.where(fwd_lane, hf_scr[gi, 1], hb_scr[gi, 1])], axis=-1).astype(BF16)
        yi = jnp.dot(hent, wd_ref[0], preferred_element_type=F32)
        yacc = yacc + jnp.where(grp_k == gi, yi, 0.0)
    for s in range(t):
        y_ref[pl.ds(s, S5_CR, stride=t), :] = yacc[:, s * S5_LT:(s + 1) * S5_LT]


def _s5_scan(u, wcat, wst, wdense, at, h0):
    tile = lambda j, r: (j, 0, 0)
    return pl.pallas_call(
        _s5_kernel,
        grid=(S5_NLT, S5_NRB),
        in_specs=[
            pl.BlockSpec((S5_RB, S5_LT), lambda j, r: (r, j)),
            pl.BlockSpec((1, S5_LT, S5_NLAG * S5_LT), tile),
            pl.BlockSpec((1, S5_T * S5_LT, S5_NS), tile),
            pl.BlockSpec((1, S5_NS, S5_T * S5_LT), tile),
            pl.BlockSpec((1, S5_GPT, 2, 2 * S5_STATE), lambda j, r: (j, 0, 0, 0)),
            pl.BlockSpec((1, 1, S5_GPT, S5_SEQ_PER_RB, S5_NS), lambda j, r: (j, jnp.maximum(r - 1, 0), 0, 0, 0)),
        ],
        out_specs=[
            pl.BlockSpec((S5_RB, S5_LT), lambda j, r: (r, j)),
            pl.BlockSpec((1, S5_GPT, BATCH, S5_NS), lambda j, r: (j, 0, 0, 0)),
        ],
        out_shape=[
            jax.ShapeDtypeStruct((N_TOK, S5_WIDTH), F32),
            jax.ShapeDtypeStruct((S5_NLT, S5_GPT, BATCH, S5_NS), F32),
        ],
        scratch_shapes=[pltpu.VMEM((S5_T * S5_LT, S5_T * S5_LT), BF16)]
                       + [pltpu.VMEM((S5_GPT, 2, S5_CR, 2 * S5_STATE), F32)] * 3,
        compiler_params=pltpu.CompilerParams(
            dimension_semantics=("arbitrary", "arbitrary"), vmem_limit_bytes=VMEM_LIMIT),
        name="s5_scan",
    )(u, wcat, wst, wdense, at, h0)


def _split3(x):
    hi = x.astype(BF16)
    r1 = x - hi.astype(F32)
    mid = r1.astype(BF16)
    lo = (r1 - mid.astype(F32)).astype(BF16)
    return hi, mid, lo


def _piecewise_rows(b, blk, row_in_blk):
    parts = []
    for j in range(HC // blk):
        r = j * blk + row_in_blk
        parts.append(jnp.broadcast_to(b[r:r + 1, :], (blk, b.shape[1])))
    return parts[0] if len(parts) == 1 else jnp.concatenate(parts, axis=0)


def _nt(a, b):
    return lax.dot_general(a, b, (((1,), (1,)), ((), ())), preferred_element_type=F32)


def _tn(a, b):
    return lax.dot_general(a, b, (((0,), (0,)), ((), ())), preferred_element_type=F32)


def _hgrn_dir(reverse, q_ref, v_ref, lf_ref, k_ref, o_ref, st_ref):
    row = lax.broadcasted_iota(jnp.int32, (HC, HC), 0)
    col = lax.broadcasted_iota(jnp.int32, (HC, HC), 1)
    causal = (col >= row) if reverse else (col <= row)
    tri = jnp.where(causal, 1.0, 0.0).astype(BF16)
    lf = lf_ref[...]
    hi, mid, lo = _split3(lf)
    ball = (jnp.dot(tri, hi, preferred_element_type=F32) + jnp.dot(tri, mid, preferred_element_type=F32)
            + jnp.dot(tri, lo, preferred_element_type=F32))
    last = 0 if reverse else HC - 1
    masks = []
    for blk in (128, 64, 32):
        half = blk // 2
        same = (row // blk) == (col // blk)
        t_hi = (row % blk) >= half
        s_hi = (col % blk) >= half
        if reverse:
            masks.append(same & jnp.logical_not(t_hi) & s_hi)
        else:
            masks.append(same & t_hi & jnp.logical_not(s_hi))
    diag_mask = ((row // 16) == (col // 16)) & causal
    for h in range(HGRN_HEADS):
        sl = slice(h * HGRN_DK, (h + 1) * HGRN_DK)
        b = ball[:, sl]
        q = q_ref[:, sl]
        k = k_ref[:, sl]
        v = v_ref[:, sl].astype(BF16)
        st = st_ref[h]
        b_last = b[last:last + 1, :]
        q_in = (q * jnp.exp(b)).astype(BF16)
        k_in = (k * jnp.exp(b_last - b)).astype(BF16)
        o = _nt(q_in, st.astype(BF16))
        st_ref[h] = jnp.exp(b_last) * st + _tn(v, k_in)
        scores = jnp.zeros((HC, HC), F32)
        for blk, mask in zip((128, 64, 32), masks):
            half = blk // 2
            m = _piecewise_rows(b, blk, half if reverse else half - 1)
            qj = (q * jnp.exp(jnp.minimum(b - m, 0.0))).astype(BF16)
            kj = (k * jnp.exp(jnp.minimum(m - b, 0.0))).astype(BF16)
            scores = scores + jnp.where(mask, _nt(qj, kj), 0.0)
        m = _piecewise_rows(b, 16, 8 if reverse else 7)
        qd = (q * jnp.exp(jnp.minimum(b - m, EXP_CLAMP))).astype(BF16)
        kd = (k * jnp.exp(jnp.minimum(m - b, EXP_CLAMP))).astype(BF16)
        scores = scores + jnp.where(diag_mask, _nt(qd, kd), 0.0)
        o_ref[:, sl] = o + jnp.dot(scores.astype(BF16), v, preferred_element_type=F32)


def _hgrn_kernel(cf_ref, cb_ref, seq_ref, first_ref, last_ref,
                 qf_ref, vf_ref, lff_ref, kf_ref, qb_ref, vb_ref, lfb_ref, kb_ref, s0_ref,
                 of_ref, ob_ref, sfin_ref, st_scr):
    del cf_ref, cb_ref, seq_ref
    j = pl.program_id(0)

    @pl.when(first_ref[j] == 1)
    def _():
        st_scr[...] = s0_ref[0]

    _hgrn_dir(False, qf_ref, vf_ref, lff_ref, kf_ref, of_ref, st_scr.at[0])
    _hgrn_dir(True, qb_ref, vb_ref, lfb_ref, kb_ref, ob_ref, st_scr.at[1])

    @pl.when(last_ref[j] == 1)
    def _():
        sfin_ref[0] = st_scr[...]


def _hgrn_tables():
    cf, cb, sq, first, last = [], [], [], [], []
    base = 0
    for s in range(N_SEQ):
        nc = (SEQ if s < BATCH else DEC_SEQ) // HC
        for t in range(nc):
            cf.append(base + t)
            cb.append(base + nc - 1 - t)
            sq.append(s)
            first.append(int(t == 0))
            last.append(int(t == nc - 1))
        base += nc
    return tuple(jnp.asarray(x, jnp.int32) for x in (cf, cb, sq, first, last))


def _hgrn_scan(qs, v, lf_f, k_f, lf_b, k_b, s0):
    fwd = lambda j, cf, cb, sq, fi, la: (cf[j], 0)
    bwd = lambda j, cf, cb, sq, fi, la: (cb[j], 0)
    seq = lambda j, cf, cb, sq, fi, la: (sq[j], 0, 0, 0, 0)
    tile = (HC, HGRN_WIDTH)
    sblk = (1, 2, HGRN_HEADS, HGRN_DV, HGRN_DK)
    out = jax.ShapeDtypeStruct((N_TOK, HGRN_WIDTH), F32)
    return pl.pallas_call(
        _hgrn_kernel,
        grid_spec=pltpu.PrefetchScalarGridSpec(
            num_scalar_prefetch=5,
            grid=(N_HCHUNK,),
            in_specs=[pl.BlockSpec(tile, fwd)] * 4 + [pl.BlockSpec(tile, bwd)] * 4
                     + [pl.BlockSpec(sblk, seq)],
            out_specs=[pl.BlockSpec(tile, fwd), pl.BlockSpec(tile, bwd), pl.BlockSpec(sblk, seq)],
            scratch_shapes=[pltpu.VMEM(sblk[1:], F32)],
        ),
        out_shape=[out, out, jax.ShapeDtypeStruct((N_SEQ,) + sblk[1:], F32)],
        compiler_params=pltpu.CompilerParams(
            dimension_semantics=("arbitrary",), vmem_limit_bytes=VMEM_LIMIT),
        name="hgrn_scan",
    )(*_hgrn_tables(), qs, v, lf_f, k_f, qs, v, lf_b, k_b, s0)


def _gelu_tanh(x):
    return 0.5 * x * (1.0 + jnp.tanh(math.sqrt(2.0 / math.pi) * (x + 0.044715 * (x * x * x))))


def _out_kernel(cond_ref, x_ref, yc_ref, u_ref, of_ref, ob_ref, g_ref, mod_ref, nw_ref, d_ref,
                wglu_ref, bglu_ref, hnw_ref, wout_ref, rw_ref, rb_ref,
                x1_ref, h2_ref, ri_ref, rg_ref, cnt_ref):
    del cond_ref
    mod = mod_ref[0]
    g1 = mod[:, 2 * D_MODEL:3 * D_MODEL]
    sh2 = mod[:, 3 * D_MODEL:4 * D_MODEL]
    sc2 = mod[:, 4 * D_MODEL:5 * D_MODEL]
    y = _gelu_tanh(yc_ref[...] + d_ref[...] * u_ref[...])
    y_s5 = y * jax.nn.sigmoid(jnp.dot(y.astype(BF16), wglu_ref[...], preferred_element_type=F32) + bglu_ref[...])
    o = of_ref[...] + ob_ref[...]
    gs = g_ref[...]
    heads = []
    for h in range(HGRN_HEADS):
        sl = slice(h * HGRN_DV, (h + 1) * HGRN_DV)
        heads.append(_rms(o[:, sl], hnw_ref[...]) * gs[:, sl])
    y_hg = jnp.concatenate(heads, axis=-1)
    mix = (jnp.dot(y_s5.astype(BF16), wout_ref[0:S5_WIDTH, :], preferred_element_type=F32)
           + jnp.dot(y_hg.astype(BF16), wout_ref[S5_WIDTH:, :], preferred_element_type=F32))
    x1 = x_ref[...] + g1 * _rms(mix, nw_ref[1:2, :])
    x1_ref[...] = x1
    h2 = _rms(x1, nw_ref[2:3, :]) * (1.0 + sc2) + sh2
    h2_ref[...] = h2.astype(BF16)
    logits = jnp.dot(h2, rw_ref[...], precision=lax.Precision.HIGHEST, preferred_element_type=F32) + rb_ref[...]
    eidx = lax.broadcasted_iota(jnp.int32, (TM, N_EXPERTS), 1).astype(F32)
    vals = logits
    top_v, top_i, onehots = [], [], []
    for _ in range(TOP_K):
        mx = jnp.max(vals, axis=-1, keepdims=True)
        ix = jnp.min(jnp.where(vals == mx, eidx, float(N_EXPERTS)), axis=-1, keepdims=True)
        sel = eidx == ix
        top_v.append(mx)
        top_i.append(ix)
        onehots.append(sel)
        vals = jnp.where(sel, -jnp.inf, vals)
    ex = [jnp.exp(tv - top_v[0]) for tv in top_v]
    den = ex[0] + ex[1] + ex[2] + ex[3]
    tot = jnp.zeros((TM, N_EXPERTS), F32)
    for sel in onehots:
        tot = tot + jnp.where(sel, 1.0, 0.0)
    r_t = lax.broadcasted_iota(jnp.int32, (TM, TM), 0)
    r_s = lax.broadcasted_iota(jnp.int32, (TM, TM), 1)
    strict = jnp.where(r_s < r_t, 1.0, 0.0).astype(BF16)
    before = jnp.dot(strict, tot.astype(BF16), preferred_element_type=F32)
    cnt = jnp.sum(tot, axis=0, keepdims=True)
    seg = jnp.floor((cnt + (SEG_ALIGN - 1)) * (1.0 / SEG_ALIGN)) * SEG_ALIGN
    e_r = lax.broadcasted_iota(jnp.int32, (N_EXPERTS, N_EXPERTS), 0)
    e_c = lax.broadcasted_iota(jnp.int32, (N_EXPERTS, N_EXPERTS), 1)
    lstart = jnp.dot(seg, jnp.where(e_r < e_c, 1.0, 0.0), precision=lax.Precision.HIGHEST,
                     preferred_element_type=F32)
    before = before + lstart
    lane = lax.broadcasted_iota(jnp.int32, (TM, 128), 1)
    ri = jnp.zeros((TM, 128), F32)
    rg = jnp.zeros((TM, 128), F32)
    for kk in range(TOP_K):
        rank = jnp.sum(jnp.where(onehots[kk], before, 0.0), axis=-1, keepdims=True)
        ri = jnp.where(lane == kk, top_i[kk], ri)
        ri = jnp.where(lane == TOP_K + kk, rank, ri)
        rg = jnp.where(lane == kk, ex[kk] / den, rg)
    ri_ref[...] = ri.astype(jnp.int32)
    rg_ref[...] = rg
    cnt_ref[0] = cnt.astype(jnp.int32)


def _out_proj(tile_cond, x, ycore, u, o_f, o_b, gs, mod_l, norm_w_l, s5_d_l, wglu_l, bglu_l, hnw_l,
              wout_l, rw_l, rb_l):
    tok = lambda i, c: (i, 0)
    full2 = lambda i, c: (0, 0)
    half = pl.BlockSpec((TM, HGRN_WIDTH), tok)
    wide = pl.BlockSpec((TM, D_MODEL), tok)
    return pl.pallas_call(
        _out_kernel,
        grid_spec=pltpu.PrefetchScalarGridSpec(
            num_scalar_prefetch=1,
            grid=(N_TILES,),
            in_specs=[
                wide, half, half, half, half, half,
                pl.BlockSpec((1, 1, 6 * D_MODEL), lambda i, c: (c[i], 0, 0)),
                pl.BlockSpec((4, D_MODEL), full2),
                pl.BlockSpec((1, S5_WIDTH), full2),
                pl.BlockSpec((S5_WIDTH, S5_WIDTH), full2),
                pl.BlockSpec((1, S5_WIDTH), full2),
                pl.BlockSpec((1, HGRN_DV), full2),
                pl.BlockSpec((D_MODEL, D_MODEL), full2),
                pl.BlockSpec((D_MODEL, N_EXPERTS), full2),
                pl.BlockSpec((1, N_EXPERTS), full2),
            ],
            out_specs=[wide, wide, pl.BlockSpec((TM, 128), tok), pl.BlockSpec((TM, 128), tok),
                       pl.BlockSpec((1, 1, N_EXPERTS), lambda i, c: (i, 0, 0))],
        ),
        out_shape=[
            jax.ShapeDtypeStruct((N_TOK, D_MODEL), F32),
            jax.ShapeDtypeStruct((N_TOK, D_MODEL), BF16),
            jax.ShapeDtypeStruct((N_TOK, 128), jnp.int32),
            jax.ShapeDtypeStruct((N_TOK, 128), F32),
            jax.ShapeDtypeStruct((N_TILES, 1, N_EXPERTS), jnp.int32),
        ],
        compiler_params=pltpu.CompilerParams(
            dimension_semantics=("arbitrary",), vmem_limit_bytes=VMEM_LIMIT),
        name="out_proj_router",
    )(tile_cond, x, ycore, u, o_f, o_b, gs, mod_l, norm_w_l, s5_d_l, wglu_l, bglu_l, hnw_l,
      wout_l, rw_l, rb_l)


def _segment_copies(i_tile, lstart_ref, gstart_ref, ngrp_ref, local_ref, global_ref, sem, to_global, wait):
    del i_tile

    def per_expert(e, c):
        ls = lstart_ref[0, 0, e]
        gs = gstart_ref[0, 0, e]

        def per_group(g, c2):
            lo = pl.multiple_of(ls + g * SEG_ALIGN, SEG_ALIGN)
            go = pl.multiple_of(gs + g * SEG_ALIGN, SEG_ALIGN)
            loc = local_ref.at[pl.ds(lo, SEG_ALIGN)]
            glo = global_ref.at[pl.ds(go, SEG_ALIGN)]
            cp = pltpu.make_async_copy(loc, glo, sem) if to_global else pltpu.make_async_copy(glo, loc, sem)
            if wait:
                cp.wait()
            else:
                cp.start()
            return c2

        return lax.fori_loop(0, ngrp_ref[0, 0, e], per_group, c)

    lax.fori_loop(0, N_EXPERTS, per_expert, 0)


def _dispatch_kernel(zpos_ref, slot_ref, lstart_ref, gstart_ref, ngrp_ref, h_ref, xs_ref,
                     sbuf, zbuf, sem, zsem):
    i = pl.program_id(0)

    @pl.when(i == 0)
    def _():
        zbuf[...] = jnp.zeros_like(zbuf)
        for wait in (False, True):
            for e in range(N_EXPERTS):
                @pl.when(zpos_ref[e] >= 0)
                def _():
                    z0 = pl.multiple_of(zpos_ref[e], MOE_BM)
                    cp = pltpu.make_async_copy(zbuf, xs_ref.at[pl.ds(z0, MOE_BM)], zsem)
                    if wait:
                        cp.wait()
                    else:
                        cp.start()

    srow = lax.broadcasted_iota(jnp.int32, (TILE_SLOTS, TM), 0)
    perm = jnp.zeros((TILE_SLOTS, TM), F32)
    for kk in range(TOP_K):
        perm = perm + jnp.where(srow == slot_ref[0, kk:kk + 1, :], 1.0, 0.0)
    rows = jnp.dot(perm.astype(BF16), h_ref[...], preferred_element_type=F32)
    sbuf[...] = rows
    for wait in (False, True):
        _segment_copies(i, lstart_ref, gstart_ref, ngrp_ref, sbuf, xs_ref, sem, True, wait)


def _seg_specs():
    tab = lambda i, *_: (i, 0, 0)
    return [pl.BlockSpec((1, 1, N_EXPERTS), tab, memory_space=pltpu.SMEM)] * 3


def _dispatch(zpos, slot_t, lstart, gstart, ngrp, h2):
    return pl.pallas_call(
        _dispatch_kernel,
        grid_spec=pltpu.PrefetchScalarGridSpec(
            num_scalar_prefetch=1,
            grid=(N_TILES,),
            in_specs=[pl.BlockSpec((1, TOP_K, TM), lambda i, z: (i, 0, 0))] + _seg_specs()
                     + [pl.BlockSpec((TM, D_MODEL), lambda i, z: (i, 0))],
            out_specs=pl.BlockSpec(memory_space=pl.ANY),
            scratch_shapes=[pltpu.VMEM((TILE_SLOTS, D_MODEL), F32),
                            pltpu.VMEM((MOE_BM, D_MODEL), F32),
                            pltpu.SemaphoreType.DMA, pltpu.SemaphoreType.DMA],
        ),
        out_shape=jax.ShapeDtypeStruct((MOE_ROWS, D_MODEL), F32),
        compiler_params=pltpu.CompilerParams(
            dimension_semantics=("arbitrary",), vmem_limit_bytes=VMEM_LIMIT),
        name="moe_dispatch",
    )(zpos, slot_t, lstart, gstart, ngrp, h2)


def _moe_kernel(be_ref, nu_ref, x_ref, w1_ref, b1_ref, w2_ref, b2_ref, o_ref, w1b, w2b):
    i = pl.program_id(0)
    prev = be_ref[jnp.maximum(i - 1, 0)]
    fresh = (i == 0) | (be_ref[i] != prev)

    @pl.when(fresh & (i < nu_ref[0]))
    def _():
        def cast(j, c):
            r = pl.multiple_of(j * 128, 128)
            w1b[pl.ds(r, 128), :] = w1_ref[0, pl.ds(r, 128), :].astype(BF16)
            w2b[pl.ds(r, 128), :] = w2_ref[0, pl.ds(r, 128), :].astype(BF16)
            return c

        lax.fori_loop(0, D_MODEL // 128, cast, 0)

    @pl.when(i < nu_ref[0])
    def _():
        h = jnp.dot(x_ref[...].astype(BF16), w1b[...], preferred_element_type=F32) + b1_ref[0]
        glu = jnp.minimum(h[:, :D_FF], SWIGLU_LIMIT)
        lin = jnp.clip(h[:, D_FF:], -SWIGLU_LIMIT, SWIGLU_LIMIT)
        act = glu * jax.nn.sigmoid(SWIGLU_ALPHA * glu) * (lin + 1.0)
        o_ref[...] = jnp.dot(act.astype(BF16), w2b[...], preferred_element_type=F32) + b2_ref[0]


def _moe_experts(block_e, n_used, xs, w1, b1, w2, b2):
    blk = lambda i, be, nu: (jnp.minimum(i, nu[0] - 1), 0)
    exp3 = lambda i, be, nu: (be[i], 0, 0)
    return pl.pallas_call(
        _moe_kernel,
        grid_spec=pltpu.PrefetchScalarGridSpec(
            num_scalar_prefetch=2,
            grid=(MOE_BLOCKS,),
            in_specs=[
                pl.BlockSpec((MOE_BM, D_MODEL), blk),
                pl.BlockSpec((1, D_MODEL, 2 * D_FF), exp3),
                pl.BlockSpec((1, 1, 2 * D_FF), exp3),
                pl.BlockSpec((1, D_FF, D_MODEL), exp3),
                pl.BlockSpec((1, 1, D_MODEL), exp3),
            ],
            out_specs=pl.BlockSpec((MOE_BM, D_MODEL), blk),
            scratch_shapes=[pltpu.VMEM((D_MODEL, 2 * D_FF), BF16), pltpu.VMEM((D_FF, D_MODEL), BF16)],
        ),
        out_shape=jax.ShapeDtypeStruct((MOE_ROWS, D_MODEL), F32),
        compiler_params=pltpu.CompilerParams(
            dimension_semantics=("arbitrary",), vmem_limit_bytes=VMEM_LIMIT),
        name="moe_experts",
    )(block_e, n_used, xs, w1, b1.reshape(N_EXPERTS, 1, 2 * D_FF), w2, b2.reshape(N_EXPERTS, 1, D_MODEL))


def _combine_kernel(cond_ref, lstart_ref, gstart_ref, ngrp_ref, x1_ref, slot_ref, gate_ref, mod_ref, nw_ref,
                    ys_ref, o_ref, buf, sem):
    del cond_ref
    i = pl.program_id(0)

    @pl.when(i == 0)
    def _():
        buf[...] = jnp.zeros_like(buf)

    for wait in (False, True):
        _segment_copies(i, lstart_ref, gstart_ref, ngrp_ref, buf, ys_ref, sem, False, wait)
    scol = lax.broadcasted_iota(jnp.int32, (TM, TILE_SLOTS), 1)
    slot = slot_ref[...]
    gate = gate_ref[...]
    gmat = jnp.zeros((TM, TILE_SLOTS), F32)
    for kk in range(TOP_K):
        gmat = gmat + jnp.where(scol == slot[:, kk:kk + 1], gate[:, kk:kk + 1], 0.0)
    gmat = gmat.astype(BF16)
    ffn = jnp.dot(gmat, buf[...].astype(BF16), preferred_element_type=F32)
    g2 = mod_ref[0][:, 5 * D_MODEL:6 * D_MODEL]
    o_ref[...] = x1_ref[...] + g2 * _rms(ffn, nw_ref[3:4, :])


def _combine(tile_cond, lstart, gstart, ngrp, x1, slot, gate, mod_l, norm_w_l, ys):
    tok = lambda i, c: (i, 0)
    return pl.pallas_call(
        _combine_kernel,
        grid_spec=pltpu.PrefetchScalarGridSpec(
            num_scalar_prefetch=1,
            grid=(N_TILES,),
            in_specs=_seg_specs() + [
                pl.BlockSpec((TM, D_MODEL), tok),
                pl.BlockSpec((TM, TOP_K), tok),
                pl.BlockSpec((TM, 128), tok),
                pl.BlockSpec((1, 1, 6 * D_MODEL), lambda i, c: (c[i], 0, 0)),
                pl.BlockSpec((4, D_MODEL), lambda i, c: (0, 0)),
                pl.BlockSpec(memory_space=pl.ANY),
            ],
            out_specs=pl.BlockSpec((TM, D_MODEL), tok),
            scratch_shapes=[pltpu.VMEM((TILE_SLOTS, D_MODEL), F32), pltpu.SemaphoreType.DMA],
        ),
        out_shape=jax.ShapeDtypeStruct((N_TOK, D_MODEL), F32),
        compiler_params=pltpu.CompilerParams(
            dimension_semantics=("arbitrary",), vmem_limit_bytes=VMEM_LIMIT),
        name="moe_combine",
    )(tile_cond, lstart, gstart, ngrp, x1, slot, gate, mod_l, norm_w_l, ys)


def _pos_embed_2d(rows, dim):
    r = jnp.repeat(jnp.arange(rows, dtype=F32), GRID_W)
    col = jnp.tile(jnp.arange(GRID_W, dtype=F32), rows)
    quarter = dim // 4
    omega = 1.0 / (POS_BASE ** (jnp.arange(quarter, dtype=F32) / quarter))

    def emb(pos):
        ang = pos[:, None] * omega[None, :]
        return jnp.concatenate([jnp.sin(ang), jnp.cos(ang)], axis=-1)

    return jnp.concatenate([emb(r), emb(col)], axis=-1)


def _routing_tables(ri, counts):
    i32 = jnp.int32
    slot = ri[:, TOP_K:2 * TOP_K].reshape(N_TILES, TM, TOP_K)
    cnt = counts.reshape(N_TILES, N_EXPERTS)
    seg = (cnt + SEG_ALIGN - 1) // SEG_ALIGN * SEG_ALIGN
    lstart = jnp.cumsum(seg, axis=1) - seg
    region = jnp.sum(seg, axis=0)
    padded = (region + MOE_BM - 1) // MOE_BM * MOE_BM
    pend = jnp.cumsum(padded)
    pstart = pend - padded
    gstart = pstart[None, :] + jnp.cumsum(seg, axis=0) - seg
    slot_t = slot.transpose(0, 2, 1).astype(i32)
    blk_start = jnp.arange(MOE_BLOCKS, dtype=i32) * MOE_BM
    n_used = (pend[-1] // MOE_BM).astype(i32).reshape(1)
    block_e = jnp.minimum(jnp.sum(pend[None, :] <= blk_start[:, None], axis=1), N_EXPERTS - 1)
    last_e = block_e[jnp.maximum(n_used[0] - 1, 0)]
    block_e = jnp.where(blk_start < pend[-1], block_e, last_e).astype(i32)
    zpos = jnp.where(region > 0, pend - MOE_BM, -1).astype(i32)
    tab = lambda t: t.reshape(N_TILES, 1, N_EXPERTS).astype(i32)
    return (slot.reshape(N_TOK, TOP_K).astype(i32), slot_t, tab(lstart), tab(gstart),
            tab(seg // SEG_ALIGN), block_e, n_used, zpos)


def kernel(x_prompt, x_sample, state_s5_re, state_s5_im, state_hgrn, c, c_ctx, ada_w, ada_b, norm_w, w_in,
           s5_a_re, s5_a_im, s5_log_dt, s5_b_re, s5_b_im, s5_c_re, s5_c_im, s5_d, s5_w_glu, s5_b_glu,
           hgrn_lb, hgrn_norm_w, w_out, router_w, router_b, exp_w1, exp_b1, exp_w2, exp_b2):
    pos = _pos_embed_2d(DEC_SEQ // GRID_W, D_MODEL)
    x = jnp.concatenate([x_prompt.reshape(N_PROMPT, D_MODEL),
                         (x_sample + pos[None]).reshape(N_SAMPLE, D_MODEL)], axis=0)
    cond = jnp.concatenate([c_ctx[None, :], c, jnp.zeros((N_COND - 1 - DEC_BATCH, D_MODEL), F32)], axis=0)
    mod = _modulation(cond, ada_w, ada_b).reshape(DEPTH, N_COND, 1, 6 * D_MODEL)
    tiles = jnp.arange(N_TILES, dtype=jnp.int32)
    tile_cond = jnp.where(tiles < N_PROMPT // TM, 0, 1 + (tiles - N_PROMPT // TM) // (DEC_SEQ // TM)).astype(jnp.int32)
    w_in_b = w_in.astype(BF16)
    w_out_b = w_out.astype(BF16)
    w_glu_b = s5_w_glu.astype(BF16)

    fin_re, fin_im, fin_hg = [], [], []
    n = S5_STATE
    for l in range(DEPTH):
        u, qs, v, gs, lf_f, k_f, lf_b, k_b = _in_proj(l, tile_cond, x, mod[l], norm_w[l], w_in_b[l], hgrn_lb)

        wcat, wst, wdense, at = _s5_weights(l, s5_a_re, s5_a_im, s5_log_dt, s5_b_re, s5_b_im, s5_c_re, s5_c_im)
        h0 = jnp.concatenate([
            state_s5_re[:, l].transpose(2, 0, 1, 3).reshape(S5_GROUPS, DEC_BATCH, 2 * n),
            state_s5_im[:, l].transpose(2, 0, 1, 3).reshape(S5_GROUPS, DEC_BATCH, 2 * n)], axis=-1)
        h0 = h0.reshape(S5_NLT, S5_GPT, S5_NRB - 1, S5_SEQ_PER_RB, S5_NS).transpose(0, 2, 1, 3, 4)
        ycore, hfin = _s5_scan(u, wcat, wst, wdense, at, h0)
        hfin = hfin.reshape(S5_GROUPS, BATCH, S5_NS)
        fin_re.append(hfin[:, :, 0:2 * n].reshape(S5_GROUPS, BATCH, 2, n).transpose(1, 2, 0, 3))
        fin_im.append(hfin[:, :, 2 * n:].reshape(S5_GROUPS, BATCH, 2, n).transpose(1, 2, 0, 3))

        s0 = jnp.concatenate([jnp.zeros((BATCH, 2, HGRN_HEADS, HGRN_DV, HGRN_DK), F32),
                              jnp.swapaxes(state_hgrn[:, l], -1, -2)], axis=0)
        o_f, o_b, sfin = _hgrn_scan(qs, v, lf_f, k_f, lf_b, k_b, s0)
        fin_hg.append(jnp.swapaxes(sfin[:BATCH], -1, -2))

        x1, h2, ri, gate, counts = _out_proj(
            tile_cond, x, ycore, u, o_f, o_b, gs, mod[l], norm_w[l], s5_d[l].reshape(1, S5_WIDTH),
            w_glu_b[l], s5_b_glu[l].reshape(1, S5_WIDTH), hgrn_norm_w[l].reshape(1, HGRN_DV),
            w_out_b[l], router_w[l], router_b[l].reshape(1, N_EXPERTS))

        slot, slot_t, lstart, gstart, ngrp, block_e, n_used, zpos = _routing_tables(ri, counts)
        xs = _dispatch(zpos, slot_t, lstart, gstart, ngrp, h2)
        ys = _moe_experts(block_e, n_used, xs, exp_w1[l], exp_b1[l], exp_w2[l], exp_b2[l])
        x = _combine(tile_cond, lstart, gstart, ngrp, x1, slot, gate, mod[l], norm_w[l], ys)

    y_prompt = x[:N_PROMPT].reshape(BATCH, SEQ, D_MODEL)
    y_sample = x[N_PROMPT:].reshape(DEC_BATCH, DEC_SEQ, D_MODEL)
    return (y_prompt, y_sample, jnp.stack(fin_re, axis=1), jnp.stack(fin_im, axis=1),
            jnp.stack(fin_hg, axis=1))
```

```python
import functools
import math

import jax
import jax.numpy as jnp
from jax import lax
from jax.experimental import pallas as pl
from jax.experimental.pallas import tpu as pltpu

F32 = jnp.float32
BF16 = jnp.bfloat16

D_MODEL = 1024
BATCH = 16
SEQ = 256
DEPTH = 2
DEC_BATCH = 4
DEC_SEQ = 2048
GRID_W = 64
S5_WIDTH = 512
S5_GROUP_CH = 16
S5_GROUPS = 32
S5_STATE = 64
HGRN_WIDTH = 512
HGRN_DK = 128
HGRN_HEADS = 4
HGRN_DV = 128
IN_COLS = S5_WIDTH + 5 * HGRN_WIDTH
N_EXPERTS = 32
TOP_K = 4
D_FF = D_MODEL
SWIGLU_LIMIT = 7.0
SWIGLU_ALPHA = 1.702
NORM_EPS = 1e-6
POS_BASE = 10000.0

N_PROMPT = BATCH * SEQ
N_SAMPLE = DEC_BATCH * DEC_SEQ
N_TOK = N_PROMPT + N_SAMPLE
N_SEQ = BATCH + DEC_BATCH
N_COND = 8

TM = 256
N_TILES = N_TOK // TM
S5_T = 16
S5_LT = 128
S5_GPT = S5_LT // S5_GROUP_CH
S5_NLT = S5_WIDTH // S5_LT
S5_RB = N_PROMPT
S5_CR = S5_RB // S5_T
S5_NRB = N_TOK // S5_RB
S5_SEQ_PER_RB = S5_RB // DEC_SEQ
S5_NLAG = 2 * S5_T - 1
S5_NS = 4 * S5_STATE
HC = 128
N_HCHUNK = N_TOK // HC
EXP_CLAMP = 80.0
MOE_BM = 256
N_ASSIGN = N_TOK * TOP_K
SEG_ALIGN = 8
TILE_SLOTS = -(-(TM * TOP_K + N_EXPERTS * (SEG_ALIGN - 1)) // 128) * 128
MOE_BLOCKS = -(-(N_ASSIGN + N_TILES * N_EXPERTS * (SEG_ALIGN - 1) + N_EXPERTS * (MOE_BM - 1)) // MOE_BM)
MOE_ROWS = MOE_BLOCKS * MOE_BM
VMEM_LIMIT = 56 * 1024 * 1024


def _rms(x, w):
    return x * lax.rsqrt(jnp.mean(x * x, axis=-1, keepdims=True) + NORM_EPS) * w


def _silu(x):
    return x * jax.nn.sigmoid(x)


MOD_TN = 1536


def _mod_kernel(cond_ref, w_ref, b_ref, o_ref):
    s = _silu(cond_ref[...]).astype(BF16)
    o_ref[0] = jnp.dot(s, w_ref[0].astype(BF16), preferred_element_type=F32) + b_ref[0]


def _modulation(cond, ada_w, ada_b):
    return pl.pallas_call(
        _mod_kernel,
        grid=(DEPTH, 6 * D_MODEL // MOD_TN),
        in_specs=[
            pl.BlockSpec((N_COND, D_MODEL), lambda l, j: (0, 0)),
            pl.BlockSpec((1, D_MODEL, MOD_TN), lambda l, j: (l, 0, j)),
            pl.BlockSpec((1, 1, MOD_TN), lambda l, j: (l, 0, j)),
        ],
        out_specs=pl.BlockSpec((1, N_COND, MOD_TN), lambda l, j: (l, 0, j)),
        out_shape=jax.ShapeDtypeStruct((DEPTH, N_COND, 6 * D_MODEL), F32),
        compiler_params=pltpu.CompilerParams(vmem_limit_bytes=VMEM_LIMIT),
        name="adaln_mod",
    )(cond, ada_w, ada_b.reshape(DEPTH, 1, 6 * D_MODEL))


def _in_kernel(layer, cond_ref, x_ref, mod_ref, nw_ref, w_ref, lb_ref,
               u_ref, q_ref, v_ref, g_ref, lff_ref, kf_ref, lfb_ref, kb_ref):
    del cond_ref
    mod = mod_ref[0]
    sh1 = mod[:, 0:D_MODEL]
    sc1 = mod[:, D_MODEL:2 * D_MODEL]
    h = _rms(x_ref[...], nw_ref[0:1, :]) * (1.0 + sc1) + sh1
    z = jnp.dot(h.astype(BF16), w_ref[...], preferred_element_type=F32)
    w = HGRN_WIDTH
    c0 = S5_WIDTH
    u_ref[...] = z[:, 0:c0]
    q_ref[...] = _silu(z[:, c0:c0 + w]) * (HGRN_DK ** -0.5)
    v_ref[...] = z[:, c0 + 3 * w:c0 + 4 * w]
    g_ref[...] = _silu(z[:, c0 + 4 * w:c0 + 5 * w])
    lbp = lb_ref[...]
    e = jnp.exp(lbp - jnp.max(lbp, axis=0, keepdims=True))
    probs = e / jnp.sum(e, axis=0, keepdims=True)
    lb = jnp.sum(probs[0:layer + 1], axis=0) - probs[0]
    for d, (lf_ref, k_ref) in enumerate(((lff_ref, kf_ref), (lfb_ref, kb_ref))):
        f = z[:, c0 + (1 + d) * w:c0 + (2 + d) * w]
        lbd = lb[d:d + 1, :]
        fg = lbd + (1.0 - lbd) * jax.nn.sigmoid(f)
        lf_ref[...] = jnp.log(fg)
        k_ref[...] = 1.0 - fg


def _in_proj(layer, tile_cond, x, mod_l, norm_w_l, w_in_l, hgrn_lb):
    tok = lambda i, c: (i, 0)
    out = jax.ShapeDtypeStruct((N_TOK, HGRN_WIDTH), F32)
    return pl.pallas_call(
        functools.partial(_in_kernel, layer),
        grid_spec=pltpu.PrefetchScalarGridSpec(
            num_scalar_prefetch=1,
            grid=(N_TILES,),
            in_specs=[
                pl.BlockSpec((TM, D_MODEL), tok),
                pl.BlockSpec((1, 1, 6 * D_MODEL), lambda i, c: (c[i], 0, 0)),
                pl.BlockSpec((4, D_MODEL), lambda i, c: (0, 0)),
                pl.BlockSpec((D_MODEL, IN_COLS), lambda i, c: (0, 0)),
                pl.BlockSpec((DEPTH, 2, HGRN_WIDTH), lambda i, c: (0, 0, 0)),
            ],
            out_specs=[pl.BlockSpec((TM, HGRN_WIDTH), tok)] * 8,
        ),
        out_shape=[out] * 8,
        compiler_params=pltpu.CompilerParams(
            dimension_semantics=("arbitrary",), vmem_limit_bytes=VMEM_LIMIT),
        name="in_proj",
    )(tile_cond, x, mod_l, norm_w_l, w_in_l, hgrn_lb)


def _s5_layer_weights(a_re, a_im, log_dt, b_re, b_im, c_re, c_im):
    hp = lax.Precision.HIGHEST
    t = S5_T
    dt = jnp.exp(log_dt)[..., None]
    lam_re = jnp.minimum(a_re, -1e-4)
    lam_im = a_im
    mag = jnp.exp(dt * lam_re)
    ang = dt * lam_im
    ab_re = mag * jnp.cos(ang)
    ab_im = mag * jnp.sin(ang)
    den = lam_re * lam_re + lam_im * lam_im
    nr = ab_re - 1.0
    ni = ab_im
    co_re = (nr * lam_re + ni * lam_im) / den
    co_im = (ni * lam_re - nr * lam_im) / den
    bb_re = co_re[..., None] * b_re - co_im[..., None] * b_im
    bb_im = co_re[..., None] * b_im + co_im[..., None] * b_re
    pr = [jnp.ones_like(ab_re)]
    pi = [jnp.zeros_like(ab_im)]
    for _ in range(t):
        pr.append(pr[-1] * ab_re - pi[-1] * ab_im)
        pi.append(pr[-2] * ab_im + pi[-1] * ab_re)
    pr = jnp.stack(pr, axis=1)
    pi = jnp.stack(pi, axis=1)
    abr = pr[..., None] * bb_re[:, None] - pi[..., None] * bb_im[:, None]
    abi = pr[..., None] * bb_im[:, None] + pi[..., None] * bb_re[:, None]
    kk = jnp.einsum('dgon,dkgni->dkgio', jnp.concatenate([c_re, -c_im], axis=-1),
                    jnp.concatenate([abr, abi], axis=-2), precision=hp)
    kern = (kk[0], kk[1])
    abs_ = ((abr[0], abi[0]), (abr[1], abi[1]))
    pows = ((pr[0], pi[0]), (pr[1], pi[1]))
    p = S5_GROUP_CH
    lags = jnp.concatenate([kern[1][t - 1:0:-1], (kern[0][0] + kern[1][0])[None], kern[0][1:t]], axis=0)
    m = lags.reshape(S5_NLAG, S5_NLT, S5_GPT, p, p)
    bd = jnp.einsum('ltaio,ab->ltaibo', m, jnp.eye(S5_GPT, dtype=F32))
    wcat = bd.reshape(S5_NLAG, S5_NLT, S5_LT, S5_LT).transpose(1, 2, 0, 3).reshape(S5_NLT, S5_LT, S5_NLAG * S5_LT)
    (abr_f, abi_f), (abr_b, abi_b) = abs_
    parts = (abr_f[t - 1::-1], abr_b[:t], abi_f[t - 1::-1], abi_b[:t])
    wst = jnp.concatenate([x.transpose(1, 0, 3, 2) for x in parts], axis=-1)
    wst = wst.reshape(S5_NLT, S5_GPT, t, p, S5_NS).transpose(0, 2, 1, 3, 4).reshape(S5_NLT, t * S5_LT, S5_NS)
    (pr_f, pi_f), (pr_b, pi_b) = pows

    def out_rows(cr, ci, pr_, pi_):
        re_rows = cr[None] * pr_[:, :, None, :] - ci[None] * pi_[:, :, None, :]
        im_rows = -(cr[None] * pi_[:, :, None, :] + ci[None] * pr_[:, :, None, :])
        return re_rows.transpose(1, 3, 0, 2), im_rows.transpose(1, 3, 0, 2)

    fre, fim = out_rows(c_re[0], c_im[0], pr_f[1:t + 1], pi_f[1:t + 1])
    bre, bim = out_rows(c_re[1], c_im[1], pr_b[t:0:-1], pi_b[t:0:-1])
    wout = lax.optimization_barrier(jnp.concatenate([fre, bre, fim, bim], axis=1))
    wdense = (wout.reshape(S5_NLT, S5_GPT, S5_NS, t, p).transpose(0, 2, 3, 1, 4)
              .reshape(S5_NLT, S5_NS, t * S5_LT))
    at = jnp.stack([jnp.concatenate([pr_f[t], pr_b[t]], axis=-1),
                    jnp.concatenate([pi_f[t], pi_b[t]], axis=-1)], axis=1)
    return (wcat.astype(BF16), wst.astype(BF16), wdense.astype(BF16),
            at.reshape(S5_NLT, S5_GPT, 2, 2 * S5_STATE))


def _s5_weights(*params):
    return jax.vmap(_s5_layer_weights)(*params)


def _s5_kernel(u_ref, wcat_ref, wst_ref, wd_ref, at_ref, h0_ref, y_ref, hfin_ref,
               wbig, dh_scr, hf_scr, hb_scr):
    t = S5_T
    n2 = 2 * S5_STATE
    r = pl.program_id(1)

    @pl.when(r == 0)
    def _():
        for s in range(t):
            wbig[s * S5_LT:(s + 1) * S5_LT, :] = wcat_ref[0, :, (t - 1 - s) * S5_LT:(2 * t - 1 - s) * S5_LT]

    xcat = jnp.concatenate([u_ref[pl.ds(s, S5_CR, stride=t), :].astype(BF16) for s in range(t)], axis=-1)
    yacc = jnp.dot(xcat, wbig[...], preferred_element_type=F32)
    lane_k = lax.broadcasted_iota(jnp.int32, (1, t * S5_LT), 1)
    grp_k = (lane_k % S5_LT) // S5_GROUP_CH
    for gi in range(S5_GPT):
        xg = jnp.where(grp_k == gi, xcat, jnp.zeros_like(xcat))
        dh = jnp.dot(xg, wst_ref[0], preferred_element_type=F32)
        dh_scr[gi, 0] = dh[:, 0:n2]
        dh_scr[gi, 1] = dh[:, n2:]

    lane = lax.broadcasted_iota(jnp.int32, (1, n2), 1)
    fwd_lane = lane < S5_STATE
    are = [at_ref[0, gi, 0:1, :] for gi in range(S5_GPT)]
    aim = [at_ref[0, gi, 1:2, :] for gi in range(S5_GPT)]

    def advance(gi, hre, him, dre, dim):
        return (are[gi] * hre - aim[gi] * him + dre, are[gi] * him + aim[gi] * hre + dim)

    @pl.when(r == 0)
    def _():
        nc = SEQ // t
        for gi in range(S5_GPT):
            hre = jnp.zeros((BATCH, n2), F32)
            him = jnp.zeros((BATCH, n2), F32)
            for s in range(nc):
                rows_f = pl.ds(s, BATCH, stride=nc)
                rows_b = pl.ds(nc - 1 - s, BATCH, stride=nc)
                hf_scr[gi, 0, rows_f, :] = hre
                hf_scr[gi, 1, rows_f, :] = him
                hb_scr[gi, 0, rows_b, :] = hre
                hb_scr[gi, 1, rows_b, :] = him
                dre = jnp.where(fwd_lane, dh_scr[gi, 0, rows_f, :], dh_scr[gi, 0, rows_b, :])
                dim = jnp.where(fwd_lane, dh_scr[gi, 1, rows_f, :], dh_scr[gi, 1, rows_b, :])
                hre, him = advance(gi, hre, him, dre, dim)
            hfin_ref[0, gi] = jnp.concatenate([hre, him], axis=-1)

    @pl.when(r > 0)
    def _():
        nc = DEC_SEQ // t
        nb = S5_SEQ_PER_RB
        init = tuple((h0_ref[0, 0, gi, b:b + 1, 0:n2], h0_ref[0, 0, gi, b:b + 1, n2:])
                     for gi in range(S5_GPT) for b in range(nb))

        def step(o, carry):
            new = []
            for gi in range(S5_GPT):
                for b in range(nb):
                    hre, him = carry[gi * nb + b]
                    rf = pl.multiple_of(b * nc + o * 8, 8)
                    rb = pl.multiple_of(b * nc + nc - 8 - o * 8, 8)
                    dfr = dh_scr[gi, 0, pl.ds(rf, 8), :]
                    dfi = dh_scr[gi, 1, pl.ds(rf, 8), :]
                    dbr = dh_scr[gi, 0, pl.ds(rb, 8), :]
                    dbi = dh_scr[gi, 1, pl.ds(rb, 8), :]
                    ent_re, ent_im = [], []
                    for i in range(8):
                        ent_re.append(hre)
                        ent_im.append(him)
                        dre = jnp.where(fwd_lane, dfr[i:i + 1], dbr[7 - i:8 - i])
                        dim = jnp.where(fwd_lane, dfi[i:i + 1], dbi[7 - i:8 - i])
                        hre, him = advance(gi, hre, him, dre, dim)
                    hf_scr[gi, 0, pl.ds(rf, 8), :] = jnp.concatenate(ent_re, axis=0)
                    hf_scr[gi, 1, pl.ds(rf, 8), :] = jnp.concatenate(ent_im, axis=0)
                    hb_scr[gi, 0, pl.ds(rb, 8), :] = jnp.concatenate(ent_re[::-1], axis=0)
                    hb_scr[gi, 1, pl.ds(rb, 8), :] = jnp.concatenate(ent_im[::-1], axis=0)
                    new.append((hre, him))
            return tuple(new)

        lax.fori_loop(0, nc // 8, step, init)

    for gi in range(S5_GPT):
        hent = jnp.concatenate([jnp.where(fwd_lane, hf_scr[gi, 0], hb_scr[gi, 0]),
                                jnp.where(fwd_lane, hf_scr[gi, 1], hb_scr[gi, 1])], axis=-1).astype(BF16)
        yi = jnp.dot(hent, wd_ref[0], preferred_element_type=F32)
        yacc = yacc + jnp.where(grp_k == gi, yi, 0.0)
    for s in range(t):
        y_ref[pl.ds(s, S5_CR, stride=t), :] = yacc[:, s * S5_LT:(s + 1) * S5_LT]


def _s5_scan(layer, u, wcat, wst, wdense, at, h0):
    tile = lambda j, r: (layer * S5_NLT + j, 0, 0)
    merge = lambda w: w.reshape((DEPTH * S5_NLT,) + w.shape[2:])
    wcat, wst, wdense, at = merge(wcat), merge(wst), merge(wdense), merge(at)
    return pl.pallas_call(
        _s5_kernel,
        grid=(S5_NLT, S5_NRB),
        in_specs=[
            pl.BlockSpec((S5_RB, S5_LT), lambda j, r: (r, j)),
            pl.BlockSpec((1, S5_LT, S5_NLAG * S5_LT), tile),
            pl.BlockSpec((1, S5_T * S5_LT, S5_NS), tile),
            pl.BlockSpec((1, S5_NS, S5_T * S5_LT), tile),
            pl.BlockSpec((1, S5_GPT, 2, 2 * S5_STATE), lambda j, r: (layer * S5_NLT + j, 0, 0, 0)),
            pl.BlockSpec((1, 1, S5_GPT, S5_SEQ_PER_RB, S5_NS), lambda j, r: (j, jnp.maximum(r - 1, 0), 0, 0, 0)),
        ],
        out_specs=[
            pl.BlockSpec((S5_RB, S5_LT), lambda j, r: (r, j)),
            pl.BlockSpec((1, S5_GPT, BATCH, S5_NS), lambda j, r: (j, 0, 0, 0)),
        ],
        out_shape=[
            jax.ShapeDtypeStruct((N_TOK, S5_WIDTH), F32),
            jax.ShapeDtypeStruct((S5_NLT, S5_GPT, BATCH, S5_NS), F32),
        ],
        scratch_shapes=[pltpu.VMEM((S5_T * S5_LT, S5_T * S5_LT), BF16)]
                       + [pltpu.VMEM((S5_GPT, 2, S5_CR, 2 * S5_STATE), F32)] * 3,
        compiler_params=pltpu.CompilerParams(
            dimension_semantics=("arbitrary", "arbitrary"), vmem_limit_bytes=VMEM_LIMIT),
        name="s5_scan",
    )(u, wcat, wst, wdense, at, h0)


def _split3(x):
    hi = x.astype(BF16)
    r1 = x - hi.astype(F32)
    mid = r1.astype(BF16)
    lo = (r1 - mid.astype(F32)).astype(BF16)
    return hi, mid, lo


def _piecewise_rows(b, blk, row_in_blk):
    parts = []
    for j in range(HC // blk):
        r = j * blk + row_in_blk
        parts.append(jnp.broadcast_to(b[r:r + 1, :], (blk, b.shape[1])))
    return parts[0] if len(parts) == 1 else jnp.concatenate(parts, axis=0)


def _nt(a, b):
    return lax.dot_general(a, b, (((1,), (1,)), ((), ())), preferred_element_type=F32)


def _tn(a, b):
    return lax.dot_general(a, b, (((0,), (0,)), ((), ())), preferred_element_type=F32)


def _hgrn_dir(reverse, q_ref, v_ref, lf_ref, k_ref, o_ref, st_ref):
    row = lax.broadcasted_iota(jnp.int32, (HC, HC), 0)
    col = lax.broadcasted_iota(jnp.int32, (HC, HC), 1)
    causal = (col >= row) if reverse else (col <= row)
    tri = jnp.where(causal, 1.0, 0.0).astype(BF16)
    lf = lf_ref[...]
    hi, mid, lo = _split3(lf)
    ball = (jnp.dot(tri, hi, preferred_element_type=F32) + jnp.dot(tri, mid, preferred_element_type=F32)
            + jnp.dot(tri, lo, preferred_element_type=F32))
    last = 0 if reverse else HC - 1
    masks = []
    for blk in (128, 64, 32):
        half = blk // 2
        same = (row // blk) == (col // blk)
        t_hi = (row % blk) >= half
        s_hi = (col % blk) >= half
        if reverse:
            masks.append(same & jnp.logical_not(t_hi) & s_hi)
        else:
            masks.append(same & t_hi & jnp.logical_not(s_hi))
    diag_mask = ((row // 16) == (col // 16)) & causal
    for h in range(HGRN_HEADS):
        sl = slice(h * HGRN_DK, (h + 1) * HGRN_DK)
        b = ball[:, sl]
        q = q_ref[:, sl]
        k = k_ref[:, sl]
        v = v_ref[:, sl].astype(BF16)
        st = st_ref[h]
        b_last = b[last:last + 1, :]
        q_in = (q * jnp.exp(b)).astype(BF16)
        k_in = (k * jnp.exp(b_last - b)).astype(BF16)
        o = _nt(q_in, st.astype(BF16))
        st_ref[h] = jnp.exp(b_last) * st + _tn(v, k_in)
        scores = jnp.zeros((HC, HC), F32)
        for blk, mask in zip((128, 64, 32), masks):
            half = blk // 2
            m = _piecewise_rows(b, blk, half if reverse else half - 1)
            qj = (q * jnp.exp(jnp.minimum(b - m, 0.0))).astype(BF16)
            kj = (k * jnp.exp(jnp.minimum(m - b, 0.0))).astype(BF16)
            scores = scores + jnp.where(mask, _nt(qj, kj), 0.0)
        m = _piecewise_rows(b, 16, 8 if reverse else 7)
        qd = (q * jnp.exp(jnp.minimum(b - m, EXP_CLAMP))).astype(BF16)
        kd = (k * jnp.exp(jnp.minimum(m - b, EXP_CLAMP))).astype(BF16)
        scores = scores + jnp.where(diag_mask, _nt(qd, kd), 0.0)
        o_ref[:, sl] = o + jnp.dot(scores.astype(BF16), v, preferred_element_type=F32)


def _hgrn_kernel(cf_ref, cb_ref, seq_ref, first_ref, last_ref,
                 qf_ref, vf_ref, lff_ref, kf_ref, qb_ref, vb_ref, lfb_ref, kb_ref, s0_ref,
                 of_ref, ob_ref, sfin_ref, st_scr):
    del cf_ref, cb_ref, seq_ref
    j = pl.program_id(0)

    @pl.when(first_ref[j] == 1)
    def _():
        st_scr[...] = s0_ref[0]

    _hgrn_dir(False, qf_ref, vf_ref, lff_ref, kf_ref, of_ref, st_scr.at[0])
    _hgrn_dir(True, qb_ref, vb_ref, lfb_ref, kb_ref, ob_ref, st_scr.at[1])

    @pl.when(last_ref[j] == 1)
    def _():
        sfin_ref[0] = st_scr[...]


def _hgrn_tables():
    cf, cb, sq, first, last = [], [], [], [], []
    base = 0
    for s in range(N_SEQ):
        nc = (SEQ if s < BATCH else DEC_SEQ) // HC
        for t in range(nc):
            cf.append(base + t)
            cb.append(base + nc - 1 - t)
            sq.append(s)
            first.append(int(t == 0))
            last.append(int(t == nc - 1))
        base += nc
    return tuple(jnp.asarray(x, jnp.int32) for x in (cf, cb, sq, first, last))


def _hgrn_scan(qs, v, lf_f, k_f, lf_b, k_b, s0):
    fwd = lambda j, cf, cb, sq, fi, la: (cf[j], 0)
    bwd = lambda j, cf, cb, sq, fi, la: (cb[j], 0)
    seq = lambda j, cf, cb, sq, fi, la: (sq[j], 0, 0, 0, 0)
    tile = (HC, HGRN_WIDTH)
    sblk = (1, 2, HGRN_HEADS, HGRN_DV, HGRN_DK)
    out = jax.ShapeDtypeStruct((N_TOK, HGRN_WIDTH), F32)
    return pl.pallas_call(
        _hgrn_kernel,
        grid_spec=pltpu.PrefetchScalarGridSpec(
            num_scalar_prefetch=5,
            grid=(N_HCHUNK,),
            in_specs=[pl.BlockSpec(tile, fwd)] * 4 + [pl.BlockSpec(tile, bwd)] * 4
                     + [pl.BlockSpec(sblk, seq)],
            out_specs=[pl.BlockSpec(tile, fwd), pl.BlockSpec(tile, bwd), pl.BlockSpec(sblk, seq)],
            scratch_shapes=[pltpu.VMEM(sblk[1:], F32)],
        ),
        out_shape=[out, out, jax.ShapeDtypeStruct((N_SEQ,) + sblk[1:], F32)],
        compiler_params=pltpu.CompilerParams(
            dimension_semantics=("arbitrary",), vmem_limit_bytes=VMEM_LIMIT),
        name="hgrn_scan",
    )(*_hgrn_tables(), qs, v, lf_f, k_f, qs, v, lf_b, k_b, s0)


def _gelu_tanh(x):
    return 0.5 * x * (1.0 + jnp.tanh(math.sqrt(2.0 / math.pi) * (x + 0.044715 * (x * x * x))))


def _out_kernel(cond_ref, x_ref, yc_ref, u_ref, of_ref, ob_ref, g_ref, mod_ref, nw_ref, d_ref,
                wglu_ref, bglu_ref, hnw_ref, wout_ref, rw_ref, rb_ref,
                x1_ref, h2_ref, ri_ref, rg_ref, cnt_ref):
    del cond_ref
    mod = mod_ref[0]
    g1 = mod[:, 2 * D_MODEL:3 * D_MODEL]
    sh2 = mod[:, 3 * D_MODEL:4 * D_MODEL]
    sc2 = mod[:, 4 * D_MODEL:5 * D_MODEL]
    y = _gelu_tanh(yc_ref[...] + d_ref[...] * u_ref[...])
    y_s5 = y * jax.nn.sigmoid(jnp.dot(y.astype(BF16), wglu_ref[...], preferred_element_type=F32) + bglu_ref[...])
    o = of_ref[...] + ob_ref[...]
    gs = g_ref[...]
    heads = []
    for h in range(HGRN_HEADS):
        sl = slice(h * HGRN_DV, (h + 1) * HGRN_DV)
        heads.append(_rms(o[:, sl], hnw_ref[...]) * gs[:, sl])
    y_hg = jnp.concatenate(heads, axis=-1)
    mix = (jnp.dot(y_s5.astype(BF16), wout_ref[0:S5_WIDTH, :], preferred_element_type=F32)
           + jnp.dot(y_hg.astype(BF16), wout_ref[S5_WIDTH:, :], preferred_element_type=F32))
    x1 = x_ref[...] + g1 * _rms(mix, nw_ref[1:2, :])
    x1_ref[...] = x1
    h2 = _rms(x1, nw_ref[2:3, :]) * (1.0 + sc2) + sh2
    h2_ref[...] = h2.astype(BF16)
    logits = jnp.dot(h2, rw_ref[...], precision=lax.Precision.HIGHEST, preferred_element_type=F32) + rb_ref[...]
    eidx = lax.broadcasted_iota(jnp.int32, (TM, N_EXPERTS), 1).astype(F32)
    vals = logits
    top_v, top_i, onehots = [], [], []
    for _ in range(TOP_K):
        mx = jnp.max(vals, axis=-1, keepdims=True)
        ix = jnp.min(jnp.where(vals == mx, eidx, float(N_EXPERTS)), axis=-1, keepdims=True)
        sel = eidx == ix
        top_v.append(mx)
        top_i.append(ix)
        onehots.append(sel)
        vals = jnp.where(sel, -jnp.inf, vals)
    ex = [jnp.exp(tv - top_v[0]) for tv in top_v]
    den = ex[0] + ex[1] + ex[2] + ex[3]
    tot = jnp.zeros((TM, N_EXPERTS), F32)
    for sel in onehots:
        tot = tot + jnp.where(sel, 1.0, 0.0)
    r_t = lax.broadcasted_iota(jnp.int32, (TM, TM), 0)
    r_s = lax.broadcasted_iota(jnp.int32, (TM, TM), 1)
    strict = jnp.where(r_s < r_t, 1.0, 0.0).astype(BF16)
    before = jnp.dot(strict, tot.astype(BF16), preferred_element_type=F32)
    cnt = jnp.sum(tot, axis=0, keepdims=True)
    seg = jnp.floor((cnt + (SEG_ALIGN - 1)) * (1.0 / SEG_ALIGN)) * SEG_ALIGN
    e_r = lax.broadcasted_iota(jnp.int32, (N_EXPERTS, N_EXPERTS), 0)
    e_c = lax.broadcasted_iota(jnp.int32, (N_EXPERTS, N_EXPERTS), 1)
    lstart = jnp.dot(seg, jnp.where(e_r < e_c, 1.0, 0.0), precision=lax.Precision.HIGHEST,
                     preferred_element_type=F32)
    before = before + lstart
    lane = lax.broadcasted_iota(jnp.int32, (TM, 128), 1)
    ri = jnp.zeros((TM, 128), F32)
    rg = jnp.zeros((TM, 128), F32)
    for kk in range(TOP_K):
        rank = jnp.sum(jnp.where(onehots[kk], before, 0.0), axis=-1, keepdims=True)
        ri = jnp.where(lane == kk, top_i[kk], ri)
        ri = jnp.where(lane == TOP_K + kk, rank, ri)
        rg = jnp.where(lane == kk, ex[kk] / den, rg)
    ri_ref[...] = ri.astype(jnp.int32)
    rg_ref[...] = rg
    cnt_ref[0] = cnt.astype(jnp.int32)


def _out_proj(tile_cond, x, ycore, u, o_f, o_b, gs, mod_l, norm_w_l, s5_d_l, wglu_l, bglu_l, hnw_l,
              wout_l, rw_l, rb_l):
    tok = lambda i, c: (i, 0)
    full2 = lambda i, c: (0, 0)
    half = pl.BlockSpec((TM, HGRN_WIDTH), tok)
    wide = pl.BlockSpec((TM, D_MODEL), tok)
    return pl.pallas_call(
        _out_kernel,
        grid_spec=pltpu.PrefetchScalarGridSpec(
            num_scalar_prefetch=1,
            grid=(N_TILES,),
            in_specs=[
                wide, half, half, half, half, half,
                pl.BlockSpec((1, 1, 6 * D_MODEL), lambda i, c: (c[i], 0, 0)),
                pl.BlockSpec((4, D_MODEL), full2),
                pl.BlockSpec((1, S5_WIDTH), full2),
                pl.BlockSpec((S5_WIDTH, S5_WIDTH), full2),
                pl.BlockSpec((1, S5_WIDTH), full2),
                pl.BlockSpec((1, HGRN_DV), full2),
                pl.BlockSpec((D_MODEL, D_MODEL), full2),
                pl.BlockSpec((D_MODEL, N_EXPERTS), full2),
                pl.BlockSpec((1, N_EXPERTS), full2),
            ],
            out_specs=[wide, wide, pl.BlockSpec((TM, 128), tok), pl.BlockSpec((TM, 128), tok),
                       pl.BlockSpec((1, 1, N_EXPERTS), lambda i, c: (i, 0, 0))],
        ),
        out_shape=[
            jax.ShapeDtypeStruct((N_TOK, D_MODEL), F32),
            jax.ShapeDtypeStruct((N_TOK, D_MODEL), BF16),
            jax.ShapeDtypeStruct((N_TOK, 128), jnp.int32),
            jax.ShapeDtypeStruct((N_TOK, 128), F32),
            jax.ShapeDtypeStruct((N_TILES, 1, N_EXPERTS), jnp.int32),
        ],
        compiler_params=pltpu.CompilerParams(
            dimension_semantics=("arbitrary",), vmem_limit_bytes=VMEM_LIMIT),
        name="out_proj_router",
    )(tile_cond, x, ycore, u, o_f, o_b, gs, mod_l, norm_w_l, s5_d_l, wglu_l, bglu_l, hnw_l,
      wout_l, rw_l, rb_l)


def _segment_copies(i_tile, lstart_ref, gstart_ref, ngrp_ref, local_ref, global_ref, sem, to_global, wait):
    del i_tile

    def per_expert(e, c):
        ls = lstart_ref[0, 0, e]
        gs = gstart_ref[0, 0, e]

        def per_group(g, c2):
            lo = pl.multiple_of(ls + g * SEG_ALIGN, SEG_ALIGN)
            go = pl.multiple_of(gs + g * SEG_ALIGN, SEG_ALIGN)
            loc = local_ref.at[pl.ds(lo, SEG_ALIGN)]
            glo = global_ref.at[pl.ds(go, SEG_ALIGN)]
            cp = pltpu.make_async_copy(loc, glo, sem) if to_global else pltpu.make_async_copy(glo, loc, sem)
            if wait:
                cp.wait()
            else:
                cp.start()
            return c2

        return lax.fori_loop(0, ngrp_ref[0, 0, e], per_group, c)

    lax.fori_loop(0, N_EXPERTS, per_expert, 0)


def _dispatch_kernel(zpos_ref, slot_ref, lstart_ref, gstart_ref, ngrp_ref, h_ref, xs_ref,
                     sbuf, zbuf, sem, zsem):
    i = pl.program_id(0)

    @pl.when(i == 0)
    def _():
        zbuf[...] = jnp.zeros_like(zbuf)
        for wait in (False, True):
            for e in range(N_EXPERTS):
                @pl.when(zpos_ref[e] >= 0)
                def _():
                    z0 = pl.multiple_of(zpos_ref[e], MOE_BM)
                    cp = pltpu.make_async_copy(zbuf, xs_ref.at[pl.ds(z0, MOE_BM)], zsem)
                    if wait:
                        cp.wait()
                    else:
                        cp.start()

    srow = lax.broadcasted_iota(jnp.int32, (TILE_SLOTS, TM), 0)
    perm = jnp.zeros((TILE_SLOTS, TM), F32)
    for kk in range(TOP_K):
        perm = perm + jnp.where(srow == slot_ref[0, kk:kk + 1, :], 1.0, 0.0)
    rows = jnp.dot(perm.astype(BF16), h_ref[...], preferred_element_type=F32)
    sbuf[...] = rows
    for wait in (False, True):
        _segment_copies(i, lstart_ref, gstart_ref, ngrp_ref, sbuf, xs_ref, sem, True, wait)


def _seg_specs():
    tab = lambda i, *_: (i, 0, 0)
    return [pl.BlockSpec((1, 1, N_EXPERTS), tab, memory_space=pltpu.SMEM)] * 3


def _dispatch(zpos, slot_t, lstart, gstart, ngrp, h2):
    return pl.pallas_call(
        _dispatch_kernel,
        grid_spec=pltpu.PrefetchScalarGridSpec(
            num_scalar_prefetch=1,
            grid=(N_TILES,),
            in_specs=[pl.BlockSpec((1, TOP_K, TM), lambda i, z: (i, 0, 0))] + _seg_specs()
                     + [pl.BlockSpec((TM, D_MODEL), lambda i, z: (i, 0))],
            out_specs=pl.BlockSpec(memory_space=pl.ANY),
            scratch_shapes=[pltpu.VMEM((TILE_SLOTS, D_MODEL), F32),
                            pltpu.VMEM((MOE_BM, D_MODEL), F32),
                            pltpu.SemaphoreType.DMA, pltpu.SemaphoreType.DMA],
        ),
        out_shape=jax.ShapeDtypeStruct((MOE_ROWS, D_MODEL), F32),
        compiler_params=pltpu.CompilerParams(
            dimension_semantics=("arbitrary",), vmem_limit_bytes=VMEM_LIMIT),
        name="moe_dispatch",
    )(zpos, slot_t, lstart, gstart, ngrp, h2)


def _moe_kernel(be_ref, nu_ref, x_ref, w1_ref, b1_ref, w2_ref, b2_ref, o_ref, w1b, w2b):
    i = pl.program_id(0)
    prev = be_ref[jnp.maximum(i - 1, 0)]
    fresh = (i == 0) | (be_ref[i] != prev)

    @pl.when(fresh & (i < nu_ref[0]))
    def _():
        def cast(j, c):
            r = pl.multiple_of(j * 128, 128)
            w1b[pl.ds(r, 128), :] = w1_ref[0, pl.ds(r, 128), :].astype(BF16)
            w2b[pl.ds(r, 128), :] = w2_ref[0, pl.ds(r, 128), :].astype(BF16)
            return c

        lax.fori_loop(0, D_MODEL // 128, cast, 0)

    @pl.when(i < nu_ref[0])
    def _():
        h = jnp.dot(x_ref[...].astype(BF16), w1b[...], preferred_element_type=F32) + b1_ref[0]
        glu = jnp.minimum(h[:, :D_FF], SWIGLU_LIMIT)
        lin = jnp.clip(h[:, D_FF:], -SWIGLU_LIMIT, SWIGLU_LIMIT)
        act = glu * jax.nn.sigmoid(SWIGLU_ALPHA * glu) * (lin + 1.0)
        o_ref[...] = jnp.dot(act.astype(BF16), w2b[...], preferred_element_type=F32) + b2_ref[0]


def _moe_experts(layer, block_e, n_used, xs, w1, b1, w2, b2):
    blk = lambda i, be, nu: (jnp.maximum(jnp.minimum(i, nu[0] - 1), 0), 0)
    exp3 = lambda i, be, nu: (layer * N_EXPERTS + be[i], 0, 0)
    w1 = w1.reshape(DEPTH * N_EXPERTS, D_MODEL, 2 * D_FF)
    w2 = w2.reshape(DEPTH * N_EXPERTS, D_FF, D_MODEL)
    return pl.pallas_call(
        _moe_kernel,
        grid_spec=pltpu.PrefetchScalarGridSpec(
            num_scalar_prefetch=2,
            grid=(MOE_BLOCKS,),
            in_specs=[
                pl.BlockSpec((MOE_BM, D_MODEL), blk),
                pl.BlockSpec((1, D_MODEL, 2 * D_FF), exp3),
                pl.BlockSpec((1, 1, 2 * D_FF), exp3),
                pl.BlockSpec((1, D_FF, D_MODEL), exp3),
                pl.BlockSpec((1, 1, D_MODEL), exp3),
            ],
            out_specs=pl.BlockSpec((MOE_BM, D_MODEL), blk),
            scratch_shapes=[pltpu.VMEM((D_MODEL, 2 * D_FF), BF16), pltpu.VMEM((D_FF, D_MODEL), BF16)],
        ),
        out_shape=jax.ShapeDtypeStruct((MOE_ROWS, D_MODEL), F32),
        compiler_params=pltpu.CompilerParams(
            dimension_semantics=("arbitrary",), vmem_limit_bytes=VMEM_LIMIT),
        name="moe_experts",
    )(block_e, n_used, xs, w1, b1.reshape(DEPTH * N_EXPERTS, 1, 2 * D_FF), w2,
      b2.reshape(DEPTH * N_EXPERTS, 1, D_MODEL))


def _combine_kernel(cond_ref, lstart_ref, gstart_ref, ngrp_ref, x1_ref, slot_ref, gate_ref, mod_ref, nw_ref,
                    ys_ref, o_ref, buf, sem):
    del cond_ref
    i = pl.program_id(0)

    @pl.when(i == 0)
    def _():
        buf[...] = jnp.zeros_like(buf)

    for wait in (False, True):
        _segment_copies(i, lstart_ref, gstart_ref, ngrp_ref, buf, ys_ref, sem, False, wait)
    scol = lax.broadcasted_iota(jnp.int32, (TM, TILE_SLOTS), 1)
    slot = slot_ref[...]
    gate = gate_ref[...]
    gmat = jnp.zeros((TM, TILE_SLOTS), F32)
    for kk in range(TOP_K):
        gmat = gmat + jnp.where(scol == slot[:, kk:kk + 1], gate[:, kk:kk + 1], 0.0)
    gmat = gmat.astype(BF16)
    ffn = jnp.dot(gmat, buf[...].astype(BF16), preferred_element_type=F32)
    g2 = mod_ref[0][:, 5 * D_MODEL:6 * D_MODEL]
    o_ref[...] = x1_ref[...] + g2 * _rms(ffn, nw_ref[3:4, :])


def _combine(tile_cond, lstart, gstart, ngrp, x1, slot, gate, mod_l, norm_w_l, ys):
    tok = lambda i, c: (i, 0)
    return pl.pallas_call(
        _combine_kernel,
        grid_spec=pltpu.PrefetchScalarGridSpec(
            num_scalar_prefetch=1,
            grid=(N_TILES,),
            in_specs=_seg_specs() + [
                pl.BlockSpec((TM, D_MODEL), tok),
                pl.BlockSpec((TM, TOP_K), tok),
                pl.BlockSpec((TM, 128), tok),
                pl.BlockSpec((1, 1, 6 * D_MODEL), lambda i, c: (c[i], 0, 0)),
                pl.BlockSpec((4, D_MODEL), lambda i, c: (0, 0)),
                pl.BlockSpec(memory_space=pl.ANY),
            ],
            out_specs=pl.BlockSpec((TM, D_MODEL), tok),
            scratch_shapes=[pltpu.VMEM((TILE_SLOTS, D_MODEL), F32), pltpu.SemaphoreType.DMA],
        ),
        out_shape=jax.ShapeDtypeStruct((N_TOK, D_MODEL), F32),
        compiler_params=pltpu.CompilerParams(
            dimension_semantics=("arbitrary",), vmem_limit_bytes=VMEM_LIMIT),
        name="moe_combine",
    )(tile_cond, lstart, gstart, ngrp, x1, slot, gate, mod_l, norm_w_l, ys)


def _pos_embed_2d(rows, dim):
    r = jnp.repeat(jnp.arange(rows, dtype=F32), GRID_W)
    col = jnp.tile(jnp.arange(GRID_W, dtype=F32), rows)
    quarter = dim // 4
    omega = 1.0 / (POS_BASE ** (jnp.arange(quarter, dtype=F32) / quarter))

    def emb(pos):
        ang = pos[:, None] * omega[None, :]
        return jnp.concatenate([jnp.sin(ang), jnp.cos(ang)], axis=-1)

    return jnp.concatenate([emb(r), emb(col)], axis=-1)


def _routing_tables(ri, counts):
    i32 = jnp.int32
    slot = ri[:, TOP_K:2 * TOP_K].reshape(N_TILES, TM, TOP_K)
    cnt = counts.reshape(N_TILES, N_EXPERTS)
    seg = (cnt + SEG_ALIGN - 1) // SEG_ALIGN * SEG_ALIGN
    lstart = jnp.cumsum(seg, axis=1) - seg
    region = jnp.sum(seg, axis=0)
    padded = (region + MOE_BM - 1) // MOE_BM * MOE_BM
    pend = jnp.cumsum(padded)
    pstart = pend - padded
    gstart = pstart[None, :] + jnp.cumsum(seg, axis=0) - seg
    slot_t = slot.transpose(0, 2, 1).astype(i32)
    blk_start = jnp.arange(MOE_BLOCKS, dtype=i32) * MOE_BM
    n_used = (pend[-1] // MOE_BM).astype(i32).reshape(1)
    block_e = jnp.minimum(jnp.sum(pend[None, :] <= blk_start[:, None], axis=1), N_EXPERTS - 1)
    last_e = block_e[jnp.maximum(n_used[0] - 1, 0)]
    block_e = jnp.where(blk_start < pend[-1], block_e, last_e).astype(i32)
    zpos = jnp.where(region > 0, pend - MOE_BM, -1).astype(i32)
    tab = lambda t: t.reshape(N_TILES, 1, N_EXPERTS).astype(i32)
    return (slot.reshape(N_TOK, TOP_K).astype(i32), slot_t, tab(lstart), tab(gstart),
            tab(seg // SEG_ALIGN), block_e, n_used, zpos)


def kernel(x_prompt, x_sample, state_s5_re, state_s5_im, state_hgrn, c, c_ctx, ada_w, ada_b, norm_w, w_in,
           s5_a_re, s5_a_im, s5_log_dt, s5_b_re, s5_b_im, s5_c_re, s5_c_im, s5_d, s5_w_glu, s5_b_glu,
           hgrn_lb, hgrn_norm_w, w_out, router_w, router_b, exp_w1, exp_b1, exp_w2, exp_b2):
    pos = _pos_embed_2d(DEC_SEQ // GRID_W, D_MODEL)
    x = jnp.concatenate([x_prompt.reshape(N_PROMPT, D_MODEL),
                         (x_sample + pos[None]).reshape(N_SAMPLE, D_MODEL)], axis=0)
    cond = jnp.concatenate([c_ctx[None, :], c, jnp.zeros((N_COND - 1 - DEC_BATCH, D_MODEL), F32)], axis=0)
    mod = _modulation(cond, ada_w, ada_b).reshape(DEPTH, N_COND, 1, 6 * D_MODEL)
    tiles = jnp.arange(N_TILES, dtype=jnp.int32)
    tile_cond = jnp.where(tiles < N_PROMPT // TM, 0, 1 + (tiles - N_PROMPT // TM) // (DEC_SEQ // TM)).astype(jnp.int32)
    w_in_b = w_in.astype(BF16)
    w_out_b = w_out.astype(BF16)
    w_glu_b = s5_w_glu.astype(BF16)

    s5_w = _s5_weights(s5_a_re, s5_a_im, s5_log_dt, s5_b_re, s5_b_im, s5_c_re, s5_c_im)
    fin_re, fin_im, fin_hg = [], [], []
    n = S5_STATE
    for l in range(DEPTH):
        u, qs, v, gs, lf_f, k_f, lf_b, k_b = _in_proj(l, tile_cond, x, mod[l], norm_w[l], w_in_b[l], hgrn_lb)

        h0 = jnp.concatenate([
            state_s5_re[:, l].transpose(2, 0, 1, 3).reshape(S5_GROUPS, DEC_BATCH, 2 * n),
            state_s5_im[:, l].transpose(2, 0, 1, 3).reshape(S5_GROUPS, DEC_BATCH, 2 * n)], axis=-1)
        h0 = h0.reshape(S5_NLT, S5_GPT, S5_NRB - 1, S5_SEQ_PER_RB, S5_NS).transpose(0, 2, 1, 3, 4)
        ycore, hfin = _s5_scan(l, u, *s5_w, h0)
        hfin = hfin.reshape(S5_GROUPS, BATCH, S5_NS)
        fin_re.append(hfin[:, :, 0:2 * n].reshape(S5_GROUPS, BATCH, 2, n).transpose(1, 2, 0, 3))
        fin_im.append(hfin[:, :, 2 * n:].reshape(S5_GROUPS, BATCH, 2, n).transpose(1, 2, 0, 3))

        s0 = jnp.concatenate([jnp.zeros((BATCH, 2, HGRN_HEADS, HGRN_DV, HGRN_DK), F32),
                              jnp.swapaxes(state_hgrn[:, l], -1, -2)], axis=0)
        o_f, o_b, sfin = _hgrn_scan(qs, v, lf_f, k_f, lf_b, k_b, s0)
        fin_hg.append(jnp.swapaxes(sfin[:BATCH], -1, -2))

        x1, h2, ri, gate, counts = _out_proj(
            tile_cond, x, ycore, u, o_f, o_b, gs, mod[l], norm_w[l], s5_d[l].reshape(1, S5_WIDTH),
            w_glu_b[l], s5_b_glu[l].reshape(1, S5_WIDTH), hgrn_norm_w[l].reshape(1, HGRN_DV),
            w_out_b[l], router_w[l], router_b[l].reshape(1, N_EXPERTS))

        slot, slot_t, lstart, gstart, ngrp, block_e, n_used, zpos = _routing_tables(ri, counts)
        xs = _dispatch(zpos, slot_t, lstart, gstart, ngrp, h2)
        ys = _moe_experts(l, block_e, n_used, xs, exp_w1, exp_b1, exp_w2, exp_b2)
        x = _combine(tile_cond, lstart, gstart, ngrp, x1, slot, gate, mod[l], norm_w[l], ys)

    y_prompt = x[:N_PROMPT].reshape(BATCH, SEQ, D_MODEL)
    y_sample = x[N_PROMPT:].reshape(DEC_BATCH, DEC_SEQ, D_MODEL)
    return (y_prompt, y_sample, jnp.stack(fin_re, axis=1), jnp.stack(fin_im, axis=1),
            jnp.stack(fin_hg, axis=1))
```

```python
import functools
import math

import jax
import jax.numpy as jnp
from jax import lax
from jax.experimental import pallas as pl
from jax.experimental.pallas import tpu as pltpu

F32 = jnp.float32
BF16 = jnp.bfloat16

D_MODEL = 1024
BATCH = 16
SEQ = 256
DEPTH = 2
DEC_BATCH = 4
DEC_SEQ = 2048
GRID_W = 64
S5_WIDTH = 512
S5_GROUP_CH = 16
S5_GROUPS = 32
S5_STATE = 64
HGRN_WIDTH = 512
HGRN_DK = 128
HGRN_HEADS = 4
HGRN_DV = 128
IN_COLS = S5_WIDTH + 5 * HGRN_WIDTH
N_EXPERTS = 32
TOP_K = 4
D_FF = D_MODEL
SWIGLU_LIMIT = 7.0
SWIGLU_ALPHA = 1.702
NORM_EPS = 1e-6
POS_BASE = 10000.0

N_PROMPT = BATCH * SEQ
N_SAMPLE = DEC_BATCH * DEC_SEQ
N_TOK = N_PROMPT + N_SAMPLE
N_SEQ = BATCH + DEC_BATCH
N_COND = 8

TM = 256
N_TILES = N_TOK // TM
S5_T = 16
S5_LT = 128
S5_GPT = S5_LT // S5_GROUP_CH
S5_NLT = S5_WIDTH // S5_LT
S5_RB = N_PROMPT
S5_CR = S5_RB // S5_T
S5_NRB = N_TOK // S5_RB
S5_SEQ_PER_RB = S5_RB // DEC_SEQ
S5_NLAG = 2 * S5_T - 1
S5_NS = 4 * S5_STATE
HC = 128
N_HCHUNK = N_TOK // HC
EXP_CLAMP = 80.0
MOE_BM = 256
N_ASSIGN = N_TOK * TOP_K
SEG_ALIGN = 8
TILE_SLOTS = -(-(TM * TOP_K + N_EXPERTS * (SEG_ALIGN - 1)) // 128) * 128
MOE_BLOCKS = -(-(N_ASSIGN + N_TILES * N_EXPERTS * (SEG_ALIGN - 1) + N_EXPERTS * (MOE_BM - 1)) // MOE_BM)
MOE_ROWS = MOE_BLOCKS * MOE_BM
VMEM_LIMIT = 56 * 1024 * 1024


def _rms(x, w):
    return x * lax.rsqrt(jnp.mean(x * x, axis=-1, keepdims=True) + NORM_EPS) * w


def _silu(x):
    return x * jax.nn.sigmoid(x)


MOD_TN = 1536


def _mod_kernel(cond_ref, w_ref, b_ref, o_ref):
    s = _silu(cond_ref[...]).astype(BF16)
    o_ref[0] = jnp.dot(s, w_ref[0].astype(BF16), preferred_element_type=F32) + b_ref[0]


def _modulation(cond, ada_w, ada_b):
    return pl.pallas_call(
        _mod_kernel,
        grid=(DEPTH, 6 * D_MODEL // MOD_TN),
        in_specs=[
            pl.BlockSpec((N_COND, D_MODEL), lambda l, j: (0, 0)),
            pl.BlockSpec((1, D_MODEL, MOD_TN), lambda l, j: (l, 0, j)),
            pl.BlockSpec((1, 1, MOD_TN), lambda l, j: (l, 0, j)),
        ],
        out_specs=pl.BlockSpec((1, N_COND, MOD_TN), lambda l, j: (l, 0, j)),
        out_shape=jax.ShapeDtypeStruct((DEPTH, N_COND, 6 * D_MODEL), F32),
        compiler_params=pltpu.CompilerParams(vmem_limit_bytes=VMEM_LIMIT),
        name="adaln_mod",
    )(cond, ada_w, ada_b.reshape(DEPTH, 1, 6 * D_MODEL))


def _in_kernel(layer, cond_ref, x_ref, mod_ref, nw_ref, w_ref, lb_ref,
               u_ref, q_ref, v_ref, g_ref, lff_ref, kf_ref, lfb_ref, kb_ref):
    del cond_ref
    mod = mod_ref[0]
    sh1 = mod[:, 0:D_MODEL]
    sc1 = mod[:, D_MODEL:2 * D_MODEL]
    h = _rms(x_ref[...], nw_ref[0:1, :]) * (1.0 + sc1) + sh1
    z = jnp.dot(h.astype(BF16), w_ref[...], preferred_element_type=F32)
    w = HGRN_WIDTH
    c0 = S5_WIDTH
    u_ref[...] = z[:, 0:c0]
    q_ref[...] = _silu(z[:, c0:c0 + w]) * (HGRN_DK ** -0.5)
    v_ref[...] = z[:, c0 + 3 * w:c0 + 4 * w]
    g_ref[...] = _silu(z[:, c0 + 4 * w:c0 + 5 * w])
    lbp = lb_ref[...]
    e = jnp.exp(lbp - jnp.max(lbp, axis=0, keepdims=True))
    probs = e / jnp.sum(e, axis=0, keepdims=True)
    lb = jnp.sum(probs[0:layer + 1], axis=0) - probs[0]
    for d, (lf_ref, k_ref) in enumerate(((lff_ref, kf_ref), (lfb_ref, kb_ref))):
        f = z[:, c0 + (1 + d) * w:c0 + (2 + d) * w]
        lbd = lb[d:d + 1, :]
        fg = lbd + (1.0 - lbd) * jax.nn.sigmoid(f)
        lf_ref[...] = jnp.log(fg)
        k_ref[...] = 1.0 - fg


def _in_proj(layer, tile_cond, x, mod_l, norm_w_l, w_in_l, hgrn_lb):
    tok = lambda i, c: (i, 0)
    out = jax.ShapeDtypeStruct((N_TOK, HGRN_WIDTH), F32)
    return pl.pallas_call(
        functools.partial(_in_kernel, layer),
        grid_spec=pltpu.PrefetchScalarGridSpec(
            num_scalar_prefetch=1,
            grid=(N_TILES,),
            in_specs=[
                pl.BlockSpec((TM, D_MODEL), tok),
                pl.BlockSpec((1, 1, 6 * D_MODEL), lambda i, c: (c[i], 0, 0)),
                pl.BlockSpec((4, D_MODEL), lambda i, c: (0, 0)),
                pl.BlockSpec((D_MODEL, IN_COLS), lambda i, c: (0, 0)),
                pl.BlockSpec((DEPTH, 2, HGRN_WIDTH), lambda i, c: (0, 0, 0)),
            ],
            out_specs=[pl.BlockSpec((TM, HGRN_WIDTH), tok)] * 8,
        ),
        out_shape=[out] * 8,
        compiler_params=pltpu.CompilerParams(
            dimension_semantics=("arbitrary",), vmem_limit_bytes=VMEM_LIMIT),
        name="in_proj",
    )(tile_cond, x, mod_l, norm_w_l, w_in_l, hgrn_lb)


def _s5_layer_weights(a_re, a_im, log_dt, b_re, b_im, c_re, c_im):
    hp = lax.Precision.HIGHEST
    t = S5_T
    dt = jnp.exp(log_dt)[..., None]
    lam_re = jnp.minimum(a_re, -1e-4)
    lam_im = a_im
    mag = jnp.exp(dt * lam_re)
    ang = dt * lam_im
    ab_re = mag * jnp.cos(ang)
    ab_im = mag * jnp.sin(ang)
    den = lam_re * lam_re + lam_im * lam_im
    nr = ab_re - 1.0
    ni = ab_im
    co_re = (nr * lam_re + ni * lam_im) / den
    co_im = (ni * lam_re - nr * lam_im) / den
    bb_re = co_re[..., None] * b_re - co_im[..., None] * b_im
    bb_im = co_re[..., None] * b_im + co_im[..., None] * b_re
    pr = [jnp.ones_like(ab_re)]
    pi = [jnp.zeros_like(ab_im)]
    for _ in range(t):
        pr.append(pr[-1] * ab_re - pi[-1] * ab_im)
        pi.append(pr[-2] * ab_im + pi[-1] * ab_re)
    pr = jnp.stack(pr, axis=1)
    pi = jnp.stack(pi, axis=1)
    abr = pr[..., None] * bb_re[:, None] - pi[..., None] * bb_im[:, None]
    abi = pr[..., None] * bb_im[:, None] + pi[..., None] * bb_re[:, None]
    kk = jnp.einsum('dgon,dkgni->dkgio', jnp.concatenate([c_re, -c_im], axis=-1),
                    jnp.concatenate([abr, abi], axis=-2), precision=hp)
    kern = (kk[0], kk[1])
    abs_ = ((abr[0], abi[0]), (abr[1], abi[1]))
    pows = ((pr[0], pi[0]), (pr[1], pi[1]))
    p = S5_GROUP_CH
    lags = jnp.concatenate([kern[1][t - 1:0:-1], (kern[0][0] + kern[1][0])[None], kern[0][1:t]], axis=0)
    m = lags.reshape(S5_NLAG, S5_NLT, S5_GPT, p, p)
    bd = jnp.einsum('ltaio,ab->ltaibo', m, jnp.eye(S5_GPT, dtype=F32))
    wcat = bd.reshape(S5_NLAG, S5_NLT, S5_LT, S5_LT).transpose(1, 2, 0, 3).reshape(S5_NLT, S5_LT, S5_NLAG * S5_LT)
    (abr_f, abi_f), (abr_b, abi_b) = abs_
    parts = (abr_f[t - 1::-1], abr_b[:t], abi_f[t - 1::-1], abi_b[:t])
    wst = jnp.concatenate([x.transpose(1, 0, 3, 2) for x in parts], axis=-1)
    wst = wst.reshape(S5_NLT, S5_GPT, t, p, S5_NS).transpose(0, 2, 1, 3, 4).reshape(S5_NLT, t * S5_LT, S5_NS)
    (pr_f, pi_f), (pr_b, pi_b) = pows

    def out_rows(cr, ci, pr_, pi_):
        re_rows = cr[None] * pr_[:, :, None, :] - ci[None] * pi_[:, :, None, :]
        im_rows = -(cr[None] * pi_[:, :, None, :] + ci[None] * pr_[:, :, None, :])
        return re_rows.transpose(1, 3, 0, 2), im_rows.transpose(1, 3, 0, 2)

    fre, fim = out_rows(c_re[0], c_im[0], pr_f[1:t + 1], pi_f[1:t + 1])
    bre, bim = out_rows(c_re[1], c_im[1], pr_b[t:0:-1], pi_b[t:0:-1])
    wout = lax.optimization_barrier(jnp.concatenate([fre, bre, fim, bim], axis=1))
    wdense = (wout.reshape(S5_NLT, S5_GPT, S5_NS, t, p).transpose(0, 2, 3, 1, 4)
              .reshape(S5_NLT, S5_NS, t * S5_LT))
    at = jnp.stack([jnp.concatenate([pr_f[t], pr_b[t]], axis=-1),
                    jnp.concatenate([pi_f[t], pi_b[t]], axis=-1)], axis=1)
    return (wcat.astype(BF16), wst.astype(BF16), wdense.astype(BF16),
            at.reshape(S5_NLT, S5_GPT, 2, 2 * S5_STATE))


def _s5_weights(*params):
    return jax.vmap(_s5_layer_weights)(*params)


def _s5_kernel(u_ref, wcat_ref, wst_ref, wd_ref, at_ref, h0_ref, y_ref, hfin_ref,
               wbig, dh_scr, hf_scr, hb_scr):
    t = S5_T
    n2 = 2 * S5_STATE
    r = pl.program_id(1)

    @pl.when(r == 0)
    def _():
        for s in range(t):
            wbig[s * S5_LT:(s + 1) * S5_LT, :] = wcat_ref[0, :, (t - 1 - s) * S5_LT:(2 * t - 1 - s) * S5_LT]

    xcat = jnp.concatenate([u_ref[pl.ds(s, S5_CR, stride=t), :].astype(BF16) for s in range(t)], axis=-1)
    yacc = jnp.dot(xcat, wbig[...], preferred_element_type=F32)
    lane_k = lax.broadcasted_iota(jnp.int32, (1, t * S5_LT), 1)
    grp_k = (lane_k % S5_LT) // S5_GROUP_CH
    for gi in range(S5_GPT):
        xg = jnp.where(grp_k == gi, xcat, jnp.zeros_like(xcat))
        dh = jnp.dot(xg, wst_ref[0], preferred_element_type=F32)
        dh_scr[gi, 0] = dh[:, 0:n2]
        dh_scr[gi, 1] = dh[:, n2:]

    lane = lax.broadcasted_iota(jnp.int32, (1, n2), 1)
    fwd_lane = lane < S5_STATE
    are = [at_ref[0, gi, 0:1, :] for gi in range(S5_GPT)]
    aim = [at_ref[0, gi, 1:2, :] for gi in range(S5_GPT)]

    def advance(gi, hre, him, dre, dim):
        return (are[gi] * hre - aim[gi] * him + dre, are[gi] * him + aim[gi] * hre + dim)

    @pl.when(r == 0)
    def _():
        nc = SEQ // t
        for gi in range(S5_GPT):
            hre = jnp.zeros((BATCH, n2), F32)
            him = jnp.zeros((BATCH, n2), F32)
            for s in range(nc):
                rows_f = pl.ds(s, BATCH, stride=nc)
                rows_b = pl.ds(nc - 1 - s, BATCH, stride=nc)
                hf_scr[gi, 0, rows_f, :] = hre
                hf_scr[gi, 1, rows_f, :] = him
                hb_scr[gi, 0, rows_b, :] = hre
                hb_scr[gi, 1, rows_b, :] = him
                dre = jnp.where(fwd_lane, dh_scr[gi, 0, rows_f, :], dh_scr[gi, 0, rows_b, :])
                dim = jnp.where(fwd_lane, dh_scr[gi, 1, rows_f, :], dh_scr[gi, 1, rows_b, :])
                hre, him = advance(gi, hre, him, dre, dim)
            hfin_ref[0, gi] = jnp.concatenate([hre, him], axis=-1)

    @pl.when(r > 0)
    def _():
        nc = DEC_SEQ // t
        nb = S5_SEQ_PER_RB
        init = tuple((h0_ref[0, 0, gi, b:b + 1, 0:n2], h0_ref[0, 0, gi, b:b + 1, n2:])
                     for gi in range(S5_GPT) for b in range(nb))

        def step(o, carry):
            new = []
            for gi in range(S5_GPT):
                for b in range(nb):
                    hre, him = carry[gi * nb + b]
                    rf = pl.multiple_of(b * nc + o * 8, 8)
                    rb = pl.multiple_of(b * nc + nc - 8 - o * 8, 8)
                    dfr = dh_scr[gi, 0, pl.ds(rf, 8), :]
                    dfi = dh_scr[gi, 1, pl.ds(rf, 8), :]
                    dbr = dh_scr[gi, 0, pl.ds(rb, 8), :]
                    dbi = dh_scr[gi, 1, pl.ds(rb, 8), :]
                    ent_re, ent_im = [], []
                    for i in range(8):
                        ent_re.append(hre)
                        ent_im.append(him)
                        dre = jnp.where(fwd_lane, dfr[i:i + 1], dbr[7 - i:8 - i])
                        dim = jnp.where(fwd_lane, dfi[i:i + 1], dbi[7 - i:8 - i])
                        hre, him = advance(gi, hre, him, dre, dim)
                    hf_scr[gi, 0, pl.ds(rf, 8), :] = jnp.concatenate(ent_re, axis=0)
                    hf_scr[gi, 1, pl.ds(rf, 8), :] = jnp.concatenate(ent_im, axis=0)
                    hb_scr[gi, 0, pl.ds(rb, 8), :] = jnp.concatenate(ent_re[::-1], axis=0)
                    hb_scr[gi, 1, pl.ds(rb, 8), :] = jnp.concatenate(ent_im[::-1], axis=0)
                    new.append((hre, him))
            return tuple(new)

        lax.fori_loop(0, nc // 8, step, init)

    for gi in range(S5_GPT):
        hent = jnp.concatenate([jnp.where(fwd_lane, hf_scr[gi, 0], hb_scr[gi, 0]),
                                jnp.where(fwd_lane, hf_scr[gi, 1], hb_scr[gi, 1])], axis=-1).astype(BF16)
        yi = jnp.dot(hent, wd_ref[0], preferred_element_type=F32)
        yacc = yacc + jnp.where(grp_k == gi, yi, 0.0)
    for s in range(t):
        y_ref[pl.ds(s, S5_CR, stride=t), :] = yacc[:, s * S5_LT:(s + 1) * S5_LT]


def _s5_scan(layer, u, wcat, wst, wdense, at, h0):
    tile = lambda j, r: (layer * S5_NLT + j, 0, 0)
    merge = lambda w: w.reshape((DEPTH * S5_NLT,) + w.shape[2:])
    wcat, wst, wdense, at = merge(wcat), merge(wst), merge(wdense), merge(at)
    return pl.pallas_call(
        _s5_kernel,
        grid=(S5_NLT, S5_NRB),
        in_specs=[
            pl.BlockSpec((S5_RB, S5_LT), lambda j, r: (r, j)),
            pl.BlockSpec((1, S5_LT, S5_NLAG * S5_LT), tile),
            pl.BlockSpec((1, S5_T * S5_LT, S5_NS), tile),
            pl.BlockSpec((1, S5_NS, S5_T * S5_LT), tile),
            pl.BlockSpec((1, S5_GPT, 2, 2 * S5_STATE), lambda j, r: (layer * S5_NLT + j, 0, 0, 0)),
            pl.BlockSpec((1, 1, S5_GPT, S5_SEQ_PER_RB, S5_NS), lambda j, r: (j, jnp.maximum(r - 1, 0), 0, 0, 0)),
        ],
        out_specs=[
            pl.BlockSpec((S5_RB, S5_LT), lambda j, r: (r, j)),
            pl.BlockSpec((1, S5_GPT, BATCH, S5_NS), lambda j, r: (j, 0, 0, 0)),
        ],
        out_shape=[
            jax.ShapeDtypeStruct((N_TOK, S5_WIDTH), F32),
            jax.ShapeDtypeStruct((S5_NLT, S5_GPT, BATCH, S5_NS), F32),
        ],
        scratch_shapes=[pltpu.VMEM((S5_T * S5_LT, S5_T * S5_LT), BF16)]
                       + [pltpu.VMEM((S5_GPT, 2, S5_CR, 2 * S5_STATE), F32)] * 3,
        compiler_params=pltpu.CompilerParams(
            dimension_semantics=("arbitrary", "arbitrary"), vmem_limit_bytes=VMEM_LIMIT),
        name="s5_scan",
    )(u, wcat, wst, wdense, at, h0)


def _split3(x):
    hi = x.astype(BF16)
    r1 = x - hi.astype(F32)
    mid = r1.astype(BF16)
    lo = (r1 - mid.astype(F32)).astype(BF16)
    return hi, mid, lo


def _piecewise_rows(b, blk, row_in_blk):
    parts = []
    for j in range(HC // blk):
        r = j * blk + row_in_blk
        parts.append(jnp.broadcast_to(b[r:r + 1, :], (blk, b.shape[1])))
    return parts[0] if len(parts) == 1 else jnp.concatenate(parts, axis=0)


def _nt(a, b):
    return lax.dot_general(a, b, (((1,), (1,)), ((), ())), preferred_element_type=F32)


def _tn(a, b):
    return lax.dot_general(a, b, (((0,), (0,)), ((), ())), preferred_element_type=F32)


def _hgrn_dir(reverse, q_ref, v_ref, lf_ref, k_ref, o_ref, st_ref):
    row = lax.broadcasted_iota(jnp.int32, (HC, HC), 0)
    col = lax.broadcasted_iota(jnp.int32, (HC, HC), 1)
    causal = (col >= row) if reverse else (col <= row)
    tri = jnp.where(causal, 1.0, 0.0).astype(BF16)
    lf = lf_ref[...]
    hi, mid, lo = _split3(lf)
    ball = (jnp.dot(tri, hi, preferred_element_type=F32) + jnp.dot(tri, mid, preferred_element_type=F32)
            + jnp.dot(tri, lo, preferred_element_type=F32))
    last = 0 if reverse else HC - 1
    masks = []
    for blk in (128, 64, 32):
        half = blk // 2
        same = (row // blk) == (col // blk)
        t_hi = (row % blk) >= half
        s_hi = (col % blk) >= half
        if reverse:
            masks.append(same & jnp.logical_not(t_hi) & s_hi)
        else:
            masks.append(same & t_hi & jnp.logical_not(s_hi))
    diag_mask = ((row // 16) == (col // 16)) & causal
    for h in range(HGRN_HEADS):
        sl = slice(h * HGRN_DK, (h + 1) * HGRN_DK)
        b = ball[:, sl]
        q = q_ref[:, sl]
        k = k_ref[:, sl]
        v = v_ref[:, sl].astype(BF16)
        st = st_ref[h]
        b_last = b[last:last + 1, :]
        q_in = (q * jnp.exp(b)).astype(BF16)
        k_in = (k * jnp.exp(b_last - b)).astype(BF16)
        o = _nt(q_in, st.astype(BF16))
        st_ref[h] = jnp.exp(b_last) * st + _tn(v, k_in)
        scores = jnp.zeros((HC, HC), F32)
        for blk, mask in zip((128, 64, 32), masks):
            half = blk // 2
            m = _piecewise_rows(b, blk, half if reverse else half - 1)
            qj = (q * jnp.exp(jnp.minimum(b - m, 0.0))).astype(BF16)
            kj = (k * jnp.exp(jnp.minimum(m - b, 0.0))).astype(BF16)
            scores = scores + jnp.where(mask, _nt(qj, kj), 0.0)
        m = _piecewise_rows(b, 16, 8 if reverse else 7)
        qd = (q * jnp.exp(jnp.minimum(b - m, EXP_CLAMP))).astype(BF16)
        kd = (k * jnp.exp(jnp.minimum(m - b, EXP_CLAMP))).astype(BF16)
        scores = scores + jnp.where(diag_mask, _nt(qd, kd), 0.0)
        o_ref[:, sl] = o + jnp.dot(scores.astype(BF16), v, preferred_element_type=F32)


def _hgrn_kernel(cf_ref, cb_ref, seq_ref, first_ref, last_ref,
                 qf_ref, vf_ref, lff_ref, kf_ref, qb_ref, vb_ref, lfb_ref, kb_ref, s0_ref,
                 of_ref, ob_ref, sfin_ref, st_scr):
    del cf_ref, cb_ref, seq_ref
    j = pl.program_id(0)

    @pl.when(first_ref[j] == 1)
    def _():
        st_scr[...] = s0_ref[0]

    _hgrn_dir(False, qf_ref, vf_ref, lff_ref, kf_ref, of_ref, st_scr.at[0])
    _hgrn_dir(True, qb_ref, vb_ref, lfb_ref, kb_ref, ob_ref, st_scr.at[1])

    @pl.when(last_ref[j] == 1)
    def _():
        sfin_ref[0] = st_scr[...]


def _hgrn_tables():
    cf, cb, sq, first, last = [], [], [], [], []
    base = 0
    for s in range(N_SEQ):
        nc = (SEQ if s < BATCH else DEC_SEQ) // HC
        for t in range(nc):
            cf.append(base + t)
            cb.append(base + nc - 1 - t)
            sq.append(s)
            first.append(int(t == 0))
            last.append(int(t == nc - 1))
        base += nc
    return tuple(jnp.asarray(x, jnp.int32) for x in (cf, cb, sq, first, last))


def _hgrn_scan(qs, v, lf_f, k_f, lf_b, k_b, s0):
    fwd = lambda j, cf, cb, sq, fi, la: (cf[j], 0)
    bwd = lambda j, cf, cb, sq, fi, la: (cb[j], 0)
    seq = lambda j, cf, cb, sq, fi, la: (sq[j], 0, 0, 0, 0)
    tile = (HC, HGRN_WIDTH)
    sblk = (1, 2, HGRN_HEADS, HGRN_DV, HGRN_DK)
    out = jax.ShapeDtypeStruct((N_TOK, HGRN_WIDTH), F32)
    return pl.pallas_call(
        _hgrn_kernel,
        grid_spec=pltpu.PrefetchScalarGridSpec(
            num_scalar_prefetch=5,
            grid=(N_HCHUNK,),
            in_specs=[pl.BlockSpec(tile, fwd)] * 4 + [pl.BlockSpec(tile, bwd)] * 4
                     + [pl.BlockSpec(sblk, seq)],
            out_specs=[pl.BlockSpec(tile, fwd), pl.BlockSpec(tile, bwd), pl.BlockSpec(sblk, seq)],
            scratch_shapes=[pltpu.VMEM(sblk[1:], F32)],
        ),
        out_shape=[out, out, jax.ShapeDtypeStruct((N_SEQ,) + sblk[1:], F32)],
        compiler_params=pltpu.CompilerParams(
            dimension_semantics=("arbitrary",), vmem_limit_bytes=VMEM_LIMIT),
        name="hgrn_scan",
    )(*_hgrn_tables(), qs, v, lf_f, k_f, qs, v, lf_b, k_b, s0)


def _gelu_tanh(x):
    return 0.5 * x * (1.0 + jnp.tanh(math.sqrt(2.0 / math.pi) * (x + 0.044715 * (x * x * x))))


def _out_kernel(cond_ref, x_ref, yc_ref, u_ref, of_ref, ob_ref, g_ref, mod_ref, nw_ref, d_ref,
                wglu_ref, bglu_ref, hnw_ref, wout_ref, rw_ref, rb_ref,
                x1_ref, h2_ref, ri_ref, rg_ref, cnt_ref):
    del cond_ref
    mod = mod_ref[0]
    g1 = mod[:, 2 * D_MODEL:3 * D_MODEL]
    sh2 = mod[:, 3 * D_MODEL:4 * D_MODEL]
    sc2 = mod[:, 4 * D_MODEL:5 * D_MODEL]
    y = _gelu_tanh(yc_ref[...] + d_ref[...] * u_ref[...])
    y_s5 = y * jax.nn.sigmoid(jnp.dot(y.astype(BF16), wglu_ref[...], preferred_element_type=F32) + bglu_ref[...])
    o = of_ref[...] + ob_ref[...]
    gs = g_ref[...]
    heads = []
    for h in range(HGRN_HEADS):
        sl = slice(h * HGRN_DV, (h + 1) * HGRN_DV)
        heads.append(_rms(o[:, sl], hnw_ref[...]) * gs[:, sl])
    y_hg = jnp.concatenate(heads, axis=-1)
    mix = (jnp.dot(y_s5.astype(BF16), wout_ref[0:S5_WIDTH, :], preferred_element_type=F32)
           + jnp.dot(y_hg.astype(BF16), wout_ref[S5_WIDTH:, :], preferred_element_type=F32))
    x1 = x_ref[...] + g1 * _rms(mix, nw_ref[1:2, :])
    x1_ref[...] = x1
    h2 = _rms(x1, nw_ref[2:3, :]) * (1.0 + sc2) + sh2
    h2_hi = h2.astype(BF16)
    h2_ref[...] = h2_hi
    h2_lo = (h2 - h2_hi.astype(F32)).astype(BF16)
    logits = rb_ref[...]
    for a in (h2_hi, h2_lo):
        for part in range(2):
            logits = logits + jnp.dot(a, rw_ref[part], preferred_element_type=F32)
    eidx = lax.broadcasted_iota(jnp.int32, (TM, N_EXPERTS), 1).astype(F32)
    vals = logits
    top_v, top_i, onehots = [], [], []
    for _ in range(TOP_K):
        mx = jnp.max(vals, axis=-1, keepdims=True)
        ix = jnp.min(jnp.where(vals == mx, eidx, float(N_EXPERTS)), axis=-1, keepdims=True)
        sel = eidx == ix
        top_v.append(mx)
        top_i.append(ix)
        onehots.append(sel)
        vals = jnp.where(sel, -jnp.inf, vals)
    ex = [jnp.exp(tv - top_v[0]) for tv in top_v]
    den = ex[0] + ex[1] + ex[2] + ex[3]
    tot = jnp.zeros((TM, N_EXPERTS), F32)
    for sel in onehots:
        tot = tot + jnp.where(sel, 1.0, 0.0)
    r_t = lax.broadcasted_iota(jnp.int32, (TM, TM), 0)
    r_s = lax.broadcasted_iota(jnp.int32, (TM, TM), 1)
    strict = jnp.where(r_s < r_t, 1.0, 0.0).astype(BF16)
    before = jnp.dot(strict, tot.astype(BF16), preferred_element_type=F32)
    cnt = jnp.sum(tot, axis=0, keepdims=True)
    seg = jnp.floor((cnt + (SEG_ALIGN - 1)) * (1.0 / SEG_ALIGN)) * SEG_ALIGN
    e_r = lax.broadcasted_iota(jnp.int32, (N_EXPERTS, N_EXPERTS), 0)
    e_c = lax.broadcasted_iota(jnp.int32, (N_EXPERTS, N_EXPERTS), 1)
    lstart = jnp.dot(seg, jnp.where(e_r < e_c, 1.0, 0.0), precision=lax.Precision.HIGHEST,
                     preferred_element_type=F32)
    before = before + lstart
    lane = lax.broadcasted_iota(jnp.int32, (TM, 128), 1)
    ri = jnp.zeros((TM, 128), F32)
    rg = jnp.zeros((TM, 128), F32)
    for kk in range(TOP_K):
        rank = jnp.sum(jnp.where(onehots[kk], before, 0.0), axis=-1, keepdims=True)
        ri = jnp.where(lane == kk, top_i[kk], ri)
        ri = jnp.where(lane == TOP_K + kk, rank, ri)
        rg = jnp.where(lane == kk, ex[kk] / den, rg)
    ri_ref[...] = ri.astype(jnp.int32)
    rg_ref[...] = rg
    cnt_ref[0] = cnt.astype(jnp.int32)


def _out_proj(tile_cond, x, ycore, u, o_f, o_b, gs, mod_l, norm_w_l, s5_d_l, wglu_l, bglu_l, hnw_l,
              wout_l, rw_l, rb_l):
    tok = lambda i, c: (i, 0)
    full2 = lambda i, c: (0, 0)
    half = pl.BlockSpec((TM, HGRN_WIDTH), tok)
    wide = pl.BlockSpec((TM, D_MODEL), tok)
    return pl.pallas_call(
        _out_kernel,
        grid_spec=pltpu.PrefetchScalarGridSpec(
            num_scalar_prefetch=1,
            grid=(N_TILES,),
            in_specs=[
                wide, half, half, half, half, half,
                pl.BlockSpec((1, 1, 6 * D_MODEL), lambda i, c: (c[i], 0, 0)),
                pl.BlockSpec((4, D_MODEL), full2),
                pl.BlockSpec((1, S5_WIDTH), full2),
                pl.BlockSpec((S5_WIDTH, S5_WIDTH), full2),
                pl.BlockSpec((1, S5_WIDTH), full2),
                pl.BlockSpec((1, HGRN_DV), full2),
                pl.BlockSpec((D_MODEL, D_MODEL), full2),
                pl.BlockSpec((2, D_MODEL, N_EXPERTS), lambda i, c: (0, 0, 0)),
                pl.BlockSpec((1, N_EXPERTS), full2),
            ],
            out_specs=[wide, wide, pl.BlockSpec((TM, 128), tok), pl.BlockSpec((TM, 128), tok),
                       pl.BlockSpec((1, 1, N_EXPERTS), lambda i, c: (i, 0, 0))],
        ),
        out_shape=[
            jax.ShapeDtypeStruct((N_TOK, D_MODEL), F32),
            jax.ShapeDtypeStruct((N_TOK, D_MODEL), BF16),
            jax.ShapeDtypeStruct((N_TOK, 128), jnp.int32),
            jax.ShapeDtypeStruct((N_TOK, 128), F32),
            jax.ShapeDtypeStruct((N_TILES, 1, N_EXPERTS), jnp.int32),
        ],
        compiler_params=pltpu.CompilerParams(
            dimension_semantics=("arbitrary",), vmem_limit_bytes=VMEM_LIMIT),
        name="out_proj_router",
    )(tile_cond, x, ycore, u, o_f, o_b, gs, mod_l, norm_w_l, s5_d_l, wglu_l, bglu_l, hnw_l,
      wout_l, rw_l, rb_l)


def _segment_copies(tile, lstart_ref, gstart_ref, ngrp_ref, local_ref, global_ref, sem, to_global, wait):
    def per_expert(e, c):
        ls = lstart_ref[tile * N_EXPERTS + e]
        gs = gstart_ref[tile * N_EXPERTS + e]

        def per_group(g, c2):
            lo = pl.multiple_of(ls + g * SEG_ALIGN, SEG_ALIGN)
            go = pl.multiple_of(gs + g * SEG_ALIGN, SEG_ALIGN)
            loc = local_ref.at[pl.ds(lo, SEG_ALIGN)]
            glo = global_ref.at[pl.ds(go, SEG_ALIGN)]
            cp = pltpu.make_async_copy(loc, glo, sem) if to_global else pltpu.make_async_copy(glo, loc, sem)
            if wait:
                cp.wait()
            else:
                cp.start()
            return c2

        return lax.fori_loop(0, ngrp_ref[tile * N_EXPERTS + e], per_group, c)

    lax.fori_loop(0, N_EXPERTS, per_expert, 0)


def _dispatch_kernel(zpos_ref, lstart_ref, gstart_ref, ngrp_ref, slot_ref, h_ref, xs_ref,
                     sbuf, zbuf, sems, zsem):
    i = pl.program_id(0)
    cur = i % 2

    @pl.when(i == 0)
    def _():
        zbuf[...] = jnp.zeros_like(zbuf)
        for wait in (False, True):
            for e in range(N_EXPERTS):
                @pl.when(zpos_ref[e] >= 0)
                def _():
                    z0 = pl.multiple_of(zpos_ref[e], MOE_BM)
                    cp = pltpu.make_async_copy(zbuf, xs_ref.at[pl.ds(z0, MOE_BM)], zsem)
                    if wait:
                        cp.wait()
                    else:
                        cp.start()

    srow = lax.broadcasted_iota(jnp.int32, (TILE_SLOTS, TM), 0)
    perm = jnp.zeros((TILE_SLOTS, TM), F32)
    for kk in range(TOP_K):
        perm = perm + jnp.where(srow == slot_ref[0, kk:kk + 1, :], 1.0, 0.0)
    sbuf[cur] = jnp.dot(perm.astype(BF16), h_ref[...], preferred_element_type=F32)

    def finish(tile, buf):
        _segment_copies(tile, lstart_ref, gstart_ref, ngrp_ref, sbuf.at[buf], xs_ref, sems.at[buf], True, True)

    @pl.when(i > 0)
    def _():
        finish(i - 1, 1 - cur)

    _segment_copies(i, lstart_ref, gstart_ref, ngrp_ref, sbuf.at[cur], xs_ref, sems.at[cur], True, False)

    @pl.when(i == N_TILES - 1)
    def _():
        finish(i, cur)


def _dispatch(zpos, slot_t, lstart, gstart, ngrp, h2):
    return pl.pallas_call(
        _dispatch_kernel,
        grid_spec=pltpu.PrefetchScalarGridSpec(
            num_scalar_prefetch=4,
            grid=(N_TILES,),
            in_specs=[pl.BlockSpec((1, TOP_K, TM), lambda i, *_: (i, 0, 0)),
                      pl.BlockSpec((TM, D_MODEL), lambda i, *_: (i, 0))],
            out_specs=pl.BlockSpec(memory_space=pl.ANY),
            scratch_shapes=[pltpu.VMEM((2, TILE_SLOTS, D_MODEL), F32),
                            pltpu.VMEM((MOE_BM, D_MODEL), F32),
                            pltpu.SemaphoreType.DMA((2,)), pltpu.SemaphoreType.DMA],
        ),
        out_shape=jax.ShapeDtypeStruct((MOE_ROWS, D_MODEL), F32),
        compiler_params=pltpu.CompilerParams(
            dimension_semantics=("arbitrary",), vmem_limit_bytes=VMEM_LIMIT),
        name="moe_dispatch",
    )(zpos, lstart, gstart, ngrp, slot_t, h2)


def _moe_kernel(layer, be_ref, nu_ref, first_ref, next_ref, par_ref, x_ref, b1_ref, b2_ref, w1_hbm, w2_hbm,
                o_ref, wf1, wf2, w1b, w2b, sems):
    i = pl.program_id(0)
    active = i < nu_ref[0]

    def weight_copies(e, buf):
        row = layer * N_EXPERTS + e
        return (pltpu.make_async_copy(w1_hbm.at[row], wf1.at[buf], sems.at[buf]),
                pltpu.make_async_copy(w2_hbm.at[row], wf2.at[buf], sems.at[buf]))

    @pl.when(active & (i == 0))
    def _():
        for cp in weight_copies(be_ref[0], 0):
            cp.start()

    @pl.when(active & (first_ref[i] == 1))
    def _():
        buf = par_ref[i]
        for cp in weight_copies(be_ref[i], buf):
            cp.wait()

        @pl.when(next_ref[i] >= 0)
        def _():
            for cp in weight_copies(next_ref[i], 1 - buf):
                cp.start()

        def cast(j, c):
            r = pl.multiple_of(j * 128, 128)
            w1b[pl.ds(r, 128), :] = wf1[buf, pl.ds(r, 128), :].astype(BF16)
            w2b[pl.ds(r, 128), :] = wf2[buf, pl.ds(r, 128), :].astype(BF16)
            return c

        lax.fori_loop(0, D_MODEL // 128, cast, 0)

    @pl.when(active)
    def _():
        h = jnp.dot(x_ref[...].astype(BF16), w1b[...], preferred_element_type=F32) + b1_ref[0]
        glu = jnp.minimum(h[:, :D_FF], SWIGLU_LIMIT)
        lin = jnp.clip(h[:, D_FF:], -SWIGLU_LIMIT, SWIGLU_LIMIT)
        act = glu * jax.nn.sigmoid(SWIGLU_ALPHA * glu) * (lin + 1.0)
        o_ref[...] = jnp.dot(act.astype(BF16), w2b[...], preferred_element_type=F32) + b2_ref[0]


def _moe_experts(layer, block_e, n_used, first, next_e, parity, xs, w1, b1, w2, b2):
    blk = lambda i, be, nu, *_: (jnp.maximum(jnp.minimum(i, nu[0] - 1), 0), 0)
    exp3 = lambda i, be, *_: (layer * N_EXPERTS + be[i], 0, 0)
    w1 = w1.reshape(DEPTH * N_EXPERTS, D_MODEL, 2 * D_FF)
    w2 = w2.reshape(DEPTH * N_EXPERTS, D_FF, D_MODEL)
    return pl.pallas_call(
        functools.partial(_moe_kernel, layer),
        grid_spec=pltpu.PrefetchScalarGridSpec(
            num_scalar_prefetch=5,
            grid=(MOE_BLOCKS,),
            in_specs=[
                pl.BlockSpec((MOE_BM, D_MODEL), blk),
                pl.BlockSpec((1, 1, 2 * D_FF), exp3),
                pl.BlockSpec((1, 1, D_MODEL), exp3),
                pl.BlockSpec(memory_space=pl.ANY),
                pl.BlockSpec(memory_space=pl.ANY),
            ],
            out_specs=pl.BlockSpec((MOE_BM, D_MODEL), blk),
            scratch_shapes=[pltpu.VMEM((2, D_MODEL, 2 * D_FF), F32), pltpu.VMEM((2, D_FF, D_MODEL), F32),
                            pltpu.VMEM((D_MODEL, 2 * D_FF), BF16), pltpu.VMEM((D_FF, D_MODEL), BF16),
                            pltpu.SemaphoreType.DMA((2,))],
        ),
        out_shape=jax.ShapeDtypeStruct((MOE_ROWS, D_MODEL), F32),
        compiler_params=pltpu.CompilerParams(
            dimension_semantics=("arbitrary",), vmem_limit_bytes=VMEM_LIMIT),
        name="moe_experts",
    )(block_e, n_used, first, next_e, parity, xs, b1.reshape(DEPTH * N_EXPERTS, 1, 2 * D_FF),
      b2.reshape(DEPTH * N_EXPERTS, 1, D_MODEL), w1, w2)


def _combine_kernel(cond_ref, lstart_ref, gstart_ref, ngrp_ref, x1_ref, slot_ref, gate_ref, mod_ref, nw_ref,
                    ys_ref, o_ref, buf, sem):
    del cond_ref
    i = pl.program_id(0)
    cur = i % 2

    def fetch(tile, b, wait):
        _segment_copies(tile, lstart_ref, gstart_ref, ngrp_ref, buf.at[b], ys_ref, sem.at[b], False, wait)

    @pl.when(i == 0)
    def _():
        buf[...] = jnp.zeros_like(buf)
        fetch(0, 0, False)

    @pl.when(i + 1 < N_TILES)
    def _():
        fetch(i + 1, 1 - cur, False)

    fetch(i, cur, True)
    scol = lax.broadcasted_iota(jnp.int32, (TM, TILE_SLOTS), 1)
    slot = slot_ref[...]
    gate = gate_ref[...]
    gmat = jnp.zeros((TM, TILE_SLOTS), F32)
    for kk in range(TOP_K):
        gmat = gmat + jnp.where(scol == slot[:, kk:kk + 1], gate[:, kk:kk + 1], 0.0)
    gmat = gmat.astype(BF16)
    ffn = jnp.dot(gmat, buf[cur].astype(BF16), preferred_element_type=F32)
    g2 = mod_ref[0][:, 5 * D_MODEL:6 * D_MODEL]
    o_ref[...] = x1_ref[...] + g2 * _rms(ffn, nw_ref[3:4, :])


def _combine(tile_cond, lstart, gstart, ngrp, x1, slot, gate, mod_l, norm_w_l, ys):
    tok = lambda i, *_: (i, 0)
    return pl.pallas_call(
        _combine_kernel,
        grid_spec=pltpu.PrefetchScalarGridSpec(
            num_scalar_prefetch=4,
            grid=(N_TILES,),
            in_specs=[
                pl.BlockSpec((TM, D_MODEL), tok),
                pl.BlockSpec((TM, TOP_K), tok),
                pl.BlockSpec((TM, 128), tok),
                pl.BlockSpec((1, 1, 6 * D_MODEL), lambda i, c, *_: (c[i], 0, 0)),
                pl.BlockSpec((4, D_MODEL), lambda i, *_: (0, 0)),
                pl.BlockSpec(memory_space=pl.ANY),
            ],
            out_specs=pl.BlockSpec((TM, D_MODEL), tok),
            scratch_shapes=[pltpu.VMEM((2, TILE_SLOTS, D_MODEL), F32), pltpu.SemaphoreType.DMA((2,))],
        ),
        out_shape=jax.ShapeDtypeStruct((N_TOK, D_MODEL), F32),
        compiler_params=pltpu.CompilerParams(
            dimension_semantics=("arbitrary",), vmem_limit_bytes=VMEM_LIMIT),
        name="moe_combine",
    )(tile_cond, lstart, gstart, ngrp, x1, slot, gate, mod_l, norm_w_l, ys)


def _pos_embed_2d(rows, dim):
    r = jnp.repeat(jnp.arange(rows, dtype=F32), GRID_W)
    col = jnp.tile(jnp.arange(GRID_W, dtype=F32), rows)
    quarter = dim // 4
    omega = 1.0 / (POS_BASE ** (jnp.arange(quarter, dtype=F32) / quarter))

    def emb(pos):
        ang = pos[:, None] * omega[None, :]
        return jnp.concatenate([jnp.sin(ang), jnp.cos(ang)], axis=-1)

    return jnp.concatenate([emb(r), emb(col)], axis=-1)


def _routing_tables(ri, counts):
    i32 = jnp.int32
    slot = ri[:, TOP_K:2 * TOP_K].reshape(N_TILES, TM, TOP_K)
    cnt = counts.reshape(N_TILES, N_EXPERTS)
    seg = (cnt + SEG_ALIGN - 1) // SEG_ALIGN * SEG_ALIGN
    lstart = jnp.cumsum(seg, axis=1) - seg
    region = jnp.sum(seg, axis=0)
    padded = (region + MOE_BM - 1) // MOE_BM * MOE_BM
    pend = jnp.cumsum(padded)
    pstart = pend - padded
    gstart = pstart[None, :] + jnp.cumsum(seg, axis=0) - seg
    slot_t = slot.transpose(0, 2, 1).astype(i32)
    blk_start = jnp.arange(MOE_BLOCKS, dtype=i32) * MOE_BM
    n_used = (pend[-1] // MOE_BM).astype(i32).reshape(1)
    block_e = jnp.minimum(jnp.sum(pend[None, :] <= blk_start[:, None], axis=1), N_EXPERTS - 1)
    last_e = block_e[jnp.maximum(n_used[0] - 1, 0)]
    block_e = jnp.where(blk_start < pend[-1], block_e, last_e).astype(i32)
    zpos = jnp.where(region > 0, pend - MOE_BM, -1).astype(i32)
    present = region > 0
    eidx = jnp.arange(N_EXPERTS, dtype=i32)
    later = jnp.where(present[None, :] & (eidx[None, :] > eidx[:, None]), eidx[None, :], N_EXPERTS)
    nxt = jnp.min(later, axis=1)
    nxt = jnp.where(nxt < N_EXPERTS, nxt, -1)
    order = jnp.cumsum(present.astype(i32)) - present.astype(i32)
    onehot = (block_e[:, None] == eidx[None, :]).astype(i32)
    pick = lambda t: jnp.sum(onehot * t[None, :].astype(i32), axis=1)
    used = blk_start < pend[-1]
    first = (used & (blk_start == pick(pstart))).astype(i32)
    next_e = pick(nxt).astype(i32)
    parity = (pick(order) % 2).astype(i32)
    tab = lambda t: t.reshape(N_TILES * N_EXPERTS).astype(i32)
    return (slot.reshape(N_TOK, TOP_K).astype(i32), slot_t, tab(lstart), tab(gstart),
            tab(seg // SEG_ALIGN), (block_e, n_used, first, next_e, parity), zpos)


def kernel(x_prompt, x_sample, state_s5_re, state_s5_im, state_hgrn, c, c_ctx, ada_w, ada_b, norm_w, w_in,
           s5_a_re, s5_a_im, s5_log_dt, s5_b_re, s5_b_im, s5_c_re, s5_c_im, s5_d, s5_w_glu, s5_b_glu,
           hgrn_lb, hgrn_norm_w, w_out, router_w, router_b, exp_w1, exp_b1, exp_w2, exp_b2):
    pos = _pos_embed_2d(DEC_SEQ // GRID_W, D_MODEL)
    x = jnp.concatenate([x_prompt.reshape(N_PROMPT, D_MODEL),
                         (x_sample + pos[None]).reshape(N_SAMPLE, D_MODEL)], axis=0)
    cond = jnp.concatenate([c_ctx[None, :], c, jnp.zeros((N_COND - 1 - DEC_BATCH, D_MODEL), F32)], axis=0)
    mod = _modulation(cond, ada_w, ada_b).reshape(DEPTH, N_COND, 1, 6 * D_MODEL)
    tiles = jnp.arange(N_TILES, dtype=jnp.int32)
    tile_cond = jnp.where(tiles < N_PROMPT // TM, 0, 1 + (tiles - N_PROMPT // TM) // (DEC_SEQ // TM)).astype(jnp.int32)
    w_in_b = w_in.astype(BF16)
    w_out_b = w_out.astype(BF16)
    w_glu_b = s5_w_glu.astype(BF16)
    router_hi = router_w.astype(BF16)
    router_parts = jnp.stack([router_hi, (router_w - router_hi.astype(F32)).astype(BF16)])

    s5_w = _s5_weights(s5_a_re, s5_a_im, s5_log_dt, s5_b_re, s5_b_im, s5_c_re, s5_c_im)
    fin_re, fin_im, fin_hg = [], [], []
    n = S5_STATE
    for l in range(DEPTH):
        u, qs, v, gs, lf_f, k_f, lf_b, k_b = _in_proj(l, tile_cond, x, mod[l], norm_w[l], w_in_b[l], hgrn_lb)

        h0 = jnp.concatenate([
            state_s5_re[:, l].transpose(2, 0, 1, 3).reshape(S5_GROUPS, DEC_BATCH, 2 * n),
            state_s5_im[:, l].transpose(2, 0, 1, 3).reshape(S5_GROUPS, DEC_BATCH, 2 * n)], axis=-1)
        h0 = h0.reshape(S5_NLT, S5_GPT, S5_NRB - 1, S5_SEQ_PER_RB, S5_NS).transpose(0, 2, 1, 3, 4)
        ycore, hfin = _s5_scan(l, u, *s5_w, h0)
        hfin = hfin.reshape(S5_GROUPS, BATCH, S5_NS)
        fin_re.append(hfin[:, :, 0:2 * n].reshape(S5_GROUPS, BATCH, 2, n).transpose(1, 2, 0, 3))
        fin_im.append(hfin[:, :, 2 * n:].reshape(S5_GROUPS, BATCH, 2, n).transpose(1, 2, 0, 3))

        s0 = jnp.concatenate([jnp.zeros((BATCH, 2, HGRN_HEADS, HGRN_DV, HGRN_DK), F32),
                              jnp.swapaxes(state_hgrn[:, l], -1, -2)], axis=0)
        o_f, o_b, sfin = _hgrn_scan(qs, v, lf_f, k_f, lf_b, k_b, s0)
        fin_hg.append(jnp.swapaxes(sfin[:BATCH], -1, -2))

        x1, h2, ri, gate, counts = _out_proj(
            tile_cond, x, ycore, u, o_f, o_b, gs, mod[l], norm_w[l], s5_d[l].reshape(1, S5_WIDTH),
            w_glu_b[l], s5_b_glu[l].reshape(1, S5_WIDTH), hgrn_norm_w[l].reshape(1, HGRN_DV),
            w_out_b[l], router_parts[:, l], router_b[l].reshape(1, N_EXPERTS))

        slot, slot_t, lstart, gstart, ngrp, blocks, zpos = _routing_tables(ri, counts)
        xs = _dispatch(zpos, slot_t, lstart, gstart, ngrp, h2)
        ys = _moe_experts(l, *blocks, xs, exp_w1, exp_b1, exp_w2, exp_b2)
        x = _combine(tile_cond, lstart, gstart, ngrp, x1, slot, gate, mod[l], norm_w[l], ys)

    y_prompt = x[:N_PROMPT].reshape(BATCH, SEQ, D_MODEL)
    y_sample = x[N_PROMPT:].reshape(DEC_BATCH, DEC_SEQ, D_MODEL)
    return (y_prompt, y_sample, jnp.stack(fin_re, axis=1), jnp.stack(fin_im, axis=1),
            jnp.stack(fin_hg, axis=1))
```

```python
import functools
import math

import jax
import jax.numpy as jnp
from jax import lax
from jax.experimental import pallas as pl
from jax.experimental.pallas import tpu as pltpu

F32 = jnp.float32
BF16 = jnp.bfloat16

D_MODEL = 1024
BATCH = 16
SEQ = 256
DEPTH = 2
DEC_BATCH = 4
DEC_SEQ = 2048
GRID_W = 64
S5_WIDTH = 512
S5_GROUP_CH = 16
S5_GROUPS = 32
S5_STATE = 64
HGRN_WIDTH = 512
HGRN_DK = 128
HGRN_HEADS = 4
HGRN_DV = 128
IN_COLS = S5_WIDTH + 5 * HGRN_WIDTH
N_EXPERTS = 32
TOP_K = 4
D_FF = D_MODEL
SWIGLU_LIMIT = 7.0
SWIGLU_ALPHA = 1.702
NORM_EPS = 1e-6
POS_BASE = 10000.0

N_PROMPT = BATCH * SEQ
N_SAMPLE = DEC_BATCH * DEC_SEQ
N_TOK = N_PROMPT + N_SAMPLE
N_SEQ = BATCH + DEC_BATCH
N_COND = 8

TM = 256
N_TILES = N_TOK // TM
S5_T = 16
S5_LT = 128
S5_GPT = S5_LT // S5_GROUP_CH
S5_NLT = S5_WIDTH // S5_LT
S5_RB = N_PROMPT
S5_CR = S5_RB // S5_T
S5_NRB = N_TOK // S5_RB
S5_SEQ_PER_RB = S5_RB // DEC_SEQ
S5_NLAG = 2 * S5_T - 1
S5_NS = 4 * S5_STATE
HC = 128
N_HCHUNK = N_TOK // HC
EXP_CLAMP = 80.0
MOE_BM = 256
N_ASSIGN = N_TOK * TOP_K
SEG_ALIGN = 8
TILE_SLOTS = -(-(TM * TOP_K + N_EXPERTS * (SEG_ALIGN - 1)) // 128) * 128
MOE_BLOCKS = -(-(N_ASSIGN + N_TILES * N_EXPERTS * (SEG_ALIGN - 1) + N_EXPERTS * (MOE_BM - 1)) // MOE_BM)
MOE_GRP = SEG_ALIGN
MOE_TILE_GROUPS = TILE_SLOTS // MOE_GRP
MOE_BLK_GROUPS = MOE_BM // MOE_GRP
MOE_GROUPS = MOE_BLOCKS * MOE_BLK_GROUPS
MOE_ZERO_GROUP = N_TILES * MOE_TILE_GROUPS
MOE_PAD_GROUPS = N_EXPERTS * (MOE_BLK_GROUPS - 1)
MOE_XS_ROWS = (N_TILES + 1) * TILE_SLOTS
MOE_YS_ROWS = N_TILES * TILE_SLOTS + MOE_PAD_GROUPS * MOE_GRP
MOE_BASE_ROWS = TM * TOP_K
MOE_TAIL_ARMS = (128, 64, 32, 16, 8)
VMEM_LIMIT = 56 * 1024 * 1024


def _rms(x, w):
    return x * lax.rsqrt(jnp.mean(x * x, axis=-1, keepdims=True) + NORM_EPS) * w


def _silu(x):
    return x * jax.nn.sigmoid(x)


MOD_TN = 1536


def _mod_kernel(cond_ref, w_ref, b_ref, o_ref):
    s = _silu(cond_ref[...]).astype(BF16)
    o_ref[0] = jnp.dot(s, w_ref[0].astype(BF16), preferred_element_type=F32) + b_ref[0]


def _modulation(cond, ada_w, ada_b):
    return pl.pallas_call(
        _mod_kernel,
        grid=(DEPTH, 6 * D_MODEL // MOD_TN),
        in_specs=[
            pl.BlockSpec((N_COND, D_MODEL), lambda l, j: (0, 0)),
            pl.BlockSpec((1, D_MODEL, MOD_TN), lambda l, j: (l, 0, j)),
            pl.BlockSpec((1, 1, MOD_TN), lambda l, j: (l, 0, j)),
        ],
        out_specs=pl.BlockSpec((1, N_COND, MOD_TN), lambda l, j: (l, 0, j)),
        out_shape=jax.ShapeDtypeStruct((DEPTH, N_COND, 6 * D_MODEL), F32),
        compiler_params=pltpu.CompilerParams(vmem_limit_bytes=VMEM_LIMIT),
        name="adaln_mod",
    )(cond, ada_w, ada_b.reshape(DEPTH, 1, 6 * D_MODEL))


def _in_kernel(layer, cond_ref, x_ref, mod_ref, nw_ref, w_ref, lb_ref,
               u_ref, q_ref, v_ref, g_ref, lff_ref, kf_ref, lfb_ref, kb_ref):
    del cond_ref
    mod = mod_ref[0]
    sh1 = mod[:, 0:D_MODEL]
    sc1 = mod[:, D_MODEL:2 * D_MODEL]
    h = _rms(x_ref[...], nw_ref[0:1, :]) * (1.0 + sc1) + sh1
    z = jnp.dot(h.astype(BF16), w_ref[...], preferred_element_type=F32)
    w = HGRN_WIDTH
    c0 = S5_WIDTH
    u_ref[...] = z[:, 0:c0]
    q_ref[...] = _silu(z[:, c0:c0 + w]) * (HGRN_DK ** -0.5)
    v_ref[...] = z[:, c0 + 3 * w:c0 + 4 * w]
    g_ref[...] = _silu(z[:, c0 + 4 * w:c0 + 5 * w])
    lbp = lb_ref[...]
    e = jnp.exp(lbp - jnp.max(lbp, axis=0, keepdims=True))
    probs = e / jnp.sum(e, axis=0, keepdims=True)
    lb = jnp.sum(probs[0:layer + 1], axis=0) - probs[0]
    for d, (lf_ref, k_ref) in enumerate(((lff_ref, kf_ref), (lfb_ref, kb_ref))):
        f = z[:, c0 + (1 + d) * w:c0 + (2 + d) * w]
        lbd = lb[d:d + 1, :]
        fg = lbd + (1.0 - lbd) * jax.nn.sigmoid(f)
        lf_ref[...] = jnp.log(fg)
        k_ref[...] = 1.0 - fg


def _in_proj(layer, tile_cond, x, mod_l, norm_w_l, w_in_l, hgrn_lb):
    tok = lambda i, c: (i, 0)
    out = jax.ShapeDtypeStruct((N_TOK, HGRN_WIDTH), F32)
    return pl.pallas_call(
        functools.partial(_in_kernel, layer),
        grid_spec=pltpu.PrefetchScalarGridSpec(
            num_scalar_prefetch=1,
            grid=(N_TILES,),
            in_specs=[
                pl.BlockSpec((TM, D_MODEL), tok),
                pl.BlockSpec((1, 1, 6 * D_MODEL), lambda i, c: (c[i], 0, 0)),
                pl.BlockSpec((4, D_MODEL), lambda i, c: (0, 0)),
                pl.BlockSpec((D_MODEL, IN_COLS), lambda i, c: (0, 0)),
                pl.BlockSpec((DEPTH, 2, HGRN_WIDTH), lambda i, c: (0, 0, 0)),
            ],
            out_specs=[pl.BlockSpec((TM, HGRN_WIDTH), tok)] * 8,
        ),
        out_shape=[out] * 8,
        compiler_params=pltpu.CompilerParams(
            dimension_semantics=("arbitrary",), vmem_limit_bytes=VMEM_LIMIT),
        name="in_proj",
    )(tile_cond, x, mod_l, norm_w_l, w_in_l, hgrn_lb)


def _s5_layer_weights(a_re, a_im, log_dt, b_re, b_im, c_re, c_im):
    hp = lax.Precision.HIGHEST
    t = S5_T
    dt = jnp.exp(log_dt)[..., None]
    lam_re = jnp.minimum(a_re, -1e-4)
    lam_im = a_im
    mag = jnp.exp(dt * lam_re)
    ang = dt * lam_im
    ab_re = mag * jnp.cos(ang)
    ab_im = mag * jnp.sin(ang)
    den = lam_re * lam_re + lam_im * lam_im
    nr = ab_re - 1.0
    ni = ab_im
    co_re = (nr * lam_re + ni * lam_im) / den
    co_im = (ni * lam_re - nr * lam_im) / den
    bb_re = co_re[..., None] * b_re - co_im[..., None] * b_im
    bb_im = co_re[..., None] * b_im + co_im[..., None] * b_re
    pr = [jnp.ones_like(ab_re)]
    pi = [jnp.zeros_like(ab_im)]
    for _ in range(t):
        pr.append(pr[-1] * ab_re - pi[-1] * ab_im)
        pi.append(pr[-2] * ab_im + pi[-1] * ab_re)
    pr = jnp.stack(pr, axis=1)
    pi = jnp.stack(pi, axis=1)
    abr = pr[..., None] * bb_re[:, None] - pi[..., None] * bb_im[:, None]
    abi = pr[..., None] * bb_im[:, None] + pi[..., None] * bb_re[:, None]
    kk = jnp.einsum('dgon,dkgni->dkgio', jnp.concatenate([c_re, -c_im], axis=-1),
                    jnp.concatenate([abr, abi], axis=-2), precision=hp)
    kern = (kk[0], kk[1])
    abs_ = ((abr[0], abi[0]), (abr[1], abi[1]))
    pows = ((pr[0], pi[0]), (pr[1], pi[1]))
    p = S5_GROUP_CH
    lags = jnp.concatenate([kern[1][t - 1:0:-1], (kern[0][0] + kern[1][0])[None], kern[0][1:t]], axis=0)
    m = lags.reshape(S5_NLAG, S5_NLT, S5_GPT, p, p)
    bd = jnp.einsum('ltaio,ab->ltaibo', m, jnp.eye(S5_GPT, dtype=F32))
    wcat = bd.reshape(S5_NLAG, S5_NLT, S5_LT, S5_LT).transpose(1, 2, 0, 3).reshape(S5_NLT, S5_LT, S5_NLAG * S5_LT)
    (abr_f, abi_f), (abr_b, abi_b) = abs_
    parts = (abr_f[t - 1::-1], abr_b[:t], abi_f[t - 1::-1], abi_b[:t])
    wst = jnp.concatenate([x.transpose(1, 0, 3, 2) for x in parts], axis=-1)
    wst = wst.reshape(S5_NLT, S5_GPT, t, p, S5_NS).transpose(0, 2, 1, 3, 4).reshape(S5_NLT, t * S5_LT, S5_NS)
    (pr_f, pi_f), (pr_b, pi_b) = pows

    def out_rows(cr, ci, pr_, pi_):
        re_rows = cr[None] * pr_[:, :, None, :] - ci[None] * pi_[:, :, None, :]
        im_rows = -(cr[None] * pi_[:, :, None, :] + ci[None] * pr_[:, :, None, :])
        return re_rows.transpose(1, 3, 0, 2), im_rows.transpose(1, 3, 0, 2)

    fre, fim = out_rows(c_re[0], c_im[0], pr_f[1:t + 1], pi_f[1:t + 1])
    bre, bim = out_rows(c_re[1], c_im[1], pr_b[t:0:-1], pi_b[t:0:-1])
    wout = lax.optimization_barrier(jnp.concatenate([fre, bre, fim, bim], axis=1))
    wdense = (wout.reshape(S5_NLT, S5_GPT, S5_NS, t, p).transpose(0, 2, 3, 1, 4)
              .reshape(S5_NLT, S5_NS, t * S5_LT))
    at = jnp.stack([jnp.concatenate([pr_f[t], pr_b[t]], axis=-1),
                    jnp.concatenate([pi_f[t], pi_b[t]], axis=-1)], axis=1)
    return (wcat.astype(BF16), wst.astype(BF16), wdense.astype(BF16),
            at.reshape(S5_NLT, S5_GPT, 2, 2 * S5_STATE))


def _s5_weights(*params):
    return jax.vmap(_s5_layer_weights)(*params)


def _s5_kernel(u_ref, wcat_ref, wst_ref, wd_ref, at_ref, h0_ref, y_ref, hfin_ref,
               wbig, dh_scr, hf_scr, hb_scr):
    t = S5_T
    n2 = 2 * S5_STATE
    r = pl.program_id(1)

    @pl.when(r == 0)
    def _():
        for s in range(t):
            wbig[s * S5_LT:(s + 1) * S5_LT, :] = wcat_ref[0, :, (t - 1 - s) * S5_LT:(2 * t - 1 - s) * S5_LT]

    xcat = jnp.concatenate([u_ref[pl.ds(s, S5_CR, stride=t), :].astype(BF16) for s in range(t)], axis=-1)
    yacc = jnp.dot(xcat, wbig[...], preferred_element_type=F32)
    lane_k = lax.broadcasted_iota(jnp.int32, (1, t * S5_LT), 1)
    grp_k = (lane_k % S5_LT) // S5_GROUP_CH
    for gi in range(S5_GPT):
        xg = jnp.where(grp_k == gi, xcat, jnp.zeros_like(xcat))
        dh = jnp.dot(xg, wst_ref[0], preferred_element_type=F32)
        dh_scr[gi, 0] = dh[:, 0:n2]
        dh_scr[gi, 1] = dh[:, n2:]

    lane = lax.broadcasted_iota(jnp.int32, (1, n2), 1)
    fwd_lane = lane < S5_STATE
    are = [at_ref[0, gi, 0:1, :] for gi in range(S5_GPT)]
    aim = [at_ref[0, gi, 1:2, :] for gi in range(S5_GPT)]

    def advance(gi, hre, him, dre, dim):
        return (are[gi] * hre - aim[gi] * him + dre, are[gi] * him + aim[gi] * hre + dim)

    @pl.when(r == 0)
    def _():
        nc = SEQ // t
        for gi in range(S5_GPT):
            hre = jnp.zeros((BATCH, n2), F32)
            him = jnp.zeros((BATCH, n2), F32)
            for s in range(nc):
                rows_f = pl.ds(s, BATCH, stride=nc)
                rows_b = pl.ds(nc - 1 - s, BATCH, stride=nc)
                hf_scr[gi, 0, rows_f, :] = hre
                hf_scr[gi, 1, rows_f, :] = him
                hb_scr[gi, 0, rows_b, :] = hre
                hb_scr[gi, 1, rows_b, :] = him
                dre = jnp.where(fwd_lane, dh_scr[gi, 0, rows_f, :], dh_scr[gi, 0, rows_b, :])
                dim = jnp.where(fwd_lane, dh_scr[gi, 1, rows_f, :], dh_scr[gi, 1, rows_b, :])
                hre, him = advance(gi, hre, him, dre, dim)
            hfin_ref[0, gi] = jnp.concatenate([hre, him], axis=-1)

    @pl.when(r > 0)
    def _():
        nc = DEC_SEQ // t
        nb = S5_SEQ_PER_RB
        init = tuple((h0_ref[0, 0, gi, b:b + 1, 0:n2], h0_ref[0, 0, gi, b:b + 1, n2:])
                     for gi in range(S5_GPT) for b in range(nb))

        def step(o, carry):
            new = []
            for gi in range(S5_GPT):
                for b in range(nb):
                    hre, him = carry[gi * nb + b]
                    rf = pl.multiple_of(b * nc + o * 8, 8)
                    rb = pl.multiple_of(b * nc + nc - 8 - o * 8, 8)
                    dfr = dh_scr[gi, 0, pl.ds(rf, 8), :]
                    dfi = dh_scr[gi, 1, pl.ds(rf, 8), :]
                    dbr = dh_scr[gi, 0, pl.ds(rb, 8), :]
                    dbi = dh_scr[gi, 1, pl.ds(rb, 8), :]
                    ent_re, ent_im = [], []
                    for i in range(8):
                        ent_re.append(hre)
                        ent_im.append(him)
                        dre = jnp.where(fwd_lane, dfr[i:i + 1], dbr[7 - i:8 - i])
                        dim = jnp.where(fwd_lane, dfi[i:i + 1], dbi[7 - i:8 - i])
                        hre, him = advance(gi, hre, him, dre, dim)
                    hf_scr[gi, 0, pl.ds(rf, 8), :] = jnp.concatenate(ent_re, axis=0)
                    hf_scr[gi, 1, pl.ds(rf, 8), :] = jnp.concatenate(ent_im, axis=0)
                    hb_scr[gi, 0, pl.ds(rb, 8), :] = jnp.concatenate(ent_re[::-1], axis=0)
                    hb_scr[gi, 1, pl.ds(rb, 8), :] = jnp.concatenate(ent_im[::-1], axis=0)
                    new.append((hre, him))
            return tuple(new)

        lax.fori_loop(0, nc // 8, step, init)

    for gi in range(S5_GPT):
        hent = jnp.concatenate([jnp.where(fwd_lane, hf_scr[gi, 0], hb_scr[gi, 0]),
                                jnp.where(fwd_lane, hf_scr[gi, 1], hb_scr[gi, 1])], axis=-1).astype(BF16)
        yi = jnp.dot(hent, wd_ref[0], preferred_element_type=F32)
        yacc = yacc + jnp.where(grp_k == gi, yi, 0.0)
    for s in range(t):
        y_ref[pl.ds(s, S5_CR, stride=t), :] = yacc[:, s * S5_LT:(s + 1) * S5_LT]


def _s5_scan(layer, u, wcat, wst, wdense, at, h0):
    tile = lambda j, r: (layer * S5_NLT + j, 0, 0)
    merge = lambda w: w.reshape((DEPTH * S5_NLT,) + w.shape[2:])
    wcat, wst, wdense, at = merge(wcat), merge(wst), merge(wdense), merge(at)
    return pl.pallas_call(
        _s5_kernel,
        grid=(S5_NLT, S5_NRB),
        in_specs=[
            pl.BlockSpec((S5_RB, S5_LT), lambda j, r: (r, j)),
            pl.BlockSpec((1, S5_LT, S5_NLAG * S5_LT), tile),
            pl.BlockSpec((1, S5_T * S5_LT, S5_NS), tile),
            pl.BlockSpec((1, S5_NS, S5_T * S5_LT), tile),
            pl.BlockSpec((1, S5_GPT, 2, 2 * S5_STATE), lambda j, r: (layer * S5_NLT + j, 0, 0, 0)),
            pl.BlockSpec((1, 1, S5_GPT, S5_SEQ_PER_RB, S5_NS), lambda j, r: (j, jnp.maximum(r - 1, 0), 0, 0, 0)),
        ],
        out_specs=[
            pl.BlockSpec((S5_RB, S5_LT), lambda j, r: (r, j)),
            pl.BlockSpec((1, S5_GPT, BATCH, S5_NS), lambda j, r: (j, 0, 0, 0)),
        ],
        out_shape=[
            jax.ShapeDtypeStruct((N_TOK, S5_WIDTH), F32),
            jax.ShapeDtypeStruct((S5_NLT, S5_GPT, BATCH, S5_NS), F32),
        ],
        scratch_shapes=[pltpu.VMEM((S5_T * S5_LT, S5_T * S5_LT), BF16)]
                       + [pltpu.VMEM((S5_GPT, 2, S5_CR, 2 * S5_STATE), F32)] * 3,
        compiler_params=pltpu.CompilerParams(
            dimension_semantics=("arbitrary", "arbitrary"), vmem_limit_bytes=VMEM_LIMIT),
        name="s5_scan",
    )(u, wcat, wst, wdense, at, h0)


def _split3(x):
    hi = x.astype(BF16)
    r1 = x - hi.astype(F32)
    mid = r1.astype(BF16)
    lo = (r1 - mid.astype(F32)).astype(BF16)
    return hi, mid, lo


def _piecewise_rows(b, blk, row_in_blk):
    parts = []
    for j in range(HC // blk):
        r = j * blk + row_in_blk
        parts.append(jnp.broadcast_to(b[r:r + 1, :], (blk, b.shape[1])))
    return parts[0] if len(parts) == 1 else jnp.concatenate(parts, axis=0)


def _nt(a, b):
    return lax.dot_general(a, b, (((1,), (1,)), ((), ())), preferred_element_type=F32)


def _tn(a, b):
    return lax.dot_general(a, b, (((0,), (0,)), ((), ())), preferred_element_type=F32)


def _hgrn_dir(reverse, q_ref, v_ref, lf_ref, k_ref, o_ref, st_ref):
    row = lax.broadcasted_iota(jnp.int32, (HC, HC), 0)
    col = lax.broadcasted_iota(jnp.int32, (HC, HC), 1)
    causal = (col >= row) if reverse else (col <= row)
    tri = jnp.where(causal, 1.0, 0.0).astype(BF16)
    lf = lf_ref[...]
    hi, mid, lo = _split3(lf)
    ball = (jnp.dot(tri, hi, preferred_element_type=F32) + jnp.dot(tri, mid, preferred_element_type=F32)
            + jnp.dot(tri, lo, preferred_element_type=F32))
    last = 0 if reverse else HC - 1
    masks = []
    for blk in (128, 64, 32):
        half = blk // 2
        same = (row // blk) == (col // blk)
        t_hi = (row % blk) >= half
        s_hi = (col % blk) >= half
        if reverse:
            masks.append(same & jnp.logical_not(t_hi) & s_hi)
        else:
            masks.append(same & t_hi & jnp.logical_not(s_hi))
    diag_mask = ((row // 16) == (col // 16)) & causal
    for h in range(HGRN_HEADS):
        sl = slice(h * HGRN_DK, (h + 1) * HGRN_DK)
        b = ball[:, sl]
        q = q_ref[:, sl]
        k = k_ref[:, sl]
        v = v_ref[:, sl].astype(BF16)
        st = st_ref[h]
        b_last = b[last:last + 1, :]
        q_in = (q * jnp.exp(b)).astype(BF16)
        k_in = (k * jnp.exp(b_last - b)).astype(BF16)
        o = _nt(q_in, st.astype(BF16))
        st_ref[h] = jnp.exp(b_last) * st + _tn(v, k_in)
        scores = jnp.zeros((HC, HC), F32)
        for blk, mask in zip((128, 64, 32), masks):
            half = blk // 2
            m = _piecewise_rows(b, blk, half if reverse else half - 1)
            qj = (q * jnp.exp(jnp.minimum(b - m, 0.0))).astype(BF16)
            kj = (k * jnp.exp(jnp.minimum(m - b, 0.0))).astype(BF16)
            scores = scores + jnp.where(mask, _nt(qj, kj), 0.0)
        m = _piecewise_rows(b, 16, 8 if reverse else 7)
        qd = (q * jnp.exp(jnp.minimum(b - m, EXP_CLAMP))).astype(BF16)
        kd = (k * jnp.exp(jnp.minimum(m - b, EXP_CLAMP))).astype(BF16)
        scores = scores + jnp.where(diag_mask, _nt(qd, kd), 0.0)
        o_ref[:, sl] = o + jnp.dot(scores.astype(BF16), v, preferred_element_type=F32)


def _hgrn_kernel(cf_ref, cb_ref, seq_ref, first_ref, last_ref,
                 qf_ref, vf_ref, lff_ref, kf_ref, qb_ref, vb_ref, lfb_ref, kb_ref, s0_ref,
                 of_ref, ob_ref, sfin_ref, st_scr):
    del cf_ref, cb_ref, seq_ref
    j = pl.program_id(0)

    @pl.when(first_ref[j] == 1)
    def _():
        st_scr[...] = s0_ref[0]

    _hgrn_dir(False, qf_ref, vf_ref, lff_ref, kf_ref, of_ref, st_scr.at[0])
    _hgrn_dir(True, qb_ref, vb_ref, lfb_ref, kb_ref, ob_ref, st_scr.at[1])

    @pl.when(last_ref[j] == 1)
    def _():
        sfin_ref[0] = st_scr[...]


def _hgrn_tables():
    cf, cb, sq, first, last = [], [], [], [], []
    base = 0
    for s in range(N_SEQ):
        nc = (SEQ if s < BATCH else DEC_SEQ) // HC
        for t in range(nc):
            cf.append(base + t)
            cb.append(base + nc - 1 - t)
            sq.append(s)
            first.append(int(t == 0))
            last.append(int(t == nc - 1))
        base += nc
    return tuple(jnp.asarray(x, jnp.int32) for x in (cf, cb, sq, first, last))


def _hgrn_scan(qs, v, lf_f, k_f, lf_b, k_b, s0):
    fwd = lambda j, cf, cb, sq, fi, la: (cf[j], 0)
    bwd = lambda j, cf, cb, sq, fi, la: (cb[j], 0)
    seq = lambda j, cf, cb, sq, fi, la: (sq[j], 0, 0, 0, 0)
    tile = (HC, HGRN_WIDTH)
    sblk = (1, 2, HGRN_HEADS, HGRN_DV, HGRN_DK)
    out = jax.ShapeDtypeStruct((N_TOK, HGRN_WIDTH), F32)
    return pl.pallas_call(
        _hgrn_kernel,
        grid_spec=pltpu.PrefetchScalarGridSpec(
            num_scalar_prefetch=5,
            grid=(N_HCHUNK,),
            in_specs=[pl.BlockSpec(tile, fwd)] * 4 + [pl.BlockSpec(tile, bwd)] * 4
                     + [pl.BlockSpec(sblk, seq)],
            out_specs=[pl.BlockSpec(tile, fwd), pl.BlockSpec(tile, bwd), pl.BlockSpec(sblk, seq)],
            scratch_shapes=[pltpu.VMEM(sblk[1:], F32)],
        ),
        out_shape=[out, out, jax.ShapeDtypeStruct((N_SEQ,) + sblk[1:], F32)],
        compiler_params=pltpu.CompilerParams(
            dimension_semantics=("arbitrary",), vmem_limit_bytes=VMEM_LIMIT),
        name="hgrn_scan",
    )(*_hgrn_tables(), qs, v, lf_f, k_f, qs, v, lf_b, k_b, s0)


def _gelu_tanh(x):
    return 0.5 * x * (1.0 + jnp.tanh(math.sqrt(2.0 / math.pi) * (x + 0.044715 * (x * x * x))))


def _out_kernel(cond_ref, x_ref, yc_ref, u_ref, of_ref, ob_ref, g_ref, mod_ref, nw_ref, d_ref,
                wglu_ref, bglu_ref, hnw_ref, wout_ref, rw_ref, rb_ref,
                x1_ref, h2_ref, ri_ref, rg_ref, cnt_ref):
    del cond_ref
    mod = mod_ref[0]
    g1 = mod[:, 2 * D_MODEL:3 * D_MODEL]
    sh2 = mod[:, 3 * D_MODEL:4 * D_MODEL]
    sc2 = mod[:, 4 * D_MODEL:5 * D_MODEL]
    y = _gelu_tanh(yc_ref[...] + d_ref[...] * u_ref[...])
    y_s5 = y * jax.nn.sigmoid(jnp.dot(y.astype(BF16), wglu_ref[...], preferred_element_type=F32) + bglu_ref[...])
    o = of_ref[...] + ob_ref[...]
    gs = g_ref[...]
    heads = []
    for h in range(HGRN_HEADS):
        sl = slice(h * HGRN_DV, (h + 1) * HGRN_DV)
        heads.append(_rms(o[:, sl], hnw_ref[...]) * gs[:, sl])
    y_hg = jnp.concatenate(heads, axis=-1)
    mix = (jnp.dot(y_s5.astype(BF16), wout_ref[0:S5_WIDTH, :], preferred_element_type=F32)
           + jnp.dot(y_hg.astype(BF16), wout_ref[S5_WIDTH:, :], preferred_element_type=F32))
    x1 = x_ref[...] + g1 * _rms(mix, nw_ref[1:2, :])
    x1_ref[...] = x1
    h2 = _rms(x1, nw_ref[2:3, :]) * (1.0 + sc2) + sh2
    h2_hi = h2.astype(BF16)
    h2_ref[...] = h2_hi
    h2_lo = (h2 - h2_hi.astype(F32)).astype(BF16)
    logits = rb_ref[...]
    for a in (h2_hi, h2_lo):
        for part in range(2):
            logits = logits + jnp.dot(a, rw_ref[part], preferred_element_type=F32)
    eidx = lax.broadcasted_iota(jnp.int32, (TM, N_EXPERTS), 1).astype(F32)
    vals = logits
    top_v, top_i, onehots = [], [], []
    for _ in range(TOP_K):
        mx = jnp.max(vals, axis=-1, keepdims=True)
        ix = jnp.min(jnp.where(vals == mx, eidx, float(N_EXPERTS)), axis=-1, keepdims=True)
        sel = eidx == ix
        top_v.append(mx)
        top_i.append(ix)
        onehots.append(sel)
        vals = jnp.where(sel, -jnp.inf, vals)
    ex = [jnp.exp(tv - top_v[0]) for tv in top_v]
    den = ex[0] + ex[1] + ex[2] + ex[3]
    tot = jnp.zeros((TM, N_EXPERTS), F32)
    for sel in onehots:
        tot = tot + jnp.where(sel, 1.0, 0.0)
    r_t = lax.broadcasted_iota(jnp.int32, (TM, TM), 0)
    r_s = lax.broadcasted_iota(jnp.int32, (TM, TM), 1)
    strict = jnp.where(r_s < r_t, 1.0, 0.0).astype(BF16)
    before = jnp.dot(strict, tot.astype(BF16), preferred_element_type=F32)
    cnt = jnp.sum(tot, axis=0, keepdims=True)
    seg = jnp.floor((cnt + (SEG_ALIGN - 1)) * (1.0 / SEG_ALIGN)) * SEG_ALIGN
    e_r = lax.broadcasted_iota(jnp.int32, (N_EXPERTS, N_EXPERTS), 0)
    e_c = lax.broadcasted_iota(jnp.int32, (N_EXPERTS, N_EXPERTS), 1)
    lstart = jnp.dot(seg, jnp.where(e_r < e_c, 1.0, 0.0), precision=lax.Precision.HIGHEST,
                     preferred_element_type=F32)
    before = before + lstart
    lane = lax.broadcasted_iota(jnp.int32, (TM, 128), 1)
    ri = jnp.zeros((TM, 128), F32)
    rg = jnp.zeros((TM, 128), F32)
    for kk in range(TOP_K):
        rank = jnp.sum(jnp.where(onehots[kk], before, 0.0), axis=-1, keepdims=True)
        ri = jnp.where(lane == kk, top_i[kk], ri)
        ri = jnp.where(lane == TOP_K + kk, rank, ri)
        rg = jnp.where(lane == kk, ex[kk] / den, rg)
    ri_ref[...] = ri.astype(jnp.int32)
    rg_ref[...] = rg
    cnt_ref[0] = cnt.astype(jnp.int32)


def _out_proj(tile_cond, x, ycore, u, o_f, o_b, gs, mod_l, norm_w_l, s5_d_l, wglu_l, bglu_l, hnw_l,
              wout_l, rw_l, rb_l):
    tok = lambda i, c: (i, 0)
    full2 = lambda i, c: (0, 0)
    half = pl.BlockSpec((TM, HGRN_WIDTH), tok)
    wide = pl.BlockSpec((TM, D_MODEL), tok)
    return pl.pallas_call(
        _out_kernel,
        grid_spec=pltpu.PrefetchScalarGridSpec(
            num_scalar_prefetch=1,
            grid=(N_TILES,),
            in_specs=[
                wide, half, half, half, half, half,
                pl.BlockSpec((1, 1, 6 * D_MODEL), lambda i, c: (c[i], 0, 0)),
                pl.BlockSpec((4, D_MODEL), full2),
                pl.BlockSpec((1, S5_WIDTH), full2),
                pl.BlockSpec((S5_WIDTH, S5_WIDTH), full2),
                pl.BlockSpec((1, S5_WIDTH), full2),
                pl.BlockSpec((1, HGRN_DV), full2),
                pl.BlockSpec((D_MODEL, D_MODEL), full2),
                pl.BlockSpec((2, D_MODEL, N_EXPERTS), lambda i, c: (0, 0, 0)),
                pl.BlockSpec((1, N_EXPERTS), full2),
            ],
            out_specs=[wide, wide, pl.BlockSpec((TM, 128), tok), pl.BlockSpec((TM, 128), tok),
                       pl.BlockSpec((1, 1, N_EXPERTS), lambda i, c: (i, 0, 0))],
        ),
        out_shape=[
            jax.ShapeDtypeStruct((N_TOK, D_MODEL), F32),
            jax.ShapeDtypeStruct((N_TOK, D_MODEL), BF16),
            jax.ShapeDtypeStruct((N_TOK, 128), jnp.int32),
            jax.ShapeDtypeStruct((N_TOK, 128), F32),
            jax.ShapeDtypeStruct((N_TILES, 1, N_EXPERTS), jnp.int32),
        ],
        compiler_params=pltpu.CompilerParams(
            dimension_semantics=("arbitrary",), vmem_limit_bytes=VMEM_LIMIT),
        name="out_proj_router",
    )(tile_cond, x, ycore, u, o_f, o_b, gs, mod_l, norm_w_l, s5_d_l, wglu_l, bglu_l, hnw_l,
      wout_l, rw_l, rb_l)


def _dispatch_kernel(slot_ref, h_ref, o_ref):
    srow = lax.broadcasted_iota(jnp.int32, (TILE_SLOTS, TM), 0)
    perm = jnp.zeros((TILE_SLOTS, TM), F32)
    for kk in range(TOP_K):
        perm = perm + jnp.where(srow == slot_ref[0, kk:kk + 1, :], 1.0, 0.0)
    o_ref[...] = jnp.dot(perm.astype(BF16), h_ref[...], preferred_element_type=F32)


def _dispatch(slot_t, h2):
    return pl.pallas_call(
        _dispatch_kernel,
        grid=(N_TILES + 1,),
        in_specs=[pl.BlockSpec((1, TOP_K, TM), lambda i: (i, 0, 0)),
                  pl.BlockSpec((TM, D_MODEL), lambda i: (jnp.minimum(i, N_TILES - 1), 0))],
        out_specs=pl.BlockSpec((TILE_SLOTS, D_MODEL), lambda i: (i, 0)),
        out_shape=jax.ShapeDtypeStruct((MOE_XS_ROWS, D_MODEL), F32),
        compiler_params=pltpu.CompilerParams(
            dimension_semantics=("arbitrary",), vmem_limit_bytes=VMEM_LIMIT),
        name="moe_dispatch",
    )(slot_t, h2)


def _moe_kernel(nblk_ref, gbase_ref, gsrc_ref, gdst_ref, w1_ref, b1_ref, w2_ref, b2_ref, xs_ref, ys_ref,
                xbuf, obuf, w1b, w2b, sem_in, sem_out):
    e = pl.program_id(0)
    nb = nblk_ref[e]
    g0 = gbase_ref[e]

    def gather(b, buf):
        return [pltpu.make_async_copy(
            xs_ref.at[pl.ds(pl.multiple_of(gsrc_ref[g0 + b * MOE_BLK_GROUPS + g] * MOE_GRP, MOE_GRP), MOE_GRP)],
            xbuf.at[buf, pl.ds(g * MOE_GRP, MOE_GRP)], sem_in.at[buf]) for g in range(MOE_BLK_GROUPS)]

    def scatter(b, buf):
        return [pltpu.make_async_copy(
            obuf.at[buf, pl.ds(g * MOE_GRP, MOE_GRP)],
            ys_ref.at[pl.ds(pl.multiple_of(gdst_ref[g0 + b * MOE_BLK_GROUPS + g] * MOE_GRP, MOE_GRP), MOE_GRP)],
            sem_out.at[buf]) for g in range(MOE_BLK_GROUPS)]

    @pl.when(nb > 0)
    def _():
        def cast(j, c):
            r = pl.multiple_of(j * 128, 128)
            w1b[pl.ds(r, 128), :] = w1_ref[0, pl.ds(r, 128), :].astype(BF16)
            w2b[pl.ds(r, 128), :] = w2_ref[0, pl.ds(r, 128), :].astype(BF16)
            return c

        for cp in gather(0, 0):
            cp.start()
        lax.fori_loop(0, D_MODEL // 128, cast, 0)

        def block(b, c):
            buf = b % 2

            @pl.when(b >= 2)
            def _():
                for cp in scatter(b - 2, buf):
                    cp.wait()

            for cp in gather(b, buf):
                cp.wait()
            for cp in gather(b + 1, 1 - buf):
                cp.start()
            h = jnp.dot(xbuf[buf].astype(BF16), w1b[...], preferred_element_type=F32) + b1_ref[0]
            glu = jnp.minimum(h[:, :D_FF], SWIGLU_LIMIT)
            lin = jnp.clip(h[:, D_FF:], -SWIGLU_LIMIT, SWIGLU_LIMIT)
            act = glu * jax.nn.sigmoid(SWIGLU_ALPHA * glu) * (lin + 1.0)
            obuf[buf] = jnp.dot(act.astype(BF16), w2b[...], preferred_element_type=F32) + b2_ref[0]
            for cp in scatter(b, buf):
                cp.start()
            return c

        lax.fori_loop(0, nb, block, 0)
        for cp in gather(nb, nb % 2):
            cp.wait()

        @pl.when(nb >= 2)
        def _():
            for cp in scatter(nb - 2, nb % 2):
                cp.wait()

        for cp in scatter(nb - 1, (nb - 1) % 2):
            cp.wait()


def _moe_experts(layer, nblk, gbase, gsrc, gdst, xs, w1, b1, w2, b2):
    exp3 = lambda e, *_: (layer * N_EXPERTS + e, 0, 0)
    w1 = w1.reshape(DEPTH * N_EXPERTS, D_MODEL, 2 * D_FF)
    w2 = w2.reshape(DEPTH * N_EXPERTS, D_FF, D_MODEL)
    return pl.pallas_call(
        _moe_kernel,
        grid_spec=pltpu.PrefetchScalarGridSpec(
            num_scalar_prefetch=4,
            grid=(N_EXPERTS,),
            in_specs=[
                pl.BlockSpec((1, D_MODEL, 2 * D_FF), exp3),
                pl.BlockSpec((1, 1, 2 * D_FF), exp3),
                pl.BlockSpec((1, D_FF, D_MODEL), exp3),
                pl.BlockSpec((1, 1, D_MODEL), exp3),
                pl.BlockSpec(memory_space=pl.ANY),
            ],
            out_specs=pl.BlockSpec(memory_space=pl.ANY),
            scratch_shapes=[pltpu.VMEM((2, MOE_BM, D_MODEL), F32), pltpu.VMEM((2, MOE_BM, D_MODEL), F32),
                            pltpu.VMEM((D_MODEL, 2 * D_FF), BF16), pltpu.VMEM((D_FF, D_MODEL), BF16),
                            pltpu.SemaphoreType.DMA((2,)), pltpu.SemaphoreType.DMA((2,))],
        ),
        out_shape=jax.ShapeDtypeStruct((MOE_YS_ROWS, D_MODEL), F32),
        compiler_params=pltpu.CompilerParams(
            dimension_semantics=("arbitrary",), vmem_limit_bytes=VMEM_LIMIT),
        name="moe_experts",
    )(nblk, gbase, gsrc, gdst, w1, b1.reshape(DEPTH * N_EXPERTS, 1, 2 * D_FF), w2,
      b2.reshape(DEPTH * N_EXPERTS, 1, D_MODEL), xs)


def _combine_kernel(cond_ref, tg_ref, x1_ref, slot_ref, gate_ref, mod_ref, nw_ref, ys_ref, o_ref, buf, sem):
    del cond_ref
    i = pl.program_id(0)
    cur = i % 2

    def fetch(tile, b, wait):
        def go(cp):
            cp.wait() if wait else cp.start()

        row0 = pl.multiple_of(tile * TILE_SLOTS, TILE_SLOTS)
        go(pltpu.make_async_copy(ys_ref.at[pl.ds(row0, MOE_BASE_ROWS)], buf.at[b, pl.ds(0, MOE_BASE_ROWS)],
                                 sem.at[b]))
        extra = tg_ref[tile] * MOE_GRP - MOE_BASE_ROWS
        for arm in MOE_TAIL_ARMS:
            @pl.when((extra & arm) != 0)
            def _():
                off = pl.multiple_of(MOE_BASE_ROWS + (extra & ~(2 * arm - 1)), MOE_GRP)
                go(pltpu.make_async_copy(ys_ref.at[pl.ds(row0 + off, arm)], buf.at[b, pl.ds(off, arm)], sem.at[b]))

    @pl.when(i == 0)
    def _():
        buf[...] = jnp.zeros_like(buf)
        fetch(0, 0, False)

    @pl.when(i + 1 < N_TILES)
    def _():
        fetch(i + 1, 1 - cur, False)

    fetch(i, cur, True)
    scol = lax.broadcasted_iota(jnp.int32, (TM, TILE_SLOTS), 1)
    slot = slot_ref[...]
    gate = gate_ref[...]
    gmat = jnp.zeros((TM, TILE_SLOTS), F32)
    for kk in range(TOP_K):
        gmat = gmat + jnp.where(scol == slot[:, kk:kk + 1], gate[:, kk:kk + 1], 0.0)
    gmat = gmat.astype(BF16)
    ffn = jnp.dot(gmat, buf[cur].astype(BF16), preferred_element_type=F32)
    g2 = mod_ref[0][:, 5 * D_MODEL:6 * D_MODEL]
    o_ref[...] = x1_ref[...] + g2 * _rms(ffn, nw_ref[3:4, :])


def _combine(tile_cond, tile_groups, x1, slot, gate, mod_l, norm_w_l, ys):
    tok = lambda i, *_: (i, 0)
    return pl.pallas_call(
        _combine_kernel,
        grid_spec=pltpu.PrefetchScalarGridSpec(
            num_scalar_prefetch=2,
            grid=(N_TILES,),
            in_specs=[
                pl.BlockSpec((TM, D_MODEL), tok),
                pl.BlockSpec((TM, TOP_K), tok),
                pl.BlockSpec((TM, 128), tok),
                pl.BlockSpec((1, 1, 6 * D_MODEL), lambda i, c, *_: (c[i], 0, 0)),
                pl.BlockSpec((4, D_MODEL), lambda i, *_: (0, 0)),
                pl.BlockSpec(memory_space=pl.ANY),
            ],
            out_specs=pl.BlockSpec((TM, D_MODEL), tok),
            scratch_shapes=[pltpu.VMEM((2, TILE_SLOTS, D_MODEL), F32), pltpu.SemaphoreType.DMA((2,))],
        ),
        out_shape=jax.ShapeDtypeStruct((N_TOK, D_MODEL), F32),
        compiler_params=pltpu.CompilerParams(
            dimension_semantics=("arbitrary",), vmem_limit_bytes=VMEM_LIMIT),
        name="moe_combine",
    )(tile_cond, tile_groups, x1, slot, gate, mod_l, norm_w_l, ys)


def _pos_embed_2d(rows, dim):
    r = jnp.repeat(jnp.arange(rows, dtype=F32), GRID_W)
    col = jnp.tile(jnp.arange(GRID_W, dtype=F32), rows)
    quarter = dim // 4
    omega = 1.0 / (POS_BASE ** (jnp.arange(quarter, dtype=F32) / quarter))

    def emb(pos):
        ang = pos[:, None] * omega[None, :]
        return jnp.concatenate([jnp.sin(ang), jnp.cos(ang)], axis=-1)

    return jnp.concatenate([emb(r), emb(col)], axis=-1)


def _routing_tables(ri, counts):
    i32 = jnp.int32
    slot = ri[:, TOP_K:2 * TOP_K].reshape(N_TILES, TM, TOP_K)
    cnt = counts.reshape(N_TILES, N_EXPERTS)
    seg = (cnt + SEG_ALIGN - 1) // SEG_ALIGN
    lstart = jnp.cumsum(seg, axis=1) - seg
    tile_groups = jnp.sum(seg, axis=1)
    region = jnp.sum(seg, axis=0)
    nblk = (region + MOE_BLK_GROUPS - 1) // MOE_BLK_GROUPS
    gend = jnp.cumsum(nblk) * MOE_BLK_GROUPS
    gbase = gend - nblk * MOE_BLK_GROUPS
    cum = jnp.cumsum(seg, axis=0) - seg
    g = jnp.arange(MOE_GROUPS + MOE_BLK_GROUPS, dtype=i32)
    e_of = jnp.minimum(jnp.sum(gend[None, :] <= g[:, None], axis=1), N_EXPERTS - 1)
    onehot_e = (e_of[:, None] == jnp.arange(N_EXPERTS)[None, :]).astype(F32)
    pick_e = lambda t: jnp.dot(onehot_e, t.astype(F32), precision=lax.Precision.HIGHEST).astype(i32)
    off = g - pick_e(gbase)
    real = (g < gend[-1]) & (off < pick_e(region))
    cum_e = pick_e(cum.T)
    tile_of = jnp.sum(cum_e <= off[:, None], axis=1) - 1
    onehot_t = tile_of[:, None] == jnp.arange(N_TILES)[None, :]
    pick_t = lambda t: jnp.sum(jnp.where(onehot_t, t, 0), axis=1)
    pos = tile_of * MOE_TILE_GROUPS + pick_t(pick_e(lstart.T)) + off - pick_t(cum_e)
    pad_rank = jnp.cumsum((~real).astype(i32)) - 1
    gsrc = jnp.where(real, pos, MOE_ZERO_GROUP).astype(i32)
    gdst = jnp.where(real, pos, MOE_ZERO_GROUP + jnp.minimum(pad_rank, MOE_PAD_GROUPS - 1)).astype(i32)
    slot_t = jnp.concatenate([slot.transpose(0, 2, 1), jnp.full((1, TOP_K, TM), -1, i32)], axis=0)
    return (slot.reshape(N_TOK, TOP_K).astype(i32), slot_t.astype(i32),
            (nblk.astype(i32), gbase.astype(i32), gsrc, gdst), tile_groups.astype(i32))


def kernel(x_prompt, x_sample, state_s5_re, state_s5_im, state_hgrn, c, c_ctx, ada_w, ada_b, norm_w, w_in,
           s5_a_re, s5_a_im, s5_log_dt, s5_b_re, s5_b_im, s5_c_re, s5_c_im, s5_d, s5_w_glu, s5_b_glu,
           hgrn_lb, hgrn_norm_w, w_out, router_w, router_b, exp_w1, exp_b1, exp_w2, exp_b2):
    pos = _pos_embed_2d(DEC_SEQ // GRID_W, D_MODEL)
    x = jnp.concatenate([x_prompt.reshape(N_PROMPT, D_MODEL),
                         (x_sample + pos[None]).reshape(N_SAMPLE, D_MODEL)], axis=0)
    cond = jnp.concatenate([c_ctx[None, :], c, jnp.zeros((N_COND - 1 - DEC_BATCH, D_MODEL), F32)], axis=0)
    mod = _modulation(cond, ada_w, ada_b).reshape(DEPTH, N_COND, 1, 6 * D_MODEL)
    tiles = jnp.arange(N_TILES, dtype=jnp.int32)
    tile_cond = jnp.where(tiles < N_PROMPT // TM, 0, 1 + (tiles - N_PROMPT // TM) // (DEC_SEQ // TM)).astype(jnp.int32)
    w_in_b = w_in.astype(BF16)
    w_out_b = w_out.astype(BF16)
    w_glu_b = s5_w_glu.astype(BF16)
    router_hi = router_w.astype(BF16)
    router_parts = jnp.stack([router_hi, (router_w - router_hi.astype(F32)).astype(BF16)])

    s5_w = _s5_weights(s5_a_re, s5_a_im, s5_log_dt, s5_b_re, s5_b_im, s5_c_re, s5_c_im)
    fin_re, fin_im, fin_hg = [], [], []
    n = S5_STATE
    for l in range(DEPTH):
        u, qs, v, gs, lf_f, k_f, lf_b, k_b = _in_proj(l, tile_cond, x, mod[l], norm_w[l], w_in_b[l], hgrn_lb)

        h0 = jnp.concatenate([
            state_s5_re[:, l].transpose(2, 0, 1, 3).reshape(S5_GROUPS, DEC_BATCH, 2 * n),
            state_s5_im[:, l].transpose(2, 0, 1, 3).reshape(S5_GROUPS, DEC_BATCH, 2 * n)], axis=-1)
        h0 = h0.reshape(S5_NLT, S5_GPT, S5_NRB - 1, S5_SEQ_PER_RB, S5_NS).transpose(0, 2, 1, 3, 4)
        ycore, hfin = _s5_scan(l, u, *s5_w, h0)
        hfin = hfin.reshape(S5_GROUPS, BATCH, S5_NS)
        fin_re.append(hfin[:, :, 0:2 * n].reshape(S5_GROUPS, BATCH, 2, n).transpose(1, 2, 0, 3))
        fin_im.append(hfin[:, :, 2 * n:].reshape(S5_GROUPS, BATCH, 2, n).transpose(1, 2, 0, 3))

        s0 = jnp.concatenate([jnp.zeros((BATCH, 2, HGRN_HEADS, HGRN_DV, HGRN_DK), F32),
                              jnp.swapaxes(state_hgrn[:, l], -1, -2)], axis=0)
        o_f, o_b, sfin = _hgrn_scan(qs, v, lf_f, k_f, lf_b, k_b, s0)
        fin_hg.append(jnp.swapaxes(sfin[:BATCH], -1, -2))

        x1, h2, ri, gate, counts = _out_proj(
            tile_cond, x, ycore, u, o_f, o_b, gs, mod[l], norm_w[l], s5_d[l].reshape(1, S5_WIDTH),
            w_glu_b[l], s5_b_glu[l].reshape(1, S5_WIDTH), hgrn_norm_w[l].reshape(1, HGRN_DV),
            w_out_b[l], router_parts[:, l], router_b[l].reshape(1, N_EXPERTS))

        slot, slot_t, groups, tile_groups = _routing_tables(ri, counts)
        xs = _dispatch(slot_t, h2)
        ys = _moe_experts(l, *groups, xs, exp_w1, exp_b1, exp_w2, exp_b2)
        x = _combine(tile_cond, tile_groups, x1, slot, gate, mod[l], norm_w[l], ys)

    y_prompt = x[:N_PROMPT].reshape(BATCH, SEQ, D_MODEL)
    y_sample = x[N_PROMPT:].reshape(DEC_BATCH, DEC_SEQ, D_MODEL)
    return (y_prompt, y_sample, jnp.stack(fin_re, axis=1), jnp.stack(fin_im, axis=1),
            jnp.stack(fin_hg, axis=1))
```

```python
import functools
import math

import jax
import jax.numpy as jnp
from jax import lax
from jax.experimental import pallas as pl
from jax.experimental.pallas import tpu as pltpu

F32 = jnp.float32
BF16 = jnp.bfloat16

D_MODEL = 1024
BATCH = 16
SEQ = 256
DEPTH = 2
DEC_BATCH = 4
DEC_SEQ = 2048
GRID_W = 64
S5_WIDTH = 512
S5_GROUP_CH = 16
S5_GROUPS = 32
S5_STATE = 64
HGRN_WIDTH = 512
HGRN_DK = 128
HGRN_HEADS = 4
HGRN_DV = 128
IN_COLS = S5_WIDTH + 5 * HGRN_WIDTH
N_EXPERTS = 32
TOP_K = 4
D_FF = D_MODEL
SWIGLU_LIMIT = 7.0
SWIGLU_ALPHA = 1.702
NORM_EPS = 1e-6
POS_BASE = 10000.0

N_PROMPT = BATCH * SEQ
N_SAMPLE = DEC_BATCH * DEC_SEQ
N_TOK = N_PROMPT + N_SAMPLE
N_SEQ = BATCH + DEC_BATCH
N_COND = 8

TM = 256
N_TILES = N_TOK // TM
S5_T = 16
S5_LT = 128
S5_GPT = S5_LT // S5_GROUP_CH
S5_NLT = S5_WIDTH // S5_LT
S5_RB = N_PROMPT
S5_CR = S5_RB // S5_T
S5_NRB = N_TOK // S5_RB
S5_SEQ_PER_RB = S5_RB // DEC_SEQ
S5_NLAG = 2 * S5_T - 1
S5_NS = 4 * S5_STATE
HC = 128
N_HCHUNK = N_TOK // HC
EXP_CLAMP = 80.0
MOE_BM = 256
N_ASSIGN = N_TOK * TOP_K
SEG_ALIGN = 8
TILE_SLOTS = -(-(TM * TOP_K + N_EXPERTS * (SEG_ALIGN - 1)) // 128) * 128
MOE_BLOCKS = -(-(N_ASSIGN + N_TILES * N_EXPERTS * (SEG_ALIGN - 1) + N_EXPERTS * (MOE_BM - 1)) // MOE_BM)
MOE_GRP = SEG_ALIGN
MOE_TILE_GROUPS = TILE_SLOTS // MOE_GRP
MOE_BLK_GROUPS = MOE_BM // MOE_GRP
MOE_GROUPS = MOE_BLOCKS * MOE_BLK_GROUPS
MOE_ZERO_GROUP = N_TILES * MOE_TILE_GROUPS
MOE_PAD_GROUPS = N_EXPERTS * (MOE_BLK_GROUPS - 1)
MOE_XS_ROWS = (N_TILES + 1) * TILE_SLOTS
MOE_YS_ROWS = N_TILES * TILE_SLOTS + MOE_PAD_GROUPS * MOE_GRP
MOE_BASE_ROWS = TM * TOP_K
MOE_TAIL_ARMS = (128, 64, 32, 16, 8)
VMEM_LIMIT = 56 * 1024 * 1024


def _rms(x, w):
    return x * lax.rsqrt(jnp.mean(x * x, axis=-1, keepdims=True) + NORM_EPS) * w


def _silu(x):
    return x * jax.nn.sigmoid(x)


MOD_TN = 1536


def _mod_kernel(cond_ref, w_ref, b_ref, o_ref):
    s = _silu(cond_ref[...]).astype(BF16)
    o_ref[0] = jnp.dot(s, w_ref[0].astype(BF16), preferred_element_type=F32) + b_ref[0]


def _modulation(cond, ada_w, ada_b):
    return pl.pallas_call(
        _mod_kernel,
        grid=(DEPTH, 6 * D_MODEL // MOD_TN),
        in_specs=[
            pl.BlockSpec((N_COND, D_MODEL), lambda l, j: (0, 0)),
            pl.BlockSpec((1, D_MODEL, MOD_TN), lambda l, j: (l, 0, j)),
            pl.BlockSpec((1, 1, MOD_TN), lambda l, j: (l, 0, j)),
        ],
        out_specs=pl.BlockSpec((1, N_COND, MOD_TN), lambda l, j: (l, 0, j)),
        out_shape=jax.ShapeDtypeStruct((DEPTH, N_COND, 6 * D_MODEL), F32),
        compiler_params=pltpu.CompilerParams(vmem_limit_bytes=VMEM_LIMIT),
        name="adaln_mod",
    )(cond, ada_w, ada_b.reshape(DEPTH, 1, 6 * D_MODEL))


def _in_kernel(layer, cond_ref, x_ref, mod_ref, nw_ref, w_ref, lb_ref,
               u_ref, q_ref, v_ref, g_ref, lff_ref, kf_ref, lfb_ref, kb_ref):
    del cond_ref
    mod = mod_ref[0]
    sh1 = mod[:, 0:D_MODEL]
    sc1 = mod[:, D_MODEL:2 * D_MODEL]
    h = _rms(x_ref[...], nw_ref[0:1, :]) * (1.0 + sc1) + sh1
    z = jnp.dot(h.astype(BF16), w_ref[...], preferred_element_type=F32)
    w = HGRN_WIDTH
    c0 = S5_WIDTH
    u_ref[...] = z[:, 0:c0]
    q_ref[...] = _silu(z[:, c0:c0 + w]) * (HGRN_DK ** -0.5)
    v_ref[...] = z[:, c0 + 3 * w:c0 + 4 * w]
    g_ref[...] = _silu(z[:, c0 + 4 * w:c0 + 5 * w])
    lbp = lb_ref[...]
    e = jnp.exp(lbp - jnp.max(lbp, axis=0, keepdims=True))
    probs = e / jnp.sum(e, axis=0, keepdims=True)
    lb = jnp.sum(probs[0:layer + 1], axis=0) - probs[0]
    for d, (lf_ref, k_ref) in enumerate(((lff_ref, kf_ref), (lfb_ref, kb_ref))):
        f = z[:, c0 + (1 + d) * w:c0 + (2 + d) * w]
        lbd = lb[d:d + 1, :]
        fg = lbd + (1.0 - lbd) * jax.nn.sigmoid(f)
        lf_ref[...] = jnp.log(fg)
        k_ref[...] = 1.0 - fg


def _in_proj(layer, tile_cond, x, mod_l, norm_w_l, w_in_l, hgrn_lb):
    tok = lambda i, c: (i, 0)
    out = jax.ShapeDtypeStruct((N_TOK, HGRN_WIDTH), F32)
    return pl.pallas_call(
        functools.partial(_in_kernel, layer),
        grid_spec=pltpu.PrefetchScalarGridSpec(
            num_scalar_prefetch=1,
            grid=(N_TILES,),
            in_specs=[
                pl.BlockSpec((TM, D_MODEL), tok),
                pl.BlockSpec((1, 1, 6 * D_MODEL), lambda i, c: (c[i], 0, 0)),
                pl.BlockSpec((4, D_MODEL), lambda i, c: (0, 0)),
                pl.BlockSpec((D_MODEL, IN_COLS), lambda i, c: (0, 0)),
                pl.BlockSpec((DEPTH, 2, HGRN_WIDTH), lambda i, c: (0, 0, 0)),
            ],
            out_specs=[pl.BlockSpec((TM, HGRN_WIDTH), tok)] * 8,
        ),
        out_shape=[out] * 8,
        compiler_params=pltpu.CompilerParams(
            dimension_semantics=("arbitrary",), vmem_limit_bytes=VMEM_LIMIT),
        name="in_proj",
    )(tile_cond, x, mod_l, norm_w_l, w_in_l, hgrn_lb)


def _s5_layer_weights(a_re, a_im, log_dt, b_re, b_im, c_re, c_im):
    hp = lax.Precision.HIGHEST
    t = S5_T
    dt = jnp.exp(log_dt)[..., None]
    lam_re = jnp.minimum(a_re, -1e-4)
    lam_im = a_im
    mag = jnp.exp(dt * lam_re)
    ang = dt * lam_im
    ab_re = mag * jnp.cos(ang)
    ab_im = mag * jnp.sin(ang)
    den = lam_re * lam_re + lam_im * lam_im
    nr = ab_re - 1.0
    ni = ab_im
    co_re = (nr * lam_re + ni * lam_im) / den
    co_im = (ni * lam_re - nr * lam_im) / den
    bb_re = co_re[..., None] * b_re - co_im[..., None] * b_im
    bb_im = co_re[..., None] * b_im + co_im[..., None] * b_re
    pr = [jnp.ones_like(ab_re)]
    pi = [jnp.zeros_like(ab_im)]
    for _ in range(t):
        pr.append(pr[-1] * ab_re - pi[-1] * ab_im)
        pi.append(pr[-2] * ab_im + pi[-1] * ab_re)
    pr = jnp.stack(pr, axis=1)
    pi = jnp.stack(pi, axis=1)
    abr = pr[..., None] * bb_re[:, None] - pi[..., None] * bb_im[:, None]
    abi = pr[..., None] * bb_im[:, None] + pi[..., None] * bb_re[:, None]
    kk = jnp.einsum('dgon,dkgni->dkgio', jnp.concatenate([c_re, -c_im], axis=-1),
                    jnp.concatenate([abr, abi], axis=-2), precision=hp)
    kern = (kk[0], kk[1])
    abs_ = ((abr[0], abi[0]), (abr[1], abi[1]))
    pows = ((pr[0], pi[0]), (pr[1], pi[1]))
    p = S5_GROUP_CH
    lags = jnp.concatenate([kern[1][t - 1:0:-1], (kern[0][0] + kern[1][0])[None], kern[0][1:t]], axis=0)
    m = lags.reshape(S5_NLAG, S5_NLT, S5_GPT, p, p)
    bd = jnp.einsum('ltaio,ab->ltaibo', m, jnp.eye(S5_GPT, dtype=F32))
    wcat = bd.reshape(S5_NLAG, S5_NLT, S5_LT, S5_LT).transpose(1, 2, 0, 3).reshape(S5_NLT, S5_LT, S5_NLAG * S5_LT)
    (abr_f, abi_f), (abr_b, abi_b) = abs_
    parts = (abr_f[t - 1::-1], abr_b[:t], abi_f[t - 1::-1], abi_b[:t])
    wst = jnp.concatenate([x.transpose(1, 0, 3, 2) for x in parts], axis=-1)
    wst = wst.reshape(S5_NLT, S5_GPT, t, p, S5_NS).transpose(0, 2, 1, 3, 4).reshape(S5_NLT, t * S5_LT, S5_NS)
    (pr_f, pi_f), (pr_b, pi_b) = pows

    def out_rows(cr, ci, pr_, pi_):
        re_rows = cr[None] * pr_[:, :, None, :] - ci[None] * pi_[:, :, None, :]
        im_rows = -(cr[None] * pi_[:, :, None, :] + ci[None] * pr_[:, :, None, :])
        return re_rows.transpose(1, 3, 0, 2), im_rows.transpose(1, 3, 0, 2)

    fre, fim = out_rows(c_re[0], c_im[0], pr_f[1:t + 1], pi_f[1:t + 1])
    bre, bim = out_rows(c_re[1], c_im[1], pr_b[t:0:-1], pi_b[t:0:-1])
    wout = lax.optimization_barrier(jnp.concatenate([fre, bre, fim, bim], axis=1))
    wdense = (wout.reshape(S5_NLT, S5_GPT, S5_NS, t, p).transpose(0, 2, 3, 1, 4)
              .reshape(S5_NLT, S5_NS, t * S5_LT))
    at = jnp.stack([jnp.concatenate([pr_f[t], pr_b[t]], axis=-1),
                    jnp.concatenate([pi_f[t], pi_b[t]], axis=-1)], axis=1)
    return (wcat.astype(BF16), wst.astype(BF16), wdense.astype(BF16),
            at.reshape(S5_NLT, S5_GPT, 2, 2 * S5_STATE))


def _s5_weights(*params):
    return jax.vmap(_s5_layer_weights)(*params)


def _s5_kernel(u_ref, wcat_ref, wst_ref, wd_ref, at_ref, h0_ref, y_ref, hfin_ref,
               wbig, dh_scr, hf_scr, hb_scr):
    t = S5_T
    n2 = 2 * S5_STATE
    r = pl.program_id(1)

    @pl.when(r == 0)
    def _():
        for s in range(t):
            wbig[s * S5_LT:(s + 1) * S5_LT, :] = wcat_ref[0, :, (t - 1 - s) * S5_LT:(2 * t - 1 - s) * S5_LT]

    xcat = jnp.concatenate([u_ref[pl.ds(s, S5_CR, stride=t), :].astype(BF16) for s in range(t)], axis=-1)
    yacc = jnp.dot(xcat, wbig[...], preferred_element_type=F32)
    lane_k = lax.broadcasted_iota(jnp.int32, (1, t * S5_LT), 1)
    grp_k = (lane_k % S5_LT) // S5_GROUP_CH
    for gi in range(S5_GPT):
        xg = jnp.where(grp_k == gi, xcat, jnp.zeros_like(xcat))
        dh = jnp.dot(xg, wst_ref[0], preferred_element_type=F32)
        dh_scr[gi, 0] = dh[:, 0:n2]
        dh_scr[gi, 1] = dh[:, n2:]

    lane = lax.broadcasted_iota(jnp.int32, (1, n2), 1)
    fwd_lane = lane < S5_STATE
    are = [at_ref[0, gi, 0:1, :] for gi in range(S5_GPT)]
    aim = [at_ref[0, gi, 1:2, :] for gi in range(S5_GPT)]

    def advance(gi, hre, him, dre, dim):
        return (are[gi] * hre - aim[gi] * him + dre, are[gi] * him + aim[gi] * hre + dim)

    @pl.when(r == 0)
    def _():
        nc = SEQ // t
        for gi in range(S5_GPT):
            hre = jnp.zeros((BATCH, n2), F32)
            him = jnp.zeros((BATCH, n2), F32)
            for s in range(nc):
                rows_f = pl.ds(s, BATCH, stride=nc)
                rows_b = pl.ds(nc - 1 - s, BATCH, stride=nc)
                hf_scr[gi, 0, rows_f, :] = hre
                hf_scr[gi, 1, rows_f, :] = him
                hb_scr[gi, 0, rows_b, :] = hre
                hb_scr[gi, 1, rows_b, :] = him
                dre = jnp.where(fwd_lane, dh_scr[gi, 0, rows_f, :], dh_scr[gi, 0, rows_b, :])
                dim = jnp.where(fwd_lane, dh_scr[gi, 1, rows_f, :], dh_scr[gi, 1, rows_b, :])
                hre, him = advance(gi, hre, him, dre, dim)
            hfin_ref[0, gi] = jnp.concatenate([hre, him], axis=-1)

    @pl.when(r > 0)
    def _():
        nc = DEC_SEQ // t
        nb = S5_SEQ_PER_RB
        init = tuple((h0_ref[0, 0, gi, b:b + 1, 0:n2], h0_ref[0, 0, gi, b:b + 1, n2:])
                     for gi in range(S5_GPT) for b in range(nb))

        def step(o, carry):
            new = []
            for gi in range(S5_GPT):
                for b in range(nb):
                    hre, him = carry[gi * nb + b]
                    rf = pl.multiple_of(b * nc + o * 8, 8)
                    rb = pl.multiple_of(b * nc + nc - 8 - o * 8, 8)
                    dfr = dh_scr[gi, 0, pl.ds(rf, 8), :]
                    dfi = dh_scr[gi, 1, pl.ds(rf, 8), :]
                    dbr = dh_scr[gi, 0, pl.ds(rb, 8), :]
                    dbi = dh_scr[gi, 1, pl.ds(rb, 8), :]
                    ent_re, ent_im = [], []
                    for i in range(8):
                        ent_re.append(hre)
                        ent_im.append(him)
                        dre = jnp.where(fwd_lane, dfr[i:i + 1], dbr[7 - i:8 - i])
                        dim = jnp.where(fwd_lane, dfi[i:i + 1], dbi[7 - i:8 - i])
                        hre, him = advance(gi, hre, him, dre, dim)
                    hf_scr[gi, 0, pl.ds(rf, 8), :] = jnp.concatenate(ent_re, axis=0)
                    hf_scr[gi, 1, pl.ds(rf, 8), :] = jnp.concatenate(ent_im, axis=0)
                    hb_scr[gi, 0, pl.ds(rb, 8), :] = jnp.concatenate(ent_re[::-1], axis=0)
                    hb_scr[gi, 1, pl.ds(rb, 8), :] = jnp.concatenate(ent_im[::-1], axis=0)
                    new.append((hre, him))
            return tuple(new)

        lax.fori_loop(0, nc // 8, step, init)

    for gi in range(S5_GPT):
        hent = jnp.concatenate([jnp.where(fwd_lane, hf_scr[gi, 0], hb_scr[gi, 0]),
                                jnp.where(fwd_lane, hf_scr[gi, 1], hb_scr[gi, 1])], axis=-1).astype(BF16)
        yi = jnp.dot(hent, wd_ref[0], preferred_element_type=F32)
        yacc = yacc + jnp.where(grp_k == gi, yi, 0.0)
    for s in range(t):
        y_ref[pl.ds(s, S5_CR, stride=t), :] = yacc[:, s * S5_LT:(s + 1) * S5_LT]


def _s5_scan(layer, u, wcat, wst, wdense, at, h0):
    tile = lambda j, r: (layer * S5_NLT + j, 0, 0)
    merge = lambda w: w.reshape((DEPTH * S5_NLT,) + w.shape[2:])
    wcat, wst, wdense, at = merge(wcat), merge(wst), merge(wdense), merge(at)
    return pl.pallas_call(
        _s5_kernel,
        grid=(S5_NLT, S5_NRB),
        in_specs=[
            pl.BlockSpec((S5_RB, S5_LT), lambda j, r: (r, j)),
            pl.BlockSpec((1, S5_LT, S5_NLAG * S5_LT), tile),
            pl.BlockSpec((1, S5_T * S5_LT, S5_NS), tile),
            pl.BlockSpec((1, S5_NS, S5_T * S5_LT), tile),
            pl.BlockSpec((1, S5_GPT, 2, 2 * S5_STATE), lambda j, r: (layer * S5_NLT + j, 0, 0, 0)),
            pl.BlockSpec((1, 1, S5_GPT, S5_SEQ_PER_RB, S5_NS), lambda j, r: (j, jnp.maximum(r - 1, 0), 0, 0, 0)),
        ],
        out_specs=[
            pl.BlockSpec((S5_RB, S5_LT), lambda j, r: (r, j)),
            pl.BlockSpec((1, S5_GPT, BATCH, S5_NS), lambda j, r: (j, 0, 0, 0)),
        ],
        out_shape=[
            jax.ShapeDtypeStruct((N_TOK, S5_WIDTH), F32),
            jax.ShapeDtypeStruct((S5_NLT, S5_GPT, BATCH, S5_NS), F32),
        ],
        scratch_shapes=[pltpu.VMEM((S5_T * S5_LT, S5_T * S5_LT), BF16)]
                       + [pltpu.VMEM((S5_GPT, 2, S5_CR, 2 * S5_STATE), F32)] * 3,
        compiler_params=pltpu.CompilerParams(
            dimension_semantics=("arbitrary", "arbitrary"), vmem_limit_bytes=VMEM_LIMIT),
        name="s5_scan",
    )(u, wcat, wst, wdense, at, h0)


def _split3(x):
    hi = x.astype(BF16)
    r1 = x - hi.astype(F32)
    mid = r1.astype(BF16)
    lo = (r1 - mid.astype(F32)).astype(BF16)
    return hi, mid, lo


def _piecewise_rows(b, blk, row_in_blk):
    parts = []
    for j in range(HC // blk):
        r = j * blk + row_in_blk
        parts.append(jnp.broadcast_to(b[r:r + 1, :], (blk, b.shape[1])))
    return parts[0] if len(parts) == 1 else jnp.concatenate(parts, axis=0)


def _nt(a, b):
    return lax.dot_general(a, b, (((1,), (1,)), ((), ())), preferred_element_type=F32)


def _tn(a, b):
    return lax.dot_general(a, b, (((0,), (0,)), ((), ())), preferred_element_type=F32)


def _hgrn_dir(reverse, q_ref, v_ref, lf_ref, k_ref, o_ref, st_ref):
    row = lax.broadcasted_iota(jnp.int32, (HC, HC), 0)
    col = lax.broadcasted_iota(jnp.int32, (HC, HC), 1)
    causal = (col >= row) if reverse else (col <= row)
    tri = jnp.where(causal, 1.0, 0.0).astype(BF16)
    lf = lf_ref[...]
    hi, mid, lo = _split3(lf)
    ball = (jnp.dot(tri, hi, preferred_element_type=F32) + jnp.dot(tri, mid, preferred_element_type=F32)
            + jnp.dot(tri, lo, preferred_element_type=F32))
    last = 0 if reverse else HC - 1
    masks = []
    for blk in (128, 64, 32):
        half = blk // 2
        same = (row // blk) == (col // blk)
        t_hi = (row % blk) >= half
        s_hi = (col % blk) >= half
        if reverse:
            masks.append(same & jnp.logical_not(t_hi) & s_hi)
        else:
            masks.append(same & t_hi & jnp.logical_not(s_hi))
    diag_mask = ((row // 16) == (col // 16)) & causal
    for h in range(HGRN_HEADS):
        sl = slice(h * HGRN_DK, (h + 1) * HGRN_DK)
        b = ball[:, sl]
        q = q_ref[:, sl]
        k = k_ref[:, sl]
        v = v_ref[:, sl].astype(BF16)
        st = st_ref[h]
        b_last = b[last:last + 1, :]
        q_in = (q * jnp.exp(b)).astype(BF16)
        k_in = (k * jnp.exp(b_last - b)).astype(BF16)
        o = _nt(q_in, st.astype(BF16))
        st_ref[h] = jnp.exp(b_last) * st + _tn(v, k_in)
        scores = jnp.zeros((HC, HC), F32)
        for blk, mask in zip((128, 64, 32), masks):
            half = blk // 2
            zero = jnp.zeros((half, HGRN_DK), BF16)
            qparts, kparts = [], []
            for j in range(HC // blk):
                early = slice(j * blk, j * blk + half)
                late = slice(j * blk + half, (j + 1) * blk)
                if reverse:
                    m = b[j * blk + half:j * blk + half + 1, :]
                    qrows, krows = early, late
                else:
                    m = b[j * blk + half - 1:j * blk + half, :]
                    qrows, krows = late, early
                qe = (q[qrows] * jnp.exp(b[qrows] - m)).astype(BF16)
                ke = (k[krows] * jnp.exp(m - b[krows])).astype(BF16)
                qparts += [qe, zero] if reverse else [zero, qe]
                kparts += [zero, ke] if reverse else [ke, zero]
            part = _nt(jnp.concatenate(qparts, axis=0), jnp.concatenate(kparts, axis=0))
            scores = scores + (part if blk == HC else jnp.where(mask, part, 0.0))
        m = _piecewise_rows(b, 16, 8 if reverse else 7)
        qd = (q * jnp.exp(jnp.minimum(b - m, EXP_CLAMP))).astype(BF16)
        kd = (k * jnp.exp(jnp.minimum(m - b, EXP_CLAMP))).astype(BF16)
        scores = scores + jnp.where(diag_mask, _nt(qd, kd), 0.0)
        o_ref[:, sl] = o + jnp.dot(scores.astype(BF16), v, preferred_element_type=F32)


def _hgrn_kernel(cf_ref, cb_ref, seq_ref, first_ref, last_ref,
                 qf_ref, vf_ref, lff_ref, kf_ref, qb_ref, vb_ref, lfb_ref, kb_ref, s0_ref,
                 of_ref, ob_ref, sfin_ref, st_scr):
    del cf_ref, cb_ref, seq_ref
    j = pl.program_id(0)

    @pl.when(first_ref[j] == 1)
    def _():
        st_scr[...] = s0_ref[0]

    _hgrn_dir(False, qf_ref, vf_ref, lff_ref, kf_ref, of_ref, st_scr.at[0])
    _hgrn_dir(True, qb_ref, vb_ref, lfb_ref, kb_ref, ob_ref, st_scr.at[1])

    @pl.when(last_ref[j] == 1)
    def _():
        sfin_ref[0] = st_scr[...]


def _hgrn_tables():
    cf, cb, sq, first, last = [], [], [], [], []
    base = 0
    for s in range(N_SEQ):
        nc = (SEQ if s < BATCH else DEC_SEQ) // HC
        for t in range(nc):
            cf.append(base + t)
            cb.append(base + nc - 1 - t)
            sq.append(s)
            first.append(int(t == 0))
            last.append(int(t == nc - 1))
        base += nc
    return tuple(jnp.asarray(x, jnp.int32) for x in (cf, cb, sq, first, last))


def _hgrn_scan(qs, v, lf_f, k_f, lf_b, k_b, s0):
    fwd = lambda j, cf, cb, sq, fi, la: (cf[j], 0)
    bwd = lambda j, cf, cb, sq, fi, la: (cb[j], 0)
    seq = lambda j, cf, cb, sq, fi, la: (sq[j], 0, 0, 0, 0)
    tile = (HC, HGRN_WIDTH)
    sblk = (1, 2, HGRN_HEADS, HGRN_DV, HGRN_DK)
    out = jax.ShapeDtypeStruct((N_TOK, HGRN_WIDTH), F32)
    return pl.pallas_call(
        _hgrn_kernel,
        grid_spec=pltpu.PrefetchScalarGridSpec(
            num_scalar_prefetch=5,
            grid=(N_HCHUNK,),
            in_specs=[pl.BlockSpec(tile, fwd)] * 4 + [pl.BlockSpec(tile, bwd)] * 4
                     + [pl.BlockSpec(sblk, seq)],
            out_specs=[pl.BlockSpec(tile, fwd), pl.BlockSpec(tile, bwd), pl.BlockSpec(sblk, seq)],
            scratch_shapes=[pltpu.VMEM(sblk[1:], F32)],
        ),
        out_shape=[out, out, jax.ShapeDtypeStruct((N_SEQ,) + sblk[1:], F32)],
        compiler_params=pltpu.CompilerParams(
            dimension_semantics=("arbitrary",), vmem_limit_bytes=VMEM_LIMIT),
        name="hgrn_scan",
    )(*_hgrn_tables(), qs, v, lf_f, k_f, qs, v, lf_b, k_b, s0)


def _gelu_tanh(x):
    return 0.5 * x * (1.0 + jnp.tanh(math.sqrt(2.0 / math.pi) * (x + 0.044715 * (x * x * x))))


def _out_kernel(cond_ref, x_ref, yc_ref, u_ref, of_ref, ob_ref, g_ref, mod_ref, nw_ref, d_ref,
                wglu_ref, bglu_ref, hnw_ref, wout_ref, rw_ref, rb_ref,
                x1_ref, h2_ref, ri_ref, rg_ref, cnt_ref):
    del cond_ref
    mod = mod_ref[0]
    g1 = mod[:, 2 * D_MODEL:3 * D_MODEL]
    sh2 = mod[:, 3 * D_MODEL:4 * D_MODEL]
    sc2 = mod[:, 4 * D_MODEL:5 * D_MODEL]
    y = _gelu_tanh(yc_ref[...] + d_ref[...] * u_ref[...])
    y_s5 = y * jax.nn.sigmoid(jnp.dot(y.astype(BF16), wglu_ref[...], preferred_element_type=F32) + bglu_ref[...])
    o = of_ref[...] + ob_ref[...]
    gs = g_ref[...]
    heads = []
    for h in range(HGRN_HEADS):
        sl = slice(h * HGRN_DV, (h + 1) * HGRN_DV)
        heads.append(_rms(o[:, sl], hnw_ref[...]) * gs[:, sl])
    y_hg = jnp.concatenate(heads, axis=-1)
    mix = (jnp.dot(y_s5.astype(BF16), wout_ref[0:S5_WIDTH, :], preferred_element_type=F32)
           + jnp.dot(y_hg.astype(BF16), wout_ref[S5_WIDTH:, :], preferred_element_type=F32))
    x1 = x_ref[...] + g1 * _rms(mix, nw_ref[1:2, :])
    x1_ref[...] = x1
    h2 = _rms(x1, nw_ref[2:3, :]) * (1.0 + sc2) + sh2
    h2_hi = h2.astype(BF16)
    h2_ref[...] = h2_hi
    h2_lo = (h2 - h2_hi.astype(F32)).astype(BF16)
    logits = rb_ref[...]
    for a in (h2_hi, h2_lo):
        for part in range(2):
            logits = logits + jnp.dot(a, rw_ref[part], preferred_element_type=F32)
    eidx = lax.broadcasted_iota(jnp.int32, (TM, N_EXPERTS), 1).astype(F32)
    vals = logits
    top_v, top_i, onehots = [], [], []
    for _ in range(TOP_K):
        mx = jnp.max(vals, axis=-1, keepdims=True)
        ix = jnp.min(jnp.where(vals == mx, eidx, float(N_EXPERTS)), axis=-1, keepdims=True)
        sel = eidx == ix
        top_v.append(mx)
        top_i.append(ix)
        onehots.append(sel)
        vals = jnp.where(sel, -jnp.inf, vals)
    ex = [jnp.exp(tv - top_v[0]) for tv in top_v]
    den = ex[0] + ex[1] + ex[2] + ex[3]
    tot = jnp.zeros((TM, N_EXPERTS), F32)
    for sel in onehots:
        tot = tot + jnp.where(sel, 1.0, 0.0)
    r_t = lax.broadcasted_iota(jnp.int32, (TM, TM), 0)
    r_s = lax.broadcasted_iota(jnp.int32, (TM, TM), 1)
    strict = jnp.where(r_s < r_t, 1.0, 0.0).astype(BF16)
    before = jnp.dot(strict, tot.astype(BF16), preferred_element_type=F32)
    cnt = jnp.sum(tot, axis=0, keepdims=True)
    seg = jnp.floor((cnt + (SEG_ALIGN - 1)) * (1.0 / SEG_ALIGN)) * SEG_ALIGN
    e_r = lax.broadcasted_iota(jnp.int32, (N_EXPERTS, N_EXPERTS), 0)
    e_c = lax.broadcasted_iota(jnp.int32, (N_EXPERTS, N_EXPERTS), 1)
    lstart = jnp.dot(seg, jnp.where(e_r < e_c, 1.0, 0.0), precision=lax.Precision.HIGHEST,
                     preferred_element_type=F32)
    before = before + lstart
    lane = lax.broadcasted_iota(jnp.int32, (TM, 128), 1)
    ri = jnp.zeros((TM, 128), F32)
    rg = jnp.zeros((TM, 128), F32)
    for kk in range(TOP_K):
        rank = jnp.sum(jnp.where(onehots[kk], before, 0.0), axis=-1, keepdims=True)
        ri = jnp.where(lane == kk, top_i[kk], ri)
        ri = jnp.where(lane == TOP_K + kk, rank, ri)
        rg = jnp.where(lane == kk, ex[kk] / den, rg)
    ri_ref[...] = ri.astype(jnp.int32)
    rg_ref[...] = rg
    cnt_ref[0] = cnt.astype(jnp.int32)


def _out_proj(tile_cond, x, ycore, u, o_f, o_b, gs, mod_l, norm_w_l, s5_d_l, wglu_l, bglu_l, hnw_l,
              wout_l, rw_l, rb_l):
    tok = lambda i, c: (i, 0)
    full2 = lambda i, c: (0, 0)
    half = pl.BlockSpec((TM, HGRN_WIDTH), tok)
    wide = pl.BlockSpec((TM, D_MODEL), tok)
    return pl.pallas_call(
        _out_kernel,
        grid_spec=pltpu.PrefetchScalarGridSpec(
            num_scalar_prefetch=1,
            grid=(N_TILES,),
            in_specs=[
                wide, half, half, half, half, half,
                pl.BlockSpec((1, 1, 6 * D_MODEL), lambda i, c: (c[i], 0, 0)),
                pl.BlockSpec((4, D_MODEL), full2),
                pl.BlockSpec((1, S5_WIDTH), full2),
                pl.BlockSpec((S5_WIDTH, S5_WIDTH), full2),
                pl.BlockSpec((1, S5_WIDTH), full2),
                pl.BlockSpec((1, HGRN_DV), full2),
                pl.BlockSpec((D_MODEL, D_MODEL), full2),
                pl.BlockSpec((2, D_MODEL, N_EXPERTS), lambda i, c: (0, 0, 0)),
                pl.BlockSpec((1, N_EXPERTS), full2),
            ],
            out_specs=[wide, wide, pl.BlockSpec((TM, 128), tok), pl.BlockSpec((TM, 128), tok),
                       pl.BlockSpec((1, 1, N_EXPERTS), lambda i, c: (i, 0, 0))],
        ),
        out_shape=[
            jax.ShapeDtypeStruct((N_TOK, D_MODEL), F32),
            jax.ShapeDtypeStruct((N_TOK, D_MODEL), BF16),
            jax.ShapeDtypeStruct((N_TOK, 128), jnp.int32),
            jax.ShapeDtypeStruct((N_TOK, 128), F32),
            jax.ShapeDtypeStruct((N_TILES, 1, N_EXPERTS), jnp.int32),
        ],
        compiler_params=pltpu.CompilerParams(
            dimension_semantics=("arbitrary",), vmem_limit_bytes=VMEM_LIMIT),
        name="out_proj_router",
    )(tile_cond, x, ycore, u, o_f, o_b, gs, mod_l, norm_w_l, s5_d_l, wglu_l, bglu_l, hnw_l,
      wout_l, rw_l, rb_l)


def _dispatch_kernel(slot_ref, h_ref, o_ref):
    srow = lax.broadcasted_iota(jnp.int32, (TILE_SLOTS, TM), 0)
    perm = jnp.zeros((TILE_SLOTS, TM), F32)
    for kk in range(TOP_K):
        perm = perm + jnp.where(srow == slot_ref[0, kk:kk + 1, :], 1.0, 0.0)
    o_ref[...] = jnp.dot(perm.astype(BF16), h_ref[...], preferred_element_type=F32)


def _dispatch(slot_t, h2):
    return pl.pallas_call(
        _dispatch_kernel,
        grid=(N_TILES + 1,),
        in_specs=[pl.BlockSpec((1, TOP_K, TM), lambda i: (i, 0, 0)),
                  pl.BlockSpec((TM, D_MODEL), lambda i: (jnp.minimum(i, N_TILES - 1), 0))],
        out_specs=pl.BlockSpec((TILE_SLOTS, D_MODEL), lambda i: (i, 0)),
        out_shape=jax.ShapeDtypeStruct((MOE_XS_ROWS, D_MODEL), F32),
        compiler_params=pltpu.CompilerParams(
            dimension_semantics=("arbitrary",), vmem_limit_bytes=VMEM_LIMIT),
        name="moe_dispatch",
    )(slot_t, h2)


def _moe_kernel(nblk_ref, gbase_ref, gsrc_ref, gdst_ref, w1_ref, b1_ref, w2_ref, b2_ref, xs_ref, ys_ref,
                xbuf, obuf, w1b, w2b, sem_in, sem_out):
    e = pl.program_id(0)
    nb = nblk_ref[e]
    g0 = gbase_ref[e]

    def gather(b, buf):
        return [pltpu.make_async_copy(
            xs_ref.at[pl.ds(pl.multiple_of(gsrc_ref[g0 + b * MOE_BLK_GROUPS + g] * MOE_GRP, MOE_GRP), MOE_GRP)],
            xbuf.at[buf, pl.ds(g * MOE_GRP, MOE_GRP)], sem_in.at[buf]) for g in range(MOE_BLK_GROUPS)]

    def scatter(b, buf):
        return [pltpu.make_async_copy(
            obuf.at[buf, pl.ds(g * MOE_GRP, MOE_GRP)],
            ys_ref.at[pl.ds(pl.multiple_of(gdst_ref[g0 + b * MOE_BLK_GROUPS + g] * MOE_GRP, MOE_GRP), MOE_GRP)],
            sem_out.at[buf]) for g in range(MOE_BLK_GROUPS)]

    @pl.when(nb > 0)
    def _():
        def cast(j, c):
            r = pl.multiple_of(j * 128, 128)
            w1b[pl.ds(r, 128), :] = w1_ref[0, pl.ds(r, 128), :].astype(BF16)
            w2b[pl.ds(r, 128), :] = w2_ref[0, pl.ds(r, 128), :].astype(BF16)
            return c

        for cp in gather(0, 0):
            cp.start(priority=1)
        lax.fori_loop(0, D_MODEL // 128, cast, 0)

        def block(b, c):
            buf = b % 2

            @pl.when(b >= 2)
            def _():
                for cp in scatter(b - 2, buf):
                    cp.wait()

            for cp in gather(b, buf):
                cp.wait()
            for cp in gather(b + 1, 1 - buf):
                cp.start(priority=1)
            h = jnp.dot(xbuf[buf].astype(BF16), w1b[...], preferred_element_type=F32) + b1_ref[0]
            glu = jnp.minimum(h[:, :D_FF], SWIGLU_LIMIT)
            lin = jnp.clip(h[:, D_FF:], -SWIGLU_LIMIT, SWIGLU_LIMIT)
            act = glu * jax.nn.sigmoid(SWIGLU_ALPHA * glu) * (lin + 1.0)
            obuf[buf] = jnp.dot(act.astype(BF16), w2b[...], preferred_element_type=F32) + b2_ref[0]
            for cp in scatter(b, buf):
                cp.start(priority=1)
            return c

        lax.fori_loop(0, nb, block, 0)
        for cp in gather(nb, nb % 2):
            cp.wait()

        @pl.when(nb >= 2)
        def _():
            for cp in scatter(nb - 2, nb % 2):
                cp.wait()

        for cp in scatter(nb - 1, (nb - 1) % 2):
            cp.wait()


def _moe_experts(layer, nblk, gbase, gsrc, gdst, xs, w1, b1, w2, b2):
    exp3 = lambda e, *_: (layer * N_EXPERTS + e, 0, 0)
    w1 = w1.reshape(DEPTH * N_EXPERTS, D_MODEL, 2 * D_FF)
    w2 = w2.reshape(DEPTH * N_EXPERTS, D_FF, D_MODEL)
    return pl.pallas_call(
        _moe_kernel,
        grid_spec=pltpu.PrefetchScalarGridSpec(
            num_scalar_prefetch=4,
            grid=(N_EXPERTS,),
            in_specs=[
                pl.BlockSpec((1, D_MODEL, 2 * D_FF), exp3),
                pl.BlockSpec((1, 1, 2 * D_FF), exp3),
                pl.BlockSpec((1, D_FF, D_MODEL), exp3),
                pl.BlockSpec((1, 1, D_MODEL), exp3),
                pl.BlockSpec(memory_space=pl.ANY),
            ],
            out_specs=pl.BlockSpec(memory_space=pl.ANY),
            scratch_shapes=[pltpu.VMEM((2, MOE_BM, D_MODEL), F32), pltpu.VMEM((2, MOE_BM, D_MODEL), F32),
                            pltpu.VMEM((D_MODEL, 2 * D_FF), BF16), pltpu.VMEM((D_FF, D_MODEL), BF16),
                            pltpu.SemaphoreType.DMA((2,)), pltpu.SemaphoreType.DMA((2,))],
        ),
        out_shape=jax.ShapeDtypeStruct((MOE_YS_ROWS, D_MODEL), F32),
        compiler_params=pltpu.CompilerParams(
            dimension_semantics=("arbitrary",), vmem_limit_bytes=VMEM_LIMIT),
        name="moe_experts",
    )(nblk, gbase, gsrc, gdst, w1, b1.reshape(DEPTH * N_EXPERTS, 1, 2 * D_FF), w2,
      b2.reshape(DEPTH * N_EXPERTS, 1, D_MODEL), xs)


def _combine_kernel(split, cond_ref, tg_ref, x1_ref, slot_ref, gate_ref, mod_ref, nw_ref, ys_ref, *rest):
    del cond_ref
    outs, (buf, sem) = rest[:-2], rest[-2:]
    i = pl.program_id(0)
    cur = i % 2

    def fetch(tile, b, wait):
        def go(cp):
            cp.wait() if wait else cp.start()

        row0 = pl.multiple_of(tile * TILE_SLOTS, TILE_SLOTS)
        go(pltpu.make_async_copy(ys_ref.at[pl.ds(row0, MOE_BASE_ROWS)], buf.at[b, pl.ds(0, MOE_BASE_ROWS)],
                                 sem.at[b]))
        extra = tg_ref[tile] * MOE_GRP - MOE_BASE_ROWS
        for arm in MOE_TAIL_ARMS:
            @pl.when((extra & arm) != 0)
            def _():
                off = pl.multiple_of(MOE_BASE_ROWS + (extra & ~(2 * arm - 1)), MOE_GRP)
                go(pltpu.make_async_copy(ys_ref.at[pl.ds(row0 + off, arm)], buf.at[b, pl.ds(off, arm)], sem.at[b]))

    @pl.when(i == 0)
    def _():
        buf[...] = jnp.zeros_like(buf)
        fetch(0, 0, False)

    @pl.when(i + 1 < N_TILES)
    def _():
        fetch(i + 1, 1 - cur, False)

    fetch(i, cur, True)
    scol = lax.broadcasted_iota(jnp.int32, (TM, TILE_SLOTS), 1)
    slot = slot_ref[...]
    gate = gate_ref[...]
    gmat = jnp.zeros((TM, TILE_SLOTS), F32)
    for kk in range(TOP_K):
        gmat = gmat + jnp.where(scol == slot[:, kk:kk + 1], gate[:, kk:kk + 1], 0.0)
    gmat = gmat.astype(BF16)
    ffn = jnp.dot(gmat, buf[cur].astype(BF16), preferred_element_type=F32)
    g2 = mod_ref[0][:, 5 * D_MODEL:6 * D_MODEL]
    out = x1_ref[...] + g2 * _rms(ffn, nw_ref[3:4, :])
    if split:
        @pl.when(i < N_PROMPT // TM)
        def _():
            outs[0][...] = out

        @pl.when(i >= N_PROMPT // TM)
        def _():
            outs[1][...] = out
    else:
        outs[0][...] = out


def _combine(split, tile_cond, tile_groups, x1, slot, gate, mod_l, norm_w_l, ys):
    tok = lambda i, *_: (i, 0)
    if split:
        np_tiles = N_PROMPT // TM
        out_specs = [pl.BlockSpec((TM, D_MODEL), lambda i, *_: (jnp.minimum(i, np_tiles - 1), 0)),
                     pl.BlockSpec((TM, D_MODEL), lambda i, *_: (jnp.maximum(i - np_tiles, 0), 0))]
        out_shape = [jax.ShapeDtypeStruct((N_PROMPT, D_MODEL), F32), jax.ShapeDtypeStruct((N_SAMPLE, D_MODEL), F32)]
    else:
        out_specs = pl.BlockSpec((TM, D_MODEL), tok)
        out_shape = jax.ShapeDtypeStruct((N_TOK, D_MODEL), F32)
    return pl.pallas_call(
        functools.partial(_combine_kernel, split),
        grid_spec=pltpu.PrefetchScalarGridSpec(
            num_scalar_prefetch=2,
            grid=(N_TILES,),
            in_specs=[
                pl.BlockSpec((TM, D_MODEL), tok),
                pl.BlockSpec((TM, TOP_K), tok),
                pl.BlockSpec((TM, 128), tok),
                pl.BlockSpec((1, 1, 6 * D_MODEL), lambda i, c, *_: (c[i], 0, 0)),
                pl.BlockSpec((4, D_MODEL), lambda i, *_: (0, 0)),
                pl.BlockSpec(memory_space=pl.ANY),
            ],
            out_specs=out_specs,
            scratch_shapes=[pltpu.VMEM((2, TILE_SLOTS, D_MODEL), F32), pltpu.SemaphoreType.DMA((2,))],
        ),
        out_shape=out_shape,
        compiler_params=pltpu.CompilerParams(
            dimension_semantics=("arbitrary",), vmem_limit_bytes=VMEM_LIMIT),
        name="moe_combine",
    )(tile_cond, tile_groups, x1, slot, gate, mod_l, norm_w_l, ys)


def _pos_embed_2d(rows, dim):
    r = jnp.repeat(jnp.arange(rows, dtype=F32), GRID_W)
    col = jnp.tile(jnp.arange(GRID_W, dtype=F32), rows)
    quarter = dim // 4
    omega = 1.0 / (POS_BASE ** (jnp.arange(quarter, dtype=F32) / quarter))

    def emb(pos):
        ang = pos[:, None] * omega[None, :]
        return jnp.concatenate([jnp.sin(ang), jnp.cos(ang)], axis=-1)

    return jnp.concatenate([emb(r), emb(col)], axis=-1)


def _assemble_kernel(xp_ref, xs_ref, pos_ref, o_ref):
    i = pl.program_id(0)

    @pl.when(i < N_PROMPT // TM)
    def _():
        o_ref[...] = xp_ref[...]

    @pl.when(i >= N_PROMPT // TM)
    def _():
        o_ref[...] = xs_ref[...] + pos_ref[...]


def _assemble(xp, xs, pos):
    np_tiles = N_PROMPT // TM
    return pl.pallas_call(
        _assemble_kernel,
        grid=(N_TILES,),
        in_specs=[pl.BlockSpec((TM, D_MODEL), lambda i: (jnp.minimum(i, np_tiles - 1), 0)),
                  pl.BlockSpec((TM, D_MODEL), lambda i: (jnp.maximum(i - np_tiles, 0), 0)),
                  pl.BlockSpec((TM, D_MODEL), lambda i: (jnp.maximum(i - np_tiles, 0) % (DEC_SEQ // TM), 0))],
        out_specs=pl.BlockSpec((TM, D_MODEL), lambda i: (i, 0)),
        out_shape=jax.ShapeDtypeStruct((N_TOK, D_MODEL), F32),
        compiler_params=pltpu.CompilerParams(dimension_semantics=("arbitrary",)),
        name="assemble_tokens",
    )(xp, xs, pos)


def _routing_tables(ri, counts):
    i32 = jnp.int32
    slot = ri[:, TOP_K:2 * TOP_K].reshape(N_TILES, TM, TOP_K)
    cnt = counts.reshape(N_TILES, N_EXPERTS)
    seg = (cnt + SEG_ALIGN - 1) // SEG_ALIGN
    lstart = jnp.cumsum(seg, axis=1) - seg
    tile_groups = jnp.sum(seg, axis=1)
    region = jnp.sum(seg, axis=0)
    nblk = (region + MOE_BLK_GROUPS - 1) // MOE_BLK_GROUPS
    gend = jnp.cumsum(nblk) * MOE_BLK_GROUPS
    gbase = gend - nblk * MOE_BLK_GROUPS
    cum = jnp.cumsum(seg, axis=0) - seg
    g = jnp.arange(MOE_GROUPS + MOE_BLK_GROUPS, dtype=i32)
    e_of = jnp.minimum(jnp.sum(gend[None, :] <= g[:, None], axis=1), N_EXPERTS - 1)
    onehot_e = (e_of[:, None] == jnp.arange(N_EXPERTS)[None, :]).astype(F32)
    pick_e = lambda t: jnp.dot(onehot_e, t.astype(F32), precision=lax.Precision.HIGHEST).astype(i32)
    off = g - pick_e(gbase)
    real = (g < gend[-1]) & (off < pick_e(region))
    cum_e = pick_e(cum.T)
    tile_of = jnp.sum(cum_e <= off[:, None], axis=1) - 1
    onehot_t = tile_of[:, None] == jnp.arange(N_TILES)[None, :]
    pick_t = lambda t: jnp.sum(jnp.where(onehot_t, t, 0), axis=1)
    pos = tile_of * MOE_TILE_GROUPS + pick_t(pick_e(lstart.T)) + off - pick_t(cum_e)
    pad_rank = jnp.cumsum((~real).astype(i32)) - 1
    gsrc = jnp.where(real, pos, MOE_ZERO_GROUP).astype(i32)
    gdst = jnp.where(real, pos, MOE_ZERO_GROUP + jnp.minimum(pad_rank, MOE_PAD_GROUPS - 1)).astype(i32)
    slot_t = jnp.concatenate([slot.transpose(0, 2, 1), jnp.full((1, TOP_K, TM), -1, i32)], axis=0)
    return (slot.reshape(N_TOK, TOP_K).astype(i32), slot_t.astype(i32),
            (nblk.astype(i32), gbase.astype(i32), gsrc, gdst), tile_groups.astype(i32))


def kernel(x_prompt, x_sample, state_s5_re, state_s5_im, state_hgrn, c, c_ctx, ada_w, ada_b, norm_w, w_in,
           s5_a_re, s5_a_im, s5_log_dt, s5_b_re, s5_b_im, s5_c_re, s5_c_im, s5_d, s5_w_glu, s5_b_glu,
           hgrn_lb, hgrn_norm_w, w_out, router_w, router_b, exp_w1, exp_b1, exp_w2, exp_b2):
    x = _assemble(x_prompt.reshape(N_PROMPT, D_MODEL), x_sample.reshape(N_SAMPLE, D_MODEL),
                  _pos_embed_2d(DEC_SEQ // GRID_W, D_MODEL))
    cond = jnp.concatenate([c_ctx[None, :], c, jnp.zeros((N_COND - 1 - DEC_BATCH, D_MODEL), F32)], axis=0)
    mod = _modulation(cond, ada_w, ada_b).reshape(DEPTH, N_COND, 1, 6 * D_MODEL)
    tiles = jnp.arange(N_TILES, dtype=jnp.int32)
    tile_cond = jnp.where(tiles < N_PROMPT // TM, 0, 1 + (tiles - N_PROMPT // TM) // (DEC_SEQ // TM)).astype(jnp.int32)
    w_in_b = w_in.astype(BF16)
    w_out_b = w_out.astype(BF16)
    w_glu_b = s5_w_glu.astype(BF16)
    router_hi = router_w.astype(BF16)
    router_parts = jnp.stack([router_hi, (router_w - router_hi.astype(F32)).astype(BF16)])

    s5_w = _s5_weights(s5_a_re, s5_a_im, s5_log_dt, s5_b_re, s5_b_im, s5_c_re, s5_c_im)
    fin_re, fin_im, fin_hg = [], [], []
    n = S5_STATE
    for l in range(DEPTH):
        u, qs, v, gs, lf_f, k_f, lf_b, k_b = _in_proj(l, tile_cond, x, mod[l], norm_w[l], w_in_b[l], hgrn_lb)

        h0 = jnp.concatenate([
            state_s5_re[:, l].transpose(2, 0, 1, 3).reshape(S5_GROUPS, DEC_BATCH, 2 * n),
            state_s5_im[:, l].transpose(2, 0, 1, 3).reshape(S5_GROUPS, DEC_BATCH, 2 * n)], axis=-1)
        h0 = h0.reshape(S5_NLT, S5_GPT, S5_NRB - 1, S5_SEQ_PER_RB, S5_NS).transpose(0, 2, 1, 3, 4)
        ycore, hfin = _s5_scan(l, u, *s5_w, h0)
        hfin = hfin.reshape(S5_GROUPS, BATCH, S5_NS)
        fin_re.append(hfin[:, :, 0:2 * n].reshape(S5_GROUPS, BATCH, 2, n).transpose(1, 2, 0, 3))
        fin_im.append(hfin[:, :, 2 * n:].reshape(S5_GROUPS, BATCH, 2, n).transpose(1, 2, 0, 3))

        s0 = jnp.concatenate([jnp.zeros((BATCH, 2, HGRN_HEADS, HGRN_DV, HGRN_DK), F32),
                              jnp.swapaxes(state_hgrn[:, l], -1, -2)], axis=0)
        o_f, o_b, sfin = _hgrn_scan(qs, v, lf_f, k_f, lf_b, k_b, s0)
        fin_hg.append(jnp.swapaxes(sfin[:BATCH], -1, -2))

        x1, h2, ri, gate, counts = _out_proj(
            tile_cond, x, ycore, u, o_f, o_b, gs, mod[l], norm_w[l], s5_d[l].reshape(1, S5_WIDTH),
            w_glu_b[l], s5_b_glu[l].reshape(1, S5_WIDTH), hgrn_norm_w[l].reshape(1, HGRN_DV),
            w_out_b[l], router_parts[:, l], router_b[l].reshape(1, N_EXPERTS))

        slot, slot_t, groups, tile_groups = _routing_tables(ri, counts)
        xs = _dispatch(slot_t, h2)
        ys = _moe_experts(l, *groups, xs, exp_w1, exp_b1, exp_w2, exp_b2)
        x = _combine(l == DEPTH - 1, tile_cond, tile_groups, x1, slot, gate, mod[l], norm_w[l], ys)

    y_prompt, y_sample = x
    return (y_prompt.reshape(BATCH, SEQ, D_MODEL), y_sample.reshape(DEC_BATCH, DEC_SEQ, D_MODEL),
            jnp.stack(fin_re, axis=1), jnp.stack(fin_im, axis=1), jnp.stack(fin_hg, axis=1))
```

```python
import functools
import math

import jax
import jax.numpy as jnp
from jax import lax
from jax.experimental import pallas as pl
from jax.experimental.pallas import tpu as pltpu

F32 = jnp.float32
BF16 = jnp.bfloat16

D_MODEL = 1024
BATCH = 16
SEQ = 256
DEPTH = 2
DEC_BATCH = 4
DEC_SEQ = 2048
GRID_W = 64
S5_WIDTH = 512
S5_GROUP_CH = 16
S5_GROUPS = 32
S5_STATE = 64
HGRN_WIDTH = 512
HGRN_DK = 128
HGRN_HEADS = 4
HGRN_DV = 128
IN_COLS = S5_WIDTH + 5 * HGRN_WIDTH
N_EXPERTS = 32
TOP_K = 4
D_FF = D_MODEL
SWIGLU_LIMIT = 7.0
SWIGLU_ALPHA = 1.702
NORM_EPS = 1e-6
POS_BASE = 10000.0

N_PROMPT = BATCH * SEQ
N_SAMPLE = DEC_BATCH * DEC_SEQ
N_TOK = N_PROMPT + N_SAMPLE
N_SEQ = BATCH + DEC_BATCH
N_COND = 8

TM = 256
N_TILES = N_TOK // TM
S5_T = 16
S5_LT = 128
S5_GPT = S5_LT // S5_GROUP_CH
S5_NLT = S5_WIDTH // S5_LT
S5_RB = N_PROMPT
S5_CR = S5_RB // S5_T
S5_NRB = N_TOK // S5_RB
S5_SEQ_PER_RB = S5_RB // DEC_SEQ
S5_NLAG = 2 * S5_T - 1
S5_NS = 4 * S5_STATE
HC = 128
N_HCHUNK = N_TOK // HC
EXP_CLAMP = 80.0
MOE_BM = 512
N_ASSIGN = N_TOK * TOP_K
SEG_ALIGN = 8
TILE_SLOTS = -(-(TM * TOP_K + N_EXPERTS * (SEG_ALIGN - 1)) // 128) * 128
MOE_BLOCKS = -(-(N_ASSIGN + N_TILES * N_EXPERTS * (SEG_ALIGN - 1) + N_EXPERTS * (MOE_BM - 1)) // MOE_BM)
MOE_GRP = SEG_ALIGN
MOE_TILE_GROUPS = TILE_SLOTS // MOE_GRP
MOE_BLK_GROUPS = MOE_BM // MOE_GRP
MOE_GROUPS = MOE_BLOCKS * MOE_BLK_GROUPS
MOE_ZERO_GROUP = N_TILES * MOE_TILE_GROUPS
MOE_PAD_GROUPS = N_EXPERTS * (MOE_BLK_GROUPS - 1)
MOE_XS_ROWS = (N_TILES + 1) * TILE_SLOTS
MOE_YS_ROWS = N_TILES * TILE_SLOTS + MOE_PAD_GROUPS * MOE_GRP
MOE_BASE_ROWS = TM * TOP_K
MOE_TAIL_ARMS = (128, 64, 32, 16, 8)
VMEM_LIMIT = 56 * 1024 * 1024


def _rms(x, w):
    return x * lax.rsqrt(jnp.mean(x * x, axis=-1, keepdims=True) + NORM_EPS) * w


def _silu(x):
    return x * jax.nn.sigmoid(x)


MOD_TN = 1536


def _mod_kernel(cond_ref, w_ref, b_ref, o_ref):
    s = _silu(cond_ref[...]).astype(BF16)
    o_ref[0] = jnp.dot(s, w_ref[0].astype(BF16), preferred_element_type=F32) + b_ref[0]


def _modulation(cond, ada_w, ada_b):
    return pl.pallas_call(
        _mod_kernel,
        grid=(DEPTH, 6 * D_MODEL // MOD_TN),
        in_specs=[
            pl.BlockSpec((N_COND, D_MODEL), lambda l, j: (0, 0)),
            pl.BlockSpec((1, D_MODEL, MOD_TN), lambda l, j: (l, 0, j)),
            pl.BlockSpec((1, 1, MOD_TN), lambda l, j: (l, 0, j)),
        ],
        out_specs=pl.BlockSpec((1, N_COND, MOD_TN), lambda l, j: (l, 0, j)),
        out_shape=jax.ShapeDtypeStruct((DEPTH, N_COND, 6 * D_MODEL), F32),
        compiler_params=pltpu.CompilerParams(vmem_limit_bytes=VMEM_LIMIT),
        name="adaln_mod",
    )(cond, ada_w, ada_b.reshape(DEPTH, 1, 6 * D_MODEL))


def _in_kernel(layer, cond_ref, x_ref, mod_ref, nw_ref, w_ref, lb_ref,
               u_ref, q_ref, v_ref, g_ref, lff_ref, kf_ref, lfb_ref, kb_ref):
    del cond_ref
    mod = mod_ref[0]
    sh1 = mod[:, 0:D_MODEL]
    sc1 = mod[:, D_MODEL:2 * D_MODEL]
    h = _rms(x_ref[...], nw_ref[0:1, :]) * (1.0 + sc1) + sh1
    z = jnp.dot(h.astype(BF16), w_ref[...], preferred_element_type=F32)
    w = HGRN_WIDTH
    c0 = S5_WIDTH
    u_ref[...] = z[:, 0:c0]
    q_ref[...] = _silu(z[:, c0:c0 + w]) * (HGRN_DK ** -0.5)
    v_ref[...] = z[:, c0 + 3 * w:c0 + 4 * w]
    g_ref[...] = _silu(z[:, c0 + 4 * w:c0 + 5 * w])
    lbp = lb_ref[...]
    e = jnp.exp(lbp - jnp.max(lbp, axis=0, keepdims=True))
    probs = e / jnp.sum(e, axis=0, keepdims=True)
    lb = jnp.sum(probs[0:layer + 1], axis=0) - probs[0]
    for d, (lf_ref, k_ref) in enumerate(((lff_ref, kf_ref), (lfb_ref, kb_ref))):
        f = z[:, c0 + (1 + d) * w:c0 + (2 + d) * w]
        lbd = lb[d:d + 1, :]
        fg = lbd + (1.0 - lbd) * jax.nn.sigmoid(f)
        lf_ref[...] = jnp.log(fg)
        k_ref[...] = 1.0 - fg


def _in_proj(layer, tile_cond, x, mod_l, norm_w_l, w_in_l, hgrn_lb):
    tok = lambda i, c: (i, 0)
    out = jax.ShapeDtypeStruct((N_TOK, HGRN_WIDTH), F32)
    return pl.pallas_call(
        functools.partial(_in_kernel, layer),
        grid_spec=pltpu.PrefetchScalarGridSpec(
            num_scalar_prefetch=1,
            grid=(N_TILES,),
            in_specs=[
                pl.BlockSpec((TM, D_MODEL), tok),
                pl.BlockSpec((1, 1, 6 * D_MODEL), lambda i, c: (c[i], 0, 0)),
                pl.BlockSpec((4, D_MODEL), lambda i, c: (0, 0)),
                pl.BlockSpec((D_MODEL, IN_COLS), lambda i, c: (0, 0)),
                pl.BlockSpec((DEPTH, 2, HGRN_WIDTH), lambda i, c: (0, 0, 0)),
            ],
            out_specs=[pl.BlockSpec((TM, HGRN_WIDTH), tok)] * 8,
        ),
        out_shape=[out] * 8,
        compiler_params=pltpu.CompilerParams(
            dimension_semantics=("arbitrary",), vmem_limit_bytes=VMEM_LIMIT),
        name="in_proj",
    )(tile_cond, x, mod_l, norm_w_l, w_in_l, hgrn_lb)


def _s5_layer_weights(a_re, a_im, log_dt, b_re, b_im, c_re, c_im):
    hp = lax.Precision.HIGHEST
    t = S5_T
    dt = jnp.exp(log_dt)[..., None]
    lam_re = jnp.minimum(a_re, -1e-4)
    lam_im = a_im
    mag = jnp.exp(dt * lam_re)
    ang = dt * lam_im
    ab_re = mag * jnp.cos(ang)
    ab_im = mag * jnp.sin(ang)
    den = lam_re * lam_re + lam_im * lam_im
    nr = ab_re - 1.0
    ni = ab_im
    co_re = (nr * lam_re + ni * lam_im) / den
    co_im = (ni * lam_re - nr * lam_im) / den
    bb_re = co_re[..., None] * b_re - co_im[..., None] * b_im
    bb_im = co_re[..., None] * b_im + co_im[..., None] * b_re
    pr = [jnp.ones_like(ab_re)]
    pi = [jnp.zeros_like(ab_im)]
    for _ in range(t):
        pr.append(pr[-1] * ab_re - pi[-1] * ab_im)
        pi.append(pr[-2] * ab_im + pi[-1] * ab_re)
    pr = jnp.stack(pr, axis=1)
    pi = jnp.stack(pi, axis=1)
    abr = pr[..., None] * bb_re[:, None] - pi[..., None] * bb_im[:, None]
    abi = pr[..., None] * bb_im[:, None] + pi[..., None] * bb_re[:, None]
    kk = jnp.einsum('dgon,dkgni->dkgio', jnp.concatenate([c_re, -c_im], axis=-1),
                    jnp.concatenate([abr, abi], axis=-2), precision=hp)
    kern = (kk[0], kk[1])
    abs_ = ((abr[0], abi[0]), (abr[1], abi[1]))
    pows = ((pr[0], pi[0]), (pr[1], pi[1]))
    p = S5_GROUP_CH
    lags = jnp.concatenate([kern[1][t - 1:0:-1], (kern[0][0] + kern[1][0])[None], kern[0][1:t]], axis=0)
    m = lags.reshape(S5_NLAG, S5_NLT, S5_GPT, p, p)
    bd = jnp.einsum('ltaio,ab->ltaibo', m, jnp.eye(S5_GPT, dtype=F32))
    wcat = bd.reshape(S5_NLAG, S5_NLT, S5_LT, S5_LT).transpose(1, 2, 0, 3).reshape(S5_NLT, S5_LT, S5_NLAG * S5_LT)
    (abr_f, abi_f), (abr_b, abi_b) = abs_
    parts = (abr_f[t - 1::-1], abr_b[:t], abi_f[t - 1::-1], abi_b[:t])
    wst = jnp.concatenate([x.transpose(1, 0, 3, 2) for x in parts], axis=-1)
    wst = wst.reshape(S5_NLT, S5_GPT, t, p, S5_NS).transpose(0, 2, 1, 3, 4).reshape(S5_NLT, t * S5_LT, S5_NS)
    (pr_f, pi_f), (pr_b, pi_b) = pows

    def out_rows(cr, ci, pr_, pi_):
        re_rows = cr[None] * pr_[:, :, None, :] - ci[None] * pi_[:, :, None, :]
        im_rows = -(cr[None] * pi_[:, :, None, :] + ci[None] * pr_[:, :, None, :])
        return re_rows.transpose(1, 3, 0, 2), im_rows.transpose(1, 3, 0, 2)

    fre, fim = out_rows(c_re[0], c_im[0], pr_f[1:t + 1], pi_f[1:t + 1])
    bre, bim = out_rows(c_re[1], c_im[1], pr_b[t:0:-1], pi_b[t:0:-1])
    wout = lax.optimization_barrier(jnp.concatenate([fre, bre, fim, bim], axis=1))
    wdense = (wout.reshape(S5_NLT, S5_GPT, S5_NS, t, p).transpose(0, 2, 3, 1, 4)
              .reshape(S5_NLT, S5_NS, t * S5_LT))
    at = jnp.stack([jnp.concatenate([pr_f[t], pr_b[t]], axis=-1),
                    jnp.concatenate([pi_f[t], pi_b[t]], axis=-1)], axis=1)
    return (wcat.astype(BF16), wst.astype(BF16), wdense.astype(BF16),
            at.reshape(S5_NLT, S5_GPT, 2, 2 * S5_STATE))


def _s5_weights(*params):
    return jax.vmap(_s5_layer_weights)(*params)


def _s5_kernel(u_ref, wcat_ref, wst_ref, wd_ref, at_ref, h0_ref, y_ref, hfin_ref,
               wbig, dh_scr, hf_scr, hb_scr):
    t = S5_T
    n2 = 2 * S5_STATE
    r = pl.program_id(1)

    @pl.when(r == 0)
    def _():
        for s in range(t):
            wbig[s * S5_LT:(s + 1) * S5_LT, :] = wcat_ref[0, :, (t - 1 - s) * S5_LT:(2 * t - 1 - s) * S5_LT]

    xcat = jnp.concatenate([u_ref[pl.ds(s, S5_CR, stride=t), :].astype(BF16) for s in range(t)], axis=-1)
    yacc = jnp.dot(xcat, wbig[...], preferred_element_type=F32)
    lane_k = lax.broadcasted_iota(jnp.int32, (1, t * S5_LT), 1)
    grp_k = (lane_k % S5_LT) // S5_GROUP_CH
    for gi in range(S5_GPT):
        xg = jnp.where(grp_k == gi, xcat, jnp.zeros_like(xcat))
        dh = jnp.dot(xg, wst_ref[0], preferred_element_type=F32)
        dh_scr[gi, 0] = dh[:, 0:n2]
        dh_scr[gi, 1] = dh[:, n2:]

    lane = lax.broadcasted_iota(jnp.int32, (1, n2), 1)
    fwd_lane = lane < S5_STATE
    are = [at_ref[0, gi, 0:1, :] for gi in range(S5_GPT)]
    aim = [at_ref[0, gi, 1:2, :] for gi in range(S5_GPT)]

    def advance(gi, hre, him, dre, dim):
        return (are[gi] * hre - aim[gi] * him + dre, are[gi] * him + aim[gi] * hre + dim)

    @pl.when(r == 0)
    def _():
        nc = SEQ // t
        for gi in range(S5_GPT):
            hre = jnp.zeros((BATCH, n2), F32)
            him = jnp.zeros((BATCH, n2), F32)
            for s in range(nc):
                rows_f = pl.ds(s, BATCH, stride=nc)
                rows_b = pl.ds(nc - 1 - s, BATCH, stride=nc)
                hf_scr[gi, 0, rows_f, :] = hre
                hf_scr[gi, 1, rows_f, :] = him
                hb_scr[gi, 0, rows_b, :] = hre
                hb_scr[gi, 1, rows_b, :] = him
                dre = jnp.where(fwd_lane, dh_scr[gi, 0, rows_f, :], dh_scr[gi, 0, rows_b, :])
                dim = jnp.where(fwd_lane, dh_scr[gi, 1, rows_f, :], dh_scr[gi, 1, rows_b, :])
                hre, him = advance(gi, hre, him, dre, dim)
            hfin_ref[0, gi] = jnp.concatenate([hre, him], axis=-1)

    @pl.when(r > 0)
    def _():
        nc = DEC_SEQ // t
        nb = S5_SEQ_PER_RB
        init = tuple((h0_ref[0, 0, gi, b:b + 1, 0:n2], h0_ref[0, 0, gi, b:b + 1, n2:])
                     for gi in range(S5_GPT) for b in range(nb))

        def step(o, carry):
            new = []
            for gi in range(S5_GPT):
                for b in range(nb):
                    hre, him = carry[gi * nb + b]
                    rf = pl.multiple_of(b * nc + o * 8, 8)
                    rb = pl.multiple_of(b * nc + nc - 8 - o * 8, 8)
                    dfr = dh_scr[gi, 0, pl.ds(rf, 8), :]
                    dfi = dh_scr[gi, 1, pl.ds(rf, 8), :]
                    dbr = dh_scr[gi, 0, pl.ds(rb, 8), :]
                    dbi = dh_scr[gi, 1, pl.ds(rb, 8), :]
                    ent_re, ent_im = [], []
                    for i in range(8):
                        ent_re.append(hre)
                        ent_im.append(him)
                        dre = jnp.where(fwd_lane, dfr[i:i + 1], dbr[7 - i:8 - i])
                        dim = jnp.where(fwd_lane, dfi[i:i + 1], dbi[7 - i:8 - i])
                        hre, him = advance(gi, hre, him, dre, dim)
                    hf_scr[gi, 0, pl.ds(rf, 8), :] = jnp.concatenate(ent_re, axis=0)
                    hf_scr[gi, 1, pl.ds(rf, 8), :] = jnp.concatenate(ent_im, axis=0)
                    hb_scr[gi, 0, pl.ds(rb, 8), :] = jnp.concatenate(ent_re[::-1], axis=0)
                    hb_scr[gi, 1, pl.ds(rb, 8), :] = jnp.concatenate(ent_im[::-1], axis=0)
                    new.append((hre, him))
            return tuple(new)

        lax.fori_loop(0, nc // 8, step, init)

    for gi in range(S5_GPT):
        hent = jnp.concatenate([jnp.where(fwd_lane, hf_scr[gi, 0], hb_scr[gi, 0]),
                                jnp.where(fwd_lane, hf_scr[gi, 1], hb_scr[gi, 1])], axis=-1).astype(BF16)
        yi = jnp.dot(hent, wd_ref[0], preferred_element_type=F32)
        yacc = yacc + jnp.where(grp_k == gi, yi, 0.0)
    for s in range(t):
        y_ref[pl.ds(s, S5_CR, stride=t), :] = yacc[:, s * S5_LT:(s + 1) * S5_LT]


def _s5_scan(layer, u, wcat, wst, wdense, at, h0):
    tile = lambda j, r: (layer * S5_NLT + j, 0, 0)
    merge = lambda w: w.reshape((DEPTH * S5_NLT,) + w.shape[2:])
    wcat, wst, wdense, at = merge(wcat), merge(wst), merge(wdense), merge(at)
    return pl.pallas_call(
        _s5_kernel,
        grid=(S5_NLT, S5_NRB),
        in_specs=[
            pl.BlockSpec((S5_RB, S5_LT), lambda j, r: (r, j)),
            pl.BlockSpec((1, S5_LT, S5_NLAG * S5_LT), tile),
            pl.BlockSpec((1, S5_T * S5_LT, S5_NS), tile),
            pl.BlockSpec((1, S5_NS, S5_T * S5_LT), tile),
            pl.BlockSpec((1, S5_GPT, 2, 2 * S5_STATE), lambda j, r: (layer * S5_NLT + j, 0, 0, 0)),
            pl.BlockSpec((1, 1, S5_GPT, S5_SEQ_PER_RB, S5_NS), lambda j, r: (j, jnp.maximum(r - 1, 0), 0, 0, 0)),
        ],
        out_specs=[
            pl.BlockSpec((S5_RB, S5_LT), lambda j, r: (r, j)),
            pl.BlockSpec((1, S5_GPT, BATCH, S5_NS), lambda j, r: (j, 0, 0, 0)),
        ],
        out_shape=[
            jax.ShapeDtypeStruct((N_TOK, S5_WIDTH), F32),
            jax.ShapeDtypeStruct((S5_NLT, S5_GPT, BATCH, S5_NS), F32),
        ],
        scratch_shapes=[pltpu.VMEM((S5_T * S5_LT, S5_T * S5_LT), BF16)]
                       + [pltpu.VMEM((S5_GPT, 2, S5_CR, 2 * S5_STATE), F32)] * 3,
        compiler_params=pltpu.CompilerParams(
            dimension_semantics=("arbitrary", "arbitrary"), vmem_limit_bytes=VMEM_LIMIT),
        name="s5_scan",
    )(u, wcat, wst, wdense, at, h0)


def _split3(x):
    hi = x.astype(BF16)
    r1 = x - hi.astype(F32)
    mid = r1.astype(BF16)
    lo = (r1 - mid.astype(F32)).astype(BF16)
    return hi, mid, lo


def _piecewise_rows(b, blk, row_in_blk):
    parts = []
    for j in range(HC // blk):
        r = j * blk + row_in_blk
        parts.append(jnp.broadcast_to(b[r:r + 1, :], (blk, b.shape[1])))
    return parts[0] if len(parts) == 1 else jnp.concatenate(parts, axis=0)


def _nt(a, b):
    return lax.dot_general(a, b, (((1,), (1,)), ((), ())), preferred_element_type=F32)


def _tn(a, b):
    return lax.dot_general(a, b, (((0,), (0,)), ((), ())), preferred_element_type=F32)


def _hgrn_dir(reverse, q_ref, v_ref, lf_ref, k_ref, o_ref, st_ref, b_scr):
    row = lax.broadcasted_iota(jnp.int32, (HC, HC), 0)
    col = lax.broadcasted_iota(jnp.int32, (HC, HC), 1)
    causal = (col >= row) if reverse else (col <= row)
    tri = jnp.where(causal, 1.0, 0.0).astype(BF16)
    lf = lf_ref[...]
    hi, mid, lo = _split3(lf)
    ball = (jnp.dot(tri, hi, preferred_element_type=F32) + jnp.dot(tri, mid, preferred_element_type=F32)
            + jnp.dot(tri, lo, preferred_element_type=F32))
    last = 0 if reverse else HC - 1
    masks = []
    for blk in (128, 64, 32):
        half = blk // 2
        same = (row // blk) == (col // blk)
        t_hi = (row % blk) >= half
        s_hi = (col % blk) >= half
        if reverse:
            masks.append(same & jnp.logical_not(t_hi) & s_hi)
        else:
            masks.append(same & t_hi & jnp.logical_not(s_hi))
    b_scr[...] = ball
    mid = _piecewise_rows(ball, 16, 8 if reverse else 7)
    stable = jnp.max(jnp.abs(ball - mid)) <= EXP_CLAMP
    diag_ok = ((row // 16) == (col // 16)) & causal & stable
    for h in range(HGRN_HEADS):
        sl = slice(h * HGRN_DK, (h + 1) * HGRN_DK)
        b = ball[:, sl]
        q = q_ref[:, sl]
        k = k_ref[:, sl]
        v = v_ref[:, sl].astype(BF16)
        st = st_ref[h]
        b_last = b[last:last + 1, :]
        q_in = (q * jnp.exp(b)).astype(BF16)
        k_in = (k * jnp.exp(b_last - b)).astype(BF16)
        o = _nt(q_in, st.astype(BF16))
        st_ref[h] = jnp.exp(b_last) * st + _tn(v, k_in)
        scores = jnp.zeros((HC, HC), F32)
        for blk, mask in zip((128, 64, 32), masks):
            half = blk // 2
            zero = jnp.zeros((half, HGRN_DK), BF16)
            qparts, kparts = [], []
            for j in range(HC // blk):
                early = slice(j * blk, j * blk + half)
                late = slice(j * blk + half, (j + 1) * blk)
                if reverse:
                    m = b[j * blk + half:j * blk + half + 1, :]
                    qrows, krows = early, late
                else:
                    m = b[j * blk + half - 1:j * blk + half, :]
                    qrows, krows = late, early
                qe = (q[qrows] * jnp.exp(b[qrows] - m)).astype(BF16)
                ke = (k[krows] * jnp.exp(m - b[krows])).astype(BF16)
                qparts += [qe, zero] if reverse else [zero, qe]
                kparts += [zero, ke] if reverse else [ke, zero]
            part = _nt(jnp.concatenate(qparts, axis=0), jnp.concatenate(kparts, axis=0))
            scores = scores + (part if blk == HC else jnp.where(mask, part, 0.0))
        m = mid[:, sl]
        qd = (q * jnp.exp(jnp.minimum(b - m, EXP_CLAMP))).astype(BF16)
        kd = (k * jnp.exp(jnp.minimum(m - b, EXP_CLAMP))).astype(BF16)
        scores = scores + jnp.where(diag_ok, _nt(qd, kd), 0.0)
        o_ref[:, sl] = o + jnp.dot(scores.astype(BF16), v, preferred_element_type=F32)

    @pl.when(jnp.logical_not(stable))
    def _():
        pos = lax.broadcasted_iota(jnp.int32, (HC, 1), 0) % 16

        def lag(d, c):
            shift = (HC - d) % HC if reverse else d
            valid = (pos + d <= 15) if reverse else (pos >= d)
            for h in range(HGRN_HEADS):
                sl = slice(h * HGRN_DK, (h + 1) * HGRN_DK)
                b = b_scr[:, sl]
                bs = pltpu.roll(b, shift, 0)
                ks = pltpu.roll(k_ref[:, sl], shift, 0)
                vs = pltpu.roll(v_ref[:, sl], shift, 0)
                e = jnp.exp(jnp.where(valid, b - bs, 0.0))
                w = jnp.sum(q_ref[:, sl] * ks * e, axis=-1, keepdims=True)
                o_ref[:, sl] = o_ref[:, sl] + jnp.where(valid, w, 0.0) * vs
            return c

        lax.fori_loop(0, 16, lag, 0)


def _hgrn_kernel(cf_ref, cb_ref, seq_ref, first_ref, last_ref,
                 qf_ref, vf_ref, lff_ref, kf_ref, qb_ref, vb_ref, lfb_ref, kb_ref, s0_ref,
                 of_ref, ob_ref, sfin_ref, st_scr, b_scr):
    del cf_ref, cb_ref, seq_ref
    j = pl.program_id(0)

    @pl.when(first_ref[j] == 1)
    def _():
        st_scr[...] = s0_ref[0]

    _hgrn_dir(False, qf_ref, vf_ref, lff_ref, kf_ref, of_ref, st_scr.at[0], b_scr)
    _hgrn_dir(True, qb_ref, vb_ref, lfb_ref, kb_ref, ob_ref, st_scr.at[1], b_scr)

    @pl.when(last_ref[j] == 1)
    def _():
        sfin_ref[0] = st_scr[...]


def _hgrn_tables():
    cf, cb, sq, first, last = [], [], [], [], []
    base = 0
    for s in range(N_SEQ):
        nc = (SEQ if s < BATCH else DEC_SEQ) // HC
        for t in range(nc):
            cf.append(base + t)
            cb.append(base + nc - 1 - t)
            sq.append(s)
            first.append(int(t == 0))
            last.append(int(t == nc - 1))
        base += nc
    return tuple(jnp.asarray(x, jnp.int32) for x in (cf, cb, sq, first, last))


def _hgrn_scan(qs, v, lf_f, k_f, lf_b, k_b, s0):
    fwd = lambda j, cf, cb, sq, fi, la: (cf[j], 0)
    bwd = lambda j, cf, cb, sq, fi, la: (cb[j], 0)
    seq = lambda j, cf, cb, sq, fi, la: (sq[j], 0, 0, 0, 0)
    tile = (HC, HGRN_WIDTH)
    sblk = (1, 2, HGRN_HEADS, HGRN_DV, HGRN_DK)
    out = jax.ShapeDtypeStruct((N_TOK, HGRN_WIDTH), F32)
    return pl.pallas_call(
        _hgrn_kernel,
        grid_spec=pltpu.PrefetchScalarGridSpec(
            num_scalar_prefetch=5,
            grid=(N_HCHUNK,),
            in_specs=[pl.BlockSpec(tile, fwd)] * 4 + [pl.BlockSpec(tile, bwd)] * 4
                     + [pl.BlockSpec(sblk, seq)],
            out_specs=[pl.BlockSpec(tile, fwd), pl.BlockSpec(tile, bwd), pl.BlockSpec(sblk, seq)],
            scratch_shapes=[pltpu.VMEM(sblk[1:], F32), pltpu.VMEM((HC, HGRN_WIDTH), F32)],
        ),
        out_shape=[out, out, jax.ShapeDtypeStruct((N_SEQ,) + sblk[1:], F32)],
        compiler_params=pltpu.CompilerParams(
            dimension_semantics=("arbitrary",), vmem_limit_bytes=VMEM_LIMIT),
        name="hgrn_scan",
    )(*_hgrn_tables(), qs, v, lf_f, k_f, qs, v, lf_b, k_b, s0)


def _gelu_tanh(x):
    return 0.5 * x * (1.0 + jnp.tanh(math.sqrt(2.0 / math.pi) * (x + 0.044715 * (x * x * x))))


def _out_kernel(cond_ref, x_ref, yc_ref, u_ref, of_ref, ob_ref, g_ref, mod_ref, nw_ref, d_ref,
                wglu_ref, bglu_ref, hnw_ref, wout_ref, rw_ref, rb_ref,
                x1_ref, h2_ref, ri_ref, rg_ref, cnt_ref):
    del cond_ref
    mod = mod_ref[0]
    g1 = mod[:, 2 * D_MODEL:3 * D_MODEL]
    sh2 = mod[:, 3 * D_MODEL:4 * D_MODEL]
    sc2 = mod[:, 4 * D_MODEL:5 * D_MODEL]
    y = _gelu_tanh(yc_ref[...] + d_ref[...] * u_ref[...])
    y_s5 = y * jax.nn.sigmoid(jnp.dot(y.astype(BF16), wglu_ref[...], preferred_element_type=F32) + bglu_ref[...])
    o = of_ref[...] + ob_ref[...]
    gs = g_ref[...]
    heads = []
    for h in range(HGRN_HEADS):
        sl = slice(h * HGRN_DV, (h + 1) * HGRN_DV)
        heads.append(_rms(o[:, sl], hnw_ref[...]) * gs[:, sl])
    y_hg = jnp.concatenate(heads, axis=-1)
    mix = (jnp.dot(y_s5.astype(BF16), wout_ref[0:S5_WIDTH, :], preferred_element_type=F32)
           + jnp.dot(y_hg.astype(BF16), wout_ref[S5_WIDTH:, :], preferred_element_type=F32))
    x1 = x_ref[...] + g1 * _rms(mix, nw_ref[1:2, :])
    x1_ref[...] = x1
    h2 = _rms(x1, nw_ref[2:3, :]) * (1.0 + sc2) + sh2
    h2_hi = h2.astype(BF16)
    h2_ref[...] = h2_hi
    h2_lo = (h2 - h2_hi.astype(F32)).astype(BF16)
    logits = rb_ref[...]
    for a in (h2_hi, h2_lo):
        for part in range(2):
            logits = logits + jnp.dot(a, rw_ref[part], preferred_element_type=F32)
    eidx = lax.broadcasted_iota(jnp.int32, (TM, N_EXPERTS), 1).astype(F32)
    vals = logits
    top_v, top_i, onehots = [], [], []
    for _ in range(TOP_K):
        mx = jnp.max(vals, axis=-1, keepdims=True)
        ix = jnp.min(jnp.where(vals == mx, eidx, float(N_EXPERTS)), axis=-1, keepdims=True)
        sel = eidx == ix
        top_v.append(mx)
        top_i.append(ix)
        onehots.append(sel)
        vals = jnp.where(sel, -jnp.inf, vals)
    ex = [jnp.exp(tv - top_v[0]) for tv in top_v]
    den = ex[0] + ex[1] + ex[2] + ex[3]
    tot = jnp.zeros((TM, N_EXPERTS), F32)
    for sel in onehots:
        tot = tot + jnp.where(sel, 1.0, 0.0)
    r_t = lax.broadcasted_iota(jnp.int32, (TM, TM), 0)
    r_s = lax.broadcasted_iota(jnp.int32, (TM, TM), 1)
    strict = jnp.where(r_s < r_t, 1.0, 0.0).astype(BF16)
    before = jnp.dot(strict, tot.astype(BF16), preferred_element_type=F32)
    cnt = jnp.sum(tot, axis=0, keepdims=True)
    seg = jnp.floor((cnt + (SEG_ALIGN - 1)) * (1.0 / SEG_ALIGN)) * SEG_ALIGN
    e_r = lax.broadcasted_iota(jnp.int32, (N_EXPERTS, N_EXPERTS), 0)
    e_c = lax.broadcasted_iota(jnp.int32, (N_EXPERTS, N_EXPERTS), 1)
    lstart = jnp.dot(seg, jnp.where(e_r < e_c, 1.0, 0.0), precision=lax.Precision.HIGHEST,
                     preferred_element_type=F32)
    before = before + lstart
    lane = lax.broadcasted_iota(jnp.int32, (TM, 128), 1)
    ri = jnp.zeros((TM, 128), F32)
    rg = jnp.zeros((TM, 128), F32)
    for kk in range(TOP_K):
        rank = jnp.sum(jnp.where(onehots[kk], before, 0.0), axis=-1, keepdims=True)
        ri = jnp.where(lane == kk, top_i[kk], ri)
        ri = jnp.where(lane == TOP_K + kk, rank, ri)
        rg = jnp.where(lane == kk, ex[kk] / den, rg)
    ri_ref[...] = ri.astype(jnp.int32)
    rg_ref[...] = rg
    cnt_ref[0] = cnt.astype(jnp.int32)


def _out_proj(tile_cond, x, ycore, u, o_f, o_b, gs, mod_l, norm_w_l, s5_d_l, wglu_l, bglu_l, hnw_l,
              wout_l, rw_l, rb_l):
    tok = lambda i, c: (i, 0)
    full2 = lambda i, c: (0, 0)
    half = pl.BlockSpec((TM, HGRN_WIDTH), tok)
    wide = pl.BlockSpec((TM, D_MODEL), tok)
    return pl.pallas_call(
        _out_kernel,
        grid_spec=pltpu.PrefetchScalarGridSpec(
            num_scalar_prefetch=1,
            grid=(N_TILES,),
            in_specs=[
                wide, half, half, half, half, half,
                pl.BlockSpec((1, 1, 6 * D_MODEL), lambda i, c: (c[i], 0, 0)),
                pl.BlockSpec((4, D_MODEL), full2),
                pl.BlockSpec((1, S5_WIDTH), full2),
                pl.BlockSpec((S5_WIDTH, S5_WIDTH), full2),
                pl.BlockSpec((1, S5_WIDTH), full2),
                pl.BlockSpec((1, HGRN_DV), full2),
                pl.BlockSpec((D_MODEL, D_MODEL), full2),
                pl.BlockSpec((2, D_MODEL, N_EXPERTS), lambda i, c: (0, 0, 0)),
                pl.BlockSpec((1, N_EXPERTS), full2),
            ],
            out_specs=[wide, wide, pl.BlockSpec((TM, 128), tok), pl.BlockSpec((TM, 128), tok),
                       pl.BlockSpec((1, 1, N_EXPERTS), lambda i, c: (i, 0, 0))],
        ),
        out_shape=[
            jax.ShapeDtypeStruct((N_TOK, D_MODEL), F32),
            jax.ShapeDtypeStruct((N_TOK, D_MODEL), BF16),
            jax.ShapeDtypeStruct((N_TOK, 128), jnp.int32),
            jax.ShapeDtypeStruct((N_TOK, 128), F32),
            jax.ShapeDtypeStruct((N_TILES, 1, N_EXPERTS), jnp.int32),
        ],
        compiler_params=pltpu.CompilerParams(
            dimension_semantics=("arbitrary",), vmem_limit_bytes=VMEM_LIMIT),
        name="out_proj_router",
    )(tile_cond, x, ycore, u, o_f, o_b, gs, mod_l, norm_w_l, s5_d_l, wglu_l, bglu_l, hnw_l,
      wout_l, rw_l, rb_l)


def _dispatch_kernel(slot_ref, h_ref, o_ref):
    srow = lax.broadcasted_iota(jnp.int32, (TILE_SLOTS, TM), 0)
    perm = jnp.zeros((TILE_SLOTS, TM), F32)
    for kk in range(TOP_K):
        perm = perm + jnp.where(srow == slot_ref[0, kk:kk + 1, :], 1.0, 0.0)
    o_ref[...] = jnp.dot(perm.astype(BF16), h_ref[...], preferred_element_type=F32)


def _dispatch(slot_t, h2):
    return pl.pallas_call(
        _dispatch_kernel,
        grid=(N_TILES + 1,),
        in_specs=[pl.BlockSpec((1, TOP_K, TM), lambda i: (i, 0, 0)),
                  pl.BlockSpec((TM, D_MODEL), lambda i: (jnp.minimum(i, N_TILES - 1), 0))],
        out_specs=pl.BlockSpec((TILE_SLOTS, D_MODEL), lambda i: (i, 0)),
        out_shape=jax.ShapeDtypeStruct((MOE_XS_ROWS, D_MODEL), F32),
        compiler_params=pltpu.CompilerParams(
            dimension_semantics=("arbitrary",), vmem_limit_bytes=VMEM_LIMIT),
        name="moe_dispatch",
    )(slot_t, h2)


def _moe_kernel(nblk_ref, gbase_ref, gsrc_ref, gdst_ref, w1_ref, b1_ref, w2_ref, b2_ref, xs_ref, ys_ref,
                xbuf, obuf, w1b, w2b, sem_in, sem_out):
    e = pl.program_id(0)
    nb = nblk_ref[e]
    g0 = gbase_ref[e]

    def gather(b, buf):
        return [pltpu.make_async_copy(
            xs_ref.at[pl.ds(pl.multiple_of(gsrc_ref[g0 + b * MOE_BLK_GROUPS + g] * MOE_GRP, MOE_GRP), MOE_GRP)],
            xbuf.at[buf, pl.ds(g * MOE_GRP, MOE_GRP)], sem_in.at[buf]) for g in range(MOE_BLK_GROUPS)]

    def scatter(b, buf):
        return [pltpu.make_async_copy(
            obuf.at[buf, pl.ds(g * MOE_GRP, MOE_GRP)],
            ys_ref.at[pl.ds(pl.multiple_of(gdst_ref[g0 + b * MOE_BLK_GROUPS + g] * MOE_GRP, MOE_GRP), MOE_GRP)],
            sem_out.at[buf]) for g in range(MOE_BLK_GROUPS)]

    @pl.when(nb > 0)
    def _():
        def cast(j, c):
            r = pl.multiple_of(j * 128, 128)
            w1b[pl.ds(r, 128), :] = w1_ref[0, pl.ds(r, 128), :].astype(BF16)
            w2b[pl.ds(r, 128), :] = w2_ref[0, pl.ds(r, 128), :].astype(BF16)
            return c

        for cp in gather(0, 0):
            cp.start(priority=1)
        lax.fori_loop(0, D_MODEL // 128, cast, 0)

        def block(b, c):
            buf = b % 2

            @pl.when(b >= 2)
            def _():
                for cp in scatter(b - 2, buf):
                    cp.wait()

            for cp in gather(b, buf):
                cp.wait()
            for cp in gather(b + 1, 1 - buf):
                cp.start(priority=1)
            h = jnp.dot(xbuf[buf].astype(BF16), w1b[...], preferred_element_type=F32) + b1_ref[0]
            glu = jnp.minimum(h[:, :D_FF], SWIGLU_LIMIT)
            lin = jnp.clip(h[:, D_FF:], -SWIGLU_LIMIT, SWIGLU_LIMIT)
            act = glu * jax.nn.sigmoid(SWIGLU_ALPHA * glu) * (lin + 1.0)
            obuf[buf] = jnp.dot(act.astype(BF16), w2b[...], preferred_element_type=F32) + b2_ref[0]
            for cp in scatter(b, buf):
                cp.start(priority=1)
            return c

        lax.fori_loop(0, nb, block, 0)
        for cp in gather(nb, nb % 2):
            cp.wait()

        @pl.when(nb >= 2)
        def _():
            for cp in scatter(nb - 2, nb % 2):
                cp.wait()

        for cp in scatter(nb - 1, (nb - 1) % 2):
            cp.wait()


def _moe_experts(layer, nblk, gbase, gsrc, gdst, xs, w1, b1, w2, b2):
    exp3 = lambda e, *_: (layer * N_EXPERTS + e, 0, 0)
    w1 = w1.reshape(DEPTH * N_EXPERTS, D_MODEL, 2 * D_FF)
    w2 = w2.reshape(DEPTH * N_EXPERTS, D_FF, D_MODEL)
    return pl.pallas_call(
        _moe_kernel,
        grid_spec=pltpu.PrefetchScalarGridSpec(
            num_scalar_prefetch=4,
            grid=(N_EXPERTS,),
            in_specs=[
                pl.BlockSpec((1, D_MODEL, 2 * D_FF), exp3),
                pl.BlockSpec((1, 1, 2 * D_FF), exp3),
                pl.BlockSpec((1, D_FF, D_MODEL), exp3),
                pl.BlockSpec((1, 1, D_MODEL), exp3),
                pl.BlockSpec(memory_space=pl.ANY),
            ],
            out_specs=pl.BlockSpec(memory_space=pl.ANY),
            scratch_shapes=[pltpu.VMEM((2, MOE_BM, D_MODEL), F32), pltpu.VMEM((2, MOE_BM, D_MODEL), F32),
                            pltpu.VMEM((D_MODEL, 2 * D_FF), BF16), pltpu.VMEM((D_FF, D_MODEL), BF16),
                            pltpu.SemaphoreType.DMA((2,)), pltpu.SemaphoreType.DMA((2,))],
        ),
        out_shape=jax.ShapeDtypeStruct((MOE_YS_ROWS, D_MODEL), F32),
        compiler_params=pltpu.CompilerParams(
            dimension_semantics=("arbitrary",), vmem_limit_bytes=VMEM_LIMIT),
        name="moe_experts",
    )(nblk, gbase, gsrc, gdst, w1, b1.reshape(DEPTH * N_EXPERTS, 1, 2 * D_FF), w2,
      b2.reshape(DEPTH * N_EXPERTS, 1, D_MODEL), xs)


def _combine_kernel(split, cond_ref, tg_ref, x1_ref, slot_ref, gate_ref, mod_ref, nw_ref, ys_ref, *rest):
    del cond_ref
    outs, (buf, sem) = rest[:-2], rest[-2:]
    i = pl.program_id(0)
    cur = i % 2

    def fetch(tile, b, wait):
        def go(cp):
            cp.wait() if wait else cp.start()

        row0 = pl.multiple_of(tile * TILE_SLOTS, TILE_SLOTS)
        go(pltpu.make_async_copy(ys_ref.at[pl.ds(row0, MOE_BASE_ROWS)], buf.at[b, pl.ds(0, MOE_BASE_ROWS)],
                                 sem.at[b]))
        extra = tg_ref[tile] * MOE_GRP - MOE_BASE_ROWS
        for arm in MOE_TAIL_ARMS:
            @pl.when((extra & arm) != 0)
            def _():
                off = pl.multiple_of(MOE_BASE_ROWS + (extra & ~(2 * arm - 1)), MOE_GRP)
                go(pltpu.make_async_copy(ys_ref.at[pl.ds(row0 + off, arm)], buf.at[b, pl.ds(off, arm)], sem.at[b]))

    @pl.when(i == 0)
    def _():
        buf[...] = jnp.zeros_like(buf)
        fetch(0, 0, False)

    @pl.when(i + 1 < N_TILES)
    def _():
        fetch(i + 1, 1 - cur, False)

    fetch(i, cur, True)
    scol = lax.broadcasted_iota(jnp.int32, (TM, TILE_SLOTS), 1)
    slot = slot_ref[...]
    gate = gate_ref[...]
    gmat = jnp.zeros((TM, TILE_SLOTS), F32)
    for kk in range(TOP_K):
        gmat = gmat + jnp.where(scol == slot[:, kk:kk + 1], gate[:, kk:kk + 1], 0.0)
    gmat = gmat.astype(BF16)
    ffn = jnp.dot(gmat, buf[cur].astype(BF16), preferred_element_type=F32)
    g2 = mod_ref[0][:, 5 * D_MODEL:6 * D_MODEL]
    out = x1_ref[...] + g2 * _rms(ffn, nw_ref[3:4, :])
    if split:
        @pl.when(i < N_PROMPT // TM)
        def _():
            outs[0][...] = out

        @pl.when(i >= N_PROMPT // TM)
        def _():
            outs[1][...] = out
    else:
        outs[0][...] = out


def _combine(split, tile_cond, tile_groups, x1, slot, gate, mod_l, norm_w_l, ys):
    tok = lambda i, *_: (i, 0)
    if split:
        np_tiles = N_PROMPT // TM
        out_specs = [pl.BlockSpec((TM, D_MODEL), lambda i, *_: (jnp.minimum(i, np_tiles - 1), 0)),
                     pl.BlockSpec((TM, D_MODEL), lambda i, *_: (jnp.maximum(i - np_tiles, 0), 0))]
        out_shape = [jax.ShapeDtypeStruct((N_PROMPT, D_MODEL), F32), jax.ShapeDtypeStruct((N_SAMPLE, D_MODEL), F32)]
    else:
        out_specs = pl.BlockSpec((TM, D_MODEL), tok)
        out_shape = jax.ShapeDtypeStruct((N_TOK, D_MODEL), F32)
    return pl.pallas_call(
        functools.partial(_combine_kernel, split),
        grid_spec=pltpu.PrefetchScalarGridSpec(
            num_scalar_prefetch=2,
            grid=(N_TILES,),
            in_specs=[
                pl.BlockSpec((TM, D_MODEL), tok),
                pl.BlockSpec((TM, TOP_K), tok),
                pl.BlockSpec((TM, 128), tok),
                pl.BlockSpec((1, 1, 6 * D_MODEL), lambda i, c, *_: (c[i], 0, 0)),
                pl.BlockSpec((4, D_MODEL), lambda i, *_: (0, 0)),
                pl.BlockSpec(memory_space=pl.ANY),
            ],
            out_specs=out_specs,
            scratch_shapes=[pltpu.VMEM((2, TILE_SLOTS, D_MODEL), F32), pltpu.SemaphoreType.DMA((2,))],
        ),
        out_shape=out_shape,
        compiler_params=pltpu.CompilerParams(
            dimension_semantics=("arbitrary",), vmem_limit_bytes=VMEM_LIMIT),
        name="moe_combine",
    )(tile_cond, tile_groups, x1, slot, gate, mod_l, norm_w_l, ys)


def _pos_embed_2d(rows, dim):
    r = jnp.repeat(jnp.arange(rows, dtype=F32), GRID_W)
    col = jnp.tile(jnp.arange(GRID_W, dtype=F32), rows)
    quarter = dim // 4
    omega = 1.0 / (POS_BASE ** (jnp.arange(quarter, dtype=F32) / quarter))

    def emb(pos):
        ang = pos[:, None] * omega[None, :]
        return jnp.concatenate([jnp.sin(ang), jnp.cos(ang)], axis=-1)

    return jnp.concatenate([emb(r), emb(col)], axis=-1)


def _assemble_kernel(xp_ref, xs_ref, pos_ref, o_ref):
    i = pl.program_id(0)

    @pl.when(i < N_PROMPT // TM)
    def _():
        o_ref[...] = xp_ref[...]

    @pl.when(i >= N_PROMPT // TM)
    def _():
        o_ref[...] = xs_ref[...] + pos_ref[...]


def _assemble(xp, xs, pos):
    np_tiles = N_PROMPT // TM
    return pl.pallas_call(
        _assemble_kernel,
        grid=(N_TILES,),
        in_specs=[pl.BlockSpec((TM, D_MODEL), lambda i: (jnp.minimum(i, np_tiles - 1), 0)),
                  pl.BlockSpec((TM, D_MODEL), lambda i: (jnp.maximum(i - np_tiles, 0), 0)),
                  pl.BlockSpec((TM, D_MODEL), lambda i: (jnp.maximum(i - np_tiles, 0) % (DEC_SEQ // TM), 0))],
        out_specs=pl.BlockSpec((TM, D_MODEL), lambda i: (i, 0)),
        out_shape=jax.ShapeDtypeStruct((N_TOK, D_MODEL), F32),
        compiler_params=pltpu.CompilerParams(dimension_semantics=("arbitrary",)),
        name="assemble_tokens",
    )(xp, xs, pos)


def _routing_tables(ri, counts):
    i32 = jnp.int32
    slot = ri[:, TOP_K:2 * TOP_K].reshape(N_TILES, TM, TOP_K)
    cnt = counts.reshape(N_TILES, N_EXPERTS)
    seg = (cnt + SEG_ALIGN - 1) // SEG_ALIGN
    lstart = jnp.cumsum(seg, axis=1) - seg
    tile_groups = jnp.sum(seg, axis=1)
    region = jnp.sum(seg, axis=0)
    nblk = (region + MOE_BLK_GROUPS - 1) // MOE_BLK_GROUPS
    gend = jnp.cumsum(nblk) * MOE_BLK_GROUPS
    gbase = gend - nblk * MOE_BLK_GROUPS
    cum = jnp.cumsum(seg, axis=0) - seg
    g = jnp.arange(MOE_GROUPS + MOE_BLK_GROUPS, dtype=i32)
    e_of = jnp.minimum(jnp.sum(gend[None, :] <= g[:, None], axis=1), N_EXPERTS - 1)
    onehot_e = (e_of[:, None] == jnp.arange(N_EXPERTS)[None, :]).astype(F32)
    pick_e = lambda t: jnp.dot(onehot_e, t.astype(F32), precision=lax.Precision.HIGHEST).astype(i32)
    off = g - pick_e(gbase)
    real = (g < gend[-1]) & (off < pick_e(region))
    cum_e = pick_e(cum.T)
    tile_of = jnp.sum(cum_e <= off[:, None], axis=1) - 1
    onehot_t = tile_of[:, None] == jnp.arange(N_TILES)[None, :]
    pick_t = lambda t: jnp.sum(jnp.where(onehot_t, t, 0), axis=1)
    pos = tile_of * MOE_TILE_GROUPS + pick_t(pick_e(lstart.T)) + off - pick_t(cum_e)
    pad_rank = jnp.cumsum((~real).astype(i32)) - 1
    gsrc = jnp.where(real, pos, MOE_ZERO_GROUP).astype(i32)
    gdst = jnp.where(real, pos, MOE_ZERO_GROUP + jnp.minimum(pad_rank, MOE_PAD_GROUPS - 1)).astype(i32)
    slot_t = jnp.concatenate([slot.transpose(0, 2, 1), jnp.full((1, TOP_K, TM), -1, i32)], axis=0)
    return (slot.reshape(N_TOK, TOP_K).astype(i32), slot_t.astype(i32),
            (nblk.astype(i32), gbase.astype(i32), gsrc, gdst), tile_groups.astype(i32))


def kernel(x_prompt, x_sample, state_s5_re, state_s5_im, state_hgrn, c, c_ctx, ada_w, ada_b, norm_w, w_in,
           s5_a_re, s5_a_im, s5_log_dt, s5_b_re, s5_b_im, s5_c_re, s5_c_im, s5_d, s5_w_glu, s5_b_glu,
           hgrn_lb, hgrn_norm_w, w_out, router_w, router_b, exp_w1, exp_b1, exp_w2, exp_b2):
    x = _assemble(x_prompt.reshape(N_PROMPT, D_MODEL), x_sample.reshape(N_SAMPLE, D_MODEL),
                  _pos_embed_2d(DEC_SEQ // GRID_W, D_MODEL))
    cond = jnp.concatenate([c_ctx[None, :], c, jnp.zeros((N_COND - 1 - DEC_BATCH, D_MODEL), F32)], axis=0)
    mod = _modulation(cond, ada_w, ada_b).reshape(DEPTH, N_COND, 1, 6 * D_MODEL)
    tiles = jnp.arange(N_TILES, dtype=jnp.int32)
    tile_cond = jnp.where(tiles < N_PROMPT // TM, 0, 1 + (tiles - N_PROMPT // TM) // (DEC_SEQ // TM)).astype(jnp.int32)
    w_in_b = w_in.astype(BF16)
    w_out_b = w_out.astype(BF16)
    w_glu_b = s5_w_glu.astype(BF16)
    router_hi = router_w.astype(BF16)
    router_parts = jnp.stack([router_hi, (router_w - router_hi.astype(F32)).astype(BF16)])

    s5_w = _s5_weights(s5_a_re, s5_a_im, s5_log_dt, s5_b_re, s5_b_im, s5_c_re, s5_c_im)
    fin_re, fin_im, fin_hg = [], [], []
    n = S5_STATE
    for l in range(DEPTH):
        u, qs, v, gs, lf_f, k_f, lf_b, k_b = _in_proj(l, tile_cond, x, mod[l], norm_w[l], w_in_b[l], hgrn_lb)

        h0 = jnp.concatenate([
            state_s5_re[:, l].transpose(2, 0, 1, 3).reshape(S5_GROUPS, DEC_BATCH, 2 * n),
            state_s5_im[:, l].transpose(2, 0, 1, 3).reshape(S5_GROUPS, DEC_BATCH, 2 * n)], axis=-1)
        h0 = h0.reshape(S5_NLT, S5_GPT, S5_NRB - 1, S5_SEQ_PER_RB, S5_NS).transpose(0, 2, 1, 3, 4)
        ycore, hfin = _s5_scan(l, u, *s5_w, h0)
        hfin = hfin.reshape(S5_GROUPS, BATCH, S5_NS)
        fin_re.append(hfin[:, :, 0:2 * n].reshape(S5_GROUPS, BATCH, 2, n).transpose(1, 2, 0, 3))
        fin_im.append(hfin[:, :, 2 * n:].reshape(S5_GROUPS, BATCH, 2, n).transpose(1, 2, 0, 3))

        s0 = jnp.concatenate([jnp.zeros((BATCH, 2, HGRN_HEADS, HGRN_DV, HGRN_DK), F32),
                              jnp.swapaxes(state_hgrn[:, l], -1, -2)], axis=0)
        o_f, o_b, sfin = _hgrn_scan(qs, v, lf_f, k_f, lf_b, k_b, s0)
        fin_hg.append(jnp.swapaxes(sfin[:BATCH], -1, -2))

        x1, h2, ri, gate, counts = _out_proj(
            tile_cond, x, ycore, u, o_f, o_b, gs, mod[l], norm_w[l], s5_d[l].reshape(1, S5_WIDTH),
            w_glu_b[l], s5_b_glu[l].reshape(1, S5_WIDTH), hgrn_norm_w[l].reshape(1, HGRN_DV),
            w_out_b[l], router_parts[:, l], router_b[l].reshape(1, N_EXPERTS))

        slot, slot_t, groups, tile_groups = _routing_tables(ri, counts)
        xs = _dispatch(slot_t, h2)
        ys = _moe_experts(l, *groups, xs, exp_w1, exp_b1, exp_w2, exp_b2)
        x = _combine(l == DEPTH - 1, tile_cond, tile_groups, x1, slot, gate, mod[l], norm_w[l], ys)

    y_prompt, y_sample = x
    return (y_prompt.reshape(BATCH, SEQ, D_MODEL), y_sample.reshape(DEC_BATCH, DEC_SEQ, D_MODEL),
            jnp.stack(fin_re, axis=1), jnp.stack(fin_im, axis=1), jnp.stack(fin_hg, axis=1))
```

```python
import functools
import math

import jax
import jax.numpy as jnp
from jax import lax
from jax.experimental import pallas as pl
from jax.experimental.pallas import tpu as pltpu

F32 = jnp.float32
BF16 = jnp.bfloat16

D_MODEL = 1024
BATCH = 16
SEQ = 256
DEPTH = 2
DEC_BATCH = 4
DEC_SEQ = 2048
GRID_W = 64
S5_WIDTH = 512
S5_GROUP_CH = 16
S5_GROUPS = 32
S5_STATE = 64
HGRN_WIDTH = 512
HGRN_DK = 128
HGRN_HEADS = 4
HGRN_DV = 128
IN_COLS = S5_WIDTH + 5 * HGRN_WIDTH
N_EXPERTS = 32
TOP_K = 4
D_FF = D_MODEL
SWIGLU_LIMIT = 7.0
SWIGLU_ALPHA = 1.702
NORM_EPS = 1e-6
POS_BASE = 10000.0

N_PROMPT = BATCH * SEQ
N_SAMPLE = DEC_BATCH * DEC_SEQ
N_TOK = N_PROMPT + N_SAMPLE
N_SEQ = BATCH + DEC_BATCH
N_COND = 8

TM = 256
N_TILES = N_TOK // TM
S5_T = 16
S5_LT = 128
S5_GPT = S5_LT // S5_GROUP_CH
S5_NLT = S5_WIDTH // S5_LT
S5_RB = N_PROMPT
S5_CR = S5_RB // S5_T
S5_NRB = N_TOK // S5_RB
S5_SEQ_PER_RB = S5_RB // DEC_SEQ
S5_NLAG = 2 * S5_T - 1
S5_NS = 4 * S5_STATE
HC = 128
N_HCHUNK = N_TOK // HC
EXP_CLAMP = 80.0
MOE_BM = 256
N_ASSIGN = N_TOK * TOP_K
SEG_ALIGN = 16
TILE_SLOTS = -(-(TM * TOP_K + N_EXPERTS * (SEG_ALIGN - 1)) // 128) * 128
MOE_BLOCKS = -(-(N_ASSIGN + N_TILES * N_EXPERTS * (SEG_ALIGN - 1) + N_EXPERTS * (MOE_BM - 1)) // MOE_BM)
MOE_GRP = SEG_ALIGN
MOE_TILE_GROUPS = TILE_SLOTS // MOE_GRP
MOE_BLK_GROUPS = MOE_BM // MOE_GRP
MOE_GROUPS = MOE_BLOCKS * MOE_BLK_GROUPS
MOE_ZERO_GROUP = N_TILES * MOE_TILE_GROUPS
MOE_PAD_GROUPS = N_EXPERTS * (MOE_BLK_GROUPS - 1)
MOE_XS_ROWS = (N_TILES + 1) * TILE_SLOTS
MOE_YS_ROWS = N_TILES * TILE_SLOTS + MOE_PAD_GROUPS * MOE_GRP
MOE_BASE_ROWS = TM * TOP_K
MOE_TAIL_ARMS = (256, 128, 64, 32, 16)
VMEM_LIMIT = 56 * 1024 * 1024


def _rms(x, w):
    return x * lax.rsqrt(jnp.mean(x * x, axis=-1, keepdims=True) + NORM_EPS) * w


def _silu(x):
    return x * jax.nn.sigmoid(x)


MOD_TN = 1536


def _mod_kernel(cond_ref, w_ref, b_ref, o_ref):
    s = _silu(cond_ref[...]).astype(BF16)
    o_ref[0] = jnp.dot(s, w_ref[0].astype(BF16), preferred_element_type=F32) + b_ref[0]


def _modulation(cond, ada_w, ada_b):
    return pl.pallas_call(
        _mod_kernel,
        grid=(DEPTH, 6 * D_MODEL // MOD_TN),
        in_specs=[
            pl.BlockSpec((N_COND, D_MODEL), lambda l, j: (0, 0)),
            pl.BlockSpec((1, D_MODEL, MOD_TN), lambda l, j: (l, 0, j)),
            pl.BlockSpec((1, 1, MOD_TN), lambda l, j: (l, 0, j)),
        ],
        out_specs=pl.BlockSpec((1, N_COND, MOD_TN), lambda l, j: (l, 0, j)),
        out_shape=jax.ShapeDtypeStruct((DEPTH, N_COND, 6 * D_MODEL), F32),
        compiler_params=pltpu.CompilerParams(vmem_limit_bytes=VMEM_LIMIT),
        name="adaln_mod",
    )(cond, ada_w, ada_b.reshape(DEPTH, 1, 6 * D_MODEL))


def _in_kernel(layer, cond_ref, x_ref, mod_ref, nw_ref, w_ref, lb_ref,
               u_ref, q_ref, v_ref, g_ref, lff_ref, kf_ref, lfb_ref, kb_ref):
    del cond_ref
    mod = mod_ref[0]
    sh1 = mod[:, 0:D_MODEL]
    sc1 = mod[:, D_MODEL:2 * D_MODEL]
    h = _rms(x_ref[...], nw_ref[0:1, :]) * (1.0 + sc1) + sh1
    z = jnp.dot(h.astype(BF16), w_ref[...], preferred_element_type=F32)
    w = HGRN_WIDTH
    c0 = S5_WIDTH
    u_ref[...] = z[:, 0:c0]
    q_ref[...] = _silu(z[:, c0:c0 + w]) * (HGRN_DK ** -0.5)
    v_ref[...] = z[:, c0 + 3 * w:c0 + 4 * w]
    g_ref[...] = _silu(z[:, c0 + 4 * w:c0 + 5 * w])
    lbp = lb_ref[...]
    e = jnp.exp(lbp - jnp.max(lbp, axis=0, keepdims=True))
    probs = e / jnp.sum(e, axis=0, keepdims=True)
    lb = jnp.sum(probs[0:layer + 1], axis=0) - probs[0]
    for d, (lf_ref, k_ref) in enumerate(((lff_ref, kf_ref), (lfb_ref, kb_ref))):
        f = z[:, c0 + (1 + d) * w:c0 + (2 + d) * w]
        lbd = lb[d:d + 1, :]
        fg = lbd + (1.0 - lbd) * jax.nn.sigmoid(f)
        lf_ref[...] = jnp.log(fg)
        k_ref[...] = 1.0 - fg


def _in_proj(layer, tile_cond, x, mod_l, norm_w_l, w_in_l, hgrn_lb):
    tok = lambda i, c: (i, 0)
    out = jax.ShapeDtypeStruct((N_TOK, HGRN_WIDTH), F32)
    return pl.pallas_call(
        functools.partial(_in_kernel, layer),
        grid_spec=pltpu.PrefetchScalarGridSpec(
            num_scalar_prefetch=1,
            grid=(N_TILES,),
            in_specs=[
                pl.BlockSpec((TM, D_MODEL), tok),
                pl.BlockSpec((1, 1, 6 * D_MODEL), lambda i, c: (c[i], 0, 0)),
                pl.BlockSpec((4, D_MODEL), lambda i, c: (0, 0)),
                pl.BlockSpec((D_MODEL, IN_COLS), lambda i, c: (0, 0)),
                pl.BlockSpec((DEPTH, 2, HGRN_WIDTH), lambda i, c: (0, 0, 0)),
            ],
            out_specs=[pl.BlockSpec((TM, HGRN_WIDTH), tok)] * 8,
        ),
        out_shape=[out] * 8,
        compiler_params=pltpu.CompilerParams(
            dimension_semantics=("arbitrary",), vmem_limit_bytes=VMEM_LIMIT),
        name="in_proj",
    )(tile_cond, x, mod_l, norm_w_l, w_in_l, hgrn_lb)


def _s5_layer_weights(a_re, a_im, log_dt, b_re, b_im, c_re, c_im):
    hp = lax.Precision.HIGHEST
    t = S5_T
    dt = jnp.exp(log_dt)[..., None]
    lam_re = jnp.minimum(a_re, -1e-4)
    lam_im = a_im
    mag = jnp.exp(dt * lam_re)
    ang = dt * lam_im
    ab_re = mag * jnp.cos(ang)
    ab_im = mag * jnp.sin(ang)
    den = lam_re * lam_re + lam_im * lam_im
    nr = ab_re - 1.0
    ni = ab_im
    co_re = (nr * lam_re + ni * lam_im) / den
    co_im = (ni * lam_re - nr * lam_im) / den
    bb_re = co_re[..., None] * b_re - co_im[..., None] * b_im
    bb_im = co_re[..., None] * b_im + co_im[..., None] * b_re
    pr = [jnp.ones_like(ab_re)]
    pi = [jnp.zeros_like(ab_im)]
    for _ in range(t):
        pr.append(pr[-1] * ab_re - pi[-1] * ab_im)
        pi.append(pr[-2] * ab_im + pi[-1] * ab_re)
    pr = jnp.stack(pr, axis=1)
    pi = jnp.stack(pi, axis=1)
    abr = pr[..., None] * bb_re[:, None] - pi[..., None] * bb_im[:, None]
    abi = pr[..., None] * bb_im[:, None] + pi[..., None] * bb_re[:, None]
    kk = jnp.einsum('dgon,dkgni->dkgio', jnp.concatenate([c_re, -c_im], axis=-1),
                    jnp.concatenate([abr, abi], axis=-2), precision=hp)
    kern = (kk[0], kk[1])
    abs_ = ((abr[0], abi[0]), (abr[1], abi[1]))
    pows = ((pr[0], pi[0]), (pr[1], pi[1]))
    p = S5_GROUP_CH
    lags = jnp.concatenate([kern[1][t - 1:0:-1], (kern[0][0] + kern[1][0])[None], kern[0][1:t]], axis=0)
    m = lags.reshape(S5_NLAG, S5_NLT, S5_GPT, p, p)
    bd = jnp.einsum('ltaio,ab->ltaibo', m, jnp.eye(S5_GPT, dtype=F32))
    wcat = bd.reshape(S5_NLAG, S5_NLT, S5_LT, S5_LT).transpose(1, 2, 0, 3).reshape(S5_NLT, S5_LT, S5_NLAG * S5_LT)
    (abr_f, abi_f), (abr_b, abi_b) = abs_
    parts = (abr_f[t - 1::-1], abr_b[:t], abi_f[t - 1::-1], abi_b[:t])
    wst = jnp.concatenate([x.transpose(1, 0, 3, 2) for x in parts], axis=-1)
    wst = wst.reshape(S5_NLT, S5_GPT, t, p, S5_NS).transpose(0, 2, 1, 3, 4).reshape(S5_NLT, t * S5_LT, S5_NS)
    (pr_f, pi_f), (pr_b, pi_b) = pows

    def out_rows(cr, ci, pr_, pi_):
        re_rows = cr[None] * pr_[:, :, None, :] - ci[None] * pi_[:, :, None, :]
        im_rows = -(cr[None] * pi_[:, :, None, :] + ci[None] * pr_[:, :, None, :])
        return re_rows.transpose(1, 3, 0, 2), im_rows.transpose(1, 3, 0, 2)

    fre, fim = out_rows(c_re[0], c_im[0], pr_f[1:t + 1], pi_f[1:t + 1])
    bre, bim = out_rows(c_re[1], c_im[1], pr_b[t:0:-1], pi_b[t:0:-1])
    wout = lax.optimization_barrier(jnp.concatenate([fre, bre, fim, bim], axis=1))
    wdense = (wout.reshape(S5_NLT, S5_GPT, S5_NS, t, p).transpose(0, 2, 3, 1, 4)
              .reshape(S5_NLT, S5_NS, t * S5_LT))
    at = jnp.stack([jnp.concatenate([pr_f[t], pr_b[t]], axis=-1),
                    jnp.concatenate([pi_f[t], pi_b[t]], axis=-1)], axis=1)
    return (wcat.astype(BF16), wst.astype(BF16), wdense.astype(BF16),
            at.reshape(S5_NLT, S5_GPT, 2, 2 * S5_STATE))


def _s5_weights(*params):
    return jax.vmap(_s5_layer_weights)(*params)


def _s5_kernel(u_ref, wcat_ref, wst_ref, wd_ref, at_ref, h0_ref, y_ref, hfin_ref,
               wbig, dh_scr, hf_scr, hb_scr):
    t = S5_T
    n2 = 2 * S5_STATE
    r = pl.program_id(1)

    @pl.when(r == 0)
    def _():
        for s in range(t):
            wbig[s * S5_LT:(s + 1) * S5_LT, :] = wcat_ref[0, :, (t - 1 - s) * S5_LT:(2 * t - 1 - s) * S5_LT]

    xcat = jnp.concatenate([u_ref[pl.ds(s, S5_CR, stride=t), :].astype(BF16) for s in range(t)], axis=-1)
    yacc = jnp.dot(xcat, wbig[...], preferred_element_type=F32)
    lane_k = lax.broadcasted_iota(jnp.int32, (1, t * S5_LT), 1)
    grp_k = (lane_k % S5_LT) // S5_GROUP_CH
    for gi in range(S5_GPT):
        xg = jnp.where(grp_k == gi, xcat, jnp.zeros_like(xcat))
        dh = jnp.dot(xg, wst_ref[0], preferred_element_type=F32)
        dh_scr[gi, 0] = dh[:, 0:n2]
        dh_scr[gi, 1] = dh[:, n2:]

    lane = lax.broadcasted_iota(jnp.int32, (1, n2), 1)
    fwd_lane = lane < S5_STATE
    are = [at_ref[0, gi, 0:1, :] for gi in range(S5_GPT)]
    aim = [at_ref[0, gi, 1:2, :] for gi in range(S5_GPT)]

    def advance(gi, hre, him, dre, dim):
        return (are[gi] * hre - aim[gi] * him + dre, are[gi] * him + aim[gi] * hre + dim)

    @pl.when(r == 0)
    def _():
        nc = SEQ // t
        for gi in range(S5_GPT):
            hre = jnp.zeros((BATCH, n2), F32)
            him = jnp.zeros((BATCH, n2), F32)
            for s in range(nc):
                rows_f = pl.ds(s, BATCH, stride=nc)
                rows_b = pl.ds(nc - 1 - s, BATCH, stride=nc)
                hf_scr[gi, 0, rows_f, :] = hre
                hf_scr[gi, 1, rows_f, :] = him
                hb_scr[gi, 0, rows_b, :] = hre
                hb_scr[gi, 1, rows_b, :] = him
                dre = jnp.where(fwd_lane, dh_scr[gi, 0, rows_f, :], dh_scr[gi, 0, rows_b, :])
                dim = jnp.where(fwd_lane, dh_scr[gi, 1, rows_f, :], dh_scr[gi, 1, rows_b, :])
                hre, him = advance(gi, hre, him, dre, dim)
            hfin_ref[0, gi] = jnp.concatenate([hre, him], axis=-1)

    @pl.when(r > 0)
    def _():
        nc = DEC_SEQ // t
        nb = S5_SEQ_PER_RB
        init = tuple((h0_ref[0, 0, gi, b:b + 1, 0:n2], h0_ref[0, 0, gi, b:b + 1, n2:])
                     for gi in range(S5_GPT) for b in range(nb))

        def step(o, carry):
            new = []
            for gi in range(S5_GPT):
                for b in range(nb):
                    hre, him = carry[gi * nb + b]
                    rf = pl.multiple_of(b * nc + o * 8, 8)
                    rb = pl.multiple_of(b * nc + nc - 8 - o * 8, 8)
                    dfr = dh_scr[gi, 0, pl.ds(rf, 8), :]
                    dfi = dh_scr[gi, 1, pl.ds(rf, 8), :]
                    dbr = dh_scr[gi, 0, pl.ds(rb, 8), :]
                    dbi = dh_scr[gi, 1, pl.ds(rb, 8), :]
                    ent_re, ent_im = [], []
                    for i in range(8):
                        ent_re.append(hre)
                        ent_im.append(him)
                        dre = jnp.where(fwd_lane, dfr[i:i + 1], dbr[7 - i:8 - i])
                        dim = jnp.where(fwd_lane, dfi[i:i + 1], dbi[7 - i:8 - i])
                        hre, him = advance(gi, hre, him, dre, dim)
                    hf_scr[gi, 0, pl.ds(rf, 8), :] = jnp.concatenate(ent_re, axis=0)
                    hf_scr[gi, 1, pl.ds(rf, 8), :] = jnp.concatenate(ent_im, axis=0)
                    hb_scr[gi, 0, pl.ds(rb, 8), :] = jnp.concatenate(ent_re[::-1], axis=0)
                    hb_scr[gi, 1, pl.ds(rb, 8), :] = jnp.concatenate(ent_im[::-1], axis=0)
                    new.append((hre, him))
            return tuple(new)

        lax.fori_loop(0, nc // 8, step, init)

    for gi in range(S5_GPT):
        hent = jnp.concatenate([jnp.where(fwd_lane, hf_scr[gi, 0], hb_scr[gi, 0]),
                                jnp.where(fwd_lane, hf_scr[gi, 1], hb_scr[gi, 1])], axis=-1).astype(BF16)
        yi = jnp.dot(hent, wd_ref[0], preferred_element_type=F32)
        yacc = yacc + jnp.where(grp_k == gi, yi, 0.0)
    for s in range(t):
        y_ref[pl.ds(s, S5_CR, stride=t), :] = yacc[:, s * S5_LT:(s + 1) * S5_LT]


def _s5_scan(layer, u, wcat, wst, wdense, at, h0):
    tile = lambda j, r: (layer * S5_NLT + j, 0, 0)
    merge = lambda w: w.reshape((DEPTH * S5_NLT,) + w.shape[2:])
    wcat, wst, wdense, at = merge(wcat), merge(wst), merge(wdense), merge(at)
    return pl.pallas_call(
        _s5_kernel,
        grid=(S5_NLT, S5_NRB),
        in_specs=[
            pl.BlockSpec((S5_RB, S5_LT), lambda j, r: (r, j)),
            pl.BlockSpec((1, S5_LT, S5_NLAG * S5_LT), tile),
            pl.BlockSpec((1, S5_T * S5_LT, S5_NS), tile),
            pl.BlockSpec((1, S5_NS, S5_T * S5_LT), tile),
            pl.BlockSpec((1, S5_GPT, 2, 2 * S5_STATE), lambda j, r: (layer * S5_NLT + j, 0, 0, 0)),
            pl.BlockSpec((1, 1, S5_GPT, S5_SEQ_PER_RB, S5_NS), lambda j, r: (j, jnp.maximum(r - 1, 0), 0, 0, 0)),
        ],
        out_specs=[
            pl.BlockSpec((S5_RB, S5_LT), lambda j, r: (r, j)),
            pl.BlockSpec((1, S5_GPT, BATCH, S5_NS), lambda j, r: (j, 0, 0, 0)),
        ],
        out_shape=[
            jax.ShapeDtypeStruct((N_TOK, S5_WIDTH), F32),
            jax.ShapeDtypeStruct((S5_NLT, S5_GPT, BATCH, S5_NS), F32),
        ],
        scratch_shapes=[pltpu.VMEM((S5_T * S5_LT, S5_T * S5_LT), BF16)]
                       + [pltpu.VMEM((S5_GPT, 2, S5_CR, 2 * S5_STATE), F32)] * 3,
        compiler_params=pltpu.CompilerParams(
            dimension_semantics=("arbitrary", "arbitrary"), vmem_limit_bytes=VMEM_LIMIT),
        name="s5_scan",
    )(u, wcat, wst, wdense, at, h0)


def _split3(x):
    hi = x.astype(BF16)
    r1 = x - hi.astype(F32)
    mid = r1.astype(BF16)
    lo = (r1 - mid.astype(F32)).astype(BF16)
    return hi, mid, lo


def _piecewise_rows(b, blk, row_in_blk):
    parts = []
    for j in range(HC // blk):
        r = j * blk + row_in_blk
        parts.append(jnp.broadcast_to(b[r:r + 1, :], (blk, b.shape[1])))
    return parts[0] if len(parts) == 1 else jnp.concatenate(parts, axis=0)


def _nt(a, b):
    return lax.dot_general(a, b, (((1,), (1,)), ((), ())), preferred_element_type=F32)


def _tn(a, b):
    return lax.dot_general(a, b, (((0,), (0,)), ((), ())), preferred_element_type=F32)


def _hgrn_dir(reverse, q_ref, v_ref, lf_ref, k_ref, o_ref, st_ref, b_scr):
    row = lax.broadcasted_iota(jnp.int32, (HC, HC), 0)
    col = lax.broadcasted_iota(jnp.int32, (HC, HC), 1)
    causal = (col >= row) if reverse else (col <= row)
    tri = jnp.where(causal, 1.0, 0.0).astype(BF16)
    lf = lf_ref[...]
    hi, mid, lo = _split3(lf)
    ball = (jnp.dot(tri, hi, preferred_element_type=F32) + jnp.dot(tri, mid, preferred_element_type=F32)
            + jnp.dot(tri, lo, preferred_element_type=F32))
    last = 0 if reverse else HC - 1
    masks = []
    for blk in (128, 64, 32):
        half = blk // 2
        same = (row // blk) == (col // blk)
        t_hi = (row % blk) >= half
        s_hi = (col % blk) >= half
        if reverse:
            masks.append(same & jnp.logical_not(t_hi) & s_hi)
        else:
            masks.append(same & t_hi & jnp.logical_not(s_hi))
    b_scr[...] = ball
    mrow = 8 if reverse else 7
    mid = _piecewise_rows(ball, 16, mrow)
    spread = jnp.zeros((1, HGRN_WIDTH), F32)
    for j in range(HC // 16):
        for edge in (16 * j, 16 * j + 15):
            spread = jnp.maximum(spread, jnp.abs(ball[edge:edge + 1, :] - ball[16 * j + mrow:16 * j + mrow + 1, :]))
    stable = jnp.max(spread) <= EXP_CLAMP
    diag_ok = ((row // 16) == (col // 16)) & causal & stable
    for h in range(HGRN_HEADS):
        sl = slice(h * HGRN_DK, (h + 1) * HGRN_DK)
        b = ball[:, sl]
        q = q_ref[:, sl]
        k = k_ref[:, sl]
        v = v_ref[:, sl].astype(BF16)
        st = st_ref[h]
        b_last = b[last:last + 1, :]
        q_in = (q * jnp.exp(b)).astype(BF16)
        k_in = (k * jnp.exp(b_last - b)).astype(BF16)
        o = _nt(q_in, st.astype(BF16))
        st_ref[h] = jnp.exp(b_last) * st + _tn(v, k_in)
        scores = jnp.zeros((HC, HC), F32)
        for blk, mask in zip((128, 64, 32), masks):
            half = blk // 2
            zero = jnp.zeros((half, HGRN_DK), BF16)
            qparts, kparts = [], []
            for j in range(HC // blk):
                early = slice(j * blk, j * blk + half)
                late = slice(j * blk + half, (j + 1) * blk)
                if reverse:
                    m = b[j * blk + half:j * blk + half + 1, :]
                    qrows, krows = early, late
                else:
                    m = b[j * blk + half - 1:j * blk + half, :]
                    qrows, krows = late, early
                qe = (q[qrows] * jnp.exp(b[qrows] - m)).astype(BF16)
                ke = (k[krows] * jnp.exp(m - b[krows])).astype(BF16)
                qparts += [qe, zero] if reverse else [zero, qe]
                kparts += [zero, ke] if reverse else [ke, zero]
            part = _nt(jnp.concatenate(qparts, axis=0), jnp.concatenate(kparts, axis=0))
            scores = scores + (part if blk == HC else jnp.where(mask, part, 0.0))
        m = mid[:, sl]
        qd = (q * jnp.exp(jnp.minimum(b - m, EXP_CLAMP))).astype(BF16)
        kd = (k * jnp.exp(jnp.minimum(m - b, EXP_CLAMP))).astype(BF16)
        scores = scores + jnp.where(diag_ok, _nt(qd, kd), 0.0)
        o_ref[:, sl] = o + jnp.dot(scores.astype(BF16), v, preferred_element_type=F32)

    @pl.when(jnp.logical_not(stable))
    def _():
        pos = lax.broadcasted_iota(jnp.int32, (HC, 1), 0) % 16

        def lag(d, c):
            shift = (HC - d) % HC if reverse else d
            valid = (pos + d <= 15) if reverse else (pos >= d)
            for h in range(HGRN_HEADS):
                sl = slice(h * HGRN_DK, (h + 1) * HGRN_DK)
                b = b_scr[:, sl]
                bs = pltpu.roll(b, shift, 0)
                ks = pltpu.roll(k_ref[:, sl], shift, 0)
                vs = pltpu.roll(v_ref[:, sl], shift, 0)
                e = jnp.exp(jnp.where(valid, b - bs, 0.0))
                w = jnp.sum(q_ref[:, sl] * ks * e, axis=-1, keepdims=True)
                o_ref[:, sl] = o_ref[:, sl] + jnp.where(valid, w, 0.0) * vs
            return c

        lax.fori_loop(0, 16, lag, 0)


def _hgrn_kernel(cf_ref, cb_ref, seq_ref, first_ref, last_ref,
                 qf_ref, vf_ref, lff_ref, kf_ref, qb_ref, vb_ref, lfb_ref, kb_ref, s0_ref,
                 of_ref, ob_ref, sfin_ref, st_scr, b_scr):
    del cf_ref, cb_ref, seq_ref
    j = pl.program_id(0)

    @pl.when(first_ref[j] == 1)
    def _():
        st_scr[...] = s0_ref[0]

    _hgrn_dir(False, qf_ref, vf_ref, lff_ref, kf_ref, of_ref, st_scr.at[0], b_scr)
    _hgrn_dir(True, qb_ref, vb_ref, lfb_ref, kb_ref, ob_ref, st_scr.at[1], b_scr)

    @pl.when(last_ref[j] == 1)
    def _():
        sfin_ref[0] = st_scr[...]


def _hgrn_tables():
    cf, cb, sq, first, last = [], [], [], [], []
    base = 0
    for s in range(N_SEQ):
        nc = (SEQ if s < BATCH else DEC_SEQ) // HC
        for t in range(nc):
            cf.append(base + t)
            cb.append(base + nc - 1 - t)
            sq.append(s)
            first.append(int(t == 0))
            last.append(int(t == nc - 1))
        base += nc
    return tuple(jnp.asarray(x, jnp.int32) for x in (cf, cb, sq, first, last))


def _hgrn_scan(qs, v, lf_f, k_f, lf_b, k_b, s0):
    fwd = lambda j, cf, cb, sq, fi, la: (cf[j], 0)
    bwd = lambda j, cf, cb, sq, fi, la: (cb[j], 0)
    seq = lambda j, cf, cb, sq, fi, la: (sq[j], 0, 0, 0, 0)
    tile = (HC, HGRN_WIDTH)
    sblk = (1, 2, HGRN_HEADS, HGRN_DV, HGRN_DK)
    out = jax.ShapeDtypeStruct((N_TOK, HGRN_WIDTH), F32)
    return pl.pallas_call(
        _hgrn_kernel,
        grid_spec=pltpu.PrefetchScalarGridSpec(
            num_scalar_prefetch=5,
            grid=(N_HCHUNK,),
            in_specs=[pl.BlockSpec(tile, fwd)] * 4 + [pl.BlockSpec(tile, bwd)] * 4
                     + [pl.BlockSpec(sblk, seq)],
            out_specs=[pl.BlockSpec(tile, fwd), pl.BlockSpec(tile, bwd), pl.BlockSpec(sblk, seq)],
            scratch_shapes=[pltpu.VMEM(sblk[1:], F32), pltpu.VMEM((HC, HGRN_WIDTH), F32)],
        ),
        out_shape=[out, out, jax.ShapeDtypeStruct((N_SEQ,) + sblk[1:], F32)],
        compiler_params=pltpu.CompilerParams(
            dimension_semantics=("arbitrary",), vmem_limit_bytes=VMEM_LIMIT),
        name="hgrn_scan",
    )(*_hgrn_tables(), qs, v, lf_f, k_f, qs, v, lf_b, k_b, s0)


def _gelu_tanh(x):
    return 0.5 * x * (1.0 + jnp.tanh(math.sqrt(2.0 / math.pi) * (x + 0.044715 * (x * x * x))))


def _out_kernel(cond_ref, x_ref, yc_ref, u_ref, of_ref, ob_ref, g_ref, mod_ref, nw_ref, d_ref,
                wglu_ref, bglu_ref, hnw_ref, wout_ref, rw_ref, rb_ref,
                x1_ref, h2_ref, ri_ref, rg_ref, cnt_ref):
    del cond_ref
    mod = mod_ref[0]
    g1 = mod[:, 2 * D_MODEL:3 * D_MODEL]
    sh2 = mod[:, 3 * D_MODEL:4 * D_MODEL]
    sc2 = mod[:, 4 * D_MODEL:5 * D_MODEL]
    y = _gelu_tanh(yc_ref[...] + d_ref[...] * u_ref[...])
    y_s5 = y * jax.nn.sigmoid(jnp.dot(y.astype(BF16), wglu_ref[...], preferred_element_type=F32) + bglu_ref[...])
    o = of_ref[...] + ob_ref[...]
    gs = g_ref[...]
    heads = []
    for h in range(HGRN_HEADS):
        sl = slice(h * HGRN_DV, (h + 1) * HGRN_DV)
        heads.append(_rms(o[:, sl], hnw_ref[...]) * gs[:, sl])
    y_hg = jnp.concatenate(heads, axis=-1)
    mix = (jnp.dot(y_s5.astype(BF16), wout_ref[0:S5_WIDTH, :], preferred_element_type=F32)
           + jnp.dot(y_hg.astype(BF16), wout_ref[S5_WIDTH:, :], preferred_element_type=F32))
    x1 = x_ref[...] + g1 * _rms(mix, nw_ref[1:2, :])
    x1_ref[...] = x1
    h2 = _rms(x1, nw_ref[2:3, :]) * (1.0 + sc2) + sh2
    h2_hi = h2.astype(BF16)
    h2_ref[...] = h2_hi
    h2_lo = (h2 - h2_hi.astype(F32)).astype(BF16)
    logits = rb_ref[...]
    for a in (h2_hi, h2_lo):
        for part in range(2):
            logits = logits + jnp.dot(a, rw_ref[part], preferred_element_type=F32)
    eidx = lax.broadcasted_iota(jnp.int32, (TM, N_EXPERTS), 1).astype(F32)
    vals = logits
    top_v, top_i, onehots = [], [], []
    for _ in range(TOP_K):
        mx = jnp.max(vals, axis=-1, keepdims=True)
        ix = jnp.min(jnp.where(vals == mx, eidx, float(N_EXPERTS)), axis=-1, keepdims=True)
        sel = eidx == ix
        top_v.append(mx)
        top_i.append(ix)
        onehots.append(sel)
        vals = jnp.where(sel, -jnp.inf, vals)
    ex = [jnp.exp(tv - top_v[0]) for tv in top_v]
    den = ex[0] + ex[1] + ex[2] + ex[3]
    tot = jnp.zeros((TM, N_EXPERTS), F32)
    for sel in onehots:
        tot = tot + jnp.where(sel, 1.0, 0.0)
    r_t = lax.broadcasted_iota(jnp.int32, (TM, TM), 0)
    r_s = lax.broadcasted_iota(jnp.int32, (TM, TM), 1)
    strict = jnp.where(r_s < r_t, 1.0, 0.0).astype(BF16)
    before = jnp.dot(strict, tot.astype(BF16), preferred_element_type=F32)
    cnt = jnp.sum(tot, axis=0, keepdims=True)
    seg = jnp.floor((cnt + (SEG_ALIGN - 1)) * (1.0 / SEG_ALIGN)) * SEG_ALIGN
    e_r = lax.broadcasted_iota(jnp.int32, (N_EXPERTS, N_EXPERTS), 0)
    e_c = lax.broadcasted_iota(jnp.int32, (N_EXPERTS, N_EXPERTS), 1)
    lstart = jnp.dot(seg, jnp.where(e_r < e_c, 1.0, 0.0), precision=lax.Precision.HIGHEST,
                     preferred_element_type=F32)
    before = before + lstart
    lane = lax.broadcasted_iota(jnp.int32, (TM, 128), 1)
    ri = jnp.zeros((TM, 128), F32)
    rg = jnp.zeros((TM, 128), F32)
    for kk in range(TOP_K):
        rank = jnp.sum(jnp.where(onehots[kk], before, 0.0), axis=-1, keepdims=True)
        ri = jnp.where(lane == kk, top_i[kk], ri)
        ri = jnp.where(lane == TOP_K + kk, rank, ri)
        rg = jnp.where(lane == kk, ex[kk] / den, rg)
    ri_ref[...] = ri.astype(jnp.int32)
    rg_ref[...] = rg
    cnt_ref[0] = cnt.astype(jnp.int32)


def _out_proj(tile_cond, x, ycore, u, o_f, o_b, gs, mod_l, norm_w_l, s5_d_l, wglu_l, bglu_l, hnw_l,
              wout_l, rw_l, rb_l):
    tok = lambda i, c: (i, 0)
    full2 = lambda i, c: (0, 0)
    half = pl.BlockSpec((TM, HGRN_WIDTH), tok)
    wide = pl.BlockSpec((TM, D_MODEL), tok)
    return pl.pallas_call(
        _out_kernel,
        grid_spec=pltpu.PrefetchScalarGridSpec(
            num_scalar_prefetch=1,
            grid=(N_TILES,),
            in_specs=[
                wide, half, half, half, half, half,
                pl.BlockSpec((1, 1, 6 * D_MODEL), lambda i, c: (c[i], 0, 0)),
                pl.BlockSpec((4, D_MODEL), full2),
                pl.BlockSpec((1, S5_WIDTH), full2),
                pl.BlockSpec((S5_WIDTH, S5_WIDTH), full2),
                pl.BlockSpec((1, S5_WIDTH), full2),
                pl.BlockSpec((1, HGRN_DV), full2),
                pl.BlockSpec((D_MODEL, D_MODEL), full2),
                pl.BlockSpec((2, D_MODEL, N_EXPERTS), lambda i, c: (0, 0, 0)),
                pl.BlockSpec((1, N_EXPERTS), full2),
            ],
            out_specs=[wide, wide, pl.BlockSpec((TM, 128), tok), pl.BlockSpec((TM, 128), tok),
                       pl.BlockSpec((1, 1, N_EXPERTS), lambda i, c: (i, 0, 0))],
        ),
        out_shape=[
            jax.ShapeDtypeStruct((N_TOK, D_MODEL), F32),
            jax.ShapeDtypeStruct((N_TOK, D_MODEL), BF16),
            jax.ShapeDtypeStruct((N_TOK, 128), jnp.int32),
            jax.ShapeDtypeStruct((N_TOK, 128), F32),
            jax.ShapeDtypeStruct((N_TILES, 1, N_EXPERTS), jnp.int32),
        ],
        compiler_params=pltpu.CompilerParams(
            dimension_semantics=("arbitrary",), vmem_limit_bytes=VMEM_LIMIT),
        name="out_proj_router",
    )(tile_cond, x, ycore, u, o_f, o_b, gs, mod_l, norm_w_l, s5_d_l, wglu_l, bglu_l, hnw_l,
      wout_l, rw_l, rb_l)


def _dispatch_kernel(slot_ref, h_ref, o_ref):
    srow = lax.broadcasted_iota(jnp.int32, (TILE_SLOTS, TM), 0)
    perm = jnp.zeros((TILE_SLOTS, TM), F32)
    for kk in range(TOP_K):
        perm = perm + jnp.where(srow == slot_ref[0, kk:kk + 1, :], 1.0, 0.0)
    o_ref[...] = jnp.dot(perm.astype(BF16), h_ref[...], preferred_element_type=F32).astype(BF16)


def _dispatch(slot_t, h2):
    return pl.pallas_call(
        _dispatch_kernel,
        grid=(N_TILES + 1,),
        in_specs=[pl.BlockSpec((1, TOP_K, TM), lambda i: (i, 0, 0)),
                  pl.BlockSpec((TM, D_MODEL), lambda i: (jnp.minimum(i, N_TILES - 1), 0))],
        out_specs=pl.BlockSpec((TILE_SLOTS, D_MODEL), lambda i: (i, 0)),
        out_shape=jax.ShapeDtypeStruct((MOE_XS_ROWS, D_MODEL), BF16),
        compiler_params=pltpu.CompilerParams(
            dimension_semantics=("arbitrary",), vmem_limit_bytes=VMEM_LIMIT),
        name="moe_dispatch",
    )(slot_t, h2)


def _moe_kernel(nblk_ref, gbase_ref, gsrc_ref, gdst_ref, w1_ref, b1_ref, w2_ref, b2_ref, xs_ref, ys_ref,
                xbuf, obuf, w1b, w2b, sem_in, sem_out):
    e = pl.program_id(0)
    nb = nblk_ref[e]
    g0 = gbase_ref[e]

    def gather(b, buf):
        return [pltpu.make_async_copy(
            xs_ref.at[pl.ds(pl.multiple_of(gsrc_ref[g0 + b * MOE_BLK_GROUPS + g] * MOE_GRP, MOE_GRP), MOE_GRP)],
            xbuf.at[buf, pl.ds(g * MOE_GRP, MOE_GRP)], sem_in.at[buf]) for g in range(MOE_BLK_GROUPS)]

    def scatter(b, buf):
        return [pltpu.make_async_copy(
            obuf.at[buf, pl.ds(g * MOE_GRP, MOE_GRP)],
            ys_ref.at[pl.ds(pl.multiple_of(gdst_ref[g0 + b * MOE_BLK_GROUPS + g] * MOE_GRP, MOE_GRP), MOE_GRP)],
            sem_out.at[buf]) for g in range(MOE_BLK_GROUPS)]

    @pl.when(nb > 0)
    def _():
        def cast(j, c):
            r = pl.multiple_of(j * 128, 128)
            w1b[pl.ds(r, 128), :] = w1_ref[0, pl.ds(r, 128), :].astype(BF16)
            w2b[pl.ds(r, 128), :] = w2_ref[0, pl.ds(r, 128), :].astype(BF16)
            return c

        for cp in gather(0, 0):
            cp.start(priority=1)
        lax.fori_loop(0, D_MODEL // 128, cast, 0)

        def block(b, c):
            buf = b % 2

            @pl.when(b >= 2)
            def _():
                for cp in scatter(b - 2, buf):
                    cp.wait()

            for cp in gather(b, buf):
                cp.wait()
            for cp in gather(b + 1, 1 - buf):
                cp.start(priority=1)
            h = jnp.dot(xbuf[buf], w1b[...], preferred_element_type=F32) + b1_ref[0]
            glu = jnp.minimum(h[:, :D_FF], SWIGLU_LIMIT)
            lin = jnp.clip(h[:, D_FF:], -SWIGLU_LIMIT, SWIGLU_LIMIT)
            act = glu * jax.nn.sigmoid(SWIGLU_ALPHA * glu) * (lin + 1.0)
            obuf[buf] = (jnp.dot(act.astype(BF16), w2b[...], preferred_element_type=F32) + b2_ref[0]).astype(BF16)
            for cp in scatter(b, buf):
                cp.start(priority=1)
            return c

        lax.fori_loop(0, nb, block, 0)
        for cp in gather(nb, nb % 2):
            cp.wait()

        @pl.when(nb >= 2)
        def _():
            for cp in scatter(nb - 2, nb % 2):
                cp.wait()

        for cp in scatter(nb - 1, (nb - 1) % 2):
            cp.wait()


def _moe_experts(layer, nblk, gbase, gsrc, gdst, xs, w1, b1, w2, b2):
    exp3 = lambda e, *_: (layer * N_EXPERTS + e, 0, 0)
    w1 = w1.reshape(DEPTH * N_EXPERTS, D_MODEL, 2 * D_FF)
    w2 = w2.reshape(DEPTH * N_EXPERTS, D_FF, D_MODEL)
    return pl.pallas_call(
        _moe_kernel,
        grid_spec=pltpu.PrefetchScalarGridSpec(
            num_scalar_prefetch=4,
            grid=(N_EXPERTS,),
            in_specs=[
                pl.BlockSpec((1, D_MODEL, 2 * D_FF), exp3),
                pl.BlockSpec((1, 1, 2 * D_FF), exp3),
                pl.BlockSpec((1, D_FF, D_MODEL), exp3),
                pl.BlockSpec((1, 1, D_MODEL), exp3),
                pl.BlockSpec(memory_space=pl.ANY),
            ],
            out_specs=pl.BlockSpec(memory_space=pl.ANY),
            scratch_shapes=[pltpu.VMEM((2, MOE_BM, D_MODEL), BF16), pltpu.VMEM((2, MOE_BM, D_MODEL), BF16),
                            pltpu.VMEM((D_MODEL, 2 * D_FF), BF16), pltpu.VMEM((D_FF, D_MODEL), BF16),
                            pltpu.SemaphoreType.DMA((2,)), pltpu.SemaphoreType.DMA((2,))],
        ),
        out_shape=jax.ShapeDtypeStruct((MOE_YS_ROWS, D_MODEL), BF16),
        compiler_params=pltpu.CompilerParams(
            dimension_semantics=("arbitrary",), vmem_limit_bytes=VMEM_LIMIT),
        name="moe_experts",
    )(nblk, gbase, gsrc, gdst, w1, b1.reshape(DEPTH * N_EXPERTS, 1, 2 * D_FF), w2,
      b2.reshape(DEPTH * N_EXPERTS, 1, D_MODEL), xs)


def _combine_kernel(split, cond_ref, tg_ref, x1_ref, slot_ref, gate_ref, mod_ref, nw_ref, ys_ref, *rest):
    del cond_ref
    outs, (buf, sem) = rest[:-2], rest[-2:]
    i = pl.program_id(0)
    cur = i % 2

    def fetch(tile, b, wait):
        def go(cp):
            cp.wait() if wait else cp.start()

        row0 = pl.multiple_of(tile * TILE_SLOTS, TILE_SLOTS)
        go(pltpu.make_async_copy(ys_ref.at[pl.ds(row0, MOE_BASE_ROWS)], buf.at[b, pl.ds(0, MOE_BASE_ROWS)],
                                 sem.at[b]))
        extra = tg_ref[tile] * MOE_GRP - MOE_BASE_ROWS
        for arm in MOE_TAIL_ARMS:
            @pl.when((extra & arm) != 0)
            def _():
                off = pl.multiple_of(MOE_BASE_ROWS + (extra & ~(2 * arm - 1)), MOE_GRP)
                go(pltpu.make_async_copy(ys_ref.at[pl.ds(row0 + off, arm)], buf.at[b, pl.ds(off, arm)], sem.at[b]))

    @pl.when(i == 0)
    def _():
        buf[...] = jnp.zeros_like(buf)
        fetch(0, 0, False)

    @pl.when(i + 1 < N_TILES)
    def _():
        fetch(i + 1, 1 - cur, False)

    fetch(i, cur, True)
    scol = lax.broadcasted_iota(jnp.int32, (TM, TILE_SLOTS), 1)
    slot = slot_ref[...]
    gate = gate_ref[...]
    gmat = jnp.zeros((TM, TILE_SLOTS), F32)
    for kk in range(TOP_K):
        gmat = gmat + jnp.where(scol == slot[:, kk:kk + 1], gate[:, kk:kk + 1], 0.0)
    gmat = gmat.astype(BF16)
    ffn = jnp.dot(gmat, buf[cur], preferred_element_type=F32)
    g2 = mod_ref[0][:, 5 * D_MODEL:6 * D_MODEL]
    out = x1_ref[...] + g2 * _rms(ffn, nw_ref[3:4, :])
    if split:
        @pl.when(i < N_PROMPT // TM)
        def _():
            outs[0][...] = out

        @pl.when(i >= N_PROMPT // TM)
        def _():
            outs[1][...] = out
    else:
        outs[0][...] = out


def _combine(split, tile_cond, tile_groups, x1, slot, gate, mod_l, norm_w_l, ys):
    tok = lambda i, *_: (i, 0)
    if split:
        np_tiles = N_PROMPT // TM
        out_specs = [pl.BlockSpec((TM, D_MODEL), lambda i, *_: (jnp.minimum(i, np_tiles - 1), 0)),
                     pl.BlockSpec((TM, D_MODEL), lambda i, *_: (jnp.maximum(i - np_tiles, 0), 0))]
        out_shape = [jax.ShapeDtypeStruct((N_PROMPT, D_MODEL), F32), jax.ShapeDtypeStruct((N_SAMPLE, D_MODEL), F32)]
    else:
        out_specs = pl.BlockSpec((TM, D_MODEL), tok)
        out_shape = jax.ShapeDtypeStruct((N_TOK, D_MODEL), F32)
    return pl.pallas_call(
        functools.partial(_combine_kernel, split),
        grid_spec=pltpu.PrefetchScalarGridSpec(
            num_scalar_prefetch=2,
            grid=(N_TILES,),
            in_specs=[
                pl.BlockSpec((TM, D_MODEL), tok),
                pl.BlockSpec((TM, TOP_K), tok),
                pl.BlockSpec((TM, 128), tok),
                pl.BlockSpec((1, 1, 6 * D_MODEL), lambda i, c, *_: (c[i], 0, 0)),
                pl.BlockSpec((4, D_MODEL), lambda i, *_: (0, 0)),
                pl.BlockSpec(memory_space=pl.ANY),
            ],
            out_specs=out_specs,
            scratch_shapes=[pltpu.VMEM((2, TILE_SLOTS, D_MODEL), BF16), pltpu.SemaphoreType.DMA((2,))],
        ),
        out_shape=out_shape,
        compiler_params=pltpu.CompilerParams(
            dimension_semantics=("arbitrary",), vmem_limit_bytes=VMEM_LIMIT),
        name="moe_combine",
    )(tile_cond, tile_groups, x1, slot, gate, mod_l, norm_w_l, ys)


def _pos_embed_2d(rows, dim):
    r = jnp.repeat(jnp.arange(rows, dtype=F32), GRID_W)
    col = jnp.tile(jnp.arange(GRID_W, dtype=F32), rows)
    quarter = dim // 4
    omega = 1.0 / (POS_BASE ** (jnp.arange(quarter, dtype=F32) / quarter))

    def emb(pos):
        ang = pos[:, None] * omega[None, :]
        return jnp.concatenate([jnp.sin(ang), jnp.cos(ang)], axis=-1)

    return jnp.concatenate([emb(r), emb(col)], axis=-1)


def _assemble_kernel(xp_ref, xs_ref, pos_ref, o_ref):
    i = pl.program_id(0)

    @pl.when(i < N_PROMPT // TM)
    def _():
        o_ref[...] = xp_ref[...]

    @pl.when(i >= N_PROMPT // TM)
    def _():
        o_ref[...] = xs_ref[...] + pos_ref[...]


def _assemble(xp, xs, pos):
    np_tiles = N_PROMPT // TM
    return pl.pallas_call(
        _assemble_kernel,
        grid=(N_TILES,),
        in_specs=[pl.BlockSpec((TM, D_MODEL), lambda i: (jnp.minimum(i, np_tiles - 1), 0)),
                  pl.BlockSpec((TM, D_MODEL), lambda i: (jnp.maximum(i - np_tiles, 0), 0)),
                  pl.BlockSpec((TM, D_MODEL), lambda i: (jnp.maximum(i - np_tiles, 0) % (DEC_SEQ // TM), 0))],
        out_specs=pl.BlockSpec((TM, D_MODEL), lambda i: (i, 0)),
        out_shape=jax.ShapeDtypeStruct((N_TOK, D_MODEL), F32),
        compiler_params=pltpu.CompilerParams(dimension_semantics=("arbitrary",)),
        name="assemble_tokens",
    )(xp, xs, pos)


def _routing_tables(ri, counts):
    i32 = jnp.int32
    slot = ri[:, TOP_K:2 * TOP_K].reshape(N_TILES, TM, TOP_K)
    cnt = counts.reshape(N_TILES, N_EXPERTS)
    seg = (cnt + SEG_ALIGN - 1) // SEG_ALIGN
    lstart = jnp.cumsum(seg, axis=1) - seg
    tile_groups = jnp.sum(seg, axis=1)
    region = jnp.sum(seg, axis=0)
    nblk = (region + MOE_BLK_GROUPS - 1) // MOE_BLK_GROUPS
    gend = jnp.cumsum(nblk) * MOE_BLK_GROUPS
    gbase = gend - nblk * MOE_BLK_GROUPS
    cum = jnp.cumsum(seg, axis=0) - seg
    g = jnp.arange(MOE_GROUPS + MOE_BLK_GROUPS, dtype=i32)
    e_of = jnp.minimum(jnp.sum(gend[None, :] <= g[:, None], axis=1), N_EXPERTS - 1)
    onehot_e = (e_of[:, None] == jnp.arange(N_EXPERTS)[None, :]).astype(F32)
    pick_e = lambda t: jnp.dot(onehot_e, t.astype(F32), precision=lax.Precision.HIGHEST).astype(i32)
    off = g - pick_e(gbase)
    real = (g < gend[-1]) & (off < pick_e(region))
    cum_e = pick_e(cum.T)
    tile_of = jnp.sum(cum_e <= off[:, None], axis=1) - 1
    onehot_t = tile_of[:, None] == jnp.arange(N_TILES)[None, :]
    pick_t = lambda t: jnp.sum(jnp.where(onehot_t, t, 0), axis=1)
    pos = tile_of * MOE_TILE_GROUPS + pick_t(pick_e(lstart.T)) + off - pick_t(cum_e)
    pad_rank = jnp.cumsum((~real).astype(i32)) - 1
    gsrc = jnp.where(real, pos, MOE_ZERO_GROUP).astype(i32)
    gdst = jnp.where(real, pos, MOE_ZERO_GROUP + jnp.minimum(pad_rank, MOE_PAD_GROUPS - 1)).astype(i32)
    slot_t = jnp.concatenate([slot.transpose(0, 2, 1), jnp.full((1, TOP_K, TM), -1, i32)], axis=0)
    return (slot.reshape(N_TOK, TOP_K).astype(i32), slot_t.astype(i32),
            (nblk.astype(i32), gbase.astype(i32), gsrc, gdst), tile_groups.astype(i32))


def kernel(x_prompt, x_sample, state_s5_re, state_s5_im, state_hgrn, c, c_ctx, ada_w, ada_b, norm_w, w_in,
           s5_a_re, s5_a_im, s5_log_dt, s5_b_re, s5_b_im, s5_c_re, s5_c_im, s5_d, s5_w_glu, s5_b_glu,
           hgrn_lb, hgrn_norm_w, w_out, router_w, router_b, exp_w1, exp_b1, exp_w2, exp_b2):
    x = _assemble(x_prompt.reshape(N_PROMPT, D_MODEL), x_sample.reshape(N_SAMPLE, D_MODEL),
                  _pos_embed_2d(DEC_SEQ // GRID_W, D_MODEL))
    cond = jnp.concatenate([c_ctx[None, :], c, jnp.zeros((N_COND - 1 - DEC_BATCH, D_MODEL), F32)], axis=0)
    mod = _modulation(cond, ada_w, ada_b).reshape(DEPTH, N_COND, 1, 6 * D_MODEL)
    tiles = jnp.arange(N_TILES, dtype=jnp.int32)
    tile_cond = jnp.where(tiles < N_PROMPT // TM, 0, 1 + (tiles - N_PROMPT // TM) // (DEC_SEQ // TM)).astype(jnp.int32)
    w_in_b = w_in.astype(BF16)
    w_out_b = w_out.astype(BF16)
    w_glu_b = s5_w_glu.astype(BF16)
    router_hi = router_w.astype(BF16)
    router_parts = jnp.stack([router_hi, (router_w - router_hi.astype(F32)).astype(BF16)])

    s5_w = _s5_weights(s5_a_re, s5_a_im, s5_log_dt, s5_b_re, s5_b_im, s5_c_re, s5_c_im)
    fin_re, fin_im, fin_hg = [], [], []
    n = S5_STATE
    for l in range(DEPTH):
        u, qs, v, gs, lf_f, k_f, lf_b, k_b = _in_proj(l, tile_cond, x, mod[l], norm_w[l], w_in_b[l], hgrn_lb)

        h0 = jnp.concatenate([
            state_s5_re[:, l].transpose(2, 0, 1, 3).reshape(S5_GROUPS, DEC_BATCH, 2 * n),
            state_s5_im[:, l].transpose(2, 0, 1, 3).reshape(S5_GROUPS, DEC_BATCH, 2 * n)], axis=-1)
        h0 = h0.reshape(S5_NLT, S5_GPT, S5_NRB - 1, S5_SEQ_PER_RB, S5_NS).transpose(0, 2, 1, 3, 4)
        ycore, hfin = _s5_scan(l, u, *s5_w, h0)
        hfin = hfin.reshape(S5_GROUPS, BATCH, S5_NS)
        fin_re.append(hfin[:, :, 0:2 * n].reshape(S5_GROUPS, BATCH, 2, n).transpose(1, 2, 0, 3))
        fin_im.append(hfin[:, :, 2 * n:].reshape(S5_GROUPS, BATCH, 2, n).transpose(1, 2, 0, 3))

        s0 = jnp.concatenate([jnp.zeros((BATCH, 2, HGRN_HEADS, HGRN_DV, HGRN_DK), F32),
                              jnp.swapaxes(state_hgrn[:, l], -1, -2)], axis=0)
        o_f, o_b, sfin = _hgrn_scan(qs, v, lf_f, k_f, lf_b, k_b, s0)
        fin_hg.append(jnp.swapaxes(sfin[:BATCH], -1, -2))

        x1, h2, ri, gate, counts = _out_proj(
            tile_cond, x, ycore, u, o_f, o_b, gs, mod[l], norm_w[l], s5_d[l].reshape(1, S5_WIDTH),
            w_glu_b[l], s5_b_glu[l].reshape(1, S5_WIDTH), hgrn_norm_w[l].reshape(1, HGRN_DV),
            w_out_b[l], router_parts[:, l], router_b[l].reshape(1, N_EXPERTS))

        slot, slot_t, groups, tile_groups = _routing_tables(ri, counts)
        xs = _dispatch(slot_t, h2)
        ys = _moe_experts(l, *groups, xs, exp_w1, exp_b1, exp_w2, exp_b2)
        x = _combine(l == DEPTH - 1, tile_cond, tile_groups, x1, slot, gate, mod[l], norm_w[l], ys)

    y_prompt, y_sample = x
    return (y_prompt.reshape(BATCH, SEQ, D_MODEL), y_sample.reshape(DEC_BATCH, DEC_SEQ, D_MODEL),
            jnp.stack(fin_re, axis=1), jnp.stack(fin_im, axis=1), jnp.stack(fin_hg, axis=1))
```

```python
import functools
import math

import jax
import jax.numpy as jnp
from jax import lax
from jax.experimental import pallas as pl
from jax.experimental.pallas import tpu as pltpu

F32 = jnp.float32
BF16 = jnp.bfloat16

D_MODEL = 1024
BATCH = 16
SEQ = 256
DEPTH = 2
DEC_BATCH = 4
DEC_SEQ = 2048
GRID_W = 64
S5_WIDTH = 512
S5_GROUP_CH = 16
S5_GROUPS = 32
S5_STATE = 64
HGRN_WIDTH = 512
HGRN_DK = 128
HGRN_HEADS = 4
HGRN_DV = 128
IN_COLS = S5_WIDTH + 5 * HGRN_WIDTH
N_EXPERTS = 32
TOP_K = 4
D_FF = D_MODEL
SWIGLU_LIMIT = 7.0
SWIGLU_ALPHA = 1.702
NORM_EPS = 1e-6
POS_BASE = 10000.0

N_PROMPT = BATCH * SEQ
N_SAMPLE = DEC_BATCH * DEC_SEQ
N_TOK = N_PROMPT + N_SAMPLE
N_SEQ = BATCH + DEC_BATCH
N_COND = 8

TM = 256
N_TILES = N_TOK // TM
S5_T = 16
S5_LT = 128
S5_GPT = S5_LT // S5_GROUP_CH
S5_NLT = S5_WIDTH // S5_LT
S5_RB = N_PROMPT
S5_CR = S5_RB // S5_T
S5_NRB = N_TOK // S5_RB
S5_SEQ_PER_RB = S5_RB // DEC_SEQ
S5_NLAG = 2 * S5_T - 1
S5_NS = 4 * S5_STATE
HC = 128
N_HCHUNK = N_TOK // HC
EXP_CLAMP = 80.0
MOE_BM = 256
N_ASSIGN = N_TOK * TOP_K
SEG_ALIGN = 16
TILE_SLOTS = -(-(TM * TOP_K + N_EXPERTS * (SEG_ALIGN - 1)) // 128) * 128
MOE_BLOCKS = -(-(N_ASSIGN + N_TILES * N_EXPERTS * (SEG_ALIGN - 1) + N_EXPERTS * (MOE_BM - 1)) // MOE_BM)
MOE_GRP = SEG_ALIGN
MOE_TILE_GROUPS = TILE_SLOTS // MOE_GRP
MOE_BLK_GROUPS = MOE_BM // MOE_GRP
MOE_GROUPS = MOE_BLOCKS * MOE_BLK_GROUPS
MOE_ZERO_GROUP = N_TILES * MOE_TILE_GROUPS
MOE_PAD_GROUPS = N_EXPERTS * (MOE_BLK_GROUPS - 1)
MOE_XS_ROWS = (N_TILES + 1) * TILE_SLOTS
MOE_YS_ROWS = N_TILES * TILE_SLOTS + MOE_PAD_GROUPS * MOE_GRP
MOE_BASE_ROWS = TM * TOP_K
MOE_TAIL_ARMS = (256, 128, 64, 32, 16)
VMEM_LIMIT = 56 * 1024 * 1024


def _rms(x, w):
    return x * lax.rsqrt(jnp.mean(x * x, axis=-1, keepdims=True) + NORM_EPS) * w


def _silu(x):
    return x * jax.nn.sigmoid(x)


MOD_TN = 1536


def _mod_kernel(cond_ref, w_ref, b_ref, o_ref):
    s = _silu(cond_ref[...]).astype(BF16)
    o_ref[0] = jnp.dot(s, w_ref[0].astype(BF16), preferred_element_type=F32) + b_ref[0]


def _modulation(cond, ada_w, ada_b):
    return pl.pallas_call(
        _mod_kernel,
        grid=(DEPTH, 6 * D_MODEL // MOD_TN),
        in_specs=[
            pl.BlockSpec((N_COND, D_MODEL), lambda l, j: (0, 0)),
            pl.BlockSpec((1, D_MODEL, MOD_TN), lambda l, j: (l, 0, j)),
            pl.BlockSpec((1, 1, MOD_TN), lambda l, j: (l, 0, j)),
        ],
        out_specs=pl.BlockSpec((1, N_COND, MOD_TN), lambda l, j: (l, 0, j)),
        out_shape=jax.ShapeDtypeStruct((DEPTH, N_COND, 6 * D_MODEL), F32),
        compiler_params=pltpu.CompilerParams(vmem_limit_bytes=VMEM_LIMIT),
        name="adaln_mod",
    )(cond, ada_w, ada_b.reshape(DEPTH, 1, 6 * D_MODEL))


def _in_kernel(layer, cond_ref, x_ref, mod_ref, nw_ref, w_ref, lb_ref,
               u_ref, q_ref, v_ref, g_ref, lff_ref, kf_ref, lfb_ref, kb_ref):
    del cond_ref
    mod = mod_ref[0]
    sh1 = mod[:, 0:D_MODEL]
    sc1 = mod[:, D_MODEL:2 * D_MODEL]
    h = _rms(x_ref[...], nw_ref[0:1, :]) * (1.0 + sc1) + sh1
    z = jnp.dot(h.astype(BF16), w_ref[...], preferred_element_type=F32)
    w = HGRN_WIDTH
    c0 = S5_WIDTH
    u_ref[...] = z[:, 0:c0]
    q_ref[...] = _silu(z[:, c0:c0 + w]) * (HGRN_DK ** -0.5)
    v_ref[...] = z[:, c0 + 3 * w:c0 + 4 * w]
    g_ref[...] = _silu(z[:, c0 + 4 * w:c0 + 5 * w])
    lbp = lb_ref[...]
    e = jnp.exp(lbp - jnp.max(lbp, axis=0, keepdims=True))
    probs = e / jnp.sum(e, axis=0, keepdims=True)
    lb = jnp.sum(probs[0:layer + 1], axis=0) - probs[0]
    for d, (lf_ref, k_ref) in enumerate(((lff_ref, kf_ref), (lfb_ref, kb_ref))):
        f = z[:, c0 + (1 + d) * w:c0 + (2 + d) * w]
        lbd = lb[d:d + 1, :]
        fg = lbd + (1.0 - lbd) * jax.nn.sigmoid(f)
        lf_ref[...] = jnp.log(fg)
        k_ref[...] = 1.0 - fg


def _in_proj(layer, tile_cond, x, mod_l, norm_w_l, w_in_l, hgrn_lb):
    tok = lambda i, c: (i, 0)
    out = jax.ShapeDtypeStruct((N_TOK, HGRN_WIDTH), F32)
    return pl.pallas_call(
        functools.partial(_in_kernel, layer),
        grid_spec=pltpu.PrefetchScalarGridSpec(
            num_scalar_prefetch=1,
            grid=(N_TILES,),
            in_specs=[
                pl.BlockSpec((TM, D_MODEL), tok),
                pl.BlockSpec((1, 1, 6 * D_MODEL), lambda i, c: (c[i], 0, 0)),
                pl.BlockSpec((4, D_MODEL), lambda i, c: (0, 0)),
                pl.BlockSpec((D_MODEL, IN_COLS), lambda i, c: (0, 0)),
                pl.BlockSpec((DEPTH, 2, HGRN_WIDTH), lambda i, c: (0, 0, 0)),
            ],
            out_specs=[pl.BlockSpec((TM, HGRN_WIDTH), tok)] * 8,
        ),
        out_shape=[out] * 8,
        compiler_params=pltpu.CompilerParams(
            dimension_semantics=("arbitrary",), vmem_limit_bytes=VMEM_LIMIT),
        name="in_proj",
    )(tile_cond, x, mod_l, norm_w_l, w_in_l, hgrn_lb)


def _s5_layer_weights(a_re, a_im, log_dt, b_re, b_im, c_re, c_im):
    hp = lax.Precision.HIGHEST
    t = S5_T
    dt = jnp.exp(log_dt)[..., None]
    lam_re = jnp.minimum(a_re, -1e-4)
    lam_im = a_im
    mag = jnp.exp(dt * lam_re)
    ang = dt * lam_im
    ab_re = mag * jnp.cos(ang)
    ab_im = mag * jnp.sin(ang)
    den = lam_re * lam_re + lam_im * lam_im
    nr = ab_re - 1.0
    ni = ab_im
    co_re = (nr * lam_re + ni * lam_im) / den
    co_im = (ni * lam_re - nr * lam_im) / den
    bb_re = co_re[..., None] * b_re - co_im[..., None] * b_im
    bb_im = co_re[..., None] * b_im + co_im[..., None] * b_re
    pr = [jnp.ones_like(ab_re)]
    pi = [jnp.zeros_like(ab_im)]
    for _ in range(t):
        pr.append(pr[-1] * ab_re - pi[-1] * ab_im)
        pi.append(pr[-2] * ab_im + pi[-1] * ab_re)
    pr = jnp.stack(pr, axis=1)
    pi = jnp.stack(pi, axis=1)
    abr = pr[..., None] * bb_re[:, None] - pi[..., None] * bb_im[:, None]
    abi = pr[..., None] * bb_im[:, None] + pi[..., None] * bb_re[:, None]
    kk = jnp.einsum('dgon,dkgni->dkgio', jnp.concatenate([c_re, -c_im], axis=-1),
                    jnp.concatenate([abr, abi], axis=-2), precision=hp)
    kern = (kk[0], kk[1])
    abs_ = ((abr[0], abi[0]), (abr[1], abi[1]))
    pows = ((pr[0], pi[0]), (pr[1], pi[1]))
    p = S5_GROUP_CH
    lags = jnp.concatenate([kern[1][t - 1:0:-1], (kern[0][0] + kern[1][0])[None], kern[0][1:t]], axis=0)
    m = lags.reshape(S5_NLAG, S5_NLT, S5_GPT, p, p)
    bd = jnp.einsum('ltaio,ab->ltaibo', m, jnp.eye(S5_GPT, dtype=F32))
    wcat = bd.reshape(S5_NLAG, S5_NLT, S5_LT, S5_LT).transpose(1, 2, 0, 3).reshape(S5_NLT, S5_LT, S5_NLAG * S5_LT)
    (abr_f, abi_f), (abr_b, abi_b) = abs_
    parts = (abr_f[t - 1::-1], abr_b[:t], abi_f[t - 1::-1], abi_b[:t])
    wst = jnp.concatenate([x.transpose(1, 0, 3, 2) for x in parts], axis=-1)
    wst = wst.reshape(S5_NLT, S5_GPT, t, p, S5_NS).transpose(0, 2, 1, 3, 4).reshape(S5_NLT, t * S5_LT, S5_NS)
    (pr_f, pi_f), (pr_b, pi_b) = pows

    def out_rows(cr, ci, pr_, pi_):
        re_rows = cr[None] * pr_[:, :, None, :] - ci[None] * pi_[:, :, None, :]
        im_rows = -(cr[None] * pi_[:, :, None, :] + ci[None] * pr_[:, :, None, :])
        return re_rows.transpose(1, 3, 0, 2), im_rows.transpose(1, 3, 0, 2)

    fre, fim = out_rows(c_re[0], c_im[0], pr_f[1:t + 1], pi_f[1:t + 1])
    bre, bim = out_rows(c_re[1], c_im[1], pr_b[t:0:-1], pi_b[t:0:-1])
    wout = lax.optimization_barrier(jnp.concatenate([fre, bre, fim, bim], axis=1))
    wdense = (wout.reshape(S5_NLT, S5_GPT, S5_NS, t, p).transpose(0, 2, 3, 1, 4)
              .reshape(S5_NLT, S5_NS, t * S5_LT))
    at = jnp.stack([jnp.concatenate([pr_f[t], pr_b[t]], axis=-1),
                    jnp.concatenate([pi_f[t], pi_b[t]], axis=-1)], axis=1)
    return (wcat.astype(BF16), wst.astype(BF16), wdense.astype(BF16),
            at.reshape(S5_NLT, S5_GPT, 2, 2 * S5_STATE))


def _s5_weights(*params):
    return jax.vmap(_s5_layer_weights)(*params)


def _s5_kernel(u_ref, wcat_ref, wst_ref, wd_ref, at_ref, h0_ref, y_ref, hfin_ref,
               wbig, dh_scr, hf_scr, hb_scr):
    t = S5_T
    n2 = 2 * S5_STATE
    r = pl.program_id(1)

    @pl.when(r == 0)
    def _():
        for s in range(t):
            wbig[s * S5_LT:(s + 1) * S5_LT, :] = wcat_ref[0, :, (t - 1 - s) * S5_LT:(2 * t - 1 - s) * S5_LT]

    xcat = jnp.concatenate([u_ref[pl.ds(s, S5_CR, stride=t), :].astype(BF16) for s in range(t)], axis=-1)
    yacc = jnp.dot(xcat, wbig[...], preferred_element_type=F32)
    lane_k = lax.broadcasted_iota(jnp.int32, (1, t * S5_LT), 1)
    grp_k = (lane_k % S5_LT) // S5_GROUP_CH
    for gi in range(S5_GPT):
        xg = jnp.where(grp_k == gi, xcat, jnp.zeros_like(xcat))
        dh = jnp.dot(xg, wst_ref[0], preferred_element_type=F32)
        dh_scr[gi, 0] = dh[:, 0:n2]
        dh_scr[gi, 1] = dh[:, n2:]

    lane = lax.broadcasted_iota(jnp.int32, (1, n2), 1)
    fwd_lane = lane < S5_STATE
    are = [at_ref[0, gi, 0:1, :] for gi in range(S5_GPT)]
    aim = [at_ref[0, gi, 1:2, :] for gi in range(S5_GPT)]

    def advance(gi, hre, him, dre, dim):
        return (are[gi] * hre - aim[gi] * him + dre, are[gi] * him + aim[gi] * hre + dim)

    @pl.when(r == 0)
    def _():
        nc = SEQ // t
        for gi in range(S5_GPT):
            hre = jnp.zeros((BATCH, n2), F32)
            him = jnp.zeros((BATCH, n2), F32)
            for s in range(nc):
                rows_f = pl.ds(s, BATCH, stride=nc)
                rows_b = pl.ds(nc - 1 - s, BATCH, stride=nc)
                hf_scr[gi, 0, rows_f, :] = hre
                hf_scr[gi, 1, rows_f, :] = him
                hb_scr[gi, 0, rows_b, :] = hre
                hb_scr[gi, 1, rows_b, :] = him
                dre = jnp.where(fwd_lane, dh_scr[gi, 0, rows_f, :], dh_scr[gi, 0, rows_b, :])
                dim = jnp.where(fwd_lane, dh_scr[gi, 1, rows_f, :], dh_scr[gi, 1, rows_b, :])
                hre, him = advance(gi, hre, him, dre, dim)
            hfin_ref[0, gi] = jnp.concatenate([hre, him], axis=-1)

    @pl.when(r > 0)
    def _():
        nc = DEC_SEQ // t
        nb = S5_SEQ_PER_RB
        init = tuple((h0_ref[0, 0, gi, b:b + 1, 0:n2], h0_ref[0, 0, gi, b:b + 1, n2:])
                     for gi in range(S5_GPT) for b in range(nb))

        def step(o, carry):
            new = []
            for gi in range(S5_GPT):
                for b in range(nb):
                    hre, him = carry[gi * nb + b]
                    rf = pl.multiple_of(b * nc + o * 8, 8)
                    rb = pl.multiple_of(b * nc + nc - 8 - o * 8, 8)
                    dfr = dh_scr[gi, 0, pl.ds(rf, 8), :]
                    dfi = dh_scr[gi, 1, pl.ds(rf, 8), :]
                    dbr = dh_scr[gi, 0, pl.ds(rb, 8), :]
                    dbi = dh_scr[gi, 1, pl.ds(rb, 8), :]
                    ent_re, ent_im = [], []
                    for i in range(8):
                        ent_re.append(hre)
                        ent_im.append(him)
                        dre = jnp.where(fwd_lane, dfr[i:i + 1], dbr[7 - i:8 - i])
                        dim = jnp.where(fwd_lane, dfi[i:i + 1], dbi[7 - i:8 - i])
                        hre, him = advance(gi, hre, him, dre, dim)
                    hf_scr[gi, 0, pl.ds(rf, 8), :] = jnp.concatenate(ent_re, axis=0)
                    hf_scr[gi, 1, pl.ds(rf, 8), :] = jnp.concatenate(ent_im, axis=0)
                    hb_scr[gi, 0, pl.ds(rb, 8), :] = jnp.concatenate(ent_re[::-1], axis=0)
                    hb_scr[gi, 1, pl.ds(rb, 8), :] = jnp.concatenate(ent_im[::-1], axis=0)
                    new.append((hre, him))
            return tuple(new)

        lax.fori_loop(0, nc // 8, step, init)

    for gi in range(S5_GPT):
        hent = jnp.concatenate([jnp.where(fwd_lane, hf_scr[gi, 0], hb_scr[gi, 0]),
                                jnp.where(fwd_lane, hf_scr[gi, 1], hb_scr[gi, 1])], axis=-1).astype(BF16)
        yi = jnp.dot(hent, wd_ref[0], preferred_element_type=F32)
        yacc = yacc + jnp.where(grp_k == gi, yi, 0.0)
    for s in range(t):
        y_ref[pl.ds(s, S5_CR, stride=t), :] = yacc[:, s * S5_LT:(s + 1) * S5_LT]


def _s5_scan(layer, u, wcat, wst, wdense, at, h0):
    tile = lambda j, r: (layer * S5_NLT + j, 0, 0)
    merge = lambda w: w.reshape((DEPTH * S5_NLT,) + w.shape[2:])
    wcat, wst, wdense, at = merge(wcat), merge(wst), merge(wdense), merge(at)
    return pl.pallas_call(
        _s5_kernel,
        grid=(S5_NLT, S5_NRB),
        in_specs=[
            pl.BlockSpec((S5_RB, S5_LT), lambda j, r: (r, j)),
            pl.BlockSpec((1, S5_LT, S5_NLAG * S5_LT), tile),
            pl.BlockSpec((1, S5_T * S5_LT, S5_NS), tile),
            pl.BlockSpec((1, S5_NS, S5_T * S5_LT), tile),
            pl.BlockSpec((1, S5_GPT, 2, 2 * S5_STATE), lambda j, r: (layer * S5_NLT + j, 0, 0, 0)),
            pl.BlockSpec((1, 1, S5_GPT, S5_SEQ_PER_RB, S5_NS), lambda j, r: (j, jnp.maximum(r - 1, 0), 0, 0, 0)),
        ],
        out_specs=[
            pl.BlockSpec((S5_RB, S5_LT), lambda j, r: (r, j)),
            pl.BlockSpec((1, S5_GPT, BATCH, S5_NS), lambda j, r: (j, 0, 0, 0)),
        ],
        out_shape=[
            jax.ShapeDtypeStruct((N_TOK, S5_WIDTH), F32),
            jax.ShapeDtypeStruct((S5_NLT, S5_GPT, BATCH, S5_NS), F32),
        ],
        scratch_shapes=[pltpu.VMEM((S5_T * S5_LT, S5_T * S5_LT), BF16)]
                       + [pltpu.VMEM((S5_GPT, 2, S5_CR, 2 * S5_STATE), F32)] * 3,
        compiler_params=pltpu.CompilerParams(
            dimension_semantics=("arbitrary", "arbitrary"), vmem_limit_bytes=VMEM_LIMIT),
        name="s5_scan",
    )(u, wcat, wst, wdense, at, h0)


def _split3(x):
    hi = x.astype(BF16)
    r1 = x - hi.astype(F32)
    mid = r1.astype(BF16)
    lo = (r1 - mid.astype(F32)).astype(BF16)
    return hi, mid, lo


def _piecewise_rows(b, blk, row_in_blk):
    parts = []
    for j in range(HC // blk):
        r = j * blk + row_in_blk
        parts.append(jnp.broadcast_to(b[r:r + 1, :], (blk, b.shape[1])))
    return parts[0] if len(parts) == 1 else jnp.concatenate(parts, axis=0)


def _nt(a, b):
    return lax.dot_general(a, b, (((1,), (1,)), ((), ())), preferred_element_type=F32)


def _tn(a, b):
    return lax.dot_general(a, b, (((0,), (0,)), ((), ())), preferred_element_type=F32)


def _hgrn_dir(reverse, q_ref, v_ref, lf_ref, k_ref, o_ref, st_ref, b_scr):
    row = lax.broadcasted_iota(jnp.int32, (HC, HC), 0)
    col = lax.broadcasted_iota(jnp.int32, (HC, HC), 1)
    causal = (col >= row) if reverse else (col <= row)
    tri = jnp.where(causal, 1.0, 0.0).astype(BF16)
    lf = lf_ref[...]
    hi, mid, lo = _split3(lf)
    ball = (jnp.dot(tri, hi, preferred_element_type=F32) + jnp.dot(tri, mid, preferred_element_type=F32)
            + jnp.dot(tri, lo, preferred_element_type=F32))
    last = 0 if reverse else HC - 1
    masks = []
    for blk in (128, 64, 32):
        half = blk // 2
        same = (row // blk) == (col // blk)
        t_hi = (row % blk) >= half
        s_hi = (col % blk) >= half
        if reverse:
            masks.append(same & jnp.logical_not(t_hi) & s_hi)
        else:
            masks.append(same & t_hi & jnp.logical_not(s_hi))
    b_scr[...] = ball
    mrow = 8 if reverse else 7
    mid = _piecewise_rows(ball, 16, mrow)
    spread = jnp.zeros((1, HGRN_WIDTH), F32)
    for j in range(HC // 16):
        for edge in (16 * j, 16 * j + 15):
            spread = jnp.maximum(spread, jnp.abs(ball[edge:edge + 1, :] - ball[16 * j + mrow:16 * j + mrow + 1, :]))
    stable = jnp.max(spread) <= EXP_CLAMP
    diag_ok = ((row // 16) == (col // 16)) & causal & stable
    for h in range(HGRN_HEADS):
        sl = slice(h * HGRN_DK, (h + 1) * HGRN_DK)
        b = ball[:, sl]
        q = q_ref[:, sl]
        k = k_ref[:, sl]
        v = v_ref[:, sl].astype(BF16)
        st = st_ref[h]
        b_last = b[last:last + 1, :]
        q_in = (q * jnp.exp(b)).astype(BF16)
        k_in = (k * jnp.exp(b_last - b)).astype(BF16)
        o = _nt(q_in, st.astype(BF16))
        st_ref[h] = jnp.exp(b_last) * st + _tn(v, k_in)
        scores = jnp.zeros((HC, HC), F32)
        for blk, mask in zip((128, 64, 32), masks):
            half = blk // 2
            zero = jnp.zeros((half, HGRN_DK), BF16)
            qparts, kparts = [], []
            for j in range(HC // blk):
                early = slice(j * blk, j * blk + half)
                late = slice(j * blk + half, (j + 1) * blk)
                if reverse:
                    m = b[j * blk + half:j * blk + half + 1, :]
                    qrows, krows = early, late
                else:
                    m = b[j * blk + half - 1:j * blk + half, :]
                    qrows, krows = late, early
                qe = (q[qrows] * jnp.exp(b[qrows] - m)).astype(BF16)
                ke = (k[krows] * jnp.exp(m - b[krows])).astype(BF16)
                qparts += [qe, zero] if reverse else [zero, qe]
                kparts += [zero, ke] if reverse else [ke, zero]
            part = _nt(jnp.concatenate(qparts, axis=0), jnp.concatenate(kparts, axis=0))
            scores = scores + (part if blk == HC else jnp.where(mask, part, 0.0))
        m = mid[:, sl]
        qd = (q * jnp.exp(jnp.minimum(b - m, EXP_CLAMP))).astype(BF16)
        kd = (k * jnp.exp(jnp.minimum(m - b, EXP_CLAMP))).astype(BF16)
        scores = scores + jnp.where(diag_ok, _nt(qd, kd), 0.0)
        o_ref[:, sl] = o + jnp.dot(scores.astype(BF16), v, preferred_element_type=F32)

    return stable


def _hgrn_exact_diagonal(reverse, stable, q_ref, v_ref, k_ref, o_ref, b_scr):
    @pl.when(jnp.logical_not(stable))
    def _():
        pos = lax.broadcasted_iota(jnp.int32, (HC, 1), 0) % 16

        def lag(d, c):
            shift = (HC - d) % HC if reverse else d
            valid = (pos + d <= 15) if reverse else (pos >= d)
            for h in range(HGRN_HEADS):
                sl = slice(h * HGRN_DK, (h + 1) * HGRN_DK)
                b = b_scr[:, sl]
                bs = pltpu.roll(b, shift, 0)
                ks = pltpu.roll(k_ref[:, sl], shift, 0)
                vs = pltpu.roll(v_ref[:, sl], shift, 0)
                e = jnp.exp(jnp.where(valid, b - bs, 0.0))
                w = jnp.sum(q_ref[:, sl] * ks * e, axis=-1, keepdims=True)
                o_ref[:, sl] = o_ref[:, sl] + jnp.where(valid, w, 0.0) * vs
            return c

        lax.fori_loop(0, 16, lag, 0)


def _hgrn_kernel(cf_ref, cb_ref, seq_ref, first_ref, last_ref,
                 qf_ref, vf_ref, lff_ref, kf_ref, qb_ref, vb_ref, lfb_ref, kb_ref, s0_ref,
                 of_ref, ob_ref, sfin_ref, st_scr, b_scr):
    del cf_ref, cb_ref, seq_ref
    j = pl.program_id(0)

    @pl.when(first_ref[j] == 1)
    def _():
        st_scr[...] = s0_ref[0]

    ok_f = _hgrn_dir(False, qf_ref, vf_ref, lff_ref, kf_ref, of_ref, st_scr.at[0], b_scr.at[0])
    ok_b = _hgrn_dir(True, qb_ref, vb_ref, lfb_ref, kb_ref, ob_ref, st_scr.at[1], b_scr.at[1])
    _hgrn_exact_diagonal(False, ok_f, qf_ref, vf_ref, kf_ref, of_ref, b_scr.at[0])
    _hgrn_exact_diagonal(True, ok_b, qb_ref, vb_ref, kb_ref, ob_ref, b_scr.at[1])

    @pl.when(last_ref[j] == 1)
    def _():
        sfin_ref[0] = st_scr[...]


def _hgrn_tables():
    cf, cb, sq, first, last = [], [], [], [], []
    base = 0
    for s in range(N_SEQ):
        nc = (SEQ if s < BATCH else DEC_SEQ) // HC
        for t in range(nc):
            cf.append(base + t)
            cb.append(base + nc - 1 - t)
            sq.append(s)
            first.append(int(t == 0))
            last.append(int(t == nc - 1))
        base += nc
    return tuple(jnp.asarray(x, jnp.int32) for x in (cf, cb, sq, first, last))


def _hgrn_scan(qs, v, lf_f, k_f, lf_b, k_b, s0):
    fwd = lambda j, cf, cb, sq, fi, la: (cf[j], 0)
    bwd = lambda j, cf, cb, sq, fi, la: (cb[j], 0)
    seq = lambda j, cf, cb, sq, fi, la: (sq[j], 0, 0, 0, 0)
    tile = (HC, HGRN_WIDTH)
    sblk = (1, 2, HGRN_HEADS, HGRN_DV, HGRN_DK)
    out = jax.ShapeDtypeStruct((N_TOK, HGRN_WIDTH), F32)
    return pl.pallas_call(
        _hgrn_kernel,
        grid_spec=pltpu.PrefetchScalarGridSpec(
            num_scalar_prefetch=5,
            grid=(N_HCHUNK,),
            in_specs=[pl.BlockSpec(tile, fwd)] * 4 + [pl.BlockSpec(tile, bwd)] * 4
                     + [pl.BlockSpec(sblk, seq)],
            out_specs=[pl.BlockSpec(tile, fwd), pl.BlockSpec(tile, bwd), pl.BlockSpec(sblk, seq)],
            scratch_shapes=[pltpu.VMEM(sblk[1:], F32), pltpu.VMEM((2, HC, HGRN_WIDTH), F32)],
        ),
        out_shape=[out, out, jax.ShapeDtypeStruct((N_SEQ,) + sblk[1:], F32)],
        compiler_params=pltpu.CompilerParams(
            dimension_semantics=("arbitrary",), vmem_limit_bytes=VMEM_LIMIT),
        name="hgrn_scan",
    )(*_hgrn_tables(), qs, v, lf_f, k_f, qs, v, lf_b, k_b, s0)


def _gelu_tanh(x):
    return 0.5 * x * (1.0 + jnp.tanh(math.sqrt(2.0 / math.pi) * (x + 0.044715 * (x * x * x))))


def _out_kernel(cond_ref, x_ref, yc_ref, u_ref, of_ref, ob_ref, g_ref, mod_ref, nw_ref, d_ref,
                wglu_ref, bglu_ref, hnw_ref, wout_ref, rw_ref, rb_ref,
                x1_ref, h2_ref, ri_ref, rg_ref, cnt_ref):
    del cond_ref
    mod = mod_ref[0]
    g1 = mod[:, 2 * D_MODEL:3 * D_MODEL]
    sh2 = mod[:, 3 * D_MODEL:4 * D_MODEL]
    sc2 = mod[:, 4 * D_MODEL:5 * D_MODEL]
    y = _gelu_tanh(yc_ref[...] + d_ref[...] * u_ref[...])
    y_s5 = y * jax.nn.sigmoid(jnp.dot(y.astype(BF16), wglu_ref[...], preferred_element_type=F32) + bglu_ref[...])
    o = of_ref[...] + ob_ref[...]
    gs = g_ref[...]
    heads = []
    for h in range(HGRN_HEADS):
        sl = slice(h * HGRN_DV, (h + 1) * HGRN_DV)
        heads.append(_rms(o[:, sl], hnw_ref[...]) * gs[:, sl])
    y_hg = jnp.concatenate(heads, axis=-1)
    mix = (jnp.dot(y_s5.astype(BF16), wout_ref[0:S5_WIDTH, :], preferred_element_type=F32)
           + jnp.dot(y_hg.astype(BF16), wout_ref[S5_WIDTH:, :], preferred_element_type=F32))
    x1 = x_ref[...] + g1 * _rms(mix, nw_ref[1:2, :])
    x1_ref[...] = x1
    h2 = _rms(x1, nw_ref[2:3, :]) * (1.0 + sc2) + sh2
    h2_hi = h2.astype(BF16)
    h2_ref[...] = h2_hi
    h2_lo = (h2 - h2_hi.astype(F32)).astype(BF16)
    logits = rb_ref[...]
    for a in (h2_hi, h2_lo):
        for part in range(2):
            logits = logits + jnp.dot(a, rw_ref[part], preferred_element_type=F32)
    eidx = lax.broadcasted_iota(jnp.int32, (TM, N_EXPERTS), 1).astype(F32)
    vals = logits
    top_v, top_i, onehots = [], [], []
    for _ in range(TOP_K):
        mx = jnp.max(vals, axis=-1, keepdims=True)
        ix = jnp.min(jnp.where(vals == mx, eidx, float(N_EXPERTS)), axis=-1, keepdims=True)
        sel = eidx == ix
        top_v.append(mx)
        top_i.append(ix)
        onehots.append(sel)
        vals = jnp.where(sel, -jnp.inf, vals)
    ex = [jnp.exp(tv - top_v[0]) for tv in top_v]
    den = ex[0] + ex[1] + ex[2] + ex[3]
    tot = jnp.zeros((TM, N_EXPERTS), F32)
    for sel in onehots:
        tot = tot + jnp.where(sel, 1.0, 0.0)
    r_t = lax.broadcasted_iota(jnp.int32, (TM, TM), 0)
    r_s = lax.broadcasted_iota(jnp.int32, (TM, TM), 1)
    strict = jnp.where(r_s < r_t, 1.0, 0.0).astype(BF16)
    before = jnp.dot(strict, tot.astype(BF16), preferred_element_type=F32)
    cnt = jnp.sum(tot, axis=0, keepdims=True)
    seg = jnp.floor((cnt + (SEG_ALIGN - 1)) * (1.0 / SEG_ALIGN)) * SEG_ALIGN
    e_r = lax.broadcasted_iota(jnp.int32, (N_EXPERTS, N_EXPERTS), 0)
    e_c = lax.broadcasted_iota(jnp.int32, (N_EXPERTS, N_EXPERTS), 1)
    lstart = jnp.dot(seg, jnp.where(e_r < e_c, 1.0, 0.0), precision=lax.Precision.HIGHEST,
                     preferred_element_type=F32)
    before = before + lstart
    lane = lax.broadcasted_iota(jnp.int32, (TM, 128), 1)
    ri = jnp.zeros((TM, 128), F32)
    rg = jnp.zeros((TM, 128), F32)
    for kk in range(TOP_K):
        rank = jnp.sum(jnp.where(onehots[kk], before, 0.0), axis=-1, keepdims=True)
        ri = jnp.where(lane == kk, top_i[kk], ri)
        ri = jnp.where(lane == TOP_K + kk, rank, ri)
        rg = jnp.where(lane == kk, ex[kk] / den, rg)
    ri_ref[...] = ri.astype(jnp.int32)
    rg_ref[...] = rg
    cnt_ref[0] = cnt.astype(jnp.int32)


def _out_proj(tile_cond, x, ycore, u, o_f, o_b, gs, mod_l, norm_w_l, s5_d_l, wglu_l, bglu_l, hnw_l,
              wout_l, rw_l, rb_l):
    tok = lambda i, c: (i, 0)
    full2 = lambda i, c: (0, 0)
    half = pl.BlockSpec((TM, HGRN_WIDTH), tok)
    wide = pl.BlockSpec((TM, D_MODEL), tok)
    return pl.pallas_call(
        _out_kernel,
        grid_spec=pltpu.PrefetchScalarGridSpec(
            num_scalar_prefetch=1,
            grid=(N_TILES,),
            in_specs=[
                wide, half, half, half, half, half,
                pl.BlockSpec((1, 1, 6 * D_MODEL), lambda i, c: (c[i], 0, 0)),
                pl.BlockSpec((4, D_MODEL), full2),
                pl.BlockSpec((1, S5_WIDTH), full2),
                pl.BlockSpec((S5_WIDTH, S5_WIDTH), full2),
                pl.BlockSpec((1, S5_WIDTH), full2),
                pl.BlockSpec((1, HGRN_DV), full2),
                pl.BlockSpec((D_MODEL, D_MODEL), full2),
                pl.BlockSpec((2, D_MODEL, N_EXPERTS), lambda i, c: (0, 0, 0)),
                pl.BlockSpec((1, N_EXPERTS), full2),
            ],
            out_specs=[wide, wide, pl.BlockSpec((TM, 128), tok), pl.BlockSpec((TM, 128), tok),
                       pl.BlockSpec((1, 1, N_EXPERTS), lambda i, c: (i, 0, 0))],
        ),
        out_shape=[
            jax.ShapeDtypeStruct((N_TOK, D_MODEL), F32),
            jax.ShapeDtypeStruct((N_TOK, D_MODEL), BF16),
            jax.ShapeDtypeStruct((N_TOK, 128), jnp.int32),
            jax.ShapeDtypeStruct((N_TOK, 128), F32),
            jax.ShapeDtypeStruct((N_TILES, 1, N_EXPERTS), jnp.int32),
        ],
        compiler_params=pltpu.CompilerParams(
            dimension_semantics=("arbitrary",), vmem_limit_bytes=VMEM_LIMIT),
        name="out_proj_router",
    )(tile_cond, x, ycore, u, o_f, o_b, gs, mod_l, norm_w_l, s5_d_l, wglu_l, bglu_l, hnw_l,
      wout_l, rw_l, rb_l)


def _dispatch_kernel(slot_ref, h_ref, o_ref):
    srow = lax.broadcasted_iota(jnp.int32, (TILE_SLOTS, TM), 0)
    perm = jnp.zeros((TILE_SLOTS, TM), F32)
    for kk in range(TOP_K):
        perm = perm + jnp.where(srow == slot_ref[0, kk:kk + 1, :], 1.0, 0.0)
    o_ref[...] = jnp.dot(perm.astype(BF16), h_ref[...], preferred_element_type=F32).astype(BF16)


def _dispatch(slot_t, h2):
    return pl.pallas_call(
        _dispatch_kernel,
        grid=(N_TILES + 1,),
        in_specs=[pl.BlockSpec((1, TOP_K, TM), lambda i: (i, 0, 0)),
                  pl.BlockSpec((TM, D_MODEL), lambda i: (jnp.minimum(i, N_TILES - 1), 0))],
        out_specs=pl.BlockSpec((TILE_SLOTS, D_MODEL), lambda i: (i, 0)),
        out_shape=jax.ShapeDtypeStruct((MOE_XS_ROWS, D_MODEL), BF16),
        compiler_params=pltpu.CompilerParams(
            dimension_semantics=("arbitrary",), vmem_limit_bytes=VMEM_LIMIT),
        name="moe_dispatch",
    )(slot_t, h2)


def _moe_kernel(nblk_ref, gbase_ref, gsrc_ref, gdst_ref, w1_ref, b1_ref, w2_ref, b2_ref, xs_ref, ys_ref,
                xbuf, obuf, w1b, w2b, sem_in, sem_out):
    e = pl.program_id(0)
    nb = nblk_ref[e]
    g0 = gbase_ref[e]

    def gather(b, buf):
        return [pltpu.make_async_copy(
            xs_ref.at[pl.ds(pl.multiple_of(gsrc_ref[g0 + b * MOE_BLK_GROUPS + g] * MOE_GRP, MOE_GRP), MOE_GRP)],
            xbuf.at[buf, pl.ds(g * MOE_GRP, MOE_GRP)], sem_in.at[buf]) for g in range(MOE_BLK_GROUPS)]

    def scatter(b, buf):
        return [pltpu.make_async_copy(
            obuf.at[buf, pl.ds(g * MOE_GRP, MOE_GRP)],
            ys_ref.at[pl.ds(pl.multiple_of(gdst_ref[g0 + b * MOE_BLK_GROUPS + g] * MOE_GRP, MOE_GRP), MOE_GRP)],
            sem_out.at[buf]) for g in range(MOE_BLK_GROUPS)]

    @pl.when(nb > 0)
    def _():
        def cast(j, c):
            r = pl.multiple_of(j * 128, 128)
            w1b[pl.ds(r, 128), :] = w1_ref[0, pl.ds(r, 128), :].astype(BF16)
            w2b[pl.ds(r, 128), :] = w2_ref[0, pl.ds(r, 128), :].astype(BF16)
            return c

        for cp in gather(0, 0):
            cp.start(priority=1)
        lax.fori_loop(0, D_MODEL // 128, cast, 0)

        def block(b, c):
            buf = b % 2

            @pl.when(b >= 2)
            def _():
                for cp in scatter(b - 2, buf):
                    cp.wait()

            for cp in gather(b, buf):
                cp.wait()
            for cp in gather(b + 1, 1 - buf):
                cp.start(priority=1)
            h = jnp.dot(xbuf[buf], w1b[...], preferred_element_type=F32) + b1_ref[0]
            glu = jnp.minimum(h[:, :D_FF], SWIGLU_LIMIT)
            lin = jnp.clip(h[:, D_FF:], -SWIGLU_LIMIT, SWIGLU_LIMIT)
            act = glu * jax.nn.sigmoid(SWIGLU_ALPHA * glu) * (lin + 1.0)
            obuf[buf] = (jnp.dot(act.astype(BF16), w2b[...], preferred_element_type=F32) + b2_ref[0]).astype(BF16)
            for cp in scatter(b, buf):
                cp.start(priority=1)
            return c

        lax.fori_loop(0, nb, block, 0)
        for cp in gather(nb, nb % 2):
            cp.wait()

        @pl.when(nb >= 2)
        def _():
            for cp in scatter(nb - 2, nb % 2):
                cp.wait()

        for cp in scatter(nb - 1, (nb - 1) % 2):
            cp.wait()


def _moe_experts(layer, nblk, gbase, gsrc, gdst, xs, w1, b1, w2, b2):
    exp3 = lambda e, *_: (layer * N_EXPERTS + e, 0, 0)
    w1 = w1.reshape(DEPTH * N_EXPERTS, D_MODEL, 2 * D_FF)
    w2 = w2.reshape(DEPTH * N_EXPERTS, D_FF, D_MODEL)
    return pl.pallas_call(
        _moe_kernel,
        grid_spec=pltpu.PrefetchScalarGridSpec(
            num_scalar_prefetch=4,
            grid=(N_EXPERTS,),
            in_specs=[
                pl.BlockSpec((1, D_MODEL, 2 * D_FF), exp3),
                pl.BlockSpec((1, 1, 2 * D_FF), exp3),
                pl.BlockSpec((1, D_FF, D_MODEL), exp3),
                pl.BlockSpec((1, 1, D_MODEL), exp3),
                pl.BlockSpec(memory_space=pl.ANY),
            ],
            out_specs=pl.BlockSpec(memory_space=pl.ANY),
            scratch_shapes=[pltpu.VMEM((2, MOE_BM, D_MODEL), BF16), pltpu.VMEM((2, MOE_BM, D_MODEL), BF16),
                            pltpu.VMEM((D_MODEL, 2 * D_FF), BF16), pltpu.VMEM((D_FF, D_MODEL), BF16),
                            pltpu.SemaphoreType.DMA((2,)), pltpu.SemaphoreType.DMA((2,))],
        ),
        out_shape=jax.ShapeDtypeStruct((MOE_YS_ROWS, D_MODEL), BF16),
        compiler_params=pltpu.CompilerParams(
            dimension_semantics=("arbitrary",), vmem_limit_bytes=VMEM_LIMIT),
        name="moe_experts",
    )(nblk, gbase, gsrc, gdst, w1, b1.reshape(DEPTH * N_EXPERTS, 1, 2 * D_FF), w2,
      b2.reshape(DEPTH * N_EXPERTS, 1, D_MODEL), xs)


def _combine_kernel(split, cond_ref, tg_ref, x1_ref, slot_ref, gate_ref, mod_ref, nw_ref, ys_ref, *rest):
    del cond_ref
    outs, (buf, sem) = rest[:-2], rest[-2:]
    i = pl.program_id(0)
    cur = i % 2

    def fetch(tile, b, wait):
        def go(cp):
            cp.wait() if wait else cp.start()

        row0 = pl.multiple_of(tile * TILE_SLOTS, TILE_SLOTS)
        go(pltpu.make_async_copy(ys_ref.at[pl.ds(row0, MOE_BASE_ROWS)], buf.at[b, pl.ds(0, MOE_BASE_ROWS)],
                                 sem.at[b]))
        extra = tg_ref[tile] * MOE_GRP - MOE_BASE_ROWS
        for arm in MOE_TAIL_ARMS:
            @pl.when((extra & arm) != 0)
            def _():
                off = pl.multiple_of(MOE_BASE_ROWS + (extra & ~(2 * arm - 1)), MOE_GRP)
                go(pltpu.make_async_copy(ys_ref.at[pl.ds(row0 + off, arm)], buf.at[b, pl.ds(off, arm)], sem.at[b]))

    @pl.when(i == 0)
    def _():
        buf[...] = jnp.zeros_like(buf)
        fetch(0, 0, False)

    @pl.when(i + 1 < N_TILES)
    def _():
        fetch(i + 1, 1 - cur, False)

    fetch(i, cur, True)
    scol = lax.broadcasted_iota(jnp.int32, (TM, TILE_SLOTS), 1)
    slot = slot_ref[...]
    gate = gate_ref[...]
    gmat = jnp.zeros((TM, TILE_SLOTS), F32)
    for kk in range(TOP_K):
        gmat = gmat + jnp.where(scol == slot[:, kk:kk + 1], gate[:, kk:kk + 1], 0.0)
    gmat = gmat.astype(BF16)
    ffn = jnp.dot(gmat, buf[cur], preferred_element_type=F32)
    g2 = mod_ref[0][:, 5 * D_MODEL:6 * D_MODEL]
    out = x1_ref[...] + g2 * _rms(ffn, nw_ref[3:4, :])
    if split:
        @pl.when(i < N_PROMPT // TM)
        def _():
            outs[0][...] = out

        @pl.when(i >= N_PROMPT // TM)
        def _():
            outs[1][...] = out
    else:
        outs[0][...] = out


def _combine(split, tile_cond, tile_groups, x1, slot, gate, mod_l, norm_w_l, ys):
    tok = lambda i, *_: (i, 0)
    if split:
        np_tiles = N_PROMPT // TM
        out_specs = [pl.BlockSpec((TM, D_MODEL), lambda i, *_: (jnp.minimum(i, np_tiles - 1), 0)),
                     pl.BlockSpec((TM, D_MODEL), lambda i, *_: (jnp.maximum(i - np_tiles, 0), 0))]
        out_shape = [jax.ShapeDtypeStruct((N_PROMPT, D_MODEL), F32), jax.ShapeDtypeStruct((N_SAMPLE, D_MODEL), F32)]
    else:
        out_specs = pl.BlockSpec((TM, D_MODEL), tok)
        out_shape = jax.ShapeDtypeStruct((N_TOK, D_MODEL), F32)
    return pl.pallas_call(
        functools.partial(_combine_kernel, split),
        grid_spec=pltpu.PrefetchScalarGridSpec(
            num_scalar_prefetch=2,
            grid=(N_TILES,),
            in_specs=[
                pl.BlockSpec((TM, D_MODEL), tok),
                pl.BlockSpec((TM, TOP_K), tok),
                pl.BlockSpec((TM, 128), tok),
                pl.BlockSpec((1, 1, 6 * D_MODEL), lambda i, c, *_: (c[i], 0, 0)),
                pl.BlockSpec((4, D_MODEL), lambda i, *_: (0, 0)),
                pl.BlockSpec(memory_space=pl.ANY),
            ],
            out_specs=out_specs,
            scratch_shapes=[pltpu.VMEM((2, TILE_SLOTS, D_MODEL), BF16), pltpu.SemaphoreType.DMA((2,))],
        ),
        out_shape=out_shape,
        compiler_params=pltpu.CompilerParams(
            dimension_semantics=("arbitrary",), vmem_limit_bytes=VMEM_LIMIT),
        name="moe_combine",
    )(tile_cond, tile_groups, x1, slot, gate, mod_l, norm_w_l, ys)


def _pos_embed_2d(rows, dim):
    r = jnp.repeat(jnp.arange(rows, dtype=F32), GRID_W)
    col = jnp.tile(jnp.arange(GRID_W, dtype=F32), rows)
    quarter = dim // 4
    omega = 1.0 / (POS_BASE ** (jnp.arange(quarter, dtype=F32) / quarter))

    def emb(pos):
        ang = pos[:, None] * omega[None, :]
        return jnp.concatenate([jnp.sin(ang), jnp.cos(ang)], axis=-1)

    return jnp.concatenate([emb(r), emb(col)], axis=-1)


def _assemble_kernel(xp_ref, xs_ref, pos_ref, o_ref):
    i = pl.program_id(0)

    @pl.when(i < N_PROMPT // TM)
    def _():
        o_ref[...] = xp_ref[...]

    @pl.when(i >= N_PROMPT // TM)
    def _():
        o_ref[...] = xs_ref[...] + pos_ref[...]


def _assemble(xp, xs, pos):
    np_tiles = N_PROMPT // TM
    return pl.pallas_call(
        _assemble_kernel,
        grid=(N_TILES,),
        in_specs=[pl.BlockSpec((TM, D_MODEL), lambda i: (jnp.minimum(i, np_tiles - 1), 0)),
                  pl.BlockSpec((TM, D_MODEL), lambda i: (jnp.maximum(i - np_tiles, 0), 0)),
                  pl.BlockSpec((TM, D_MODEL), lambda i: (jnp.maximum(i - np_tiles, 0) % (DEC_SEQ // TM), 0))],
        out_specs=pl.BlockSpec((TM, D_MODEL), lambda i: (i, 0)),
        out_shape=jax.ShapeDtypeStruct((N_TOK, D_MODEL), F32),
        compiler_params=pltpu.CompilerParams(dimension_semantics=("arbitrary",)),
        name="assemble_tokens",
    )(xp, xs, pos)


def _routing_tables(ri, counts):
    i32 = jnp.int32
    slot = ri[:, TOP_K:2 * TOP_K].reshape(N_TILES, TM, TOP_K)
    cnt = counts.reshape(N_TILES, N_EXPERTS)
    seg = (cnt + SEG_ALIGN - 1) // SEG_ALIGN
    lstart = jnp.cumsum(seg, axis=1) - seg
    tile_groups = jnp.sum(seg, axis=1)
    region = jnp.sum(seg, axis=0)
    nblk = (region + MOE_BLK_GROUPS - 1) // MOE_BLK_GROUPS
    gend = jnp.cumsum(nblk) * MOE_BLK_GROUPS
    gbase = gend - nblk * MOE_BLK_GROUPS
    cum = jnp.cumsum(seg, axis=0) - seg
    g = jnp.arange(MOE_GROUPS + MOE_BLK_GROUPS, dtype=i32)
    e_of = jnp.minimum(jnp.sum(gend[None, :] <= g[:, None], axis=1), N_EXPERTS - 1)
    onehot_e = (e_of[:, None] == jnp.arange(N_EXPERTS)[None, :]).astype(F32)
    pick_e = lambda t: jnp.dot(onehot_e, t.astype(F32), precision=lax.Precision.HIGHEST).astype(i32)
    off = g - pick_e(gbase)
    real = (g < gend[-1]) & (off < pick_e(region))
    cum_e = pick_e(cum.T)
    tile_of = jnp.sum(cum_e <= off[:, None], axis=1) - 1
    onehot_t = tile_of[:, None] == jnp.arange(N_TILES)[None, :]
    pick_t = lambda t: jnp.sum(jnp.where(onehot_t, t, 0), axis=1)
    pos = tile_of * MOE_TILE_GROUPS + pick_t(pick_e(lstart.T)) + off - pick_t(cum_e)
    pad_rank = jnp.cumsum((~real).astype(i32)) - 1
    gsrc = jnp.where(real, pos, MOE_ZERO_GROUP).astype(i32)
    gdst = jnp.where(real, pos, MOE_ZERO_GROUP + jnp.minimum(pad_rank, MOE_PAD_GROUPS - 1)).astype(i32)
    slot_t = jnp.concatenate([slot.transpose(0, 2, 1), jnp.full((1, TOP_K, TM), -1, i32)], axis=0)
    return (slot.reshape(N_TOK, TOP_K).astype(i32), slot_t.astype(i32),
            (nblk.astype(i32), gbase.astype(i32), gsrc, gdst), tile_groups.astype(i32))


def kernel(x_prompt, x_sample, state_s5_re, state_s5_im, state_hgrn, c, c_ctx, ada_w, ada_b, norm_w, w_in,
           s5_a_re, s5_a_im, s5_log_dt, s5_b_re, s5_b_im, s5_c_re, s5_c_im, s5_d, s5_w_glu, s5_b_glu,
           hgrn_lb, hgrn_norm_w, w_out, router_w, router_b, exp_w1, exp_b1, exp_w2, exp_b2):
    x = _assemble(x_prompt.reshape(N_PROMPT, D_MODEL), x_sample.reshape(N_SAMPLE, D_MODEL),
                  _pos_embed_2d(DEC_SEQ // GRID_W, D_MODEL))
    cond = jnp.concatenate([c_ctx[None, :], c, jnp.zeros((N_COND - 1 - DEC_BATCH, D_MODEL), F32)], axis=0)
    mod = _modulation(cond, ada_w, ada_b).reshape(DEPTH, N_COND, 1, 6 * D_MODEL)
    tiles = jnp.arange(N_TILES, dtype=jnp.int32)
    tile_cond = jnp.where(tiles < N_PROMPT // TM, 0, 1 + (tiles - N_PROMPT // TM) // (DEC_SEQ // TM)).astype(jnp.int32)
    w_in_b = w_in.astype(BF16)
    w_out_b = w_out.astype(BF16)
    w_glu_b = s5_w_glu.astype(BF16)
    router_hi = router_w.astype(BF16)
    router_parts = jnp.stack([router_hi, (router_w - router_hi.astype(F32)).astype(BF16)])

    s5_w = _s5_weights(s5_a_re, s5_a_im, s5_log_dt, s5_b_re, s5_b_im, s5_c_re, s5_c_im)
    fin_re, fin_im, fin_hg = [], [], []
    n = S5_STATE
    for l in range(DEPTH):
        u, qs, v, gs, lf_f, k_f, lf_b, k_b = _in_proj(l, tile_cond, x, mod[l], norm_w[l], w_in_b[l], hgrn_lb)

        h0 = jnp.concatenate([
            state_s5_re[:, l].transpose(2, 0, 1, 3).reshape(S5_GROUPS, DEC_BATCH, 2 * n),
            state_s5_im[:, l].transpose(2, 0, 1, 3).reshape(S5_GROUPS, DEC_BATCH, 2 * n)], axis=-1)
        h0 = h0.reshape(S5_NLT, S5_GPT, S5_NRB - 1, S5_SEQ_PER_RB, S5_NS).transpose(0, 2, 1, 3, 4)
        ycore, hfin = _s5_scan(l, u, *s5_w, h0)
        hfin = hfin.reshape(S5_GROUPS, BATCH, S5_NS)
        fin_re.append(hfin[:, :, 0:2 * n].reshape(S5_GROUPS, BATCH, 2, n).transpose(1, 2, 0, 3))
        fin_im.append(hfin[:, :, 2 * n:].reshape(S5_GROUPS, BATCH, 2, n).transpose(1, 2, 0, 3))

        s0 = jnp.concatenate([jnp.zeros((BATCH, 2, HGRN_HEADS, HGRN_DV, HGRN_DK), F32),
                              jnp.swapaxes(state_hgrn[:, l], -1, -2)], axis=0)
        o_f, o_b, sfin = _hgrn_scan(qs, v, lf_f, k_f, lf_b, k_b, s0)
        fin_hg.append(jnp.swapaxes(sfin[:BATCH], -1, -2))

        x1, h2, ri, gate, counts = _out_proj(
            tile_cond, x, ycore, u, o_f, o_b, gs, mod[l], norm_w[l], s5_d[l].reshape(1, S5_WIDTH),
            w_glu_b[l], s5_b_glu[l].reshape(1, S5_WIDTH), hgrn_norm_w[l].reshape(1, HGRN_DV),
            w_out_b[l], router_parts[:, l], router_b[l].reshape(1, N_EXPERTS))

        slot, slot_t, groups, tile_groups = _routing_tables(ri, counts)
        xs = _dispatch(slot_t, h2)
        ys = _moe_experts(l, *groups, xs, exp_w1, exp_b1, exp_w2, exp_b2)
        x = _combine(l == DEPTH - 1, tile_cond, tile_groups, x1, slot, gate, mod[l], norm_w[l], ys)

    y_prompt, y_sample = x
    return (y_prompt.reshape(BATCH, SEQ, D_MODEL), y_sample.reshape(DEC_BATCH, DEC_SEQ, D_MODEL),
            jnp.stack(fin_re, axis=1), jnp.stack(fin_im, axis=1), jnp.stack(fin_hg, axis=1))
```

```python
import functools
import math

import jax
import jax.numpy as jnp
from jax import lax
from jax.experimental import pallas as pl
from jax.experimental.pallas import tpu as pltpu

F32 = jnp.float32
BF16 = jnp.bfloat16

D_MODEL = 1024
BATCH = 16
SEQ = 256
DEPTH = 2
DEC_BATCH = 4
DEC_SEQ = 2048
GRID_W = 64
S5_WIDTH = 512
S5_GROUP_CH = 16
S5_GROUPS = 32
S5_STATE = 64
HGRN_WIDTH = 512
HGRN_DK = 128
HGRN_HEADS = 4
HGRN_DV = 128
IN_COLS = S5_WIDTH + 5 * HGRN_WIDTH
N_EXPERTS = 32
TOP_K = 4
D_FF = D_MODEL
SWIGLU_LIMIT = 7.0
SWIGLU_ALPHA = 1.702
NORM_EPS = 1e-6
POS_BASE = 10000.0

N_PROMPT = BATCH * SEQ
N_SAMPLE = DEC_BATCH * DEC_SEQ
N_TOK = N_PROMPT + N_SAMPLE
N_SEQ = BATCH + DEC_BATCH
N_COND = 8

TM = 256
N_TILES = N_TOK // TM
S5_T = 16
S5_LT = 128
S5_GPT = S5_LT // S5_GROUP_CH
S5_NLT = S5_WIDTH // S5_LT
S5_RB = N_PROMPT
S5_CR = S5_RB // S5_T
S5_NRB = N_TOK // S5_RB
S5_SEQ_PER_RB = S5_RB // DEC_SEQ
S5_NLAG = 2 * S5_T - 1
S5_NS = 4 * S5_STATE
HC = 128
N_HCHUNK = N_TOK // HC
EXP_CLAMP = 80.0
MOE_BM = 256
N_ASSIGN = N_TOK * TOP_K
SEG_ALIGN = 16
TILE_SLOTS = -(-(TM * TOP_K + N_EXPERTS * (SEG_ALIGN - 1)) // 128) * 128
MOE_BLOCKS = -(-(N_ASSIGN + N_TILES * N_EXPERTS * (SEG_ALIGN - 1) + N_EXPERTS * (MOE_BM - 1)) // MOE_BM)
MOE_GRP = SEG_ALIGN
MOE_TILE_GROUPS = TILE_SLOTS // MOE_GRP
MOE_BLK_GROUPS = MOE_BM // MOE_GRP
MOE_GROUPS = MOE_BLOCKS * MOE_BLK_GROUPS
MOE_ZERO_GROUP = N_TILES * MOE_TILE_GROUPS
MOE_PAD_GROUPS = N_EXPERTS * (MOE_BLK_GROUPS - 1)
MOE_XS_ROWS = (N_TILES + 1) * TILE_SLOTS
MOE_YS_ROWS = N_TILES * TILE_SLOTS + MOE_PAD_GROUPS * MOE_GRP
MOE_BASE_ROWS = TM * TOP_K
MOE_TAIL_ARMS = (256, 128, 64, 32, 16)
VMEM_LIMIT = 56 * 1024 * 1024


def _rms(x, w):
    return x * lax.rsqrt(jnp.mean(x * x, axis=-1, keepdims=True) + NORM_EPS) * w


def _silu(x):
    return x * jax.nn.sigmoid(x)


MOD_TN = 1536


def _mod_kernel(cond_ref, w_ref, b_ref, o_ref):
    s = _silu(cond_ref[...]).astype(BF16)
    o_ref[0] = jnp.dot(s, w_ref[0].astype(BF16), preferred_element_type=F32) + b_ref[0]


def _modulation(cond, ada_w, ada_b):
    return pl.pallas_call(
        _mod_kernel,
        grid=(DEPTH, 6 * D_MODEL // MOD_TN),
        in_specs=[
            pl.BlockSpec((N_COND, D_MODEL), lambda l, j: (0, 0)),
            pl.BlockSpec((1, D_MODEL, MOD_TN), lambda l, j: (l, 0, j)),
            pl.BlockSpec((1, 1, MOD_TN), lambda l, j: (l, 0, j)),
        ],
        out_specs=pl.BlockSpec((1, N_COND, MOD_TN), lambda l, j: (l, 0, j)),
        out_shape=jax.ShapeDtypeStruct((DEPTH, N_COND, 6 * D_MODEL), F32),
        compiler_params=pltpu.CompilerParams(vmem_limit_bytes=VMEM_LIMIT),
        name="adaln_mod",
    )(cond, ada_w, ada_b.reshape(DEPTH, 1, 6 * D_MODEL))


def _in_kernel(layer, cond_ref, x_ref, mod_ref, nw_ref, w_ref, lb_ref,
               u_ref, q_ref, v_ref, g_ref, lff_ref, kf_ref, lfb_ref, kb_ref):
    del cond_ref
    mod = mod_ref[0]
    sh1 = mod[:, 0:D_MODEL]
    sc1 = mod[:, D_MODEL:2 * D_MODEL]
    h = _rms(x_ref[...], nw_ref[0:1, :]) * (1.0 + sc1) + sh1
    z = jnp.dot(h.astype(BF16), w_ref[...], preferred_element_type=F32)
    w = HGRN_WIDTH
    c0 = S5_WIDTH
    u_ref[...] = z[:, 0:c0]
    q_ref[...] = _silu(z[:, c0:c0 + w]) * (HGRN_DK ** -0.5)
    v_ref[...] = z[:, c0 + 3 * w:c0 + 4 * w]
    g_ref[...] = _silu(z[:, c0 + 4 * w:c0 + 5 * w])
    lbp = lb_ref[...]
    e = jnp.exp(lbp - jnp.max(lbp, axis=0, keepdims=True))
    probs = e / jnp.sum(e, axis=0, keepdims=True)
    lb = jnp.sum(probs[0:layer + 1], axis=0) - probs[0]
    for d, (lf_ref, k_ref) in enumerate(((lff_ref, kf_ref), (lfb_ref, kb_ref))):
        f = z[:, c0 + (1 + d) * w:c0 + (2 + d) * w]
        lbd = lb[d:d + 1, :]
        fg = lbd + (1.0 - lbd) * jax.nn.sigmoid(f)
        lf_ref[...] = jnp.log(fg)
        k_ref[...] = 1.0 - fg


def _in_proj(layer, tile_cond, x, mod_l, norm_w_l, w_in_l, hgrn_lb):
    tok = lambda i, c: (i, 0)
    out = jax.ShapeDtypeStruct((N_TOK, HGRN_WIDTH), F32)
    return pl.pallas_call(
        functools.partial(_in_kernel, layer),
        grid_spec=pltpu.PrefetchScalarGridSpec(
            num_scalar_prefetch=1,
            grid=(N_TILES,),
            in_specs=[
                pl.BlockSpec((TM, D_MODEL), tok),
                pl.BlockSpec((1, 1, 6 * D_MODEL), lambda i, c: (c[i], 0, 0)),
                pl.BlockSpec((4, D_MODEL), lambda i, c: (0, 0)),
                pl.BlockSpec((D_MODEL, IN_COLS), lambda i, c: (0, 0)),
                pl.BlockSpec((DEPTH, 2, HGRN_WIDTH), lambda i, c: (0, 0, 0)),
            ],
            out_specs=[pl.BlockSpec((TM, HGRN_WIDTH), tok)] * 8,
        ),
        out_shape=[out] * 8,
        compiler_params=pltpu.CompilerParams(
            dimension_semantics=("arbitrary",), vmem_limit_bytes=VMEM_LIMIT),
        name="in_proj",
    )(tile_cond, x, mod_l, norm_w_l, w_in_l, hgrn_lb)


def _s5_layer_weights(a_re, a_im, log_dt, b_re, b_im, c_re, c_im):
    hp = lax.Precision.HIGHEST
    t = S5_T
    dt = jnp.exp(log_dt)[..., None]
    lam_re = jnp.minimum(a_re, -1e-4)
    lam_im = a_im
    mag = jnp.exp(dt * lam_re)
    ang = dt * lam_im
    ab_re = mag * jnp.cos(ang)
    ab_im = mag * jnp.sin(ang)
    den = lam_re * lam_re + lam_im * lam_im
    nr = ab_re - 1.0
    ni = ab_im
    co_re = (nr * lam_re + ni * lam_im) / den
    co_im = (ni * lam_re - nr * lam_im) / den
    bb_re = co_re[..., None] * b_re - co_im[..., None] * b_im
    bb_im = co_re[..., None] * b_im + co_im[..., None] * b_re
    pr = [jnp.ones_like(ab_re)]
    pi = [jnp.zeros_like(ab_im)]
    for _ in range(t):
        pr.append(pr[-1] * ab_re - pi[-1] * ab_im)
        pi.append(pr[-2] * ab_im + pi[-1] * ab_re)
    pr = jnp.stack(pr, axis=1)
    pi = jnp.stack(pi, axis=1)
    abr = pr[..., None] * bb_re[:, None] - pi[..., None] * bb_im[:, None]
    abi = pr[..., None] * bb_im[:, None] + pi[..., None] * bb_re[:, None]
    kk = jnp.einsum('dgon,dkgni->dkgio', jnp.concatenate([c_re, -c_im], axis=-1),
                    jnp.concatenate([abr, abi], axis=-2), precision=hp)
    kern = (kk[0], kk[1])
    abs_ = ((abr[0], abi[0]), (abr[1], abi[1]))
    pows = ((pr[0], pi[0]), (pr[1], pi[1]))
    p = S5_GROUP_CH
    lags = jnp.concatenate([kern[1][t - 1:0:-1], (kern[0][0] + kern[1][0])[None], kern[0][1:t]], axis=0)
    m = lags.reshape(S5_NLAG, S5_NLT, S5_GPT, p, p)
    bd = jnp.einsum('ltaio,ab->ltaibo', m, jnp.eye(S5_GPT, dtype=F32))
    wcat = bd.reshape(S5_NLAG, S5_NLT, S5_LT, S5_LT).transpose(1, 2, 0, 3).reshape(S5_NLT, S5_LT, S5_NLAG * S5_LT)
    (abr_f, abi_f), (abr_b, abi_b) = abs_
    parts = (abr_f[t - 1::-1], abr_b[:t], abi_f[t - 1::-1], abi_b[:t])
    wst = jnp.concatenate([x.transpose(1, 0, 3, 2) for x in parts], axis=-1)
    wst = wst.reshape(S5_NLT, S5_GPT, t, p, S5_NS).transpose(0, 2, 1, 3, 4).reshape(S5_NLT, t * S5_LT, S5_NS)
    (pr_f, pi_f), (pr_b, pi_b) = pows

    def out_rows(cr, ci, pr_, pi_):
        re_rows = cr[None] * pr_[:, :, None, :] - ci[None] * pi_[:, :, None, :]
        im_rows = -(cr[None] * pi_[:, :, None, :] + ci[None] * pr_[:, :, None, :])
        return re_rows.transpose(1, 3, 0, 2), im_rows.transpose(1, 3, 0, 2)

    fre, fim = out_rows(c_re[0], c_im[0], pr_f[1:t + 1], pi_f[1:t + 1])
    bre, bim = out_rows(c_re[1], c_im[1], pr_b[t:0:-1], pi_b[t:0:-1])
    wout = lax.optimization_barrier(jnp.concatenate([fre, bre, fim, bim], axis=1))
    wdense = (wout.reshape(S5_NLT, S5_GPT, S5_NS, t, p).transpose(0, 2, 3, 1, 4)
              .reshape(S5_NLT, S5_NS, t * S5_LT))
    at = jnp.stack([jnp.concatenate([pr_f[t], pr_b[t]], axis=-1),
                    jnp.concatenate([pi_f[t], pi_b[t]], axis=-1)], axis=1)
    return (wcat.astype(BF16), wst.astype(BF16), wdense.astype(BF16),
            at.reshape(S5_NLT, S5_GPT, 2, 2 * S5_STATE))


def _s5_weights(*params):
    return jax.vmap(_s5_layer_weights)(*params)


def _s5_kernel(u_ref, wcat_ref, wst_ref, wd_ref, at_ref, h0_ref, y_ref, hfin_ref,
               wbig, dh_scr, hf_scr, hb_scr):
    t = S5_T
    n2 = 2 * S5_STATE
    r = pl.program_id(1)

    @pl.when(r == 0)
    def _():
        for s in range(t):
            wbig[s * S5_LT:(s + 1) * S5_LT, :] = wcat_ref[0, :, (t - 1 - s) * S5_LT:(2 * t - 1 - s) * S5_LT]

    xcat = jnp.concatenate([u_ref[pl.ds(s, S5_CR, stride=t), :].astype(BF16) for s in range(t)], axis=-1)
    yacc = jnp.dot(xcat, wbig[...], preferred_element_type=F32)
    lane_k = lax.broadcasted_iota(jnp.int32, (1, t * S5_LT), 1)
    grp_k = (lane_k % S5_LT) // S5_GROUP_CH
    for gi in range(S5_GPT):
        xg = jnp.where(grp_k == gi, xcat, jnp.zeros_like(xcat))
        dh = jnp.dot(xg, wst_ref[0], preferred_element_type=F32)
        dh_scr[gi, 0] = dh[:, 0:n2]
        dh_scr[gi, 1] = dh[:, n2:]

    lane = lax.broadcasted_iota(jnp.int32, (1, n2), 1)
    fwd_lane = lane < S5_STATE
    are = [at_ref[0, gi, 0:1, :] for gi in range(S5_GPT)]
    aim = [at_ref[0, gi, 1:2, :] for gi in range(S5_GPT)]

    def advance(gi, hre, him, dre, dim):
        return (are[gi] * hre - aim[gi] * him + dre, are[gi] * him + aim[gi] * hre + dim)

    @pl.when(r == 0)
    def _():
        nc = SEQ // t
        for gi in range(S5_GPT):
            hre = jnp.zeros((BATCH, n2), F32)
            him = jnp.zeros((BATCH, n2), F32)
            for s in range(nc):
                rows_f = pl.ds(s, BATCH, stride=nc)
                rows_b = pl.ds(nc - 1 - s, BATCH, stride=nc)
                hf_scr[gi, 0, rows_f, :] = hre
                hf_scr[gi, 1, rows_f, :] = him
                hb_scr[gi, 0, rows_b, :] = hre
                hb_scr[gi, 1, rows_b, :] = him
                dre = jnp.where(fwd_lane, dh_scr[gi, 0, rows_f, :], dh_scr[gi, 0, rows_b, :])
                dim = jnp.where(fwd_lane, dh_scr[gi, 1, rows_f, :], dh_scr[gi, 1, rows_b, :])
                hre, him = advance(gi, hre, him, dre, dim)
            hfin_ref[0, gi] = jnp.concatenate([hre, him], axis=-1)

    @pl.when(r > 0)
    def _():
        nc = DEC_SEQ // t
        nb = S5_SEQ_PER_RB
        init = tuple((h0_ref[0, 0, gi, b:b + 1, 0:n2], h0_ref[0, 0, gi, b:b + 1, n2:])
                     for gi in range(S5_GPT) for b in range(nb))

        def step(o, carry):
            new = []
            for gi in range(S5_GPT):
                for b in range(nb):
                    hre, him = carry[gi * nb + b]
                    rf = pl.multiple_of(b * nc + o * 8, 8)
                    rb = pl.multiple_of(b * nc + nc - 8 - o * 8, 8)
                    dfr = dh_scr[gi, 0, pl.ds(rf, 8), :]
                    dfi = dh_scr[gi, 1, pl.ds(rf, 8), :]
                    dbr = dh_scr[gi, 0, pl.ds(rb, 8), :]
                    dbi = dh_scr[gi, 1, pl.ds(rb, 8), :]
                    ent_re, ent_im = [], []
                    for i in range(8):
                        ent_re.append(hre)
                        ent_im.append(him)
                        dre = jnp.where(fwd_lane, dfr[i:i + 1], dbr[7 - i:8 - i])
                        dim = jnp.where(fwd_lane, dfi[i:i + 1], dbi[7 - i:8 - i])
                        hre, him = advance(gi, hre, him, dre, dim)
                    hf_scr[gi, 0, pl.ds(rf, 8), :] = jnp.concatenate(ent_re, axis=0)
                    hf_scr[gi, 1, pl.ds(rf, 8), :] = jnp.concatenate(ent_im, axis=0)
                    hb_scr[gi, 0, pl.ds(rb, 8), :] = jnp.concatenate(ent_re[::-1], axis=0)
                    hb_scr[gi, 1, pl.ds(rb, 8), :] = jnp.concatenate(ent_im[::-1], axis=0)
                    new.append((hre, him))
            return tuple(new)

        lax.fori_loop(0, nc // 8, step, init)

    for gi in range(S5_GPT):
        hent = jnp.concatenate([jnp.where(fwd_lane, hf_scr[gi, 0], hb_scr[gi, 0]),
                                jnp.where(fwd_lane, hf_scr[gi, 1], hb_scr[gi, 1])], axis=-1).astype(BF16)
        yi = jnp.dot(hent, wd_ref[0], preferred_element_type=F32)
        yacc = yacc + jnp.where(grp_k == gi, yi, 0.0)
    for s in range(t):
        y_ref[pl.ds(s, S5_CR, stride=t), :] = yacc[:, s * S5_LT:(s + 1) * S5_LT]


def _s5_scan(layer, u, wcat, wst, wdense, at, h0):
    tile = lambda j, r: (layer * S5_NLT + j, 0, 0)
    merge = lambda w: w.reshape((DEPTH * S5_NLT,) + w.shape[2:])
    wcat, wst, wdense, at = merge(wcat), merge(wst), merge(wdense), merge(at)
    return pl.pallas_call(
        _s5_kernel,
        grid=(S5_NLT, S5_NRB),
        in_specs=[
            pl.BlockSpec((S5_RB, S5_LT), lambda j, r: (r, j)),
            pl.BlockSpec((1, S5_LT, S5_NLAG * S5_LT), tile),
            pl.BlockSpec((1, S5_T * S5_LT, S5_NS), tile),
            pl.BlockSpec((1, S5_NS, S5_T * S5_LT), tile),
            pl.BlockSpec((1, S5_GPT, 2, 2 * S5_STATE), lambda j, r: (layer * S5_NLT + j, 0, 0, 0)),
            pl.BlockSpec((1, 1, S5_GPT, S5_SEQ_PER_RB, S5_NS), lambda j, r: (j, jnp.maximum(r - 1, 0), 0, 0, 0)),
        ],
        out_specs=[
            pl.BlockSpec((S5_RB, S5_LT), lambda j, r: (r, j)),
            pl.BlockSpec((1, S5_GPT, BATCH, S5_NS), lambda j, r: (j, 0, 0, 0)),
        ],
        out_shape=[
            jax.ShapeDtypeStruct((N_TOK, S5_WIDTH), F32),
            jax.ShapeDtypeStruct((S5_NLT, S5_GPT, BATCH, S5_NS), F32),
        ],
        scratch_shapes=[pltpu.VMEM((S5_T * S5_LT, S5_T * S5_LT), BF16)]
                       + [pltpu.VMEM((S5_GPT, 2, S5_CR, 2 * S5_STATE), F32)] * 3,
        compiler_params=pltpu.CompilerParams(
            dimension_semantics=("arbitrary", "arbitrary"), vmem_limit_bytes=VMEM_LIMIT),
        name="s5_scan",
    )(u, wcat, wst, wdense, at, h0)


def _split3(x):
    hi = x.astype(BF16)
    r1 = x - hi.astype(F32)
    mid = r1.astype(BF16)
    lo = (r1 - mid.astype(F32)).astype(BF16)
    return hi, mid, lo


def _piecewise_rows(b, blk, row_in_blk):
    parts = []
    for j in range(HC // blk):
        r = j * blk + row_in_blk
        parts.append(jnp.broadcast_to(b[r:r + 1, :], (blk, b.shape[1])))
    return parts[0] if len(parts) == 1 else jnp.concatenate(parts, axis=0)


def _nt(a, b):
    return lax.dot_general(a, b, (((1,), (1,)), ((), ())), preferred_element_type=F32)


def _tn(a, b):
    return lax.dot_general(a, b, (((0,), (0,)), ((), ())), preferred_element_type=F32)


def _hgrn_dir(reverse, q_ref, v_ref, lf_ref, k_ref, o_ref, st_ref, b_scr):
    row = lax.broadcasted_iota(jnp.int32, (HC, HC), 0)
    col = lax.broadcasted_iota(jnp.int32, (HC, HC), 1)
    causal = (col >= row) if reverse else (col <= row)
    tri = jnp.where(causal, 1.0, 0.0).astype(BF16)
    lf = lf_ref[...]
    hi, mid, lo = _split3(lf)
    ball = (jnp.dot(tri, hi, preferred_element_type=F32) + jnp.dot(tri, mid, preferred_element_type=F32)
            + jnp.dot(tri, lo, preferred_element_type=F32))
    last = 0 if reverse else HC - 1
    masks = []
    for blk in (128, 64, 32):
        half = blk // 2
        same = (row // blk) == (col // blk)
        t_hi = (row % blk) >= half
        s_hi = (col % blk) >= half
        if reverse:
            masks.append(same & jnp.logical_not(t_hi) & s_hi)
        else:
            masks.append(same & t_hi & jnp.logical_not(s_hi))
    b_scr[...] = ball
    mrow = 8 if reverse else 7
    mid = _piecewise_rows(ball, 16, mrow)
    spread = jnp.zeros((1, HGRN_WIDTH), F32)
    for j in range(HC // 16):
        for edge in (16 * j, 16 * j + 15):
            spread = jnp.maximum(spread, jnp.abs(ball[edge:edge + 1, :] - ball[16 * j + mrow:16 * j + mrow + 1, :]))
    stable = jnp.max(spread) <= EXP_CLAMP
    diag_ok = ((row // 16) == (col // 16)) & causal & stable
    for h in range(HGRN_HEADS):
        sl = slice(h * HGRN_DK, (h + 1) * HGRN_DK)
        b = ball[:, sl]
        q = q_ref[:, sl]
        k = k_ref[:, sl]
        v = v_ref[:, sl].astype(BF16)
        st = st_ref[h]
        b_last = b[last:last + 1, :]
        q_in = (q * jnp.exp(b)).astype(BF16)
        k_in = (k * jnp.exp(b_last - b)).astype(BF16)
        o = _nt(q_in, st.astype(BF16))
        st_ref[h] = jnp.exp(b_last) * st + _tn(v, k_in)
        scores = jnp.zeros((HC, HC), F32)
        for blk, mask in zip((128, 64, 32), masks):
            half = blk // 2
            zero = jnp.zeros((half, HGRN_DK), BF16)
            qparts, kparts = [], []
            for j in range(HC // blk):
                early = slice(j * blk, j * blk + half)
                late = slice(j * blk + half, (j + 1) * blk)
                if reverse:
                    m = b[j * blk + half:j * blk + half + 1, :]
                    qrows, krows = early, late
                else:
                    m = b[j * blk + half - 1:j * blk + half, :]
                    qrows, krows = late, early
                qe = (q[qrows] * jnp.exp(b[qrows] - m)).astype(BF16)
                ke = (k[krows] * jnp.exp(m - b[krows])).astype(BF16)
                qparts += [qe, zero] if reverse else [zero, qe]
                kparts += [zero, ke] if reverse else [ke, zero]
            part = _nt(jnp.concatenate(qparts, axis=0), jnp.concatenate(kparts, axis=0))
            scores = scores + (part if blk == HC else jnp.where(mask, part, 0.0))
        m = mid[:, sl]
        qd = (q * jnp.exp(jnp.minimum(b - m, EXP_CLAMP))).astype(BF16)
        kd = (k * jnp.exp(jnp.minimum(m - b, EXP_CLAMP))).astype(BF16)
        scores = scores + jnp.where(diag_ok, _nt(qd, kd), 0.0)
        o_ref[:, sl] = o + jnp.dot(scores.astype(BF16), v, preferred_element_type=F32)

    return stable


def _hgrn_exact_diagonal(reverse, stable, q_ref, v_ref, k_ref, o_ref, b_scr):
    @pl.when(jnp.logical_not(stable))
    def _():
        pos = lax.broadcasted_iota(jnp.int32, (HC, 1), 0) % 16

        def lag(d, c):
            shift = (HC - d) % HC if reverse else d
            valid = (pos + d <= 15) if reverse else (pos >= d)
            for h in range(HGRN_HEADS):
                sl = slice(h * HGRN_DK, (h + 1) * HGRN_DK)
                b = b_scr[:, sl]
                bs = pltpu.roll(b, shift, 0)
                ks = pltpu.roll(k_ref[:, sl], shift, 0)
                vs = pltpu.roll(v_ref[:, sl], shift, 0)
                e = jnp.exp(jnp.where(valid, b - bs, 0.0))
                w = jnp.sum(q_ref[:, sl] * ks * e, axis=-1, keepdims=True)
                o_ref[:, sl] = o_ref[:, sl] + jnp.where(valid, w, 0.0) * vs
            return c

        lax.fori_loop(0, 16, lag, 0)


def _hgrn_kernel(cf_ref, cb_ref, seq_ref, first_ref, last_ref,
                 qf_ref, vf_ref, lff_ref, kf_ref, qb_ref, vb_ref, lfb_ref, kb_ref, s0_ref,
                 of_ref, ob_ref, sfin_ref, st_scr, b_scr):
    del cf_ref, cb_ref, seq_ref
    j = pl.program_id(0)

    @pl.when(first_ref[j] == 1)
    def _():
        st_scr[...] = s0_ref[0]

    ok_f = _hgrn_dir(False, qf_ref, vf_ref, lff_ref, kf_ref, of_ref, st_scr.at[0], b_scr.at[0])
    ok_b = _hgrn_dir(True, qb_ref, vb_ref, lfb_ref, kb_ref, ob_ref, st_scr.at[1], b_scr.at[1])
    _hgrn_exact_diagonal(False, ok_f, qf_ref, vf_ref, kf_ref, of_ref, b_scr.at[0])
    _hgrn_exact_diagonal(True, ok_b, qb_ref, vb_ref, kb_ref, ob_ref, b_scr.at[1])

    @pl.when(last_ref[j] == 1)
    def _():
        sfin_ref[0] = st_scr[...]


def _hgrn_tables():
    cf, cb, sq, first, last = [], [], [], [], []
    base = 0
    for s in range(N_SEQ):
        nc = (SEQ if s < BATCH else DEC_SEQ) // HC
        for t in range(nc):
            cf.append(base + t)
            cb.append(base + nc - 1 - t)
            sq.append(s)
            first.append(int(t == 0))
            last.append(int(t == nc - 1))
        base += nc
    return tuple(jnp.asarray(x, jnp.int32) for x in (cf, cb, sq, first, last))


def _hgrn_scan(qs, v, lf_f, k_f, lf_b, k_b, s0):
    fwd = lambda j, cf, cb, sq, fi, la: (cf[j], 0)
    bwd = lambda j, cf, cb, sq, fi, la: (cb[j], 0)
    seq = lambda j, cf, cb, sq, fi, la: (sq[j], 0, 0, 0, 0)
    tile = (HC, HGRN_WIDTH)
    sblk = (1, 2, HGRN_HEADS, HGRN_DV, HGRN_DK)
    out = jax.ShapeDtypeStruct((N_TOK, HGRN_WIDTH), F32)
    return pl.pallas_call(
        _hgrn_kernel,
        grid_spec=pltpu.PrefetchScalarGridSpec(
            num_scalar_prefetch=5,
            grid=(N_HCHUNK,),
            in_specs=[pl.BlockSpec(tile, fwd)] * 4 + [pl.BlockSpec(tile, bwd)] * 4
                     + [pl.BlockSpec(sblk, seq)],
            out_specs=[pl.BlockSpec(tile, fwd), pl.BlockSpec(tile, bwd), pl.BlockSpec(sblk, seq)],
            scratch_shapes=[pltpu.VMEM(sblk[1:], F32), pltpu.VMEM((2, HC, HGRN_WIDTH), F32)],
        ),
        out_shape=[out, out, jax.ShapeDtypeStruct((N_SEQ,) + sblk[1:], F32)],
        compiler_params=pltpu.CompilerParams(
            dimension_semantics=("arbitrary",), vmem_limit_bytes=VMEM_LIMIT),
        name="hgrn_scan",
    )(*_hgrn_tables(), qs, v, lf_f, k_f, qs, v, lf_b, k_b, s0)


def _gelu_tanh(x):
    return 0.5 * x * (1.0 + jnp.tanh(math.sqrt(2.0 / math.pi) * (x + 0.044715 * (x * x * x))))


def _out_kernel(cond_ref, x_ref, yc_ref, u_ref, of_ref, ob_ref, g_ref, mod_ref, nw_ref, d_ref,
                wglu_ref, bglu_ref, hnw_ref, wout_ref, rw_ref, rb_ref,
                x1_ref, h2_ref, ri_ref, rg_ref, cnt_ref):
    del cond_ref
    mod = mod_ref[0]
    g1 = mod[:, 2 * D_MODEL:3 * D_MODEL]
    sh2 = mod[:, 3 * D_MODEL:4 * D_MODEL]
    sc2 = mod[:, 4 * D_MODEL:5 * D_MODEL]
    y = _gelu_tanh(yc_ref[...] + d_ref[...] * u_ref[...])
    y_s5 = y * jax.nn.sigmoid(jnp.dot(y.astype(BF16), wglu_ref[...], preferred_element_type=F32) + bglu_ref[...])
    o = of_ref[...] + ob_ref[...]
    gs = g_ref[...]
    heads = []
    for h in range(HGRN_HEADS):
        sl = slice(h * HGRN_DV, (h + 1) * HGRN_DV)
        heads.append(_rms(o[:, sl], hnw_ref[...]) * gs[:, sl])
    y_hg = jnp.concatenate(heads, axis=-1)
    mix = (jnp.dot(y_s5.astype(BF16), wout_ref[0:S5_WIDTH, :], preferred_element_type=F32)
           + jnp.dot(y_hg.astype(BF16), wout_ref[S5_WIDTH:, :], preferred_element_type=F32))
    x1 = x_ref[...] + g1 * _rms(mix, nw_ref[1:2, :])
    x1_ref[...] = x1
    h2 = _rms(x1, nw_ref[2:3, :]) * (1.0 + sc2) + sh2
    h2_hi = h2.astype(BF16)
    h2_ref[...] = h2_hi
    h2_lo = (h2 - h2_hi.astype(F32)).astype(BF16)
    logits = rb_ref[...]
    for a in (h2_hi, h2_lo):
        for part in range(2):
            logits = logits + jnp.dot(a, rw_ref[part], preferred_element_type=F32)
    eidx = lax.broadcasted_iota(jnp.int32, (TM, N_EXPERTS), 1).astype(F32)
    vals = logits
    top_v, top_i, onehots = [], [], []
    for _ in range(TOP_K):
        mx = jnp.max(vals, axis=-1, keepdims=True)
        ix = jnp.min(jnp.where(vals == mx, eidx, float(N_EXPERTS)), axis=-1, keepdims=True)
        sel = eidx == ix
        top_v.append(mx)
        top_i.append(ix)
        onehots.append(sel)
        vals = jnp.where(sel, -jnp.inf, vals)
    ex = [jnp.exp(tv - top_v[0]) for tv in top_v]
    den = ex[0] + ex[1] + ex[2] + ex[3]
    tot = jnp.zeros((TM, N_EXPERTS), F32)
    for sel in onehots:
        tot = tot + jnp.where(sel, 1.0, 0.0)
    r_t = lax.broadcasted_iota(jnp.int32, (TM, TM), 0)
    r_s = lax.broadcasted_iota(jnp.int32, (TM, TM), 1)
    strict = jnp.where(r_s < r_t, 1.0, 0.0).astype(BF16)
    before = jnp.dot(strict, tot.astype(BF16), preferred_element_type=F32)
    cnt = jnp.sum(tot, axis=0, keepdims=True)
    seg = jnp.floor((cnt + (SEG_ALIGN - 1)) * (1.0 / SEG_ALIGN)) * SEG_ALIGN
    e_r = lax.broadcasted_iota(jnp.int32, (N_EXPERTS, N_EXPERTS), 0)
    e_c = lax.broadcasted_iota(jnp.int32, (N_EXPERTS, N_EXPERTS), 1)
    lstart = jnp.dot(seg, jnp.where(e_r < e_c, 1.0, 0.0), precision=lax.Precision.HIGHEST,
                     preferred_element_type=F32)
    before = before + lstart
    lane = lax.broadcasted_iota(jnp.int32, (TM, 128), 1)
    ri = jnp.zeros((TM, 128), F32)
    rg = jnp.zeros((TM, 128), F32)
    for kk in range(TOP_K):
        rank = jnp.sum(jnp.where(onehots[kk], before, 0.0), axis=-1, keepdims=True)
        ri = jnp.where(lane == kk, top_i[kk], ri)
        ri = jnp.where(lane == TOP_K + kk, rank, ri)
        rg = jnp.where(lane == kk, ex[kk] / den, rg)
    ri_ref[...] = ri.astype(jnp.int32)
    rg_ref[...] = rg
    cnt_ref[0] = cnt.astype(jnp.int32)


def _out_proj(tile_cond, x, ycore, u, o_f, o_b, gs, mod_l, norm_w_l, s5_d_l, wglu_l, bglu_l, hnw_l,
              wout_l, rw_l, rb_l):
    tok = lambda i, c: (i, 0)
    full2 = lambda i, c: (0, 0)
    half = pl.BlockSpec((TM, HGRN_WIDTH), tok)
    wide = pl.BlockSpec((TM, D_MODEL), tok)
    return pl.pallas_call(
        _out_kernel,
        grid_spec=pltpu.PrefetchScalarGridSpec(
            num_scalar_prefetch=1,
            grid=(N_TILES,),
            in_specs=[
                wide, half, half, half, half, half,
                pl.BlockSpec((1, 1, 6 * D_MODEL), lambda i, c: (c[i], 0, 0)),
                pl.BlockSpec((4, D_MODEL), full2),
                pl.BlockSpec((1, S5_WIDTH), full2),
                pl.BlockSpec((S5_WIDTH, S5_WIDTH), full2),
                pl.BlockSpec((1, S5_WIDTH), full2),
                pl.BlockSpec((1, HGRN_DV), full2),
                pl.BlockSpec((D_MODEL, D_MODEL), full2),
                pl.BlockSpec((2, D_MODEL, N_EXPERTS), lambda i, c: (0, 0, 0)),
                pl.BlockSpec((1, N_EXPERTS), full2),
            ],
            out_specs=[wide, wide, pl.BlockSpec((TM, 128), tok), pl.BlockSpec((TM, 128), tok),
                       pl.BlockSpec((1, 1, N_EXPERTS), lambda i, c: (i, 0, 0))],
        ),
        out_shape=[
            jax.ShapeDtypeStruct((N_TOK, D_MODEL), F32),
            jax.ShapeDtypeStruct((N_TOK, D_MODEL), BF16),
            jax.ShapeDtypeStruct((N_TOK, 128), jnp.int32),
            jax.ShapeDtypeStruct((N_TOK, 128), F32),
            jax.ShapeDtypeStruct((N_TILES, 1, N_EXPERTS), jnp.int32),
        ],
        compiler_params=pltpu.CompilerParams(
            dimension_semantics=("arbitrary",), vmem_limit_bytes=VMEM_LIMIT),
        name="out_proj_router",
    )(tile_cond, x, ycore, u, o_f, o_b, gs, mod_l, norm_w_l, s5_d_l, wglu_l, bglu_l, hnw_l,
      wout_l, rw_l, rb_l)


def _dispatch_kernel(slot_ref, h_ref, o_ref):
    srow = lax.broadcasted_iota(jnp.int32, (TILE_SLOTS, TM), 0)
    perm = jnp.zeros((TILE_SLOTS, TM), F32)
    for kk in range(TOP_K):
        perm = perm + jnp.where(srow == slot_ref[0, kk:kk + 1, :], 1.0, 0.0)
    o_ref[...] = jnp.dot(perm.astype(BF16), h_ref[...], preferred_element_type=F32).astype(BF16)


def _dispatch(slot_t, h2):
    return pl.pallas_call(
        _dispatch_kernel,
        grid=(N_TILES + 1,),
        in_specs=[pl.BlockSpec((1, TOP_K, TM), lambda i: (i, 0, 0)),
                  pl.BlockSpec((TM, D_MODEL), lambda i: (jnp.minimum(i, N_TILES - 1), 0))],
        out_specs=pl.BlockSpec((TILE_SLOTS, D_MODEL), lambda i: (i, 0)),
        out_shape=jax.ShapeDtypeStruct((MOE_XS_ROWS, D_MODEL), BF16),
        compiler_params=pltpu.CompilerParams(
            dimension_semantics=("arbitrary",), vmem_limit_bytes=VMEM_LIMIT),
        name="moe_dispatch",
    )(slot_t, h2)


def _moe_kernel(nblk_ref, gbase_ref, edge_ref, gsrc_ref, gdst_ref, w1_ref, b1_ref, w2_ref, b2_ref, xs_ref, ys_ref,
                xbuf, obuf, w1b, w2b, sem_in, sem_out):
    e = pl.program_id(0)
    nb = nblk_ref[e]
    g0 = gbase_ref[e]
    p0 = (g0 // MOE_BLK_GROUPS) % 2
    first_active = (edge_ref[e] & 1) != 0
    last_active = (edge_ref[e] & 2) != 0

    def gather(b, buf):
        return [pltpu.make_async_copy(
            xs_ref.at[pl.ds(pl.multiple_of(gsrc_ref[g0 + b * MOE_BLK_GROUPS + g] * MOE_GRP, MOE_GRP), MOE_GRP)],
            xbuf.at[buf, pl.ds(g * MOE_GRP, MOE_GRP)], sem_in.at[buf]) for g in range(MOE_BLK_GROUPS)]

    def scatter(b, buf):
        return [pltpu.make_async_copy(
            obuf.at[buf, pl.ds(g * MOE_GRP, MOE_GRP)],
            ys_ref.at[pl.ds(pl.multiple_of(gdst_ref[g0 + b * MOE_BLK_GROUPS + g] * MOE_GRP, MOE_GRP), MOE_GRP)],
            sem_out.at[buf]) for g in range(MOE_BLK_GROUPS)]

    @pl.when(nb > 0)
    def _():
        def cast(j, c):
            r = pl.multiple_of(j * 128, 128)
            w1b[pl.ds(r, 128), :] = w1_ref[0, pl.ds(r, 128), :].astype(BF16)
            w2b[pl.ds(r, 128), :] = w2_ref[0, pl.ds(r, 128), :].astype(BF16)
            return c

        @pl.when(first_active)
        def _():
            for cp in gather(0, p0):
                cp.start(priority=1)

        lax.fori_loop(0, D_MODEL // 128, cast, 0)

        def block(b, c):
            buf = (p0 + b) % 2

            @pl.when(b >= 2)
            def _():
                for cp in scatter(b - 2, buf):
                    cp.wait()

            for cp in gather(b, buf):
                cp.wait()
            for cp in gather(b + 1, 1 - buf):
                cp.start(priority=1)
            h = jnp.dot(xbuf[buf], w1b[...], preferred_element_type=F32) + b1_ref[0]
            glu = jnp.minimum(h[:, :D_FF], SWIGLU_LIMIT)
            lin = jnp.clip(h[:, D_FF:], -SWIGLU_LIMIT, SWIGLU_LIMIT)
            act = glu * jax.nn.sigmoid(SWIGLU_ALPHA * glu) * (lin + 1.0)
            obuf[buf] = (jnp.dot(act.astype(BF16), w2b[...], preferred_element_type=F32) + b2_ref[0]).astype(BF16)
            for cp in scatter(b, buf):
                cp.start(priority=1)
            return c

        lax.fori_loop(0, nb, block, 0)

        @pl.when(last_active)
        def _():
            for cp in gather(nb, (p0 + nb) % 2):
                cp.wait()

        @pl.when(nb >= 2)
        def _():
            for cp in scatter(nb - 2, (p0 + nb) % 2):
                cp.wait()

        for cp in scatter(nb - 1, (p0 + nb - 1) % 2):
            cp.wait()


def _moe_experts(layer, nblk, gbase, edge, gsrc, gdst, xs, w1, b1, w2, b2):
    exp3 = lambda e, *_: (layer * N_EXPERTS + e, 0, 0)
    w1 = w1.reshape(DEPTH * N_EXPERTS, D_MODEL, 2 * D_FF)
    w2 = w2.reshape(DEPTH * N_EXPERTS, D_FF, D_MODEL)
    return pl.pallas_call(
        _moe_kernel,
        grid_spec=pltpu.PrefetchScalarGridSpec(
            num_scalar_prefetch=5,
            grid=(N_EXPERTS,),
            in_specs=[
                pl.BlockSpec((1, D_MODEL, 2 * D_FF), exp3),
                pl.BlockSpec((1, 1, 2 * D_FF), exp3),
                pl.BlockSpec((1, D_FF, D_MODEL), exp3),
                pl.BlockSpec((1, 1, D_MODEL), exp3),
                pl.BlockSpec(memory_space=pl.ANY),
            ],
            out_specs=pl.BlockSpec(memory_space=pl.ANY),
            scratch_shapes=[pltpu.VMEM((2, MOE_BM, D_MODEL), BF16), pltpu.VMEM((2, MOE_BM, D_MODEL), BF16),
                            pltpu.VMEM((D_MODEL, 2 * D_FF), BF16), pltpu.VMEM((D_FF, D_MODEL), BF16),
                            pltpu.SemaphoreType.DMA((2,)), pltpu.SemaphoreType.DMA((2,))],
        ),
        out_shape=jax.ShapeDtypeStruct((MOE_YS_ROWS, D_MODEL), BF16),
        compiler_params=pltpu.CompilerParams(
            dimension_semantics=("arbitrary",), vmem_limit_bytes=VMEM_LIMIT),
        name="moe_experts",
    )(nblk, gbase, edge, gsrc, gdst, w1, b1.reshape(DEPTH * N_EXPERTS, 1, 2 * D_FF), w2,
      b2.reshape(DEPTH * N_EXPERTS, 1, D_MODEL), xs)


def _combine_kernel(split, cond_ref, tg_ref, x1_ref, slot_ref, gate_ref, mod_ref, nw_ref, ys_ref, *rest):
    del cond_ref
    outs, (buf, sem) = rest[:-2], rest[-2:]
    i = pl.program_id(0)
    cur = i % 2

    def fetch(tile, b, wait):
        def go(cp):
            cp.wait() if wait else cp.start()

        row0 = pl.multiple_of(tile * TILE_SLOTS, TILE_SLOTS)
        go(pltpu.make_async_copy(ys_ref.at[pl.ds(row0, MOE_BASE_ROWS)], buf.at[b, pl.ds(0, MOE_BASE_ROWS)],
                                 sem.at[b]))
        extra = tg_ref[tile] * MOE_GRP - MOE_BASE_ROWS
        for arm in MOE_TAIL_ARMS:
            @pl.when((extra & arm) != 0)
            def _():
                off = pl.multiple_of(MOE_BASE_ROWS + (extra & ~(2 * arm - 1)), MOE_GRP)
                go(pltpu.make_async_copy(ys_ref.at[pl.ds(row0 + off, arm)], buf.at[b, pl.ds(off, arm)], sem.at[b]))

    @pl.when(i == 0)
    def _():
        buf[...] = jnp.zeros_like(buf)
        fetch(0, 0, False)

    @pl.when(i + 1 < N_TILES)
    def _():
        fetch(i + 1, 1 - cur, False)

    fetch(i, cur, True)
    scol = lax.broadcasted_iota(jnp.int32, (TM, TILE_SLOTS), 1)
    slot = slot_ref[...]
    gate = gate_ref[...]
    gmat = jnp.zeros((TM, TILE_SLOTS), F32)
    for kk in range(TOP_K):
        gmat = gmat + jnp.where(scol == slot[:, kk:kk + 1], gate[:, kk:kk + 1], 0.0)
    gmat = gmat.astype(BF16)
    ffn = jnp.dot(gmat, buf[cur], preferred_element_type=F32)
    g2 = mod_ref[0][:, 5 * D_MODEL:6 * D_MODEL]
    out = x1_ref[...] + g2 * _rms(ffn, nw_ref[3:4, :])
    if split:
        @pl.when(i < N_PROMPT // TM)
        def _():
            outs[0][...] = out

        @pl.when(i >= N_PROMPT // TM)
        def _():
            outs[1][...] = out
    else:
        outs[0][...] = out


def _combine(split, tile_cond, tile_groups, x1, slot, gate, mod_l, norm_w_l, ys):
    tok = lambda i, *_: (i, 0)
    if split:
        np_tiles = N_PROMPT // TM
        out_specs = [pl.BlockSpec((TM, D_MODEL), lambda i, *_: (jnp.minimum(i, np_tiles - 1), 0)),
                     pl.BlockSpec((TM, D_MODEL), lambda i, *_: (jnp.maximum(i - np_tiles, 0), 0))]
        out_shape = [jax.ShapeDtypeStruct((N_PROMPT, D_MODEL), F32), jax.ShapeDtypeStruct((N_SAMPLE, D_MODEL), F32)]
    else:
        out_specs = pl.BlockSpec((TM, D_MODEL), tok)
        out_shape = jax.ShapeDtypeStruct((N_TOK, D_MODEL), F32)
    return pl.pallas_call(
        functools.partial(_combine_kernel, split),
        grid_spec=pltpu.PrefetchScalarGridSpec(
            num_scalar_prefetch=2,
            grid=(N_TILES,),
            in_specs=[
                pl.BlockSpec((TM, D_MODEL), tok),
                pl.BlockSpec((TM, TOP_K), tok),
                pl.BlockSpec((TM, 128), tok),
                pl.BlockSpec((1, 1, 6 * D_MODEL), lambda i, c, *_: (c[i], 0, 0)),
                pl.BlockSpec((4, D_MODEL), lambda i, *_: (0, 0)),
                pl.BlockSpec(memory_space=pl.ANY),
            ],
            out_specs=out_specs,
            scratch_shapes=[pltpu.VMEM((2, TILE_SLOTS, D_MODEL), BF16), pltpu.SemaphoreType.DMA((2,))],
        ),
        out_shape=out_shape,
        compiler_params=pltpu.CompilerParams(
            dimension_semantics=("arbitrary",), vmem_limit_bytes=VMEM_LIMIT),
        name="moe_combine",
    )(tile_cond, tile_groups, x1, slot, gate, mod_l, norm_w_l, ys)


def _pos_embed_2d(rows, dim):
    r = jnp.repeat(jnp.arange(rows, dtype=F32), GRID_W)
    col = jnp.tile(jnp.arange(GRID_W, dtype=F32), rows)
    quarter = dim // 4
    omega = 1.0 / (POS_BASE ** (jnp.arange(quarter, dtype=F32) / quarter))

    def emb(pos):
        ang = pos[:, None] * omega[None, :]
        return jnp.concatenate([jnp.sin(ang), jnp.cos(ang)], axis=-1)

    return jnp.concatenate([emb(r), emb(col)], axis=-1)


def _assemble_kernel(xp_ref, xs_ref, pos_ref, o_ref):
    i = pl.program_id(0)

    @pl.when(i < N_PROMPT // TM)
    def _():
        o_ref[...] = xp_ref[...]

    @pl.when(i >= N_PROMPT // TM)
    def _():
        o_ref[...] = xs_ref[...] + pos_ref[...]


def _assemble(xp, xs, pos):
    np_tiles = N_PROMPT // TM
    return pl.pallas_call(
        _assemble_kernel,
        grid=(N_TILES,),
        in_specs=[pl.BlockSpec((TM, D_MODEL), lambda i: (jnp.minimum(i, np_tiles - 1), 0)),
                  pl.BlockSpec((TM, D_MODEL), lambda i: (jnp.maximum(i - np_tiles, 0), 0)),
                  pl.BlockSpec((TM, D_MODEL), lambda i: (jnp.maximum(i - np_tiles, 0) % (DEC_SEQ // TM), 0))],
        out_specs=pl.BlockSpec((TM, D_MODEL), lambda i: (i, 0)),
        out_shape=jax.ShapeDtypeStruct((N_TOK, D_MODEL), F32),
        compiler_params=pltpu.CompilerParams(dimension_semantics=("arbitrary",)),
        name="assemble_tokens",
    )(xp, xs, pos)


def _routing_tables(ri, counts):
    i32 = jnp.int32
    slot = ri[:, TOP_K:2 * TOP_K].reshape(N_TILES, TM, TOP_K)
    cnt = counts.reshape(N_TILES, N_EXPERTS)
    seg = (cnt + SEG_ALIGN - 1) // SEG_ALIGN
    lstart = jnp.cumsum(seg, axis=1) - seg
    tile_groups = jnp.sum(seg, axis=1)
    region = jnp.sum(seg, axis=0)
    nblk = (region + MOE_BLK_GROUPS - 1) // MOE_BLK_GROUPS
    gend = jnp.cumsum(nblk) * MOE_BLK_GROUPS
    gbase = gend - nblk * MOE_BLK_GROUPS
    has = nblk > 0
    edge = (has & (gbase == 0)).astype(i32) + 2 * (has & (gend == gend[-1])).astype(i32)
    cum = jnp.cumsum(seg, axis=0) - seg
    g = jnp.arange(MOE_GROUPS + MOE_BLK_GROUPS, dtype=i32)
    e_of = jnp.minimum(jnp.sum(gend[None, :] <= g[:, None], axis=1), N_EXPERTS - 1)
    onehot_e = (e_of[:, None] == jnp.arange(N_EXPERTS)[None, :]).astype(F32)
    pick_e = lambda t: jnp.dot(onehot_e, t.astype(F32), precision=lax.Precision.HIGHEST).astype(i32)
    off = g - pick_e(gbase)
    real = (g < gend[-1]) & (off < pick_e(region))
    cum_e = pick_e(cum.T)
    tile_of = jnp.sum(cum_e <= off[:, None], axis=1) - 1
    onehot_t = tile_of[:, None] == jnp.arange(N_TILES)[None, :]
    pick_t = lambda t: jnp.sum(jnp.where(onehot_t, t, 0), axis=1)
    pos = tile_of * MOE_TILE_GROUPS + pick_t(pick_e(lstart.T)) + off - pick_t(cum_e)
    pad_rank = jnp.cumsum((~real).astype(i32)) - 1
    gsrc = jnp.where(real, pos, MOE_ZERO_GROUP).astype(i32)
    gdst = jnp.where(real, pos, MOE_ZERO_GROUP + jnp.minimum(pad_rank, MOE_PAD_GROUPS - 1)).astype(i32)
    slot_t = jnp.concatenate([slot.transpose(0, 2, 1), jnp.full((1, TOP_K, TM), -1, i32)], axis=0)
    return (slot.reshape(N_TOK, TOP_K).astype(i32), slot_t.astype(i32),
            (nblk.astype(i32), gbase.astype(i32), edge.astype(i32), gsrc, gdst), tile_groups.astype(i32))


def kernel(x_prompt, x_sample, state_s5_re, state_s5_im, state_hgrn, c, c_ctx, ada_w, ada_b, norm_w, w_in,
           s5_a_re, s5_a_im, s5_log_dt, s5_b_re, s5_b_im, s5_c_re, s5_c_im, s5_d, s5_w_glu, s5_b_glu,
           hgrn_lb, hgrn_norm_w, w_out, router_w, router_b, exp_w1, exp_b1, exp_w2, exp_b2):
    x = _assemble(x_prompt.reshape(N_PROMPT, D_MODEL), x_sample.reshape(N_SAMPLE, D_MODEL),
                  _pos_embed_2d(DEC_SEQ // GRID_W, D_MODEL))
    cond = jnp.concatenate([c_ctx[None, :], c, jnp.zeros((N_COND - 1 - DEC_BATCH, D_MODEL), F32)], axis=0)
    mod = _modulation(cond, ada_w, ada_b).reshape(DEPTH, N_COND, 1, 6 * D_MODEL)
    tiles = jnp.arange(N_TILES, dtype=jnp.int32)
    tile_cond = jnp.where(tiles < N_PROMPT // TM, 0, 1 + (tiles - N_PROMPT // TM) // (DEC_SEQ // TM)).astype(jnp.int32)
    w_in_b = w_in.astype(BF16)
    w_out_b = w_out.astype(BF16)
    w_glu_b = s5_w_glu.astype(BF16)
    router_hi = router_w.astype(BF16)
    router_parts = jnp.stack([router_hi, (router_w - router_hi.astype(F32)).astype(BF16)])

    s5_w = _s5_weights(s5_a_re, s5_a_im, s5_log_dt, s5_b_re, s5_b_im, s5_c_re, s5_c_im)
    fin_re, fin_im, fin_hg = [], [], []
    n = S5_STATE
    for l in range(DEPTH):
        u, qs, v, gs, lf_f, k_f, lf_b, k_b = _in_proj(l, tile_cond, x, mod[l], norm_w[l], w_in_b[l], hgrn_lb)

        h0 = jnp.concatenate([
            state_s5_re[:, l].transpose(2, 0, 1, 3).reshape(S5_GROUPS, DEC_BATCH, 2 * n),
            state_s5_im[:, l].transpose(2, 0, 1, 3).reshape(S5_GROUPS, DEC_BATCH, 2 * n)], axis=-1)
        h0 = h0.reshape(S5_NLT, S5_GPT, S5_NRB - 1, S5_SEQ_PER_RB, S5_NS).transpose(0, 2, 1, 3, 4)
        ycore, hfin = _s5_scan(l, u, *s5_w, h0)
        hfin = hfin.reshape(S5_GROUPS, BATCH, S5_NS)
        fin_re.append(hfin[:, :, 0:2 * n].reshape(S5_GROUPS, BATCH, 2, n).transpose(1, 2, 0, 3))
        fin_im.append(hfin[:, :, 2 * n:].reshape(S5_GROUPS, BATCH, 2, n).transpose(1, 2, 0, 3))

        s0 = jnp.concatenate([jnp.zeros((BATCH, 2, HGRN_HEADS, HGRN_DV, HGRN_DK), F32),
                              jnp.swapaxes(state_hgrn[:, l], -1, -2)], axis=0)
        o_f, o_b, sfin = _hgrn_scan(qs, v, lf_f, k_f, lf_b, k_b, s0)
        fin_hg.append(jnp.swapaxes(sfin[:BATCH], -1, -2))

        x1, h2, ri, gate, counts = _out_proj(
            tile_cond, x, ycore, u, o_f, o_b, gs, mod[l], norm_w[l], s5_d[l].reshape(1, S5_WIDTH),
            w_glu_b[l], s5_b_glu[l].reshape(1, S5_WIDTH), hgrn_norm_w[l].reshape(1, HGRN_DV),
            w_out_b[l], router_parts[:, l], router_b[l].reshape(1, N_EXPERTS))

        slot, slot_t, groups, tile_groups = _routing_tables(ri, counts)
        xs = _dispatch(slot_t, h2)
        ys = _moe_experts(l, *groups, xs, exp_w1, exp_b1, exp_w2, exp_b2)
        x = _combine(l == DEPTH - 1, tile_cond, tile_groups, x1, slot, gate, mod[l], norm_w[l], ys)

    y_prompt, y_sample = x
    return (y_prompt.reshape(BATCH, SEQ, D_MODEL), y_sample.reshape(DEC_BATCH, DEC_SEQ, D_MODEL),
            jnp.stack(fin_re, axis=1), jnp.stack(fin_im, axis=1), jnp.stack(fin_hg, axis=1))
```

```python
import functools
import math

import jax
import jax.numpy as jnp
from jax import lax
from jax.experimental import pallas as pl
from jax.experimental.pallas import tpu as pltpu

F32 = jnp.float32
BF16 = jnp.bfloat16

D_MODEL = 1024
BATCH = 16
SEQ = 256
DEPTH = 2
DEC_BATCH = 4
DEC_SEQ = 2048
GRID_W = 64
S5_WIDTH = 512
S5_GROUP_CH = 16
S5_GROUPS = 32
S5_STATE = 64
HGRN_WIDTH = 512
HGRN_DK = 128
HGRN_HEADS = 4
HGRN_DV = 128
IN_COLS = S5_WIDTH + 5 * HGRN_WIDTH
N_EXPERTS = 32
TOP_K = 4
D_FF = D_MODEL
SWIGLU_LIMIT = 7.0
SWIGLU_ALPHA = 1.702
NORM_EPS = 1e-6
POS_BASE = 10000.0

N_PROMPT = BATCH * SEQ
N_SAMPLE = DEC_BATCH * DEC_SEQ
N_TOK = N_PROMPT + N_SAMPLE
N_SEQ = BATCH + DEC_BATCH
N_COND = 8

TM = 256
N_TILES = N_TOK // TM
S5_T = 16
S5_LT = 128
S5_GPT = S5_LT // S5_GROUP_CH
S5_NLT = S5_WIDTH // S5_LT
S5_RB = N_PROMPT
S5_CR = S5_RB // S5_T
S5_NRB = N_TOK // S5_RB
S5_SEQ_PER_RB = S5_RB // DEC_SEQ
S5_NLAG = 2 * S5_T - 1
S5_NS = 4 * S5_STATE
HC = 128
N_HCHUNK = N_TOK // HC
EXP_CLAMP = 80.0
MOE_BM = 256
N_ASSIGN = N_TOK * TOP_K
SEG_ALIGN = 16
TILE_SLOTS = -(-(TM * TOP_K + N_EXPERTS * (SEG_ALIGN - 1)) // 128) * 128
MOE_BLOCKS = -(-(N_ASSIGN + N_TILES * N_EXPERTS * (SEG_ALIGN - 1) + N_EXPERTS * (MOE_BM - 1)) // MOE_BM)
MOE_GRP = SEG_ALIGN
MOE_TILE_GROUPS = TILE_SLOTS // MOE_GRP
MOE_BLK_GROUPS = MOE_BM // MOE_GRP
MOE_DEPTH = 3
MOE_AHEAD = MOE_DEPTH - 1
MOE_GROUPS = MOE_BLOCKS * MOE_BLK_GROUPS
MOE_ZERO_GROUP = N_TILES * MOE_TILE_GROUPS
MOE_PAD_GROUPS = N_EXPERTS * (MOE_BLK_GROUPS - 1)
MOE_XS_ROWS = (N_TILES + 1) * TILE_SLOTS
MOE_YS_ROWS = N_TILES * TILE_SLOTS + MOE_PAD_GROUPS * MOE_GRP
MOE_BASE_ROWS = TM * TOP_K
MOE_TAIL_ARMS = (256, 128, 64, 32, 16)
VMEM_LIMIT = 56 * 1024 * 1024


def _rms(x, w):
    return x * lax.rsqrt(jnp.mean(x * x, axis=-1, keepdims=True) + NORM_EPS) * w


def _silu(x):
    return x * jax.nn.sigmoid(x)


MOD_TN = 1536


def _mod_kernel(cond_ref, w_ref, b_ref, o_ref):
    s = _silu(cond_ref[...]).astype(BF16)
    o_ref[0] = jnp.dot(s, w_ref[0].astype(BF16), preferred_element_type=F32) + b_ref[0]


def _modulation(cond, ada_w, ada_b):
    return pl.pallas_call(
        _mod_kernel,
        grid=(DEPTH, 6 * D_MODEL // MOD_TN),
        in_specs=[
            pl.BlockSpec((N_COND, D_MODEL), lambda l, j: (0, 0)),
            pl.BlockSpec((1, D_MODEL, MOD_TN), lambda l, j: (l, 0, j)),
            pl.BlockSpec((1, 1, MOD_TN), lambda l, j: (l, 0, j)),
        ],
        out_specs=pl.BlockSpec((1, N_COND, MOD_TN), lambda l, j: (l, 0, j)),
        out_shape=jax.ShapeDtypeStruct((DEPTH, N_COND, 6 * D_MODEL), F32),
        compiler_params=pltpu.CompilerParams(vmem_limit_bytes=VMEM_LIMIT),
        name="adaln_mod",
    )(cond, ada_w, ada_b.reshape(DEPTH, 1, 6 * D_MODEL))


def _in_kernel(layer, cond_ref, x_ref, mod_ref, nw_ref, w_ref, lb_ref,
               u_ref, q_ref, v_ref, g_ref, lff_ref, kf_ref, lfb_ref, kb_ref):
    del cond_ref
    mod = mod_ref[0]
    sh1 = mod[:, 0:D_MODEL]
    sc1 = mod[:, D_MODEL:2 * D_MODEL]
    h = _rms(x_ref[...], nw_ref[0:1, :]) * (1.0 + sc1) + sh1
    z = jnp.dot(h.astype(BF16), w_ref[...], preferred_element_type=F32)
    w = HGRN_WIDTH
    c0 = S5_WIDTH
    u_ref[...] = z[:, 0:c0]
    q_ref[...] = _silu(z[:, c0:c0 + w]) * (HGRN_DK ** -0.5)
    v_ref[...] = z[:, c0 + 3 * w:c0 + 4 * w]
    g_ref[...] = _silu(z[:, c0 + 4 * w:c0 + 5 * w])
    lbp = lb_ref[...]
    e = jnp.exp(lbp - jnp.max(lbp, axis=0, keepdims=True))
    probs = e / jnp.sum(e, axis=0, keepdims=True)
    lb = jnp.sum(probs[0:layer + 1], axis=0) - probs[0]
    for d, (lf_ref, k_ref) in enumerate(((lff_ref, kf_ref), (lfb_ref, kb_ref))):
        f = z[:, c0 + (1 + d) * w:c0 + (2 + d) * w]
        lbd = lb[d:d + 1, :]
        fg = lbd + (1.0 - lbd) * jax.nn.sigmoid(f)
        lf_ref[...] = jnp.log(fg)
        k_ref[...] = 1.0 - fg


def _in_proj(layer, tile_cond, x, mod_l, norm_w_l, w_in_l, hgrn_lb):
    tok = lambda i, c: (i, 0)
    out = jax.ShapeDtypeStruct((N_TOK, HGRN_WIDTH), F32)
    return pl.pallas_call(
        functools.partial(_in_kernel, layer),
        grid_spec=pltpu.PrefetchScalarGridSpec(
            num_scalar_prefetch=1,
            grid=(N_TILES,),
            in_specs=[
                pl.BlockSpec((TM, D_MODEL), tok),
                pl.BlockSpec((1, 1, 6 * D_MODEL), lambda i, c: (c[i], 0, 0)),
                pl.BlockSpec((4, D_MODEL), lambda i, c: (0, 0)),
                pl.BlockSpec((D_MODEL, IN_COLS), lambda i, c: (0, 0)),
                pl.BlockSpec((DEPTH, 2, HGRN_WIDTH), lambda i, c: (0, 0, 0)),
            ],
            out_specs=[pl.BlockSpec((TM, HGRN_WIDTH), tok)] * 8,
        ),
        out_shape=[out] * 8,
        compiler_params=pltpu.CompilerParams(
            dimension_semantics=("arbitrary",), vmem_limit_bytes=VMEM_LIMIT),
        name="in_proj",
    )(tile_cond, x, mod_l, norm_w_l, w_in_l, hgrn_lb)


def _s5_layer_weights(a_re, a_im, log_dt, b_re, b_im, c_re, c_im):
    hp = lax.Precision.HIGHEST
    t = S5_T
    dt = jnp.exp(log_dt)[..., None]
    lam_re = jnp.minimum(a_re, -1e-4)
    lam_im = a_im
    mag = jnp.exp(dt * lam_re)
    ang = dt * lam_im
    ab_re = mag * jnp.cos(ang)
    ab_im = mag * jnp.sin(ang)
    den = lam_re * lam_re + lam_im * lam_im
    nr = ab_re - 1.0
    ni = ab_im
    co_re = (nr * lam_re + ni * lam_im) / den
    co_im = (ni * lam_re - nr * lam_im) / den
    bb_re = co_re[..., None] * b_re - co_im[..., None] * b_im
    bb_im = co_re[..., None] * b_im + co_im[..., None] * b_re
    pr = [jnp.ones_like(ab_re)]
    pi = [jnp.zeros_like(ab_im)]
    for _ in range(t):
        pr.append(pr[-1] * ab_re - pi[-1] * ab_im)
        pi.append(pr[-2] * ab_im + pi[-1] * ab_re)
    pr = jnp.stack(pr, axis=1)
    pi = jnp.stack(pi, axis=1)
    abr = pr[..., None] * bb_re[:, None] - pi[..., None] * bb_im[:, None]
    abi = pr[..., None] * bb_im[:, None] + pi[..., None] * bb_re[:, None]
    kk = jnp.einsum('dgon,dkgni->dkgio', jnp.concatenate([c_re, -c_im], axis=-1),
                    jnp.concatenate([abr, abi], axis=-2), precision=hp)
    kern = (kk[0], kk[1])
    abs_ = ((abr[0], abi[0]), (abr[1], abi[1]))
    pows = ((pr[0], pi[0]), (pr[1], pi[1]))
    p = S5_GROUP_CH
    lags = jnp.concatenate([kern[1][t - 1:0:-1], (kern[0][0] + kern[1][0])[None], kern[0][1:t]], axis=0)
    m = lags.reshape(S5_NLAG, S5_NLT, S5_GPT, p, p)
    bd = jnp.einsum('ltaio,ab->ltaibo', m, jnp.eye(S5_GPT, dtype=F32))
    wcat = bd.reshape(S5_NLAG, S5_NLT, S5_LT, S5_LT).transpose(1, 2, 0, 3).reshape(S5_NLT, S5_LT, S5_NLAG * S5_LT)
    (abr_f, abi_f), (abr_b, abi_b) = abs_
    parts = (abr_f[t - 1::-1], abr_b[:t], abi_f[t - 1::-1], abi_b[:t])
    wst = jnp.concatenate([x.transpose(1, 0, 3, 2) for x in parts], axis=-1)
    wst = wst.reshape(S5_NLT, S5_GPT, t, p, S5_NS).transpose(0, 2, 1, 3, 4).reshape(S5_NLT, t * S5_LT, S5_NS)
    (pr_f, pi_f), (pr_b, pi_b) = pows

    def out_rows(cr, ci, pr_, pi_):
        re_rows = cr[None] * pr_[:, :, None, :] - ci[None] * pi_[:, :, None, :]
        im_rows = -(cr[None] * pi_[:, :, None, :] + ci[None] * pr_[:, :, None, :])
        return re_rows.transpose(1, 3, 0, 2), im_rows.transpose(1, 3, 0, 2)

    fre, fim = out_rows(c_re[0], c_im[0], pr_f[1:t + 1], pi_f[1:t + 1])
    bre, bim = out_rows(c_re[1], c_im[1], pr_b[t:0:-1], pi_b[t:0:-1])
    wout = lax.optimization_barrier(jnp.concatenate([fre, bre, fim, bim], axis=1))
    wdense = (wout.reshape(S5_NLT, S5_GPT, S5_NS, t, p).transpose(0, 2, 3, 1, 4)
              .reshape(S5_NLT, S5_NS, t * S5_LT))
    at = jnp.stack([jnp.concatenate([pr_f[t], pr_b[t]], axis=-1),
                    jnp.concatenate([pi_f[t], pi_b[t]], axis=-1)], axis=1)
    return (wcat.astype(BF16), wst.astype(BF16), wdense.astype(BF16),
            at.reshape(S5_NLT, S5_GPT, 2, 2 * S5_STATE))


def _s5_weights(*params):
    return jax.vmap(_s5_layer_weights)(*params)


def _s5_kernel(u_ref, wcat_ref, wst_ref, wd_ref, at_ref, h0_ref, y_ref, hfin_ref,
               wbig, dh_scr, hf_scr, hb_scr):
    t = S5_T
    n2 = 2 * S5_STATE
    r = pl.program_id(1)

    @pl.when(r == 0)
    def _():
        for s in range(t):
            wbig[s * S5_LT:(s + 1) * S5_LT, :] = wcat_ref[0, :, (t - 1 - s) * S5_LT:(2 * t - 1 - s) * S5_LT]

    xcat = jnp.concatenate([u_ref[pl.ds(s, S5_CR, stride=t), :].astype(BF16) for s in range(t)], axis=-1)
    yacc = jnp.dot(xcat, wbig[...], preferred_element_type=F32)
    lane_k = lax.broadcasted_iota(jnp.int32, (1, t * S5_LT), 1)
    grp_k = (lane_k % S5_LT) // S5_GROUP_CH
    for gi in range(S5_GPT):
        xg = jnp.where(grp_k == gi, xcat, jnp.zeros_like(xcat))
        dh = jnp.dot(xg, wst_ref[0], preferred_element_type=F32)
        dh_scr[gi, 0] = dh[:, 0:n2]
        dh_scr[gi, 1] = dh[:, n2:]

    lane = lax.broadcasted_iota(jnp.int32, (1, n2), 1)
    fwd_lane = lane < S5_STATE
    are = [at_ref[0, gi, 0:1, :] for gi in range(S5_GPT)]
    aim = [at_ref[0, gi, 1:2, :] for gi in range(S5_GPT)]

    def advance(gi, hre, him, dre, dim):
        return (are[gi] * hre - aim[gi] * him + dre, are[gi] * him + aim[gi] * hre + dim)

    @pl.when(r == 0)
    def _():
        nc = SEQ // t
        for gi in range(S5_GPT):
            hre = jnp.zeros((BATCH, n2), F32)
            him = jnp.zeros((BATCH, n2), F32)
            for s in range(nc):
                rows_f = pl.ds(s, BATCH, stride=nc)
                rows_b = pl.ds(nc - 1 - s, BATCH, stride=nc)
                hf_scr[gi, 0, rows_f, :] = hre
                hf_scr[gi, 1, rows_f, :] = him
                hb_scr[gi, 0, rows_b, :] = hre
                hb_scr[gi, 1, rows_b, :] = him
                dre = jnp.where(fwd_lane, dh_scr[gi, 0, rows_f, :], dh_scr[gi, 0, rows_b, :])
                dim = jnp.where(fwd_lane, dh_scr[gi, 1, rows_f, :], dh_scr[gi, 1, rows_b, :])
                hre, him = advance(gi, hre, him, dre, dim)
            hfin_ref[0, gi] = jnp.concatenate([hre, him], axis=-1)

    @pl.when(r > 0)
    def _():
        nc = DEC_SEQ // t
        nb = S5_SEQ_PER_RB
        init = tuple((h0_ref[0, 0, gi, b:b + 1, 0:n2], h0_ref[0, 0, gi, b:b + 1, n2:])
                     for gi in range(S5_GPT) for b in range(nb))

        def step(o, carry):
            new = []
            for gi in range(S5_GPT):
                for b in range(nb):
                    hre, him = carry[gi * nb + b]
                    rf = pl.multiple_of(b * nc + o * 8, 8)
                    rb = pl.multiple_of(b * nc + nc - 8 - o * 8, 8)
                    dfr = dh_scr[gi, 0, pl.ds(rf, 8), :]
                    dfi = dh_scr[gi, 1, pl.ds(rf, 8), :]
                    dbr = dh_scr[gi, 0, pl.ds(rb, 8), :]
                    dbi = dh_scr[gi, 1, pl.ds(rb, 8), :]
                    ent_re, ent_im = [], []
                    for i in range(8):
                        ent_re.append(hre)
                        ent_im.append(him)
                        dre = jnp.where(fwd_lane, dfr[i:i + 1], dbr[7 - i:8 - i])
                        dim = jnp.where(fwd_lane, dfi[i:i + 1], dbi[7 - i:8 - i])
                        hre, him = advance(gi, hre, him, dre, dim)
                    hf_scr[gi, 0, pl.ds(rf, 8), :] = jnp.concatenate(ent_re, axis=0)
                    hf_scr[gi, 1, pl.ds(rf, 8), :] = jnp.concatenate(ent_im, axis=0)
                    hb_scr[gi, 0, pl.ds(rb, 8), :] = jnp.concatenate(ent_re[::-1], axis=0)
                    hb_scr[gi, 1, pl.ds(rb, 8), :] = jnp.concatenate(ent_im[::-1], axis=0)
                    new.append((hre, him))
            return tuple(new)

        lax.fori_loop(0, nc // 8, step, init)

    for gi in range(S5_GPT):
        hent = jnp.concatenate([jnp.where(fwd_lane, hf_scr[gi, 0], hb_scr[gi, 0]),
                                jnp.where(fwd_lane, hf_scr[gi, 1], hb_scr[gi, 1])], axis=-1).astype(BF16)
        yi = jnp.dot(hent, wd_ref[0], preferred_element_type=F32)
        yacc = yacc + jnp.where(grp_k == gi, yi, 0.0)
    for s in range(t):
        y_ref[pl.ds(s, S5_CR, stride=t), :] = yacc[:, s * S5_LT:(s + 1) * S5_LT]


def _s5_scan(layer, u, wcat, wst, wdense, at, h0):
    tile = lambda j, r: (layer * S5_NLT + j, 0, 0)
    merge = lambda w: w.reshape((DEPTH * S5_NLT,) + w.shape[2:])
    wcat, wst, wdense, at = merge(wcat), merge(wst), merge(wdense), merge(at)
    return pl.pallas_call(
        _s5_kernel,
        grid=(S5_NLT, S5_NRB),
        in_specs=[
            pl.BlockSpec((S5_RB, S5_LT), lambda j, r: (r, j)),
            pl.BlockSpec((1, S5_LT, S5_NLAG * S5_LT), tile),
            pl.BlockSpec((1, S5_T * S5_LT, S5_NS), tile),
            pl.BlockSpec((1, S5_NS, S5_T * S5_LT), tile),
            pl.BlockSpec((1, S5_GPT, 2, 2 * S5_STATE), lambda j, r: (layer * S5_NLT + j, 0, 0, 0)),
            pl.BlockSpec((1, 1, S5_GPT, S5_SEQ_PER_RB, S5_NS), lambda j, r: (j, jnp.maximum(r - 1, 0), 0, 0, 0)),
        ],
        out_specs=[
            pl.BlockSpec((S5_RB, S5_LT), lambda j, r: (r, j)),
            pl.BlockSpec((1, S5_GPT, BATCH, S5_NS), lambda j, r: (j, 0, 0, 0)),
        ],
        out_shape=[
            jax.ShapeDtypeStruct((N_TOK, S5_WIDTH), F32),
            jax.ShapeDtypeStruct((S5_NLT, S5_GPT, BATCH, S5_NS), F32),
        ],
        scratch_shapes=[pltpu.VMEM((S5_T * S5_LT, S5_T * S5_LT), BF16)]
                       + [pltpu.VMEM((S5_GPT, 2, S5_CR, 2 * S5_STATE), F32)] * 3,
        compiler_params=pltpu.CompilerParams(
            dimension_semantics=("arbitrary", "arbitrary"), vmem_limit_bytes=VMEM_LIMIT),
        name="s5_scan",
    )(u, wcat, wst, wdense, at, h0)


def _split3(x):
    hi = x.astype(BF16)
    r1 = x - hi.astype(F32)
    mid = r1.astype(BF16)
    lo = (r1 - mid.astype(F32)).astype(BF16)
    return hi, mid, lo


def _piecewise_rows(b, blk, row_in_blk):
    parts = []
    for j in range(HC // blk):
        r = j * blk + row_in_blk
        parts.append(jnp.broadcast_to(b[r:r + 1, :], (blk, b.shape[1])))
    return parts[0] if len(parts) == 1 else jnp.concatenate(parts, axis=0)


def _nt(a, b):
    return lax.dot_general(a, b, (((1,), (1,)), ((), ())), preferred_element_type=F32)


def _tn(a, b):
    return lax.dot_general(a, b, (((0,), (0,)), ((), ())), preferred_element_type=F32)


def _hgrn_dir(reverse, q_ref, v_ref, lf_ref, k_ref, o_ref, st_ref, b_scr):
    row = lax.broadcasted_iota(jnp.int32, (HC, HC), 0)
    col = lax.broadcasted_iota(jnp.int32, (HC, HC), 1)
    causal = (col >= row) if reverse else (col <= row)
    tri = jnp.where(causal, 1.0, 0.0).astype(BF16)
    lf = lf_ref[...]
    hi, mid, lo = _split3(lf)
    ball = (jnp.dot(tri, hi, preferred_element_type=F32) + jnp.dot(tri, mid, preferred_element_type=F32)
            + jnp.dot(tri, lo, preferred_element_type=F32))
    last = 0 if reverse else HC - 1
    masks = []
    for blk in (128, 64, 32):
        half = blk // 2
        same = (row // blk) == (col // blk)
        t_hi = (row % blk) >= half
        s_hi = (col % blk) >= half
        if reverse:
            masks.append(same & jnp.logical_not(t_hi) & s_hi)
        else:
            masks.append(same & t_hi & jnp.logical_not(s_hi))
    b_scr[...] = ball
    mrow = 8 if reverse else 7
    mid = _piecewise_rows(ball, 16, mrow)
    spread = jnp.zeros((1, HGRN_WIDTH), F32)
    for j in range(HC // 16):
        for edge in (16 * j, 16 * j + 15):
            spread = jnp.maximum(spread, jnp.abs(ball[edge:edge + 1, :] - ball[16 * j + mrow:16 * j + mrow + 1, :]))
    stable = jnp.max(spread) <= EXP_CLAMP
    diag_ok = ((row // 16) == (col // 16)) & causal & stable
    for h in range(HGRN_HEADS):
        sl = slice(h * HGRN_DK, (h + 1) * HGRN_DK)
        b = ball[:, sl]
        q = q_ref[:, sl]
        k = k_ref[:, sl]
        v = v_ref[:, sl].astype(BF16)
        st = st_ref[h]
        b_last = b[last:last + 1, :]
        q_in = (q * jnp.exp(b)).astype(BF16)
        k_in = (k * jnp.exp(b_last - b)).astype(BF16)
        o = _nt(q_in, st.astype(BF16))
        st_ref[h] = jnp.exp(b_last) * st + _tn(v, k_in)
        scores = jnp.zeros((HC, HC), F32)
        for blk, mask in zip((128, 64, 32), masks):
            half = blk // 2
            zero = jnp.zeros((half, HGRN_DK), BF16)
            qparts, kparts = [], []
            for j in range(HC // blk):
                early = slice(j * blk, j * blk + half)
                late = slice(j * blk + half, (j + 1) * blk)
                if reverse:
                    m = b[j * blk + half:j * blk + half + 1, :]
                    qrows, krows = early, late
                else:
                    m = b[j * blk + half - 1:j * blk + half, :]
                    qrows, krows = late, early
                qe = (q[qrows] * jnp.exp(b[qrows] - m)).astype(BF16)
                ke = (k[krows] * jnp.exp(m - b[krows])).astype(BF16)
                qparts += [qe, zero] if reverse else [zero, qe]
                kparts += [zero, ke] if reverse else [ke, zero]
            part = _nt(jnp.concatenate(qparts, axis=0), jnp.concatenate(kparts, axis=0))
            scores = scores + (part if blk == HC else jnp.where(mask, part, 0.0))
        m = mid[:, sl]
        qd = (q * jnp.exp(jnp.minimum(b - m, EXP_CLAMP))).astype(BF16)
        kd = (k * jnp.exp(jnp.minimum(m - b, EXP_CLAMP))).astype(BF16)
        scores = scores + jnp.where(diag_ok, _nt(qd, kd), 0.0)
        o_ref[:, sl] = o + jnp.dot(scores.astype(BF16), v, preferred_element_type=F32)

    return stable


def _hgrn_exact_diagonal(reverse, stable, q_ref, v_ref, k_ref, o_ref, b_scr):
    @pl.when(jnp.logical_not(stable))
    def _():
        pos = lax.broadcasted_iota(jnp.int32, (HC, 1), 0) % 16

        def lag(d, c):
            shift = (HC - d) % HC if reverse else d
            valid = (pos + d <= 15) if reverse else (pos >= d)
            for h in range(HGRN_HEADS):
                sl = slice(h * HGRN_DK, (h + 1) * HGRN_DK)
                b = b_scr[:, sl]
                bs = pltpu.roll(b, shift, 0)
                ks = pltpu.roll(k_ref[:, sl], shift, 0)
                vs = pltpu.roll(v_ref[:, sl], shift, 0)
                e = jnp.exp(jnp.where(valid, b - bs, 0.0))
                w = jnp.sum(q_ref[:, sl] * ks * e, axis=-1, keepdims=True)
                o_ref[:, sl] = o_ref[:, sl] + jnp.where(valid, w, 0.0) * vs
            return c

        lax.fori_loop(0, 16, lag, 0)


def _hgrn_kernel(cf_ref, cb_ref, seq_ref, first_ref, last_ref,
                 qf_ref, vf_ref, lff_ref, kf_ref, qb_ref, vb_ref, lfb_ref, kb_ref, s0_ref,
                 of_ref, ob_ref, sfin_ref, st_scr, b_scr):
    del cf_ref, cb_ref, seq_ref
    j = pl.program_id(0)

    @pl.when(first_ref[j] == 1)
    def _():
        st_scr[...] = s0_ref[0]

    ok_f = _hgrn_dir(False, qf_ref, vf_ref, lff_ref, kf_ref, of_ref, st_scr.at[0], b_scr.at[0])
    ok_b = _hgrn_dir(True, qb_ref, vb_ref, lfb_ref, kb_ref, ob_ref, st_scr.at[1], b_scr.at[1])
    _hgrn_exact_diagonal(False, ok_f, qf_ref, vf_ref, kf_ref, of_ref, b_scr.at[0])
    _hgrn_exact_diagonal(True, ok_b, qb_ref, vb_ref, kb_ref, ob_ref, b_scr.at[1])

    @pl.when(last_ref[j] == 1)
    def _():
        sfin_ref[0] = st_scr[...]


def _hgrn_tables():
    cf, cb, sq, first, last = [], [], [], [], []
    base = 0
    for s in range(N_SEQ):
        nc = (SEQ if s < BATCH else DEC_SEQ) // HC
        for t in range(nc):
            cf.append(base + t)
            cb.append(base + nc - 1 - t)
            sq.append(s)
            first.append(int(t == 0))
            last.append(int(t == nc - 1))
        base += nc
    return tuple(jnp.asarray(x, jnp.int32) for x in (cf, cb, sq, first, last))


def _hgrn_scan(qs, v, lf_f, k_f, lf_b, k_b, s0):
    fwd = lambda j, cf, cb, sq, fi, la: (cf[j], 0)
    bwd = lambda j, cf, cb, sq, fi, la: (cb[j], 0)
    seq = lambda j, cf, cb, sq, fi, la: (sq[j], 0, 0, 0, 0)
    tile = (HC, HGRN_WIDTH)
    sblk = (1, 2, HGRN_HEADS, HGRN_DV, HGRN_DK)
    out = jax.ShapeDtypeStruct((N_TOK, HGRN_WIDTH), F32)
    return pl.pallas_call(
        _hgrn_kernel,
        grid_spec=pltpu.PrefetchScalarGridSpec(
            num_scalar_prefetch=5,
            grid=(N_HCHUNK,),
            in_specs=[pl.BlockSpec(tile, fwd)] * 4 + [pl.BlockSpec(tile, bwd)] * 4
                     + [pl.BlockSpec(sblk, seq)],
            out_specs=[pl.BlockSpec(tile, fwd), pl.BlockSpec(tile, bwd), pl.BlockSpec(sblk, seq)],
            scratch_shapes=[pltpu.VMEM(sblk[1:], F32), pltpu.VMEM((2, HC, HGRN_WIDTH), F32)],
        ),
        out_shape=[out, out, jax.ShapeDtypeStruct((N_SEQ,) + sblk[1:], F32)],
        compiler_params=pltpu.CompilerParams(
            dimension_semantics=("arbitrary",), vmem_limit_bytes=VMEM_LIMIT),
        name="hgrn_scan",
    )(*_hgrn_tables(), qs, v, lf_f, k_f, qs, v, lf_b, k_b, s0)


def _gelu_tanh(x):
    return 0.5 * x * (1.0 + jnp.tanh(math.sqrt(2.0 / math.pi) * (x + 0.044715 * (x * x * x))))


def _out_kernel(cond_ref, x_ref, yc_ref, u_ref, of_ref, ob_ref, g_ref, mod_ref, nw_ref, d_ref,
                wglu_ref, bglu_ref, hnw_ref, wout_ref, rw_ref, rb_ref,
                x1_ref, h2_ref, ri_ref, rg_ref, cnt_ref):
    del cond_ref
    mod = mod_ref[0]
    g1 = mod[:, 2 * D_MODEL:3 * D_MODEL]
    sh2 = mod[:, 3 * D_MODEL:4 * D_MODEL]
    sc2 = mod[:, 4 * D_MODEL:5 * D_MODEL]
    y = _gelu_tanh(yc_ref[...] + d_ref[...] * u_ref[...])
    y_s5 = y * jax.nn.sigmoid(jnp.dot(y.astype(BF16), wglu_ref[...], preferred_element_type=F32) + bglu_ref[...])
    o = of_ref[...] + ob_ref[...]
    gs = g_ref[...]
    heads = []
    for h in range(HGRN_HEADS):
        sl = slice(h * HGRN_DV, (h + 1) * HGRN_DV)
        heads.append(_rms(o[:, sl], hnw_ref[...]) * gs[:, sl])
    y_hg = jnp.concatenate(heads, axis=-1)
    mix = (jnp.dot(y_s5.astype(BF16), wout_ref[0:S5_WIDTH, :], preferred_element_type=F32)
           + jnp.dot(y_hg.astype(BF16), wout_ref[S5_WIDTH:, :], preferred_element_type=F32))
    x1 = x_ref[...] + g1 * _rms(mix, nw_ref[1:2, :])
    x1_ref[...] = x1
    h2 = _rms(x1, nw_ref[2:3, :]) * (1.0 + sc2) + sh2
    h2_hi = h2.astype(BF16)
    h2_ref[...] = h2_hi
    h2_lo = (h2 - h2_hi.astype(F32)).astype(BF16)
    logits = rb_ref[...]
    for a in (h2_hi, h2_lo):
        for part in range(2):
            logits = logits + jnp.dot(a, rw_ref[part], preferred_element_type=F32)
    eidx = lax.broadcasted_iota(jnp.int32, (TM, N_EXPERTS), 1).astype(F32)
    vals = logits
    top_v, top_i, onehots = [], [], []
    for _ in range(TOP_K):
        mx = jnp.max(vals, axis=-1, keepdims=True)
        ix = jnp.min(jnp.where(vals == mx, eidx, float(N_EXPERTS)), axis=-1, keepdims=True)
        sel = eidx == ix
        top_v.append(mx)
        top_i.append(ix)
        onehots.append(sel)
        vals = jnp.where(sel, -jnp.inf, vals)
    ex = [jnp.exp(tv - top_v[0]) for tv in top_v]
    den = ex[0] + ex[1] + ex[2] + ex[3]
    tot = jnp.zeros((TM, N_EXPERTS), F32)
    for sel in onehots:
        tot = tot + jnp.where(sel, 1.0, 0.0)
    r_t = lax.broadcasted_iota(jnp.int32, (TM, TM), 0)
    r_s = lax.broadcasted_iota(jnp.int32, (TM, TM), 1)
    strict = jnp.where(r_s < r_t, 1.0, 0.0).astype(BF16)
    before = jnp.dot(strict, tot.astype(BF16), preferred_element_type=F32)
    cnt = jnp.sum(tot, axis=0, keepdims=True)
    seg = jnp.floor((cnt + (SEG_ALIGN - 1)) * (1.0 / SEG_ALIGN)) * SEG_ALIGN
    e_r = lax.broadcasted_iota(jnp.int32, (N_EXPERTS, N_EXPERTS), 0)
    e_c = lax.broadcasted_iota(jnp.int32, (N_EXPERTS, N_EXPERTS), 1)
    lstart = jnp.dot(seg, jnp.where(e_r < e_c, 1.0, 0.0), precision=lax.Precision.HIGHEST,
                     preferred_element_type=F32)
    before = before + lstart
    lane = lax.broadcasted_iota(jnp.int32, (TM, 128), 1)
    ri = jnp.zeros((TM, 128), F32)
    rg = jnp.zeros((TM, 128), F32)
    for kk in range(TOP_K):
        rank = jnp.sum(jnp.where(onehots[kk], before, 0.0), axis=-1, keepdims=True)
        ri = jnp.where(lane == kk, top_i[kk], ri)
        ri = jnp.where(lane == TOP_K + kk, rank, ri)
        rg = jnp.where(lane == kk, ex[kk] / den, rg)
    ri_ref[...] = ri.astype(jnp.int32)
    rg_ref[...] = rg
    cnt_ref[0] = cnt.astype(jnp.int32)


def _out_proj(tile_cond, x, ycore, u, o_f, o_b, gs, mod_l, norm_w_l, s5_d_l, wglu_l, bglu_l, hnw_l,
              wout_l, rw_l, rb_l):
    tok = lambda i, c: (i, 0)
    full2 = lambda i, c: (0, 0)
    half = pl.BlockSpec((TM, HGRN_WIDTH), tok)
    wide = pl.BlockSpec((TM, D_MODEL), tok)
    return pl.pallas_call(
        _out_kernel,
        grid_spec=pltpu.PrefetchScalarGridSpec(
            num_scalar_prefetch=1,
            grid=(N_TILES,),
            in_specs=[
                wide, half, half, half, half, half,
                pl.BlockSpec((1, 1, 6 * D_MODEL), lambda i, c: (c[i], 0, 0)),
                pl.BlockSpec((4, D_MODEL), full2),
                pl.BlockSpec((1, S5_WIDTH), full2),
                pl.BlockSpec((S5_WIDTH, S5_WIDTH), full2),
                pl.BlockSpec((1, S5_WIDTH), full2),
                pl.BlockSpec((1, HGRN_DV), full2),
                pl.BlockSpec((D_MODEL, D_MODEL), full2),
                pl.BlockSpec((2, D_MODEL, N_EXPERTS), lambda i, c: (0, 0, 0)),
                pl.BlockSpec((1, N_EXPERTS), full2),
            ],
            out_specs=[wide, wide, pl.BlockSpec((TM, 128), tok), pl.BlockSpec((TM, 128), tok),
                       pl.BlockSpec((1, 1, N_EXPERTS), lambda i, c: (i, 0, 0))],
        ),
        out_shape=[
            jax.ShapeDtypeStruct((N_TOK, D_MODEL), F32),
            jax.ShapeDtypeStruct((N_TOK, D_MODEL), BF16),
            jax.ShapeDtypeStruct((N_TOK, 128), jnp.int32),
            jax.ShapeDtypeStruct((N_TOK, 128), F32),
            jax.ShapeDtypeStruct((N_TILES, 1, N_EXPERTS), jnp.int32),
        ],
        compiler_params=pltpu.CompilerParams(
            dimension_semantics=("arbitrary",), vmem_limit_bytes=VMEM_LIMIT),
        name="out_proj_router",
    )(tile_cond, x, ycore, u, o_f, o_b, gs, mod_l, norm_w_l, s5_d_l, wglu_l, bglu_l, hnw_l,
      wout_l, rw_l, rb_l)


def _dispatch_kernel(slot_ref, h_ref, o_ref):
    srow = lax.broadcasted_iota(jnp.int32, (TILE_SLOTS, TM), 0)
    perm = jnp.zeros((TILE_SLOTS, TM), F32)
    for kk in range(TOP_K):
        perm = perm + jnp.where(srow == slot_ref[0, kk:kk + 1, :], 1.0, 0.0)
    o_ref[...] = jnp.dot(perm.astype(BF16), h_ref[...], preferred_element_type=F32).astype(BF16)


def _dispatch(slot_t, h2):
    return pl.pallas_call(
        _dispatch_kernel,
        grid=(N_TILES + 1,),
        in_specs=[pl.BlockSpec((1, TOP_K, TM), lambda i: (i, 0, 0)),
                  pl.BlockSpec((TM, D_MODEL), lambda i: (jnp.minimum(i, N_TILES - 1), 0))],
        out_specs=pl.BlockSpec((TILE_SLOTS, D_MODEL), lambda i: (i, 0)),
        out_shape=jax.ShapeDtypeStruct((MOE_XS_ROWS, D_MODEL), BF16),
        compiler_params=pltpu.CompilerParams(
            dimension_semantics=("arbitrary",), vmem_limit_bytes=VMEM_LIMIT),
        name="moe_dispatch",
    )(slot_t, h2)


def _moe_kernel(nblk_ref, gbase_ref, edge_ref, gsrc_ref, gdst_ref, w1_ref, b1_ref, w2_ref, b2_ref, xs_ref, ys_ref,
                xbuf, obuf, w1b, w2b, sem_in, sem_out):
    e = pl.program_id(0)
    nb = nblk_ref[e]
    k0 = gbase_ref[e] // MOE_BLK_GROUPS
    first_active = (edge_ref[e] & 1) != 0
    last_active = (edge_ref[e] & 2) != 0

    def gather(k):
        buf = k % MOE_DEPTH
        return [pltpu.make_async_copy(
            xs_ref.at[pl.ds(pl.multiple_of(gsrc_ref[k * MOE_BLK_GROUPS + g] * MOE_GRP, MOE_GRP), MOE_GRP)],
            xbuf.at[buf, pl.ds(g * MOE_GRP, MOE_GRP)], sem_in.at[buf]) for g in range(MOE_BLK_GROUPS)]

    def scatter(k):
        buf = k % MOE_DEPTH
        return [pltpu.make_async_copy(
            obuf.at[buf, pl.ds(g * MOE_GRP, MOE_GRP)],
            ys_ref.at[pl.ds(pl.multiple_of(gdst_ref[k * MOE_BLK_GROUPS + g] * MOE_GRP, MOE_GRP), MOE_GRP)],
            sem_out.at[buf]) for g in range(MOE_BLK_GROUPS)]

    @pl.when(nb > 0)
    def _():
        def cast(j, c):
            r = pl.multiple_of(j * 128, 128)
            w1b[pl.ds(r, 128), :] = w1_ref[0, pl.ds(r, 128), :].astype(BF16)
            w2b[pl.ds(r, 128), :] = w2_ref[0, pl.ds(r, 128), :].astype(BF16)
            return c

        @pl.when(first_active)
        def _():
            for a in range(MOE_AHEAD):
                for cp in gather(k0 + a):
                    cp.start(priority=1)

        lax.fori_loop(0, D_MODEL // 128, cast, 0)

        def block(b, c):
            k = k0 + b
            buf = k % MOE_DEPTH

            @pl.when(k >= MOE_DEPTH)
            def _():
                for cp in scatter(k - MOE_DEPTH):
                    cp.wait()

            for cp in gather(k):
                cp.wait()
            for cp in gather(k + MOE_AHEAD):
                cp.start(priority=1)
            h = jnp.dot(xbuf[buf], w1b[...], preferred_element_type=F32) + b1_ref[0]
            glu = jnp.minimum(h[:, :D_FF], SWIGLU_LIMIT)
            lin = jnp.clip(h[:, D_FF:], -SWIGLU_LIMIT, SWIGLU_LIMIT)
            act = glu * jax.nn.sigmoid(SWIGLU_ALPHA * glu) * (lin + 1.0)
            obuf[buf] = (jnp.dot(act.astype(BF16), w2b[...], preferred_element_type=F32) + b2_ref[0]).astype(BF16)
            for cp in scatter(k):
                cp.start(priority=1)
            return c

        lax.fori_loop(0, nb, block, 0)

        @pl.when(last_active)
        def _():
            end = k0 + nb
            for a in range(MOE_AHEAD):
                for cp in gather(end + a):
                    cp.wait()
            for a in range(MOE_DEPTH, 0, -1):
                @pl.when(end - a >= 0)
                def _():
                    for cp in scatter(end - a):
                        cp.wait()


def _moe_experts(layer, nblk, gbase, edge, gsrc, gdst, xs, w1, b1, w2, b2):
    exp3 = lambda e, *_: (layer * N_EXPERTS + e, 0, 0)
    w1 = w1.reshape(DEPTH * N_EXPERTS, D_MODEL, 2 * D_FF)
    w2 = w2.reshape(DEPTH * N_EXPERTS, D_FF, D_MODEL)
    return pl.pallas_call(
        _moe_kernel,
        grid_spec=pltpu.PrefetchScalarGridSpec(
            num_scalar_prefetch=5,
            grid=(N_EXPERTS,),
            in_specs=[
                pl.BlockSpec((1, D_MODEL, 2 * D_FF), exp3),
                pl.BlockSpec((1, 1, 2 * D_FF), exp3),
                pl.BlockSpec((1, D_FF, D_MODEL), exp3),
                pl.BlockSpec((1, 1, D_MODEL), exp3),
                pl.BlockSpec(memory_space=pl.ANY),
            ],
            out_specs=pl.BlockSpec(memory_space=pl.ANY),
            scratch_shapes=[pltpu.VMEM((MOE_DEPTH, MOE_BM, D_MODEL), BF16), pltpu.VMEM((MOE_DEPTH, MOE_BM, D_MODEL), BF16),
                            pltpu.VMEM((D_MODEL, 2 * D_FF), BF16), pltpu.VMEM((D_FF, D_MODEL), BF16),
                            pltpu.SemaphoreType.DMA((MOE_DEPTH,)), pltpu.SemaphoreType.DMA((MOE_DEPTH,))],
        ),
        out_shape=jax.ShapeDtypeStruct((MOE_YS_ROWS, D_MODEL), BF16),
        compiler_params=pltpu.CompilerParams(
            dimension_semantics=("arbitrary",), vmem_limit_bytes=VMEM_LIMIT),
        name="moe_experts",
    )(nblk, gbase, edge, gsrc, gdst, w1, b1.reshape(DEPTH * N_EXPERTS, 1, 2 * D_FF), w2,
      b2.reshape(DEPTH * N_EXPERTS, 1, D_MODEL), xs)


def _combine_kernel(split, cond_ref, tg_ref, x1_ref, slot_ref, gate_ref, mod_ref, nw_ref, ys_ref, *rest):
    del cond_ref
    outs, (buf, sem) = rest[:-2], rest[-2:]
    i = pl.program_id(0)
    cur = i % 2

    def fetch(tile, b, wait):
        def go(cp):
            cp.wait() if wait else cp.start()

        row0 = pl.multiple_of(tile * TILE_SLOTS, TILE_SLOTS)
        go(pltpu.make_async_copy(ys_ref.at[pl.ds(row0, MOE_BASE_ROWS)], buf.at[b, pl.ds(0, MOE_BASE_ROWS)],
                                 sem.at[b]))
        extra = tg_ref[tile] * MOE_GRP - MOE_BASE_ROWS
        for arm in MOE_TAIL_ARMS:
            @pl.when((extra & arm) != 0)
            def _():
                off = pl.multiple_of(MOE_BASE_ROWS + (extra & ~(2 * arm - 1)), MOE_GRP)
                go(pltpu.make_async_copy(ys_ref.at[pl.ds(row0 + off, arm)], buf.at[b, pl.ds(off, arm)], sem.at[b]))

    @pl.when(i == 0)
    def _():
        buf[...] = jnp.zeros_like(buf)
        fetch(0, 0, False)

    @pl.when(i + 1 < N_TILES)
    def _():
        fetch(i + 1, 1 - cur, False)

    fetch(i, cur, True)
    scol = lax.broadcasted_iota(jnp.int32, (TM, TILE_SLOTS), 1)
    slot = slot_ref[...]
    gate = gate_ref[...]
    gmat = jnp.zeros((TM, TILE_SLOTS), F32)
    for kk in range(TOP_K):
        gmat = gmat + jnp.where(scol == slot[:, kk:kk + 1], gate[:, kk:kk + 1], 0.0)
    gmat = gmat.astype(BF16)
    ffn = jnp.dot(gmat, buf[cur], preferred_element_type=F32)
    g2 = mod_ref[0][:, 5 * D_MODEL:6 * D_MODEL]
    out = x1_ref[...] + g2 * _rms(ffn, nw_ref[3:4, :])
    if split:
        @pl.when(i < N_PROMPT // TM)
        def _():
            outs[0][...] = out

        @pl.when(i >= N_PROMPT // TM)
        def _():
            outs[1][...] = out
    else:
        outs[0][...] = out


def _combine(split, tile_cond, tile_groups, x1, slot, gate, mod_l, norm_w_l, ys):
    tok = lambda i, *_: (i, 0)
    if split:
        np_tiles = N_PROMPT // TM
        out_specs = [pl.BlockSpec((TM, D_MODEL), lambda i, *_: (jnp.minimum(i, np_tiles - 1), 0)),
                     pl.BlockSpec((TM, D_MODEL), lambda i, *_: (jnp.maximum(i - np_tiles, 0), 0))]
        out_shape = [jax.ShapeDtypeStruct((N_PROMPT, D_MODEL), F32), jax.ShapeDtypeStruct((N_SAMPLE, D_MODEL), F32)]
    else:
        out_specs = pl.BlockSpec((TM, D_MODEL), tok)
        out_shape = jax.ShapeDtypeStruct((N_TOK, D_MODEL), F32)
    return pl.pallas_call(
        functools.partial(_combine_kernel, split),
        grid_spec=pltpu.PrefetchScalarGridSpec(
            num_scalar_prefetch=2,
            grid=(N_TILES,),
            in_specs=[
                pl.BlockSpec((TM, D_MODEL), tok),
                pl.BlockSpec((TM, TOP_K), tok),
                pl.BlockSpec((TM, 128), tok),
                pl.BlockSpec((1, 1, 6 * D_MODEL), lambda i, c, *_: (c[i], 0, 0)),
                pl.BlockSpec((4, D_MODEL), lambda i, *_: (0, 0)),
                pl.BlockSpec(memory_space=pl.ANY),
            ],
            out_specs=out_specs,
            scratch_shapes=[pltpu.VMEM((2, TILE_SLOTS, D_MODEL), BF16), pltpu.SemaphoreType.DMA((2,))],
        ),
        out_shape=out_shape,
        compiler_params=pltpu.CompilerParams(
            dimension_semantics=("arbitrary",), vmem_limit_bytes=VMEM_LIMIT),
        name="moe_combine",
    )(tile_cond, tile_groups, x1, slot, gate, mod_l, norm_w_l, ys)


def _pos_embed_2d(rows, dim):
    r = jnp.repeat(jnp.arange(rows, dtype=F32), GRID_W)
    col = jnp.tile(jnp.arange(GRID_W, dtype=F32), rows)
    quarter = dim // 4
    omega = 1.0 / (POS_BASE ** (jnp.arange(quarter, dtype=F32) / quarter))

    def emb(pos):
        ang = pos[:, None] * omega[None, :]
        return jnp.concatenate([jnp.sin(ang), jnp.cos(ang)], axis=-1)

    return jnp.concatenate([emb(r), emb(col)], axis=-1)


def _assemble_kernel(xp_ref, xs_ref, pos_ref, o_ref):
    i = pl.program_id(0)

    @pl.when(i < N_PROMPT // TM)
    def _():
        o_ref[...] = xp_ref[...]

    @pl.when(i >= N_PROMPT // TM)
    def _():
        o_ref[...] = xs_ref[...] + pos_ref[...]


def _assemble(xp, xs, pos):
    np_tiles = N_PROMPT // TM
    return pl.pallas_call(
        _assemble_kernel,
        grid=(N_TILES,),
        in_specs=[pl.BlockSpec((TM, D_MODEL), lambda i: (jnp.minimum(i, np_tiles - 1), 0)),
                  pl.BlockSpec((TM, D_MODEL), lambda i: (jnp.maximum(i - np_tiles, 0), 0)),
                  pl.BlockSpec((TM, D_MODEL), lambda i: (jnp.maximum(i - np_tiles, 0) % (DEC_SEQ // TM), 0))],
        out_specs=pl.BlockSpec((TM, D_MODEL), lambda i: (i, 0)),
        out_shape=jax.ShapeDtypeStruct((N_TOK, D_MODEL), F32),
        compiler_params=pltpu.CompilerParams(dimension_semantics=("arbitrary",)),
        name="assemble_tokens",
    )(xp, xs, pos)


def _routing_tables(ri, counts):
    i32 = jnp.int32
    slot = ri[:, TOP_K:2 * TOP_K].reshape(N_TILES, TM, TOP_K)
    cnt = counts.reshape(N_TILES, N_EXPERTS)
    seg = (cnt + SEG_ALIGN - 1) // SEG_ALIGN
    lstart = jnp.cumsum(seg, axis=1) - seg
    tile_groups = jnp.sum(seg, axis=1)
    region = jnp.sum(seg, axis=0)
    nblk = (region + MOE_BLK_GROUPS - 1) // MOE_BLK_GROUPS
    gend = jnp.cumsum(nblk) * MOE_BLK_GROUPS
    gbase = gend - nblk * MOE_BLK_GROUPS
    has = nblk > 0
    edge = (has & (gbase == 0)).astype(i32) + 2 * (has & (gend == gend[-1])).astype(i32)
    cum = jnp.cumsum(seg, axis=0) - seg
    g = jnp.arange(MOE_GROUPS + MOE_AHEAD * MOE_BLK_GROUPS, dtype=i32)
    e_of = jnp.minimum(jnp.sum(gend[None, :] <= g[:, None], axis=1), N_EXPERTS - 1)
    onehot_e = (e_of[:, None] == jnp.arange(N_EXPERTS)[None, :]).astype(F32)
    pick_e = lambda t: jnp.dot(onehot_e, t.astype(F32), precision=lax.Precision.HIGHEST).astype(i32)
    off = g - pick_e(gbase)
    real = (g < gend[-1]) & (off < pick_e(region))
    cum_e = pick_e(cum.T)
    tile_of = jnp.sum(cum_e <= off[:, None], axis=1) - 1
    onehot_t = tile_of[:, None] == jnp.arange(N_TILES)[None, :]
    pick_t = lambda t: jnp.sum(jnp.where(onehot_t, t, 0), axis=1)
    pos = tile_of * MOE_TILE_GROUPS + pick_t(pick_e(lstart.T)) + off - pick_t(cum_e)
    pad_rank = jnp.cumsum((~real).astype(i32)) - 1
    gsrc = jnp.where(real, pos, MOE_ZERO_GROUP).astype(i32)
    gdst = jnp.where(real, pos, MOE_ZERO_GROUP + jnp.minimum(pad_rank, MOE_PAD_GROUPS - 1)).astype(i32)
    slot_t = jnp.concatenate([slot.transpose(0, 2, 1), jnp.full((1, TOP_K, TM), -1, i32)], axis=0)
    return (slot.reshape(N_TOK, TOP_K).astype(i32), slot_t.astype(i32),
            (nblk.astype(i32), gbase.astype(i32), edge.astype(i32), gsrc, gdst), tile_groups.astype(i32))


def kernel(x_prompt, x_sample, state_s5_re, state_s5_im, state_hgrn, c, c_ctx, ada_w, ada_b, norm_w, w_in,
           s5_a_re, s5_a_im, s5_log_dt, s5_b_re, s5_b_im, s5_c_re, s5_c_im, s5_d, s5_w_glu, s5_b_glu,
           hgrn_lb, hgrn_norm_w, w_out, router_w, router_b, exp_w1, exp_b1, exp_w2, exp_b2):
    x = _assemble(x_prompt.reshape(N_PROMPT, D_MODEL), x_sample.reshape(N_SAMPLE, D_MODEL),
                  _pos_embed_2d(DEC_SEQ // GRID_W, D_MODEL))
    cond = jnp.concatenate([c_ctx[None, :], c, jnp.zeros((N_COND - 1 - DEC_BATCH, D_MODEL), F32)], axis=0)
    mod = _modulation(cond, ada_w, ada_b).reshape(DEPTH, N_COND, 1, 6 * D_MODEL)
    tiles = jnp.arange(N_TILES, dtype=jnp.int32)
    tile_cond = jnp.where(tiles < N_PROMPT // TM, 0, 1 + (tiles - N_PROMPT // TM) // (DEC_SEQ // TM)).astype(jnp.int32)
    w_in_b = w_in.astype(BF16)
    w_out_b = w_out.astype(BF16)
    w_glu_b = s5_w_glu.astype(BF16)
    router_hi = router_w.astype(BF16)
    router_parts = jnp.stack([router_hi, (router_w - router_hi.astype(F32)).astype(BF16)])

    s5_w = _s5_weights(s5_a_re, s5_a_im, s5_log_dt, s5_b_re, s5_b_im, s5_c_re, s5_c_im)
    fin_re, fin_im, fin_hg = [], [], []
    n = S5_STATE
    for l in range(DEPTH):
        u, qs, v, gs, lf_f, k_f, lf_b, k_b = _in_proj(l, tile_cond, x, mod[l], norm_w[l], w_in_b[l], hgrn_lb)

        h0 = jnp.concatenate([
            state_s5_re[:, l].transpose(2, 0, 1, 3).reshape(S5_GROUPS, DEC_BATCH, 2 * n),
            state_s5_im[:, l].transpose(2, 0, 1, 3).reshape(S5_GROUPS, DEC_BATCH, 2 * n)], axis=-1)
        h0 = h0.reshape(S5_NLT, S5_GPT, S5_NRB - 1, S5_SEQ_PER_RB, S5_NS).transpose(0, 2, 1, 3, 4)
        ycore, hfin = _s5_scan(l, u, *s5_w, h0)
        hfin = hfin.reshape(S5_GROUPS, BATCH, S5_NS)
        fin_re.append(hfin[:, :, 0:2 * n].reshape(S5_GROUPS, BATCH, 2, n).transpose(1, 2, 0, 3))
        fin_im.append(hfin[:, :, 2 * n:].reshape(S5_GROUPS, BATCH, 2, n).transpose(1, 2, 0, 3))

        s0 = jnp.concatenate([jnp.zeros((BATCH, 2, HGRN_HEADS, HGRN_DV, HGRN_DK), F32),
                              jnp.swapaxes(state_hgrn[:, l], -1, -2)], axis=0)
        o_f, o_b, sfin = _hgrn_scan(qs, v, lf_f, k_f, lf_b, k_b, s0)
        fin_hg.append(jnp.swapaxes(sfin[:BATCH], -1, -2))

        x1, h2, ri, gate, counts = _out_proj(
            tile_cond, x, ycore, u, o_f, o_b, gs, mod[l], norm_w[l], s5_d[l].reshape(1, S5_WIDTH),
            w_glu_b[l], s5_b_glu[l].reshape(1, S5_WIDTH), hgrn_norm_w[l].reshape(1, HGRN_DV),
            w_out_b[l], router_parts[:, l], router_b[l].reshape(1, N_EXPERTS))

        slot, slot_t, groups, tile_groups = _routing_tables(ri, counts)
        xs = _dispatch(slot_t, h2)
        ys = _moe_experts(l, *groups, xs, exp_w1, exp_b1, exp_w2, exp_b2)
        x = _combine(l == DEPTH - 1, tile_cond, tile_groups, x1, slot, gate, mod[l], norm_w[l], ys)

    y_prompt, y_sample = x
    return (y_prompt.reshape(BATCH, SEQ, D_MODEL), y_sample.reshape(DEC_BATCH, DEC_SEQ, D_MODEL),
            jnp.stack(fin_re, axis=1), jnp.stack(fin_im, axis=1), jnp.stack(fin_hg, axis=1))
```

```python
import functools
import math

import jax
import jax.numpy as jnp
from jax import lax
from jax.experimental import pallas as pl
from jax.experimental.pallas import tpu as pltpu

F32 = jnp.float32
BF16 = jnp.bfloat16

D_MODEL = 1024
BATCH = 16
SEQ = 256
DEPTH = 2
DEC_BATCH = 4
DEC_SEQ = 2048
GRID_W = 64
S5_WIDTH = 512
S5_GROUP_CH = 16
S5_GROUPS = 32
S5_STATE = 64
HGRN_WIDTH = 512
HGRN_DK = 128
HGRN_HEADS = 4
HGRN_DV = 128
IN_COLS = S5_WIDTH + 5 * HGRN_WIDTH
N_EXPERTS = 32
TOP_K = 4
D_FF = D_MODEL
SWIGLU_LIMIT = 7.0
SWIGLU_ALPHA = 1.702
NORM_EPS = 1e-6
POS_BASE = 10000.0

N_PROMPT = BATCH * SEQ
N_SAMPLE = DEC_BATCH * DEC_SEQ
N_TOK = N_PROMPT + N_SAMPLE
N_SEQ = BATCH + DEC_BATCH
N_COND = 8

TM = 256
N_TILES = N_TOK // TM
S5_T = 16
S5_LT = 128
S5_GPT = S5_LT // S5_GROUP_CH
S5_NLT = S5_WIDTH // S5_LT
S5_RB = N_PROMPT
S5_CR = S5_RB // S5_T
S5_NRB = N_TOK // S5_RB
S5_SEQ_PER_RB = S5_RB // DEC_SEQ
S5_NLAG = 2 * S5_T - 1
S5_NS = 4 * S5_STATE
HC = 128
N_HCHUNK = N_TOK // HC
EXP_CLAMP = 80.0
MOE_BM = 256
N_ASSIGN = N_TOK * TOP_K
SEG_ALIGN = 16
TILE_SLOTS = -(-(TM * TOP_K + N_EXPERTS * (SEG_ALIGN - 1)) // 128) * 128
MOE_BLOCKS = -(-(N_ASSIGN + N_TILES * N_EXPERTS * (SEG_ALIGN - 1) + N_EXPERTS * (MOE_BM - 1)) // MOE_BM)
MOE_GRP = SEG_ALIGN
MOE_TILE_GROUPS = TILE_SLOTS // MOE_GRP
MOE_BLK_GROUPS = MOE_BM // MOE_GRP
MOE_DEPTH = 4
MOE_AHEAD = MOE_DEPTH - 1
MOE_GROUPS = MOE_BLOCKS * MOE_BLK_GROUPS
MOE_ZERO_GROUP = N_TILES * MOE_TILE_GROUPS
MOE_PAD_GROUPS = N_EXPERTS * (MOE_BLK_GROUPS - 1)
MOE_XS_ROWS = (N_TILES + 1) * TILE_SLOTS
MOE_YS_ROWS = N_TILES * TILE_SLOTS + MOE_PAD_GROUPS * MOE_GRP
MOE_BASE_ROWS = TM * TOP_K
MOE_TAIL_ARMS = (256, 128, 64, 32, 16)
VMEM_LIMIT = 56 * 1024 * 1024


def _rms(x, w):
    return x * lax.rsqrt(jnp.mean(x * x, axis=-1, keepdims=True) + NORM_EPS) * w


def _silu(x):
    return x * jax.nn.sigmoid(x)


MOD_TN = 1536


def _mod_kernel(cond_ref, w_ref, b_ref, o_ref):
    s = _silu(cond_ref[...]).astype(BF16)
    o_ref[0] = jnp.dot(s, w_ref[0].astype(BF16), preferred_element_type=F32) + b_ref[0]


def _modulation(cond, ada_w, ada_b):
    return pl.pallas_call(
        _mod_kernel,
        grid=(DEPTH, 6 * D_MODEL // MOD_TN),
        in_specs=[
            pl.BlockSpec((N_COND, D_MODEL), lambda l, j: (0, 0)),
            pl.BlockSpec((1, D_MODEL, MOD_TN), lambda l, j: (l, 0, j)),
            pl.BlockSpec((1, 1, MOD_TN), lambda l, j: (l, 0, j)),
        ],
        out_specs=pl.BlockSpec((1, N_COND, MOD_TN), lambda l, j: (l, 0, j)),
        out_shape=jax.ShapeDtypeStruct((DEPTH, N_COND, 6 * D_MODEL), F32),
        compiler_params=pltpu.CompilerParams(vmem_limit_bytes=VMEM_LIMIT),
        name="adaln_mod",
    )(cond, ada_w, ada_b.reshape(DEPTH, 1, 6 * D_MODEL))


def _in_kernel(layer, cond_ref, x_ref, mod_ref, nw_ref, w_ref, lb_ref,
               u_ref, q_ref, v_ref, g_ref, lff_ref, kf_ref, lfb_ref, kb_ref):
    del cond_ref
    mod = mod_ref[0]
    sh1 = mod[:, 0:D_MODEL]
    sc1 = mod[:, D_MODEL:2 * D_MODEL]
    h = _rms(x_ref[...], nw_ref[0:1, :]) * (1.0 + sc1) + sh1
    z = jnp.dot(h.astype(BF16), w_ref[...], preferred_element_type=F32)
    w = HGRN_WIDTH
    c0 = S5_WIDTH
    u_ref[...] = z[:, 0:c0]
    q_ref[...] = _silu(z[:, c0:c0 + w]) * (HGRN_DK ** -0.5)
    v_ref[...] = z[:, c0 + 3 * w:c0 + 4 * w]
    g_ref[...] = _silu(z[:, c0 + 4 * w:c0 + 5 * w])
    lbp = lb_ref[...]
    e = jnp.exp(lbp - jnp.max(lbp, axis=0, keepdims=True))
    probs = e / jnp.sum(e, axis=0, keepdims=True)
    lb = jnp.sum(probs[0:layer + 1], axis=0) - probs[0]
    for d, (lf_ref, k_ref) in enumerate(((lff_ref, kf_ref), (lfb_ref, kb_ref))):
        f = z[:, c0 + (1 + d) * w:c0 + (2 + d) * w]
        lbd = lb[d:d + 1, :]
        fg = lbd + (1.0 - lbd) * jax.nn.sigmoid(f)
        lf_ref[...] = jnp.log(fg)
        k_ref[...] = 1.0 - fg


def _in_proj(layer, tile_cond, x, mod_l, norm_w_l, w_in_l, hgrn_lb):
    tok = lambda i, c: (i, 0)
    out = jax.ShapeDtypeStruct((N_TOK, HGRN_WIDTH), F32)
    return pl.pallas_call(
        functools.partial(_in_kernel, layer),
        grid_spec=pltpu.PrefetchScalarGridSpec(
            num_scalar_prefetch=1,
            grid=(N_TILES,),
            in_specs=[
                pl.BlockSpec((TM, D_MODEL), tok),
                pl.BlockSpec((1, 1, 6 * D_MODEL), lambda i, c: (c[i], 0, 0)),
                pl.BlockSpec((4, D_MODEL), lambda i, c: (0, 0)),
                pl.BlockSpec((D_MODEL, IN_COLS), lambda i, c: (0, 0)),
                pl.BlockSpec((DEPTH, 2, HGRN_WIDTH), lambda i, c: (0, 0, 0)),
            ],
            out_specs=[pl.BlockSpec((TM, HGRN_WIDTH), tok)] * 8,
        ),
        out_shape=[out] * 8,
        compiler_params=pltpu.CompilerParams(
            dimension_semantics=("arbitrary",), vmem_limit_bytes=VMEM_LIMIT),
        name="in_proj",
    )(tile_cond, x, mod_l, norm_w_l, w_in_l, hgrn_lb)


def _s5_layer_weights(a_re, a_im, log_dt, b_re, b_im, c_re, c_im):
    hp = lax.Precision.HIGHEST
    t = S5_T
    dt = jnp.exp(log_dt)[..., None]
    lam_re = jnp.minimum(a_re, -1e-4)
    lam_im = a_im
    mag = jnp.exp(dt * lam_re)
    ang = dt * lam_im
    ab_re = mag * jnp.cos(ang)
    ab_im = mag * jnp.sin(ang)
    den = lam_re * lam_re + lam_im * lam_im
    nr = ab_re - 1.0
    ni = ab_im
    co_re = (nr * lam_re + ni * lam_im) / den
    co_im = (ni * lam_re - nr * lam_im) / den
    bb_re = co_re[..., None] * b_re - co_im[..., None] * b_im
    bb_im = co_re[..., None] * b_im + co_im[..., None] * b_re
    pr = [jnp.ones_like(ab_re)]
    pi = [jnp.zeros_like(ab_im)]
    for _ in range(t):
        pr.append(pr[-1] * ab_re - pi[-1] * ab_im)
        pi.append(pr[-2] * ab_im + pi[-1] * ab_re)
    pr = jnp.stack(pr, axis=1)
    pi = jnp.stack(pi, axis=1)
    abr = pr[..., None] * bb_re[:, None] - pi[..., None] * bb_im[:, None]
    abi = pr[..., None] * bb_im[:, None] + pi[..., None] * bb_re[:, None]
    kk = jnp.einsum('dgon,dkgni->dkgio', jnp.concatenate([c_re, -c_im], axis=-1),
                    jnp.concatenate([abr, abi], axis=-2), precision=hp)
    kern = (kk[0], kk[1])
    abs_ = ((abr[0], abi[0]), (abr[1], abi[1]))
    pows = ((pr[0], pi[0]), (pr[1], pi[1]))
    p = S5_GROUP_CH
    lags = jnp.concatenate([kern[1][t - 1:0:-1], (kern[0][0] + kern[1][0])[None], kern[0][1:t]], axis=0)
    m = lags.reshape(S5_NLAG, S5_NLT, S5_GPT, p, p)
    bd = jnp.einsum('ltaio,ab->ltaibo', m, jnp.eye(S5_GPT, dtype=F32))
    wcat = bd.reshape(S5_NLAG, S5_NLT, S5_LT, S5_LT).transpose(1, 2, 0, 3).reshape(S5_NLT, S5_LT, S5_NLAG * S5_LT)
    (abr_f, abi_f), (abr_b, abi_b) = abs_
    parts = (abr_f[t - 1::-1], abr_b[:t], abi_f[t - 1::-1], abi_b[:t])
    wst = jnp.concatenate([x.transpose(1, 0, 3, 2) for x in parts], axis=-1)
    wst = wst.reshape(S5_NLT, S5_GPT, t, p, S5_NS).transpose(0, 2, 1, 3, 4).reshape(S5_NLT, t * S5_LT, S5_NS)
    (pr_f, pi_f), (pr_b, pi_b) = pows

    def out_rows(cr, ci, pr_, pi_):
        re_rows = cr[None] * pr_[:, :, None, :] - ci[None] * pi_[:, :, None, :]
        im_rows = -(cr[None] * pi_[:, :, None, :] + ci[None] * pr_[:, :, None, :])
        return re_rows.transpose(1, 3, 0, 2), im_rows.transpose(1, 3, 0, 2)

    fre, fim = out_rows(c_re[0], c_im[0], pr_f[1:t + 1], pi_f[1:t + 1])
    bre, bim = out_rows(c_re[1], c_im[1], pr_b[t:0:-1], pi_b[t:0:-1])
    wout = lax.optimization_barrier(jnp.concatenate([fre, bre, fim, bim], axis=1))
    wdense = (wout.reshape(S5_NLT, S5_GPT, S5_NS, t, p).transpose(0, 2, 3, 1, 4)
              .reshape(S5_NLT, S5_NS, t * S5_LT))
    at = jnp.stack([jnp.concatenate([pr_f[t], pr_b[t]], axis=-1),
                    jnp.concatenate([pi_f[t], pi_b[t]], axis=-1)], axis=1)
    return (wcat.astype(BF16), wst.astype(BF16), wdense.astype(BF16),
            at.reshape(S5_NLT, S5_GPT, 2, 2 * S5_STATE))


def _s5_weights(*params):
    return jax.vmap(_s5_layer_weights)(*params)


def _s5_kernel(u_ref, wcat_ref, wst_ref, wd_ref, at_ref, h0_ref, y_ref, hfin_ref,
               wbig, dh_scr, hf_scr, hb_scr):
    t = S5_T
    n2 = 2 * S5_STATE
    r = pl.program_id(1)

    @pl.when(r == 0)
    def _():
        for s in range(t):
            wbig[s * S5_LT:(s + 1) * S5_LT, :] = wcat_ref[0, :, (t - 1 - s) * S5_LT:(2 * t - 1 - s) * S5_LT]

    xcat = jnp.concatenate([u_ref[pl.ds(s, S5_CR, stride=t), :].astype(BF16) for s in range(t)], axis=-1)
    yacc = jnp.dot(xcat, wbig[...], preferred_element_type=F32)
    lane_k = lax.broadcasted_iota(jnp.int32, (1, t * S5_LT), 1)
    grp_k = (lane_k % S5_LT) // S5_GROUP_CH
    for gi in range(S5_GPT):
        xg = jnp.where(grp_k == gi, xcat, jnp.zeros_like(xcat))
        dh = jnp.dot(xg, wst_ref[0], preferred_element_type=F32)
        dh_scr[gi, 0] = dh[:, 0:n2]
        dh_scr[gi, 1] = dh[:, n2:]

    lane = lax.broadcasted_iota(jnp.int32, (1, n2), 1)
    fwd_lane = lane < S5_STATE
    are = [at_ref[0, gi, 0:1, :] for gi in range(S5_GPT)]
    aim = [at_ref[0, gi, 1:2, :] for gi in range(S5_GPT)]

    def advance(gi, hre, him, dre, dim):
        return (are[gi] * hre - aim[gi] * him + dre, are[gi] * him + aim[gi] * hre + dim)

    @pl.when(r == 0)
    def _():
        nc = SEQ // t
        for gi in range(S5_GPT):
            hre = jnp.zeros((BATCH, n2), F32)
            him = jnp.zeros((BATCH, n2), F32)
            for s in range(nc):
                rows_f = pl.ds(s, BATCH, stride=nc)
                rows_b = pl.ds(nc - 1 - s, BATCH, stride=nc)
                hf_scr[gi, 0, rows_f, :] = hre
                hf_scr[gi, 1, rows_f, :] = him
                hb_scr[gi, 0, rows_b, :] = hre
                hb_scr[gi, 1, rows_b, :] = him
                dre = jnp.where(fwd_lane, dh_scr[gi, 0, rows_f, :], dh_scr[gi, 0, rows_b, :])
                dim = jnp.where(fwd_lane, dh_scr[gi, 1, rows_f, :], dh_scr[gi, 1, rows_b, :])
                hre, him = advance(gi, hre, him, dre, dim)
            hfin_ref[0, gi] = jnp.concatenate([hre, him], axis=-1)

    @pl.when(r > 0)
    def _():
        nc = DEC_SEQ // t
        nb = S5_SEQ_PER_RB
        init = tuple((h0_ref[0, 0, gi, b:b + 1, 0:n2], h0_ref[0, 0, gi, b:b + 1, n2:])
                     for gi in range(S5_GPT) for b in range(nb))

        def step(o, carry):
            new = []
            for gi in range(S5_GPT):
                for b in range(nb):
                    hre, him = carry[gi * nb + b]
                    rf = pl.multiple_of(b * nc + o * 8, 8)
                    rb = pl.multiple_of(b * nc + nc - 8 - o * 8, 8)
                    dfr = dh_scr[gi, 0, pl.ds(rf, 8), :]
                    dfi = dh_scr[gi, 1, pl.ds(rf, 8), :]
                    dbr = dh_scr[gi, 0, pl.ds(rb, 8), :]
                    dbi = dh_scr[gi, 1, pl.ds(rb, 8), :]
                    ent_re, ent_im = [], []
                    for i in range(8):
                        ent_re.append(hre)
                        ent_im.append(him)
                        dre = jnp.where(fwd_lane, dfr[i:i + 1], dbr[7 - i:8 - i])
                        dim = jnp.where(fwd_lane, dfi[i:i + 1], dbi[7 - i:8 - i])
                        hre, him = advance(gi, hre, him, dre, dim)
                    hf_scr[gi, 0, pl.ds(rf, 8), :] = jnp.concatenate(ent_re, axis=0)
                    hf_scr[gi, 1, pl.ds(rf, 8), :] = jnp.concatenate(ent_im, axis=0)
                    hb_scr[gi, 0, pl.ds(rb, 8), :] = jnp.concatenate(ent_re[::-1], axis=0)
                    hb_scr[gi, 1, pl.ds(rb, 8), :] = jnp.concatenate(ent_im[::-1], axis=0)
                    new.append((hre, him))
            return tuple(new)

        lax.fori_loop(0, nc // 8, step, init)

    for gi in range(S5_GPT):
        hent = jnp.concatenate([jnp.where(fwd_lane, hf_scr[gi, 0], hb_scr[gi, 0]),
                                jnp.where(fwd_lane, hf_scr[gi, 1], hb_scr[gi, 1])], axis=-1).astype(BF16)
        yi = jnp.dot(hent, wd_ref[0], preferred_element_type=F32)
        yacc = yacc + jnp.where(grp_k == gi, yi, 0.0)
    for s in range(t):
        y_ref[pl.ds(s, S5_CR, stride=t), :] = yacc[:, s * S5_LT:(s + 1) * S5_LT]


def _s5_scan(layer, u, wcat, wst, wdense, at, h0):
    tile = lambda j, r: (layer * S5_NLT + j, 0, 0)
    merge = lambda w: w.reshape((DEPTH * S5_NLT,) + w.shape[2:])
    wcat, wst, wdense, at = merge(wcat), merge(wst), merge(wdense), merge(at)
    return pl.pallas_call(
        _s5_kernel,
        grid=(S5_NLT, S5_NRB),
        in_specs=[
            pl.BlockSpec((S5_RB, S5_LT), lambda j, r: (r, j)),
            pl.BlockSpec((1, S5_LT, S5_NLAG * S5_LT), tile),
            pl.BlockSpec((1, S5_T * S5_LT, S5_NS), tile),
            pl.BlockSpec((1, S5_NS, S5_T * S5_LT), tile),
            pl.BlockSpec((1, S5_GPT, 2, 2 * S5_STATE), lambda j, r: (layer * S5_NLT + j, 0, 0, 0)),
            pl.BlockSpec((1, 1, S5_GPT, S5_SEQ_PER_RB, S5_NS), lambda j, r: (j, jnp.maximum(r - 1, 0), 0, 0, 0)),
        ],
        out_specs=[
            pl.BlockSpec((S5_RB, S5_LT), lambda j, r: (r, j)),
            pl.BlockSpec((1, S5_GPT, BATCH, S5_NS), lambda j, r: (j, 0, 0, 0)),
        ],
        out_shape=[
            jax.ShapeDtypeStruct((N_TOK, S5_WIDTH), F32),
            jax.ShapeDtypeStruct((S5_NLT, S5_GPT, BATCH, S5_NS), F32),
        ],
        scratch_shapes=[pltpu.VMEM((S5_T * S5_LT, S5_T * S5_LT), BF16)]
                       + [pltpu.VMEM((S5_GPT, 2, S5_CR, 2 * S5_STATE), F32)] * 3,
        compiler_params=pltpu.CompilerParams(
            dimension_semantics=("arbitrary", "arbitrary"), vmem_limit_bytes=VMEM_LIMIT),
        name="s5_scan",
    )(u, wcat, wst, wdense, at, h0)


def _split3(x):
    hi = x.astype(BF16)
    r1 = x - hi.astype(F32)
    mid = r1.astype(BF16)
    lo = (r1 - mid.astype(F32)).astype(BF16)
    return hi, mid, lo


def _piecewise_rows(b, blk, row_in_blk):
    parts = []
    for j in range(HC // blk):
        r = j * blk + row_in_blk
        parts.append(jnp.broadcast_to(b[r:r + 1, :], (blk, b.shape[1])))
    return parts[0] if len(parts) == 1 else jnp.concatenate(parts, axis=0)


def _nt(a, b):
    return lax.dot_general(a, b, (((1,), (1,)), ((), ())), preferred_element_type=F32)


def _tn(a, b):
    return lax.dot_general(a, b, (((0,), (0,)), ((), ())), preferred_element_type=F32)


def _hgrn_dir(reverse, q_ref, v_ref, lf_ref, k_ref, o_ref, st_ref, b_scr):
    row = lax.broadcasted_iota(jnp.int32, (HC, HC), 0)
    col = lax.broadcasted_iota(jnp.int32, (HC, HC), 1)
    causal = (col >= row) if reverse else (col <= row)
    tri = jnp.where(causal, 1.0, 0.0).astype(BF16)
    lf = lf_ref[...]
    hi, mid, lo = _split3(lf)
    ball = (jnp.dot(tri, hi, preferred_element_type=F32) + jnp.dot(tri, mid, preferred_element_type=F32)
            + jnp.dot(tri, lo, preferred_element_type=F32))
    last = 0 if reverse else HC - 1
    masks = []
    for blk in (128, 64, 32):
        half = blk // 2
        same = (row // blk) == (col // blk)
        t_hi = (row % blk) >= half
        s_hi = (col % blk) >= half
        if reverse:
            masks.append(same & jnp.logical_not(t_hi) & s_hi)
        else:
            masks.append(same & t_hi & jnp.logical_not(s_hi))
    b_scr[...] = ball
    mrow = 8 if reverse else 7
    mid = _piecewise_rows(ball, 16, mrow)
    spread = jnp.zeros((1, HGRN_WIDTH), F32)
    for j in range(HC // 16):
        for edge in (16 * j, 16 * j + 15):
            spread = jnp.maximum(spread, jnp.abs(ball[edge:edge + 1, :] - ball[16 * j + mrow:16 * j + mrow + 1, :]))
    stable = jnp.max(spread) <= EXP_CLAMP
    diag_ok = ((row // 16) == (col // 16)) & causal & stable
    for h in range(HGRN_HEADS):
        sl = slice(h * HGRN_DK, (h + 1) * HGRN_DK)
        b = ball[:, sl]
        q = q_ref[:, sl]
        k = k_ref[:, sl]
        v = v_ref[:, sl].astype(BF16)
        st = st_ref[h]
        b_last = b[last:last + 1, :]
        q_in = (q * jnp.exp(b)).astype(BF16)
        k_in = (k * jnp.exp(b_last - b)).astype(BF16)
        o = _nt(q_in, st.astype(BF16))
        st_ref[h] = jnp.exp(b_last) * st + _tn(v, k_in)
        scores = jnp.zeros((HC, HC), F32)
        for blk, mask in zip((128, 64, 32), masks):
            half = blk // 2
            zero = jnp.zeros((half, HGRN_DK), BF16)
            qparts, kparts = [], []
            for j in range(HC // blk):
                early = slice(j * blk, j * blk + half)
                late = slice(j * blk + half, (j + 1) * blk)
                if reverse:
                    m = b[j * blk + half:j * blk + half + 1, :]
                    qrows, krows = early, late
                else:
                    m = b[j * blk + half - 1:j * blk + half, :]
                    qrows, krows = late, early
                qe = (q[qrows] * jnp.exp(b[qrows] - m)).astype(BF16)
                ke = (k[krows] * jnp.exp(m - b[krows])).astype(BF16)
                qparts += [qe, zero] if reverse else [zero, qe]
                kparts += [zero, ke] if reverse else [ke, zero]
            part = _nt(jnp.concatenate(qparts, axis=0), jnp.concatenate(kparts, axis=0))
            scores = scores + (part if blk == HC else jnp.where(mask, part, 0.0))
        m = mid[:, sl]
        qd = (q * jnp.exp(jnp.minimum(b - m, EXP_CLAMP))).astype(BF16)
        kd = (k * jnp.exp(jnp.minimum(m - b, EXP_CLAMP))).astype(BF16)
        scores = scores + jnp.where(diag_ok, _nt(qd, kd), 0.0)
        o_ref[:, sl] = o + jnp.dot(scores.astype(BF16), v, preferred_element_type=F32)

    return stable


def _hgrn_exact_diagonal(reverse, stable, q_ref, v_ref, k_ref, o_ref, b_scr):
    @pl.when(jnp.logical_not(stable))
    def _():
        pos = lax.broadcasted_iota(jnp.int32, (HC, 1), 0) % 16

        def lag(d, c):
            shift = (HC - d) % HC if reverse else d
            valid = (pos + d <= 15) if reverse else (pos >= d)
            for h in range(HGRN_HEADS):
                sl = slice(h * HGRN_DK, (h + 1) * HGRN_DK)
                b = b_scr[:, sl]
                bs = pltpu.roll(b, shift, 0)
                ks = pltpu.roll(k_ref[:, sl], shift, 0)
                vs = pltpu.roll(v_ref[:, sl], shift, 0)
                e = jnp.exp(jnp.where(valid, b - bs, 0.0))
                w = jnp.sum(q_ref[:, sl] * ks * e, axis=-1, keepdims=True)
                o_ref[:, sl] = o_ref[:, sl] + jnp.where(valid, w, 0.0) * vs
            return c

        lax.fori_loop(0, 16, lag, 0)


def _hgrn_kernel(cf_ref, cb_ref, seq_ref, first_ref, last_ref,
                 qf_ref, vf_ref, lff_ref, kf_ref, qb_ref, vb_ref, lfb_ref, kb_ref, s0_ref,
                 of_ref, ob_ref, sfin_ref, st_scr, b_scr):
    del cf_ref, cb_ref, seq_ref
    j = pl.program_id(0)

    @pl.when(first_ref[j] == 1)
    def _():
        st_scr[...] = s0_ref[0]

    ok_f = _hgrn_dir(False, qf_ref, vf_ref, lff_ref, kf_ref, of_ref, st_scr.at[0], b_scr.at[0])
    ok_b = _hgrn_dir(True, qb_ref, vb_ref, lfb_ref, kb_ref, ob_ref, st_scr.at[1], b_scr.at[1])
    _hgrn_exact_diagonal(False, ok_f, qf_ref, vf_ref, kf_ref, of_ref, b_scr.at[0])
    _hgrn_exact_diagonal(True, ok_b, qb_ref, vb_ref, kb_ref, ob_ref, b_scr.at[1])

    @pl.when(last_ref[j] == 1)
    def _():
        sfin_ref[0] = st_scr[...]


def _hgrn_tables():
    cf, cb, sq, first, last = [], [], [], [], []
    base = 0
    for s in range(N_SEQ):
        nc = (SEQ if s < BATCH else DEC_SEQ) // HC
        for t in range(nc):
            cf.append(base + t)
            cb.append(base + nc - 1 - t)
            sq.append(s)
            first.append(int(t == 0))
            last.append(int(t == nc - 1))
        base += nc
    return tuple(jnp.asarray(x, jnp.int32) for x in (cf, cb, sq, first, last))


def _hgrn_scan(qs, v, lf_f, k_f, lf_b, k_b, s0):
    fwd = lambda j, cf, cb, sq, fi, la: (cf[j], 0)
    bwd = lambda j, cf, cb, sq, fi, la: (cb[j], 0)
    seq = lambda j, cf, cb, sq, fi, la: (sq[j], 0, 0, 0, 0)
    tile = (HC, HGRN_WIDTH)
    sblk = (1, 2, HGRN_HEADS, HGRN_DV, HGRN_DK)
    out = jax.ShapeDtypeStruct((N_TOK, HGRN_WIDTH), F32)
    return pl.pallas_call(
        _hgrn_kernel,
        grid_spec=pltpu.PrefetchScalarGridSpec(
            num_scalar_prefetch=5,
            grid=(N_HCHUNK,),
            in_specs=[pl.BlockSpec(tile, fwd)] * 4 + [pl.BlockSpec(tile, bwd)] * 4
                     + [pl.BlockSpec(sblk, seq)],
            out_specs=[pl.BlockSpec(tile, fwd), pl.BlockSpec(tile, bwd), pl.BlockSpec(sblk, seq)],
            scratch_shapes=[pltpu.VMEM(sblk[1:], F32), pltpu.VMEM((2, HC, HGRN_WIDTH), F32)],
        ),
        out_shape=[out, out, jax.ShapeDtypeStruct((N_SEQ,) + sblk[1:], F32)],
        compiler_params=pltpu.CompilerParams(
            dimension_semantics=("arbitrary",), vmem_limit_bytes=VMEM_LIMIT),
        name="hgrn_scan",
    )(*_hgrn_tables(), qs, v, lf_f, k_f, qs, v, lf_b, k_b, s0)


def _gelu_tanh(x):
    return 0.5 * x * (1.0 + jnp.tanh(math.sqrt(2.0 / math.pi) * (x + 0.044715 * (x * x * x))))


def _out_kernel(cond_ref, x_ref, yc_ref, u_ref, of_ref, ob_ref, g_ref, mod_ref, nw_ref, d_ref,
                wglu_ref, bglu_ref, hnw_ref, wout_ref, rw_ref, rb_ref,
                x1_ref, h2_ref, ri_ref, rg_ref, cnt_ref):
    del cond_ref
    mod = mod_ref[0]
    g1 = mod[:, 2 * D_MODEL:3 * D_MODEL]
    sh2 = mod[:, 3 * D_MODEL:4 * D_MODEL]
    sc2 = mod[:, 4 * D_MODEL:5 * D_MODEL]
    y = _gelu_tanh(yc_ref[...] + d_ref[...] * u_ref[...])
    y_s5 = y * jax.nn.sigmoid(jnp.dot(y.astype(BF16), wglu_ref[...], preferred_element_type=F32) + bglu_ref[...])
    o = of_ref[...] + ob_ref[...]
    gs = g_ref[...]
    heads = []
    for h in range(HGRN_HEADS):
        sl = slice(h * HGRN_DV, (h + 1) * HGRN_DV)
        heads.append(_rms(o[:, sl], hnw_ref[...]) * gs[:, sl])
    y_hg = jnp.concatenate(heads, axis=-1)
    mix = (jnp.dot(y_s5.astype(BF16), wout_ref[0:S5_WIDTH, :], preferred_element_type=F32)
           + jnp.dot(y_hg.astype(BF16), wout_ref[S5_WIDTH:, :], preferred_element_type=F32))
    x1 = x_ref[...] + g1 * _rms(mix, nw_ref[1:2, :])
    x1_ref[...] = x1
    h2 = _rms(x1, nw_ref[2:3, :]) * (1.0 + sc2) + sh2
    h2_hi = h2.astype(BF16)
    h2_ref[...] = h2_hi
    h2_lo = (h2 - h2_hi.astype(F32)).astype(BF16)
    logits = rb_ref[...]
    for a in (h2_hi, h2_lo):
        for part in range(2):
            logits = logits + jnp.dot(a, rw_ref[part], preferred_element_type=F32)
    eidx = lax.broadcasted_iota(jnp.int32, (TM, N_EXPERTS), 1).astype(F32)
    vals = logits
    top_v, top_i, onehots = [], [], []
    for _ in range(TOP_K):
        mx = jnp.max(vals, axis=-1, keepdims=True)
        ix = jnp.min(jnp.where(vals == mx, eidx, float(N_EXPERTS)), axis=-1, keepdims=True)
        sel = eidx == ix
        top_v.append(mx)
        top_i.append(ix)
        onehots.append(sel)
        vals = jnp.where(sel, -jnp.inf, vals)
    ex = [jnp.exp(tv - top_v[0]) for tv in top_v]
    den = ex[0] + ex[1] + ex[2] + ex[3]
    tot = jnp.zeros((TM, N_EXPERTS), F32)
    for sel in onehots:
        tot = tot + jnp.where(sel, 1.0, 0.0)
    r_t = lax.broadcasted_iota(jnp.int32, (TM, TM), 0)
    r_s = lax.broadcasted_iota(jnp.int32, (TM, TM), 1)
    strict = jnp.where(r_s < r_t, 1.0, 0.0).astype(BF16)
    before = jnp.dot(strict, tot.astype(BF16), preferred_element_type=F32)
    cnt = jnp.sum(tot, axis=0, keepdims=True)
    seg = jnp.floor((cnt + (SEG_ALIGN - 1)) * (1.0 / SEG_ALIGN)) * SEG_ALIGN
    e_r = lax.broadcasted_iota(jnp.int32, (N_EXPERTS, N_EXPERTS), 0)
    e_c = lax.broadcasted_iota(jnp.int32, (N_EXPERTS, N_EXPERTS), 1)
    lstart = jnp.dot(seg, jnp.where(e_r < e_c, 1.0, 0.0), precision=lax.Precision.HIGHEST,
                     preferred_element_type=F32)
    before = before + lstart
    lane = lax.broadcasted_iota(jnp.int32, (TM, 128), 1)
    ri = jnp.zeros((TM, 128), F32)
    rg = jnp.zeros((TM, 128), F32)
    for kk in range(TOP_K):
        rank = jnp.sum(jnp.where(onehots[kk], before, 0.0), axis=-1, keepdims=True)
        ri = jnp.where(lane == kk, top_i[kk], ri)
        ri = jnp.where(lane == TOP_K + kk, rank, ri)
        rg = jnp.where(lane == kk, ex[kk] / den, rg)
    ri_ref[...] = ri.astype(jnp.int32)
    rg_ref[...] = rg
    cnt_ref[0] = cnt.astype(jnp.int32)


def _out_proj(tile_cond, x, ycore, u, o_f, o_b, gs, mod_l, norm_w_l, s5_d_l, wglu_l, bglu_l, hnw_l,
              wout_l, rw_l, rb_l):
    tok = lambda i, c: (i, 0)
    full2 = lambda i, c: (0, 0)
    half = pl.BlockSpec((TM, HGRN_WIDTH), tok)
    wide = pl.BlockSpec((TM, D_MODEL), tok)
    return pl.pallas_call(
        _out_kernel,
        grid_spec=pltpu.PrefetchScalarGridSpec(
            num_scalar_prefetch=1,
            grid=(N_TILES,),
            in_specs=[
                wide, half, half, half, half, half,
                pl.BlockSpec((1, 1, 6 * D_MODEL), lambda i, c: (c[i], 0, 0)),
                pl.BlockSpec((4, D_MODEL), full2),
                pl.BlockSpec((1, S5_WIDTH), full2),
                pl.BlockSpec((S5_WIDTH, S5_WIDTH), full2),
                pl.BlockSpec((1, S5_WIDTH), full2),
                pl.BlockSpec((1, HGRN_DV), full2),
                pl.BlockSpec((D_MODEL, D_MODEL), full2),
                pl.BlockSpec((2, D_MODEL, N_EXPERTS), lambda i, c: (0, 0, 0)),
                pl.BlockSpec((1, N_EXPERTS), full2),
            ],
            out_specs=[wide, wide, pl.BlockSpec((TM, 128), tok), pl.BlockSpec((TM, 128), tok),
                       pl.BlockSpec((1, 1, N_EXPERTS), lambda i, c: (i, 0, 0))],
        ),
        out_shape=[
            jax.ShapeDtypeStruct((N_TOK, D_MODEL), F32),
            jax.ShapeDtypeStruct((N_TOK, D_MODEL), BF16),
            jax.ShapeDtypeStruct((N_TOK, 128), jnp.int32),
            jax.ShapeDtypeStruct((N_TOK, 128), F32),
            jax.ShapeDtypeStruct((N_TILES, 1, N_EXPERTS), jnp.int32),
        ],
        compiler_params=pltpu.CompilerParams(
            dimension_semantics=("arbitrary",), vmem_limit_bytes=VMEM_LIMIT),
        name="out_proj_router",
    )(tile_cond, x, ycore, u, o_f, o_b, gs, mod_l, norm_w_l, s5_d_l, wglu_l, bglu_l, hnw_l,
      wout_l, rw_l, rb_l)


def _dispatch_kernel(slot_ref, h_ref, o_ref):
    srow = lax.broadcasted_iota(jnp.int32, (TILE_SLOTS, TM), 0)
    perm = jnp.zeros((TILE_SLOTS, TM), F32)
    for kk in range(TOP_K):
        perm = perm + jnp.where(srow == slot_ref[0, kk:kk + 1, :], 1.0, 0.0)
    o_ref[...] = jnp.dot(perm.astype(BF16), h_ref[...], preferred_element_type=F32).astype(BF16)


def _dispatch(slot_t, h2):
    return pl.pallas_call(
        _dispatch_kernel,
        grid=(N_TILES + 1,),
        in_specs=[pl.BlockSpec((1, TOP_K, TM), lambda i: (i, 0, 0)),
                  pl.BlockSpec((TM, D_MODEL), lambda i: (jnp.minimum(i, N_TILES - 1), 0))],
        out_specs=pl.BlockSpec((TILE_SLOTS, D_MODEL), lambda i: (i, 0)),
        out_shape=jax.ShapeDtypeStruct((MOE_XS_ROWS, D_MODEL), BF16),
        compiler_params=pltpu.CompilerParams(
            dimension_semantics=("arbitrary",), vmem_limit_bytes=VMEM_LIMIT),
        name="moe_dispatch",
    )(slot_t, h2)


def _moe_kernel(nblk_ref, gbase_ref, edge_ref, gsrc_ref, gdst_ref, w1_ref, b1_ref, w2_ref, b2_ref, xs_ref, ys_ref,
                xbuf, obuf, w1b, w2b, sem_in, sem_out):
    e = pl.program_id(0)
    nb = nblk_ref[e]
    k0 = gbase_ref[e] // MOE_BLK_GROUPS
    first_active = (edge_ref[e] & 1) != 0
    last_active = (edge_ref[e] & 2) != 0

    def gather(k):
        buf = k % MOE_DEPTH
        return [pltpu.make_async_copy(
            xs_ref.at[pl.ds(pl.multiple_of(gsrc_ref[k * MOE_BLK_GROUPS + g] * MOE_GRP, MOE_GRP), MOE_GRP)],
            xbuf.at[buf, pl.ds(g * MOE_GRP, MOE_GRP)], sem_in.at[buf]) for g in range(MOE_BLK_GROUPS)]

    def scatter(k):
        buf = k % MOE_DEPTH
        return [pltpu.make_async_copy(
            obuf.at[buf, pl.ds(g * MOE_GRP, MOE_GRP)],
            ys_ref.at[pl.ds(pl.multiple_of(gdst_ref[k * MOE_BLK_GROUPS + g] * MOE_GRP, MOE_GRP), MOE_GRP)],
            sem_out.at[buf]) for g in range(MOE_BLK_GROUPS)]

    @pl.when(nb > 0)
    def _():
        def cast(j, c):
            r = pl.multiple_of(j * 128, 128)
            w1b[pl.ds(r, 128), :] = w1_ref[0, pl.ds(r, 128), :].astype(BF16)
            w2b[pl.ds(r, 128), :] = w2_ref[0, pl.ds(r, 128), :].astype(BF16)
            return c

        @pl.when(first_active)
        def _():
            for a in range(MOE_AHEAD):
                for cp in gather(k0 + a):
                    cp.start(priority=1)

        lax.fori_loop(0, D_MODEL // 128, cast, 0)

        def block(b, c):
            k = k0 + b
            buf = k % MOE_DEPTH

            @pl.when(k >= MOE_DEPTH)
            def _():
                for cp in scatter(k - MOE_DEPTH):
                    cp.wait()

            for cp in gather(k):
                cp.wait()
            for cp in gather(k + MOE_AHEAD):
                cp.start(priority=1)
            h = jnp.dot(xbuf[buf], w1b[...], preferred_element_type=F32) + b1_ref[0]
            glu = jnp.minimum(h[:, :D_FF], SWIGLU_LIMIT)
            lin = jnp.clip(h[:, D_FF:], -SWIGLU_LIMIT, SWIGLU_LIMIT)
            act = glu * jax.nn.sigmoid(SWIGLU_ALPHA * glu) * (lin + 1.0)
            obuf[buf] = (jnp.dot(act.astype(BF16), w2b[...], preferred_element_type=F32) + b2_ref[0]).astype(BF16)
            for cp in scatter(k):
                cp.start(priority=1)
            return c

        lax.fori_loop(0, nb, block, 0)

        @pl.when(last_active)
        def _():
            end = k0 + nb
            for a in range(MOE_AHEAD):
                for cp in gather(end + a):
                    cp.wait()
            for a in range(MOE_DEPTH, 0, -1):
                @pl.when(end - a >= 0)
                def _():
                    for cp in scatter(end - a):
                        cp.wait()


def _moe_experts(layer, nblk, gbase, edge, gsrc, gdst, xs, w1, b1, w2, b2):
    exp3 = lambda e, *_: (layer * N_EXPERTS + e, 0, 0)
    w1 = w1.reshape(DEPTH * N_EXPERTS, D_MODEL, 2 * D_FF)
    w2 = w2.reshape(DEPTH * N_EXPERTS, D_FF, D_MODEL)
    return pl.pallas_call(
        _moe_kernel,
        grid_spec=pltpu.PrefetchScalarGridSpec(
            num_scalar_prefetch=5,
            grid=(N_EXPERTS,),
            in_specs=[
                pl.BlockSpec((1, D_MODEL, 2 * D_FF), exp3),
                pl.BlockSpec((1, 1, 2 * D_FF), exp3),
                pl.BlockSpec((1, D_FF, D_MODEL), exp3),
                pl.BlockSpec((1, 1, D_MODEL), exp3),
                pl.BlockSpec(memory_space=pl.ANY),
            ],
            out_specs=pl.BlockSpec(memory_space=pl.ANY),
            scratch_shapes=[pltpu.VMEM((MOE_DEPTH, MOE_BM, D_MODEL), BF16), pltpu.VMEM((MOE_DEPTH, MOE_BM, D_MODEL), BF16),
                            pltpu.VMEM((D_MODEL, 2 * D_FF), BF16), pltpu.VMEM((D_FF, D_MODEL), BF16),
                            pltpu.SemaphoreType.DMA((MOE_DEPTH,)), pltpu.SemaphoreType.DMA((MOE_DEPTH,))],
        ),
        out_shape=jax.ShapeDtypeStruct((MOE_YS_ROWS, D_MODEL), BF16),
        compiler_params=pltpu.CompilerParams(
            dimension_semantics=("arbitrary",), vmem_limit_bytes=VMEM_LIMIT),
        name="moe_experts",
    )(nblk, gbase, edge, gsrc, gdst, w1, b1.reshape(DEPTH * N_EXPERTS, 1, 2 * D_FF), w2,
      b2.reshape(DEPTH * N_EXPERTS, 1, D_MODEL), xs)


def _combine_kernel(split, cond_ref, tg_ref, x1_ref, slot_ref, gate_ref, mod_ref, nw_ref, ys_ref, *rest):
    del cond_ref
    outs, (buf, sem) = rest[:-2], rest[-2:]
    i = pl.program_id(0)
    cur = i % 2

    def fetch(tile, b, wait):
        def go(cp):
            cp.wait() if wait else cp.start()

        row0 = pl.multiple_of(tile * TILE_SLOTS, TILE_SLOTS)
        go(pltpu.make_async_copy(ys_ref.at[pl.ds(row0, MOE_BASE_ROWS)], buf.at[b, pl.ds(0, MOE_BASE_ROWS)],
                                 sem.at[b]))
        extra = tg_ref[tile] * MOE_GRP - MOE_BASE_ROWS
        for arm in MOE_TAIL_ARMS:
            @pl.when((extra & arm) != 0)
            def _():
                off = pl.multiple_of(MOE_BASE_ROWS + (extra & ~(2 * arm - 1)), MOE_GRP)
                go(pltpu.make_async_copy(ys_ref.at[pl.ds(row0 + off, arm)], buf.at[b, pl.ds(off, arm)], sem.at[b]))

    @pl.when(i == 0)
    def _():
        buf[...] = jnp.zeros_like(buf)
        fetch(0, 0, False)

    @pl.when(i + 1 < N_TILES)
    def _():
        fetch(i + 1, 1 - cur, False)

    fetch(i, cur, True)
    scol = lax.broadcasted_iota(jnp.int32, (TM, TILE_SLOTS), 1)
    slot = slot_ref[...]
    gate = gate_ref[...]
    gmat = jnp.zeros((TM, TILE_SLOTS), F32)
    for kk in range(TOP_K):
        gmat = gmat + jnp.where(scol == slot[:, kk:kk + 1], gate[:, kk:kk + 1], 0.0)
    gmat = gmat.astype(BF16)
    ffn = jnp.dot(gmat, buf[cur], preferred_element_type=F32)
    g2 = mod_ref[0][:, 5 * D_MODEL:6 * D_MODEL]
    out = x1_ref[...] + g2 * _rms(ffn, nw_ref[3:4, :])
    if split:
        @pl.when(i < N_PROMPT // TM)
        def _():
            outs[0][...] = out

        @pl.when(i >= N_PROMPT // TM)
        def _():
            outs[1][...] = out
    else:
        outs[0][...] = out


def _combine(split, tile_cond, tile_groups, x1, slot, gate, mod_l, norm_w_l, ys):
    tok = lambda i, *_: (i, 0)
    if split:
        np_tiles = N_PROMPT // TM
        out_specs = [pl.BlockSpec((TM, D_MODEL), lambda i, *_: (jnp.minimum(i, np_tiles - 1), 0)),
                     pl.BlockSpec((TM, D_MODEL), lambda i, *_: (jnp.maximum(i - np_tiles, 0), 0))]
        out_shape = [jax.ShapeDtypeStruct((N_PROMPT, D_MODEL), F32), jax.ShapeDtypeStruct((N_SAMPLE, D_MODEL), F32)]
    else:
        out_specs = pl.BlockSpec((TM, D_MODEL), tok)
        out_shape = jax.ShapeDtypeStruct((N_TOK, D_MODEL), F32)
    return pl.pallas_call(
        functools.partial(_combine_kernel, split),
        grid_spec=pltpu.PrefetchScalarGridSpec(
            num_scalar_prefetch=2,
            grid=(N_TILES,),
            in_specs=[
                pl.BlockSpec((TM, D_MODEL), tok),
                pl.BlockSpec((TM, TOP_K), tok),
                pl.BlockSpec((TM, 128), tok),
                pl.BlockSpec((1, 1, 6 * D_MODEL), lambda i, c, *_: (c[i], 0, 0)),
                pl.BlockSpec((4, D_MODEL), lambda i, *_: (0, 0)),
                pl.BlockSpec(memory_space=pl.ANY),
            ],
            out_specs=out_specs,
            scratch_shapes=[pltpu.VMEM((2, TILE_SLOTS, D_MODEL), BF16), pltpu.SemaphoreType.DMA((2,))],
        ),
        out_shape=out_shape,
        compiler_params=pltpu.CompilerParams(
            dimension_semantics=("arbitrary",), vmem_limit_bytes=VMEM_LIMIT),
        name="moe_combine",
    )(tile_cond, tile_groups, x1, slot, gate, mod_l, norm_w_l, ys)


def _pos_embed_2d(rows, dim):
    r = jnp.repeat(jnp.arange(rows, dtype=F32), GRID_W)
    col = jnp.tile(jnp.arange(GRID_W, dtype=F32), rows)
    quarter = dim // 4
    omega = 1.0 / (POS_BASE ** (jnp.arange(quarter, dtype=F32) / quarter))

    def emb(pos):
        ang = pos[:, None] * omega[None, :]
        return jnp.concatenate([jnp.sin(ang), jnp.cos(ang)], axis=-1)

    return jnp.concatenate([emb(r), emb(col)], axis=-1)


def _assemble_kernel(xp_ref, xs_ref, pos_ref, o_ref):
    i = pl.program_id(0)

    @pl.when(i < N_PROMPT // TM)
    def _():
        o_ref[...] = xp_ref[...]

    @pl.when(i >= N_PROMPT // TM)
    def _():
        o_ref[...] = xs_ref[...] + pos_ref[...]


def _assemble(xp, xs, pos):
    np_tiles = N_PROMPT // TM
    return pl.pallas_call(
        _assemble_kernel,
        grid=(N_TILES,),
        in_specs=[pl.BlockSpec((TM, D_MODEL), lambda i: (jnp.minimum(i, np_tiles - 1), 0)),
                  pl.BlockSpec((TM, D_MODEL), lambda i: (jnp.maximum(i - np_tiles, 0), 0)),
                  pl.BlockSpec((TM, D_MODEL), lambda i: (jnp.maximum(i - np_tiles, 0) % (DEC_SEQ // TM), 0))],
        out_specs=pl.BlockSpec((TM, D_MODEL), lambda i: (i, 0)),
        out_shape=jax.ShapeDtypeStruct((N_TOK, D_MODEL), F32),
        compiler_params=pltpu.CompilerParams(dimension_semantics=("arbitrary",)),
        name="assemble_tokens",
    )(xp, xs, pos)


def _routing_tables(ri, counts):
    i32 = jnp.int32
    slot = ri[:, TOP_K:2 * TOP_K].reshape(N_TILES, TM, TOP_K)
    cnt = counts.reshape(N_TILES, N_EXPERTS)
    seg = (cnt + SEG_ALIGN - 1) // SEG_ALIGN
    lstart = jnp.cumsum(seg, axis=1) - seg
    tile_groups = jnp.sum(seg, axis=1)
    region = jnp.sum(seg, axis=0)
    nblk = (region + MOE_BLK_GROUPS - 1) // MOE_BLK_GROUPS
    gend = jnp.cumsum(nblk) * MOE_BLK_GROUPS
    gbase = gend - nblk * MOE_BLK_GROUPS
    has = nblk > 0
    edge = (has & (gbase == 0)).astype(i32) + 2 * (has & (gend == gend[-1])).astype(i32)
    cum = jnp.cumsum(seg, axis=0) - seg
    g = jnp.arange(MOE_GROUPS + MOE_AHEAD * MOE_BLK_GROUPS, dtype=i32)
    e_of = jnp.minimum(jnp.sum(gend[None, :] <= g[:, None], axis=1), N_EXPERTS - 1)
    onehot_e = (e_of[:, None] == jnp.arange(N_EXPERTS)[None, :]).astype(F32)
    pick_e = lambda t: jnp.dot(onehot_e, t.astype(F32), precision=lax.Precision.HIGHEST).astype(i32)
    off = g - pick_e(gbase)
    real = (g < gend[-1]) & (off < pick_e(region))
    cum_e = pick_e(cum.T)
    tile_of = jnp.sum(cum_e <= off[:, None], axis=1) - 1
    onehot_t = tile_of[:, None] == jnp.arange(N_TILES)[None, :]
    pick_t = lambda t: jnp.sum(jnp.where(onehot_t, t, 0), axis=1)
    pos = tile_of * MOE_TILE_GROUPS + pick_t(pick_e(lstart.T)) + off - pick_t(cum_e)
    pad_rank = jnp.cumsum((~real).astype(i32)) - 1
    gsrc = jnp.where(real, pos, MOE_ZERO_GROUP).astype(i32)
    gdst = jnp.where(real, pos, MOE_ZERO_GROUP + jnp.minimum(pad_rank, MOE_PAD_GROUPS - 1)).astype(i32)
    slot_t = jnp.concatenate([slot.transpose(0, 2, 1), jnp.full((1, TOP_K, TM), -1, i32)], axis=0)
    return (slot.reshape(N_TOK, TOP_K).astype(i32), slot_t.astype(i32),
            (nblk.astype(i32), gbase.astype(i32), edge.astype(i32), gsrc, gdst), tile_groups.astype(i32))


def kernel(x_prompt, x_sample, state_s5_re, state_s5_im, state_hgrn, c, c_ctx, ada_w, ada_b, norm_w, w_in,
           s5_a_re, s5_a_im, s5_log_dt, s5_b_re, s5_b_im, s5_c_re, s5_c_im, s5_d, s5_w_glu, s5_b_glu,
           hgrn_lb, hgrn_norm_w, w_out, router_w, router_b, exp_w1, exp_b1, exp_w2, exp_b2):
    x = _assemble(x_prompt.reshape(N_PROMPT, D_MODEL), x_sample.reshape(N_SAMPLE, D_MODEL),
                  _pos_embed_2d(DEC_SEQ // GRID_W, D_MODEL))
    cond = jnp.concatenate([c_ctx[None, :], c, jnp.zeros((N_COND - 1 - DEC_BATCH, D_MODEL), F32)], axis=0)
    mod = _modulation(cond, ada_w, ada_b).reshape(DEPTH, N_COND, 1, 6 * D_MODEL)
    tiles = jnp.arange(N_TILES, dtype=jnp.int32)
    tile_cond = jnp.where(tiles < N_PROMPT // TM, 0, 1 + (tiles - N_PROMPT // TM) // (DEC_SEQ // TM)).astype(jnp.int32)
    w_in_b = w_in.astype(BF16)
    w_out_b = w_out.astype(BF16)
    w_glu_b = s5_w_glu.astype(BF16)
    router_hi = router_w.astype(BF16)
    router_parts = jnp.stack([router_hi, (router_w - router_hi.astype(F32)).astype(BF16)])

    s5_w = _s5_weights(s5_a_re, s5_a_im, s5_log_dt, s5_b_re, s5_b_im, s5_c_re, s5_c_im)
    fin_re, fin_im, fin_hg = [], [], []
    n = S5_STATE
    for l in range(DEPTH):
        u, qs, v, gs, lf_f, k_f, lf_b, k_b = _in_proj(l, tile_cond, x, mod[l], norm_w[l], w_in_b[l], hgrn_lb)

        h0 = jnp.concatenate([
            state_s5_re[:, l].transpose(2, 0, 1, 3).reshape(S5_GROUPS, DEC_BATCH, 2 * n),
            state_s5_im[:, l].transpose(2, 0, 1, 3).reshape(S5_GROUPS, DEC_BATCH, 2 * n)], axis=-1)
        h0 = h0.reshape(S5_NLT, S5_GPT, S5_NRB - 1, S5_SEQ_PER_RB, S5_NS).transpose(0, 2, 1, 3, 4)
        ycore, hfin = _s5_scan(l, u, *s5_w, h0)
        hfin = hfin.reshape(S5_GROUPS, BATCH, S5_NS)
        fin_re.append(hfin[:, :, 0:2 * n].reshape(S5_GROUPS, BATCH, 2, n).transpose(1, 2, 0, 3))
        fin_im.append(hfin[:, :, 2 * n:].reshape(S5_GROUPS, BATCH, 2, n).transpose(1, 2, 0, 3))

        s0 = jnp.concatenate([jnp.zeros((BATCH, 2, HGRN_HEADS, HGRN_DV, HGRN_DK), F32),
                              jnp.swapaxes(state_hgrn[:, l], -1, -2)], axis=0)
        o_f, o_b, sfin = _hgrn_scan(qs, v, lf_f, k_f, lf_b, k_b, s0)
        fin_hg.append(jnp.swapaxes(sfin[:BATCH], -1, -2))

        x1, h2, ri, gate, counts = _out_proj(
            tile_cond, x, ycore, u, o_f, o_b, gs, mod[l], norm_w[l], s5_d[l].reshape(1, S5_WIDTH),
            w_glu_b[l], s5_b_glu[l].reshape(1, S5_WIDTH), hgrn_norm_w[l].reshape(1, HGRN_DV),
            w_out_b[l], router_parts[:, l], router_b[l].reshape(1, N_EXPERTS))

        slot, slot_t, groups, tile_groups = _routing_tables(ri, counts)
        xs = _dispatch(slot_t, h2)
        ys = _moe_experts(l, *groups, xs, exp_w1, exp_b1, exp_w2, exp_b2)
        x = _combine(l == DEPTH - 1, tile_cond, tile_groups, x1, slot, gate, mod[l], norm_w[l], ys)

    y_prompt, y_sample = x
    return (y_prompt.reshape(BATCH, SEQ, D_MODEL), y_sample.reshape(DEC_BATCH, DEC_SEQ, D_MODEL),
            jnp.stack(fin_re, axis=1), jnp.stack(fin_im, axis=1), jnp.stack(fin_hg, axis=1))
```

```python
import functools
import math

import jax
import jax.numpy as jnp
from jax import lax
from jax.experimental import pallas as pl
from jax.experimental.pallas import tpu as pltpu

F32 = jnp.float32
BF16 = jnp.bfloat16

D_MODEL = 1024
BATCH = 16
SEQ = 256
DEPTH = 2
DEC_BATCH = 4
DEC_SEQ = 2048
GRID_W = 64
S5_WIDTH = 512
S5_GROUP_CH = 16
S5_GROUPS = 32
S5_STATE = 64
HGRN_WIDTH = 512
HGRN_DK = 128
HGRN_HEADS = 4
HGRN_DV = 128
IN_COLS = S5_WIDTH + 5 * HGRN_WIDTH
N_EXPERTS = 32
TOP_K = 4
D_FF = D_MODEL
SWIGLU_LIMIT = 7.0
SWIGLU_ALPHA = 1.702
NORM_EPS = 1e-6
POS_BASE = 10000.0

N_PROMPT = BATCH * SEQ
N_SAMPLE = DEC_BATCH * DEC_SEQ
N_TOK = N_PROMPT + N_SAMPLE
N_SEQ = BATCH + DEC_BATCH
N_COND = 8

TM = 256
N_TILES = N_TOK // TM
S5_T = 16
S5_LT = 128
S5_GPT = S5_LT // S5_GROUP_CH
S5_NLT = S5_WIDTH // S5_LT
S5_RB = N_PROMPT
S5_CR = S5_RB // S5_T
S5_NRB = N_TOK // S5_RB
S5_SEQ_PER_RB = S5_RB // DEC_SEQ
S5_NLAG = 2 * S5_T - 1
S5_NS = 4 * S5_STATE
HC = 128
N_HCHUNK = N_TOK // HC
EXP_CLAMP = 80.0
MOE_BM = 256
N_ASSIGN = N_TOK * TOP_K
SEG_ALIGN = 16
TILE_SLOTS = -(-(TM * TOP_K + N_EXPERTS * (SEG_ALIGN - 1)) // 128) * 128
MOE_BLOCKS = -(-(N_ASSIGN + N_TILES * N_EXPERTS * (SEG_ALIGN - 1) + N_EXPERTS * (MOE_BM - 1)) // MOE_BM)
MOE_GRP = SEG_ALIGN
MOE_TILE_GROUPS = TILE_SLOTS // MOE_GRP
MOE_BLK_GROUPS = MOE_BM // MOE_GRP
MOE_DEPTH = 4
MOE_AHEAD = MOE_DEPTH - 1
MOE_GROUPS = MOE_BLOCKS * MOE_BLK_GROUPS
MOE_ZERO_GROUP = N_TILES * MOE_TILE_GROUPS
MOE_PAD_GROUPS = N_EXPERTS * (MOE_BLK_GROUPS - 1)
MOE_XS_ROWS = (N_TILES + 1) * TILE_SLOTS
MOE_YS_ROWS = N_TILES * TILE_SLOTS + MOE_PAD_GROUPS * MOE_GRP
MOE_BASE_ROWS = TM * TOP_K
MOE_TAIL_ARMS = (256, 128, 64, 32, 16)
VMEM_LIMIT = 56 * 1024 * 1024


def _rms(x, w):
    return x * lax.rsqrt(jnp.mean(x * x, axis=-1, keepdims=True) + NORM_EPS) * w


def _silu(x):
    return x * jax.nn.sigmoid(x)


MOD_TN = 1536


def _mod_kernel(cond_ref, w_ref, b_ref, o_ref):
    s = _silu(cond_ref[...]).astype(BF16)
    o_ref[0] = jnp.dot(s, w_ref[0].astype(BF16), preferred_element_type=F32) + b_ref[0]


def _modulation(cond, ada_w, ada_b):
    return pl.pallas_call(
        _mod_kernel,
        grid=(DEPTH, 6 * D_MODEL // MOD_TN),
        in_specs=[
            pl.BlockSpec((N_COND, D_MODEL), lambda l, j: (0, 0)),
            pl.BlockSpec((1, D_MODEL, MOD_TN), lambda l, j: (l, 0, j)),
            pl.BlockSpec((1, 1, MOD_TN), lambda l, j: (l, 0, j)),
        ],
        out_specs=pl.BlockSpec((1, N_COND, MOD_TN), lambda l, j: (l, 0, j)),
        out_shape=jax.ShapeDtypeStruct((DEPTH, N_COND, 6 * D_MODEL), F32),
        compiler_params=pltpu.CompilerParams(vmem_limit_bytes=VMEM_LIMIT),
        name="adaln_mod",
    )(cond, ada_w, ada_b.reshape(DEPTH, 1, 6 * D_MODEL))


def _in_kernel(layer, cond_ref, x_ref, mod_ref, nw_ref, w_ref, lb_ref,
               u_ref, q_ref, v_ref, g_ref, lff_ref, kf_ref, lfb_ref, kb_ref):
    del cond_ref
    mod = mod_ref[0]
    sh1 = mod[:, 0:D_MODEL]
    sc1 = mod[:, D_MODEL:2 * D_MODEL]
    h = _rms(x_ref[...], nw_ref[0:1, :]) * (1.0 + sc1) + sh1
    z = jnp.dot(h.astype(BF16), w_ref[...], preferred_element_type=F32)
    w = HGRN_WIDTH
    c0 = S5_WIDTH
    u_ref[...] = z[:, 0:c0]
    q_ref[...] = _silu(z[:, c0:c0 + w]) * (HGRN_DK ** -0.5)
    v_ref[...] = z[:, c0 + 3 * w:c0 + 4 * w]
    g_ref[...] = _silu(z[:, c0 + 4 * w:c0 + 5 * w])
    lbp = lb_ref[...]
    e = jnp.exp(lbp - jnp.max(lbp, axis=0, keepdims=True))
    probs = e / jnp.sum(e, axis=0, keepdims=True)
    lb = jnp.sum(probs[0:layer + 1], axis=0) - probs[0]
    for d, (lf_ref, k_ref) in enumerate(((lff_ref, kf_ref), (lfb_ref, kb_ref))):
        f = z[:, c0 + (1 + d) * w:c0 + (2 + d) * w]
        lbd = lb[d:d + 1, :]
        fg = lbd + (1.0 - lbd) * jax.nn.sigmoid(f)
        lf_ref[...] = jnp.log(fg)
        k_ref[...] = 1.0 - fg


def _in_proj(layer, tile_cond, x, mod_l, norm_w_l, w_in_l, hgrn_lb):
    tok = lambda i, c: (i, 0)
    out = jax.ShapeDtypeStruct((N_TOK, HGRN_WIDTH), F32)
    return pl.pallas_call(
        functools.partial(_in_kernel, layer),
        grid_spec=pltpu.PrefetchScalarGridSpec(
            num_scalar_prefetch=1,
            grid=(N_TILES,),
            in_specs=[
                pl.BlockSpec((TM, D_MODEL), tok),
                pl.BlockSpec((1, 1, 6 * D_MODEL), lambda i, c: (c[i], 0, 0)),
                pl.BlockSpec((4, D_MODEL), lambda i, c: (0, 0)),
                pl.BlockSpec((D_MODEL, IN_COLS), lambda i, c: (0, 0)),
                pl.BlockSpec((DEPTH, 2, HGRN_WIDTH), lambda i, c: (0, 0, 0)),
            ],
            out_specs=[pl.BlockSpec((TM, HGRN_WIDTH), tok)] * 8,
        ),
        out_shape=[out] * 8,
        compiler_params=pltpu.CompilerParams(
            dimension_semantics=("arbitrary",), vmem_limit_bytes=VMEM_LIMIT),
        name="in_proj",
    )(tile_cond, x, mod_l, norm_w_l, w_in_l, hgrn_lb)


def _s5_layer_weights(a_re, a_im, log_dt, b_re, b_im, c_re, c_im):
    hp = lax.Precision.HIGHEST
    t = S5_T
    dt = jnp.exp(log_dt)[..., None]
    lam_re = jnp.minimum(a_re, -1e-4)
    lam_im = a_im
    mag = jnp.exp(dt * lam_re)
    ang = dt * lam_im
    ab_re = mag * jnp.cos(ang)
    ab_im = mag * jnp.sin(ang)
    den = lam_re * lam_re + lam_im * lam_im
    nr = ab_re - 1.0
    ni = ab_im
    co_re = (nr * lam_re + ni * lam_im) / den
    co_im = (ni * lam_re - nr * lam_im) / den
    bb_re = co_re[..., None] * b_re - co_im[..., None] * b_im
    bb_im = co_re[..., None] * b_im + co_im[..., None] * b_re
    pr = [jnp.ones_like(ab_re)]
    pi = [jnp.zeros_like(ab_im)]
    for _ in range(t):
        pr.append(pr[-1] * ab_re - pi[-1] * ab_im)
        pi.append(pr[-2] * ab_im + pi[-1] * ab_re)
    pr = jnp.stack(pr, axis=1)
    pi = jnp.stack(pi, axis=1)
    abr = pr[..., None] * bb_re[:, None] - pi[..., None] * bb_im[:, None]
    abi = pr[..., None] * bb_im[:, None] + pi[..., None] * bb_re[:, None]
    kk = jnp.einsum('dgon,dkgni->dkgio', jnp.concatenate([c_re, -c_im], axis=-1),
                    jnp.concatenate([abr, abi], axis=-2), precision=hp)
    kern = (kk[0], kk[1])
    abs_ = ((abr[0], abi[0]), (abr[1], abi[1]))
    pows = ((pr[0], pi[0]), (pr[1], pi[1]))
    p = S5_GROUP_CH
    lags = jnp.concatenate([kern[1][t - 1:0:-1], (kern[0][0] + kern[1][0])[None], kern[0][1:t]], axis=0)
    m = lags.reshape(S5_NLAG, S5_NLT, S5_GPT, p, p)
    bd = jnp.einsum('ltaio,ab->ltaibo', m, jnp.eye(S5_GPT, dtype=F32))
    wcat = bd.reshape(S5_NLAG, S5_NLT, S5_LT, S5_LT).transpose(1, 2, 0, 3).reshape(S5_NLT, S5_LT, S5_NLAG * S5_LT)
    (abr_f, abi_f), (abr_b, abi_b) = abs_
    parts = (abr_f[t - 1::-1], abr_b[:t], abi_f[t - 1::-1], abi_b[:t])
    wst = jnp.concatenate([x.transpose(1, 0, 3, 2) for x in parts], axis=-1)
    wst = wst.reshape(S5_NLT, S5_GPT, t, p, S5_NS).transpose(0, 2, 1, 3, 4).reshape(S5_NLT, t * S5_LT, S5_NS)
    (pr_f, pi_f), (pr_b, pi_b) = pows

    def out_rows(cr, ci, pr_, pi_):
        re_rows = cr[None] * pr_[:, :, None, :] - ci[None] * pi_[:, :, None, :]
        im_rows = -(cr[None] * pi_[:, :, None, :] + ci[None] * pr_[:, :, None, :])
        return re_rows.transpose(1, 3, 0, 2), im_rows.transpose(1, 3, 0, 2)

    fre, fim = out_rows(c_re[0], c_im[0], pr_f[1:t + 1], pi_f[1:t + 1])
    bre, bim = out_rows(c_re[1], c_im[1], pr_b[t:0:-1], pi_b[t:0:-1])
    wout = lax.optimization_barrier(jnp.concatenate([fre, bre, fim, bim], axis=1))
    wdense = (wout.reshape(S5_NLT, S5_GPT, S5_NS, t, p).transpose(0, 2, 3, 1, 4)
              .reshape(S5_NLT, S5_NS, t * S5_LT))
    at = jnp.stack([jnp.concatenate([pr_f[t], pr_b[t]], axis=-1),
                    jnp.concatenate([pi_f[t], pi_b[t]], axis=-1)], axis=1)
    return (wcat.astype(BF16), wst.astype(BF16), wdense.astype(BF16),
            at.reshape(S5_NLT, S5_GPT, 2, 2 * S5_STATE))


def _s5_weights(*params):
    return jax.vmap(_s5_layer_weights)(*params)


def _s5_kernel(u_ref, wcat_ref, wst_ref, wd_ref, at_ref, h0_ref, y_ref, hfin_ref,
               wbig, dh_scr, hf_scr, hb_scr):
    t = S5_T
    n2 = 2 * S5_STATE
    r = pl.program_id(1)

    @pl.when(r == 0)
    def _():
        for s in range(t):
            wbig[s * S5_LT:(s + 1) * S5_LT, :] = wcat_ref[0, :, (t - 1 - s) * S5_LT:(2 * t - 1 - s) * S5_LT]

    xcat = jnp.concatenate([u_ref[pl.ds(s, S5_CR, stride=t), :].astype(BF16) for s in range(t)], axis=-1)
    yacc = jnp.dot(xcat, wbig[...], preferred_element_type=F32)
    lane_k = lax.broadcasted_iota(jnp.int32, (1, t * S5_LT), 1)
    grp_k = (lane_k % S5_LT) // S5_GROUP_CH
    for gi in range(S5_GPT):
        xg = jnp.where(grp_k == gi, xcat, jnp.zeros_like(xcat))
        dh = jnp.dot(xg, wst_ref[0], preferred_element_type=F32)
        dh_scr[gi, 0] = dh[:, 0:n2]
        dh_scr[gi, 1] = dh[:, n2:]

    lane = lax.broadcasted_iota(jnp.int32, (1, n2), 1)
    fwd_lane = lane < S5_STATE
    are = [at_ref[0, gi, 0:1, :] for gi in range(S5_GPT)]
    aim = [at_ref[0, gi, 1:2, :] for gi in range(S5_GPT)]

    def advance(gi, hre, him, dre, dim):
        return (are[gi] * hre - aim[gi] * him + dre, are[gi] * him + aim[gi] * hre + dim)

    @pl.when(r == 0)
    def _():
        nc = SEQ // t
        for gi in range(S5_GPT):
            hre = jnp.zeros((BATCH, n2), F32)
            him = jnp.zeros((BATCH, n2), F32)
            for s in range(nc):
                rows_f = pl.ds(s, BATCH, stride=nc)
                rows_b = pl.ds(nc - 1 - s, BATCH, stride=nc)
                hf_scr[gi, 0, rows_f, :] = hre
                hf_scr[gi, 1, rows_f, :] = him
                hb_scr[gi, 0, rows_b, :] = hre
                hb_scr[gi, 1, rows_b, :] = him
                dre = jnp.where(fwd_lane, dh_scr[gi, 0, rows_f, :], dh_scr[gi, 0, rows_b, :])
                dim = jnp.where(fwd_lane, dh_scr[gi, 1, rows_f, :], dh_scr[gi, 1, rows_b, :])
                hre, him = advance(gi, hre, him, dre, dim)
            hfin_ref[0, gi] = jnp.concatenate([hre, him], axis=-1)

    @pl.when(r > 0)
    def _():
        nc = DEC_SEQ // t
        nb = S5_SEQ_PER_RB
        init = tuple((h0_ref[0, 0, gi, b:b + 1, 0:n2], h0_ref[0, 0, gi, b:b + 1, n2:])
                     for gi in range(S5_GPT) for b in range(nb))

        def step(o, carry):
            new = []
            for gi in range(S5_GPT):
                for b in range(nb):
                    hre, him = carry[gi * nb + b]
                    rf = pl.multiple_of(b * nc + o * 8, 8)
                    rb = pl.multiple_of(b * nc + nc - 8 - o * 8, 8)
                    dfr = dh_scr[gi, 0, pl.ds(rf, 8), :]
                    dfi = dh_scr[gi, 1, pl.ds(rf, 8), :]
                    dbr = dh_scr[gi, 0, pl.ds(rb, 8), :]
                    dbi = dh_scr[gi, 1, pl.ds(rb, 8), :]
                    ent_re, ent_im = [], []
                    for i in range(8):
                        ent_re.append(hre)
                        ent_im.append(him)
                        dre = jnp.where(fwd_lane, dfr[i:i + 1], dbr[7 - i:8 - i])
                        dim = jnp.where(fwd_lane, dfi[i:i + 1], dbi[7 - i:8 - i])
                        hre, him = advance(gi, hre, him, dre, dim)
                    hf_scr[gi, 0, pl.ds(rf, 8), :] = jnp.concatenate(ent_re, axis=0)
                    hf_scr[gi, 1, pl.ds(rf, 8), :] = jnp.concatenate(ent_im, axis=0)
                    hb_scr[gi, 0, pl.ds(rb, 8), :] = jnp.concatenate(ent_re[::-1], axis=0)
                    hb_scr[gi, 1, pl.ds(rb, 8), :] = jnp.concatenate(ent_im[::-1], axis=0)
                    new.append((hre, him))
            return tuple(new)

        lax.fori_loop(0, nc // 8, step, init)

    for gi in range(S5_GPT):
        hent = jnp.concatenate([jnp.where(fwd_lane, hf_scr[gi, 0], hb_scr[gi, 0]),
                                jnp.where(fwd_lane, hf_scr[gi, 1], hb_scr[gi, 1])], axis=-1).astype(BF16)
        yi = jnp.dot(hent, wd_ref[0], preferred_element_type=F32)
        yacc = yacc + jnp.where(grp_k == gi, yi, 0.0)
    for s in range(t):
        y_ref[pl.ds(s, S5_CR, stride=t), :] = yacc[:, s * S5_LT:(s + 1) * S5_LT]


def _s5_scan(layer, u, wcat, wst, wdense, at, h0):
    tile = lambda j, r: (layer * S5_NLT + j, 0, 0)
    merge = lambda w: w.reshape((DEPTH * S5_NLT,) + w.shape[2:])
    wcat, wst, wdense, at = merge(wcat), merge(wst), merge(wdense), merge(at)
    return pl.pallas_call(
        _s5_kernel,
        grid=(S5_NLT, S5_NRB),
        in_specs=[
            pl.BlockSpec((S5_RB, S5_LT), lambda j, r: (r, j)),
            pl.BlockSpec((1, S5_LT, S5_NLAG * S5_LT), tile),
            pl.BlockSpec((1, S5_T * S5_LT, S5_NS), tile),
            pl.BlockSpec((1, S5_NS, S5_T * S5_LT), tile),
            pl.BlockSpec((1, S5_GPT, 2, 2 * S5_STATE), lambda j, r: (layer * S5_NLT + j, 0, 0, 0)),
            pl.BlockSpec((1, 1, S5_GPT, S5_SEQ_PER_RB, S5_NS), lambda j, r: (j, jnp.maximum(r - 1, 0), 0, 0, 0)),
        ],
        out_specs=[
            pl.BlockSpec((S5_RB, S5_LT), lambda j, r: (r, j)),
            pl.BlockSpec((1, S5_GPT, BATCH, S5_NS), lambda j, r: (j, 0, 0, 0)),
        ],
        out_shape=[
            jax.ShapeDtypeStruct((N_TOK, S5_WIDTH), F32),
            jax.ShapeDtypeStruct((S5_NLT, S5_GPT, BATCH, S5_NS), F32),
        ],
        scratch_shapes=[pltpu.VMEM((S5_T * S5_LT, S5_T * S5_LT), BF16)]
                       + [pltpu.VMEM((S5_GPT, 2, S5_CR, 2 * S5_STATE), F32)] * 3,
        compiler_params=pltpu.CompilerParams(
            dimension_semantics=("arbitrary", "arbitrary"), vmem_limit_bytes=VMEM_LIMIT),
        name="s5_scan",
    )(u, wcat, wst, wdense, at, h0)


def _split3(x):
    hi = x.astype(BF16)
    r1 = x - hi.astype(F32)
    mid = r1.astype(BF16)
    lo = (r1 - mid.astype(F32)).astype(BF16)
    return hi, mid, lo


def _piecewise_rows(b, blk, row_in_blk):
    parts = []
    for j in range(HC // blk):
        r = j * blk + row_in_blk
        parts.append(jnp.broadcast_to(b[r:r + 1, :], (blk, b.shape[1])))
    return parts[0] if len(parts) == 1 else jnp.concatenate(parts, axis=0)


def _nt(a, b):
    return lax.dot_general(a, b, (((1,), (1,)), ((), ())), preferred_element_type=F32)


def _tn(a, b):
    return lax.dot_general(a, b, (((0,), (0,)), ((), ())), preferred_element_type=F32)


def _hgrn_dir(reverse, q_ref, v_ref, lf_ref, k_ref, o_ref, st_ref, b_scr):
    row = lax.broadcasted_iota(jnp.int32, (HC, HC), 0)
    col = lax.broadcasted_iota(jnp.int32, (HC, HC), 1)
    causal = (col >= row) if reverse else (col <= row)
    tri = jnp.where(causal, 1.0, 0.0).astype(BF16)
    lf = lf_ref[...]
    hi, mid, lo = _split3(lf)
    ball = (jnp.dot(tri, hi, preferred_element_type=F32) + jnp.dot(tri, mid, preferred_element_type=F32)
            + jnp.dot(tri, lo, preferred_element_type=F32))
    last = 0 if reverse else HC - 1
    masks = []
    for blk in (128, 64, 32):
        half = blk // 2
        same = (row // blk) == (col // blk)
        t_hi = (row % blk) >= half
        s_hi = (col % blk) >= half
        if reverse:
            masks.append(same & jnp.logical_not(t_hi) & s_hi)
        else:
            masks.append(same & t_hi & jnp.logical_not(s_hi))
    b_scr[...] = ball
    mrow = 8 if reverse else 7
    mid = _piecewise_rows(ball, 16, mrow)
    spread = jnp.zeros((1, HGRN_WIDTH), F32)
    for j in range(HC // 16):
        for edge in (16 * j, 16 * j + 15):
            spread = jnp.maximum(spread, jnp.abs(ball[edge:edge + 1, :] - ball[16 * j + mrow:16 * j + mrow + 1, :]))
    stable = jnp.max(spread) <= EXP_CLAMP
    diag_ok = ((row // 16) == (col // 16)) & causal & stable
    for h in range(HGRN_HEADS):
        sl = slice(h * HGRN_DK, (h + 1) * HGRN_DK)
        b = ball[:, sl]
        q = q_ref[:, sl]
        k = k_ref[:, sl]
        v = v_ref[:, sl].astype(BF16)
        st = st_ref[h]
        b_last = b[last:last + 1, :]
        q_in = (q * jnp.exp(b)).astype(BF16)
        k_in = (k * jnp.exp(b_last - b)).astype(BF16)
        o = _nt(q_in, st.astype(BF16))
        st_ref[h] = jnp.exp(b_last) * st + _tn(v, k_in)
        scores = jnp.zeros((HC, HC), F32)
        for blk, mask in zip((128, 64, 32), masks):
            half = blk // 2
            zero = jnp.zeros((half, HGRN_DK), BF16)
            qparts, kparts = [], []
            for j in range(HC // blk):
                early = slice(j * blk, j * blk + half)
                late = slice(j * blk + half, (j + 1) * blk)
                if reverse:
                    m = b[j * blk + half:j * blk + half + 1, :]
                    qrows, krows = early, late
                else:
                    m = b[j * blk + half - 1:j * blk + half, :]
                    qrows, krows = late, early
                qe = (q[qrows] * jnp.exp(b[qrows] - m)).astype(BF16)
                ke = (k[krows] * jnp.exp(m - b[krows])).astype(BF16)
                qparts += [qe, zero] if reverse else [zero, qe]
                kparts += [zero, ke] if reverse else [ke, zero]
            part = _nt(jnp.concatenate(qparts, axis=0), jnp.concatenate(kparts, axis=0))
            scores = scores + (part if blk == HC else jnp.where(mask, part, 0.0))
        m = mid[:, sl]
        qd = (q * jnp.exp(jnp.minimum(b - m, EXP_CLAMP))).astype(BF16)
        kd = (k * jnp.exp(jnp.minimum(m - b, EXP_CLAMP))).astype(BF16)
        scores = scores + jnp.where(diag_ok, _nt(qd, kd), 0.0)
        o_ref[:, sl] = o + jnp.dot(scores.astype(BF16), v, preferred_element_type=F32)

    return stable


def _hgrn_exact_diagonal(reverse, stable, q_ref, v_ref, k_ref, o_ref, b_scr):
    @pl.when(jnp.logical_not(stable))
    def _():
        pos = lax.broadcasted_iota(jnp.int32, (HC, 1), 0) % 16

        def lag(d, c):
            shift = (HC - d) % HC if reverse else d
            valid = (pos + d <= 15) if reverse else (pos >= d)
            for h in range(HGRN_HEADS):
                sl = slice(h * HGRN_DK, (h + 1) * HGRN_DK)
                b = b_scr[:, sl]
                bs = pltpu.roll(b, shift, 0)
                ks = pltpu.roll(k_ref[:, sl], shift, 0)
                vs = pltpu.roll(v_ref[:, sl], shift, 0)
                e = jnp.exp(jnp.where(valid, b - bs, 0.0))
                w = jnp.sum(q_ref[:, sl] * ks * e, axis=-1, keepdims=True)
                o_ref[:, sl] = o_ref[:, sl] + jnp.where(valid, w, 0.0) * vs
            return c

        lax.fori_loop(0, 16, lag, 0)


def _hgrn_kernel(cf_ref, cb_ref, seq_ref, first_ref, last_ref,
                 qf_ref, vf_ref, lff_ref, kf_ref, qb_ref, vb_ref, lfb_ref, kb_ref, s0_ref,
                 of_ref, ob_ref, sfin_ref, st_scr, b_scr):
    del cf_ref, cb_ref, seq_ref
    j = pl.program_id(0)

    @pl.when(first_ref[j] == 1)
    def _():
        st_scr[...] = s0_ref[0]

    ok_f = _hgrn_dir(False, qf_ref, vf_ref, lff_ref, kf_ref, of_ref, st_scr.at[0], b_scr.at[0])
    ok_b = _hgrn_dir(True, qb_ref, vb_ref, lfb_ref, kb_ref, ob_ref, st_scr.at[1], b_scr.at[1])
    _hgrn_exact_diagonal(False, ok_f, qf_ref, vf_ref, kf_ref, of_ref, b_scr.at[0])
    _hgrn_exact_diagonal(True, ok_b, qb_ref, vb_ref, kb_ref, ob_ref, b_scr.at[1])

    @pl.when(last_ref[j] == 1)
    def _():
        sfin_ref[0] = st_scr[...]


def _hgrn_tables():
    cf, cb, sq, first, last = [], [], [], [], []
    base = 0
    for s in range(N_SEQ):
        nc = (SEQ if s < BATCH else DEC_SEQ) // HC
        for t in range(nc):
            cf.append(base + t)
            cb.append(base + nc - 1 - t)
            sq.append(s)
            first.append(int(t == 0))
            last.append(int(t == nc - 1))
        base += nc
    return tuple(jnp.asarray(x, jnp.int32) for x in (cf, cb, sq, first, last))


def _hgrn_scan(qs, v, lf_f, k_f, lf_b, k_b, s0):
    fwd = lambda j, cf, cb, sq, fi, la: (cf[j], 0)
    bwd = lambda j, cf, cb, sq, fi, la: (cb[j], 0)
    seq = lambda j, cf, cb, sq, fi, la: (sq[j], 0, 0, 0, 0)
    tile = (HC, HGRN_WIDTH)
    sblk = (1, 2, HGRN_HEADS, HGRN_DV, HGRN_DK)
    out = jax.ShapeDtypeStruct((N_TOK, HGRN_WIDTH), F32)
    return pl.pallas_call(
        _hgrn_kernel,
        grid_spec=pltpu.PrefetchScalarGridSpec(
            num_scalar_prefetch=5,
            grid=(N_HCHUNK,),
            in_specs=[pl.BlockSpec(tile, fwd)] * 4 + [pl.BlockSpec(tile, bwd)] * 4
                     + [pl.BlockSpec(sblk, seq)],
            out_specs=[pl.BlockSpec(tile, fwd), pl.BlockSpec(tile, bwd), pl.BlockSpec(sblk, seq)],
            scratch_shapes=[pltpu.VMEM(sblk[1:], F32), pltpu.VMEM((2, HC, HGRN_WIDTH), F32)],
        ),
        out_shape=[out, out, jax.ShapeDtypeStruct((N_SEQ,) + sblk[1:], F32)],
        compiler_params=pltpu.CompilerParams(
            dimension_semantics=("arbitrary",), vmem_limit_bytes=VMEM_LIMIT),
        name="hgrn_scan",
    )(*_hgrn_tables(), qs, v, lf_f, k_f, qs, v, lf_b, k_b, s0)


def _gelu_tanh(x):
    return 0.5 * x * (1.0 + jnp.tanh(math.sqrt(2.0 / math.pi) * (x + 0.044715 * (x * x * x))))


def _out_kernel(cond_ref, x_ref, yc_ref, u_ref, of_ref, ob_ref, g_ref, mod_ref, nw_ref, d_ref,
                wglu_ref, bglu_ref, hnw_ref, wout_ref, rw_ref, rb_ref,
                x1_ref, h2_ref, ri_ref, rg_ref, cnt_ref):
    del cond_ref
    mod = mod_ref[0]
    g1 = mod[:, 2 * D_MODEL:3 * D_MODEL]
    sh2 = mod[:, 3 * D_MODEL:4 * D_MODEL]
    sc2 = mod[:, 4 * D_MODEL:5 * D_MODEL]
    y = _gelu_tanh(yc_ref[...] + d_ref[...] * u_ref[...])
    y_s5 = y * jax.nn.sigmoid(jnp.dot(y.astype(BF16), wglu_ref[...], preferred_element_type=F32) + bglu_ref[...])
    o = of_ref[...] + ob_ref[...]
    gs = g_ref[...]
    heads = []
    for h in range(HGRN_HEADS):
        sl = slice(h * HGRN_DV, (h + 1) * HGRN_DV)
        heads.append(_rms(o[:, sl], hnw_ref[...]) * gs[:, sl])
    y_hg = jnp.concatenate(heads, axis=-1)
    mix = (jnp.dot(y_s5.astype(BF16), wout_ref[0:S5_WIDTH, :], preferred_element_type=F32)
           + jnp.dot(y_hg.astype(BF16), wout_ref[S5_WIDTH:, :], preferred_element_type=F32))
    x1 = x_ref[...] + g1 * _rms(mix, nw_ref[1:2, :])
    x1_ref[...] = x1
    h2 = _rms(x1, nw_ref[2:3, :]) * (1.0 + sc2) + sh2
    h2_hi = h2.astype(BF16)
    h2_ref[...] = h2_hi
    h2_lo = (h2 - h2_hi.astype(F32)).astype(BF16)
    logits = rb_ref[...]
    for a in (h2_hi, h2_lo):
        for part in range(2):
            logits = logits + jnp.dot(a, rw_ref[part], preferred_element_type=F32)
    eidx = lax.broadcasted_iota(jnp.int32, (TM, N_EXPERTS), 1).astype(F32)
    vals = logits
    top_v, top_i, onehots = [], [], []
    for _ in range(TOP_K):
        mx = jnp.max(vals, axis=-1, keepdims=True)
        ix = jnp.min(jnp.where(vals == mx, eidx, float(N_EXPERTS)), axis=-1, keepdims=True)
        sel = eidx == ix
        top_v.append(mx)
        top_i.append(ix)
        onehots.append(sel)
        vals = jnp.where(sel, -jnp.inf, vals)
    ex = [jnp.exp(tv - top_v[0]) for tv in top_v]
    den = ex[0] + ex[1] + ex[2] + ex[3]
    tot = jnp.zeros((TM, N_EXPERTS), F32)
    for sel in onehots:
        tot = tot + jnp.where(sel, 1.0, 0.0)
    r_t = lax.broadcasted_iota(jnp.int32, (TM, TM), 0)
    r_s = lax.broadcasted_iota(jnp.int32, (TM, TM), 1)
    strict = jnp.where(r_s < r_t, 1.0, 0.0).astype(BF16)
    before = jnp.dot(strict, tot.astype(BF16), preferred_element_type=F32)
    cnt = jnp.sum(tot, axis=0, keepdims=True)
    seg = jnp.floor((cnt + (SEG_ALIGN - 1)) * (1.0 / SEG_ALIGN)) * SEG_ALIGN
    e_r = lax.broadcasted_iota(jnp.int32, (N_EXPERTS, N_EXPERTS), 0)
    e_c = lax.broadcasted_iota(jnp.int32, (N_EXPERTS, N_EXPERTS), 1)
    lstart = jnp.dot(seg, jnp.where(e_r < e_c, 1.0, 0.0), precision=lax.Precision.HIGHEST,
                     preferred_element_type=F32)
    before = before + lstart
    lane = lax.broadcasted_iota(jnp.int32, (TM, 128), 1)
    ri = jnp.zeros((TM, 128), F32)
    rg = jnp.zeros((TM, 128), F32)
    for kk in range(TOP_K):
        rank = jnp.sum(jnp.where(onehots[kk], before, 0.0), axis=-1, keepdims=True)
        ri = jnp.where(lane == kk, top_i[kk], ri)
        ri = jnp.where(lane == TOP_K + kk, rank, ri)
        rg = jnp.where(lane == kk, ex[kk] / den, rg)
    ri_ref[...] = ri.astype(jnp.int32)
    rg_ref[...] = rg
    cnt_ref[0] = cnt.astype(jnp.int32)


def _out_proj(tile_cond, x, ycore, u, o_f, o_b, gs, mod_l, norm_w_l, s5_d_l, wglu_l, bglu_l, hnw_l,
              wout_l, rw_l, rb_l):
    tok = lambda i, c: (i, 0)
    full2 = lambda i, c: (0, 0)
    half = pl.BlockSpec((TM, HGRN_WIDTH), tok)
    wide = pl.BlockSpec((TM, D_MODEL), tok)
    return pl.pallas_call(
        _out_kernel,
        grid_spec=pltpu.PrefetchScalarGridSpec(
            num_scalar_prefetch=1,
            grid=(N_TILES,),
            in_specs=[
                wide, half, half, half, half, half,
                pl.BlockSpec((1, 1, 6 * D_MODEL), lambda i, c: (c[i], 0, 0)),
                pl.BlockSpec((4, D_MODEL), full2),
                pl.BlockSpec((1, S5_WIDTH), full2),
                pl.BlockSpec((S5_WIDTH, S5_WIDTH), full2),
                pl.BlockSpec((1, S5_WIDTH), full2),
                pl.BlockSpec((1, HGRN_DV), full2),
                pl.BlockSpec((D_MODEL, D_MODEL), full2),
                pl.BlockSpec((2, D_MODEL, N_EXPERTS), lambda i, c: (0, 0, 0)),
                pl.BlockSpec((1, N_EXPERTS), full2),
            ],
            out_specs=[wide, wide, pl.BlockSpec((TM, 128), tok), pl.BlockSpec((TM, 128), tok),
                       pl.BlockSpec((1, 1, N_EXPERTS), lambda i, c: (i, 0, 0))],
        ),
        out_shape=[
            jax.ShapeDtypeStruct((N_TOK, D_MODEL), F32),
            jax.ShapeDtypeStruct((N_TOK, D_MODEL), BF16),
            jax.ShapeDtypeStruct((N_TOK, 128), jnp.int32),
            jax.ShapeDtypeStruct((N_TOK, 128), F32),
            jax.ShapeDtypeStruct((N_TILES, 1, N_EXPERTS), jnp.int32),
        ],
        compiler_params=pltpu.CompilerParams(
            dimension_semantics=("arbitrary",), vmem_limit_bytes=VMEM_LIMIT),
        name="out_proj_router",
    )(tile_cond, x, ycore, u, o_f, o_b, gs, mod_l, norm_w_l, s5_d_l, wglu_l, bglu_l, hnw_l,
      wout_l, rw_l, rb_l)


def _dispatch_kernel(slot_ref, h_ref, o_ref):
    srow = lax.broadcasted_iota(jnp.int32, (TILE_SLOTS, TM), 0)
    perm = jnp.zeros((TILE_SLOTS, TM), F32)
    for kk in range(TOP_K):
        perm = perm + jnp.where(srow == slot_ref[0, kk:kk + 1, :], 1.0, 0.0)
    o_ref[...] = jnp.dot(perm.astype(BF16), h_ref[...], preferred_element_type=F32).astype(BF16)


def _dispatch(slot_t, h2):
    return pl.pallas_call(
        _dispatch_kernel,
        grid=(N_TILES + 1,),
        in_specs=[pl.BlockSpec((1, TOP_K, TM), lambda i: (i, 0, 0)),
                  pl.BlockSpec((TM, D_MODEL), lambda i: (jnp.minimum(i, N_TILES - 1), 0))],
        out_specs=pl.BlockSpec((TILE_SLOTS, D_MODEL), lambda i: (i, 0)),
        out_shape=jax.ShapeDtypeStruct((MOE_XS_ROWS, D_MODEL), BF16),
        compiler_params=pltpu.CompilerParams(
            dimension_semantics=("arbitrary",), vmem_limit_bytes=VMEM_LIMIT),
        name="moe_dispatch",
    )(slot_t, h2)


def _moe_kernel(nblk_ref, gbase_ref, edge_ref, gsrc_ref, gdst_ref, w1_ref, b1_ref, w2_ref, b2_ref, xs_ref, ys_ref,
                xbuf, obuf, w1b, w2b, sem_in, sem_out):
    e = pl.program_id(0)
    nb = nblk_ref[e]
    k0 = gbase_ref[e] // MOE_BLK_GROUPS
    first_active = (edge_ref[e] & 1) != 0
    last_active = (edge_ref[e] & 2) != 0

    def gather(k):
        buf = k % MOE_DEPTH
        return [pltpu.make_async_copy(
            xs_ref.at[pl.ds(pl.multiple_of(gsrc_ref[k * MOE_BLK_GROUPS + g] * MOE_GRP, MOE_GRP), MOE_GRP)],
            xbuf.at[buf, pl.ds(g * MOE_GRP, MOE_GRP)], sem_in.at[buf]) for g in range(MOE_BLK_GROUPS)]

    def scatter(k):
        buf = k % MOE_DEPTH
        return [pltpu.make_async_copy(
            obuf.at[buf, pl.ds(g * MOE_GRP, MOE_GRP)],
            ys_ref.at[pl.ds(pl.multiple_of(gdst_ref[k * MOE_BLK_GROUPS + g] * MOE_GRP, MOE_GRP), MOE_GRP)],
            sem_out.at[buf]) for g in range(MOE_BLK_GROUPS)]

    @pl.when(nb > 0)
    def _():
        def cast(j, c):
            r = pl.multiple_of(j * 128, 128)
            w1b[pl.ds(r, 128), :] = w1_ref[0, pl.ds(r, 128), :].astype(BF16)
            w2b[pl.ds(r, 128), :] = w2_ref[0, pl.ds(r, 128), :].astype(BF16)
            return c

        @pl.when(first_active)
        def _():
            for a in range(MOE_AHEAD):
                for g, cp in enumerate(gather(k0 + a)):
                    cp.start(priority=g % 2)

        lax.fori_loop(0, D_MODEL // 128, cast, 0)

        def block(b, c):
            k = k0 + b
            buf = k % MOE_DEPTH

            @pl.when(k >= MOE_DEPTH)
            def _():
                for cp in scatter(k - MOE_DEPTH):
                    cp.wait()

            for cp in gather(k):
                cp.wait()
            for g, cp in enumerate(gather(k + MOE_AHEAD)):
                cp.start(priority=g % 2)
            h = jnp.dot(xbuf[buf], w1b[...], preferred_element_type=F32) + b1_ref[0]
            glu = jnp.minimum(h[:, :D_FF], SWIGLU_LIMIT)
            lin = jnp.clip(h[:, D_FF:], -SWIGLU_LIMIT, SWIGLU_LIMIT)
            act = glu * jax.nn.sigmoid(SWIGLU_ALPHA * glu) * (lin + 1.0)
            obuf[buf] = (jnp.dot(act.astype(BF16), w2b[...], preferred_element_type=F32) + b2_ref[0]).astype(BF16)
            for g, cp in enumerate(scatter(k)):
                cp.start(priority=g % 2)
            return c

        lax.fori_loop(0, nb, block, 0)

        @pl.when(last_active)
        def _():
            end = k0 + nb
            for a in range(MOE_AHEAD):
                for cp in gather(end + a):
                    cp.wait()
            for a in range(MOE_DEPTH, 0, -1):
                @pl.when(end - a >= 0)
                def _():
                    for cp in scatter(end - a):
                        cp.wait()


def _moe_experts(layer, nblk, gbase, edge, gsrc, gdst, xs, w1, b1, w2, b2):
    exp3 = lambda e, *_: (layer * N_EXPERTS + e, 0, 0)
    w1 = w1.reshape(DEPTH * N_EXPERTS, D_MODEL, 2 * D_FF)
    w2 = w2.reshape(DEPTH * N_EXPERTS, D_FF, D_MODEL)
    return pl.pallas_call(
        _moe_kernel,
        grid_spec=pltpu.PrefetchScalarGridSpec(
            num_scalar_prefetch=5,
            grid=(N_EXPERTS,),
            in_specs=[
                pl.BlockSpec((1, D_MODEL, 2 * D_FF), exp3),
                pl.BlockSpec((1, 1, 2 * D_FF), exp3),
                pl.BlockSpec((1, D_FF, D_MODEL), exp3),
                pl.BlockSpec((1, 1, D_MODEL), exp3),
                pl.BlockSpec(memory_space=pl.ANY),
            ],
            out_specs=pl.BlockSpec(memory_space=pl.ANY),
            scratch_shapes=[pltpu.VMEM((MOE_DEPTH, MOE_BM, D_MODEL), BF16), pltpu.VMEM((MOE_DEPTH, MOE_BM, D_MODEL), BF16),
                            pltpu.VMEM((D_MODEL, 2 * D_FF), BF16), pltpu.VMEM((D_FF, D_MODEL), BF16),
                            pltpu.SemaphoreType.DMA((MOE_DEPTH,)), pltpu.SemaphoreType.DMA((MOE_DEPTH,))],
        ),
        out_shape=jax.ShapeDtypeStruct((MOE_YS_ROWS, D_MODEL), BF16),
        compiler_params=pltpu.CompilerParams(
            dimension_semantics=("arbitrary",), vmem_limit_bytes=VMEM_LIMIT),
        name="moe_experts",
    )(nblk, gbase, edge, gsrc, gdst, w1, b1.reshape(DEPTH * N_EXPERTS, 1, 2 * D_FF), w2,
      b2.reshape(DEPTH * N_EXPERTS, 1, D_MODEL), xs)


def _combine_kernel(split, cond_ref, tg_ref, x1_ref, slot_ref, gate_ref, mod_ref, nw_ref, ys_ref, *rest):
    del cond_ref
    outs, (buf, sem) = rest[:-2], rest[-2:]
    i = pl.program_id(0)
    cur = i % 2

    def fetch(tile, b, wait):
        def go(cp):
            cp.wait() if wait else cp.start()

        row0 = pl.multiple_of(tile * TILE_SLOTS, TILE_SLOTS)
        go(pltpu.make_async_copy(ys_ref.at[pl.ds(row0, MOE_BASE_ROWS)], buf.at[b, pl.ds(0, MOE_BASE_ROWS)],
                                 sem.at[b]))
        extra = tg_ref[tile] * MOE_GRP - MOE_BASE_ROWS
        for arm in MOE_TAIL_ARMS:
            @pl.when((extra & arm) != 0)
            def _():
                off = pl.multiple_of(MOE_BASE_ROWS + (extra & ~(2 * arm - 1)), MOE_GRP)
                go(pltpu.make_async_copy(ys_ref.at[pl.ds(row0 + off, arm)], buf.at[b, pl.ds(off, arm)], sem.at[b]))

    @pl.when(i == 0)
    def _():
        buf[...] = jnp.zeros_like(buf)
        fetch(0, 0, False)

    @pl.when(i + 1 < N_TILES)
    def _():
        fetch(i + 1, 1 - cur, False)

    fetch(i, cur, True)
    scol = lax.broadcasted_iota(jnp.int32, (TM, TILE_SLOTS), 1)
    slot = slot_ref[...]
    gate = gate_ref[...]
    gmat = jnp.zeros((TM, TILE_SLOTS), F32)
    for kk in range(TOP_K):
        gmat = gmat + jnp.where(scol == slot[:, kk:kk + 1], gate[:, kk:kk + 1], 0.0)
    gmat = gmat.astype(BF16)
    ffn = jnp.dot(gmat, buf[cur], preferred_element_type=F32)
    g2 = mod_ref[0][:, 5 * D_MODEL:6 * D_MODEL]
    out = x1_ref[...] + g2 * _rms(ffn, nw_ref[3:4, :])
    if split:
        @pl.when(i < N_PROMPT // TM)
        def _():
            outs[0][...] = out

        @pl.when(i >= N_PROMPT // TM)
        def _():
            outs[1][...] = out
    else:
        outs[0][...] = out


def _combine(split, tile_cond, tile_groups, x1, slot, gate, mod_l, norm_w_l, ys):
    tok = lambda i, *_: (i, 0)
    if split:
        np_tiles = N_PROMPT // TM
        out_specs = [pl.BlockSpec((TM, D_MODEL), lambda i, *_: (jnp.minimum(i, np_tiles - 1), 0)),
                     pl.BlockSpec((TM, D_MODEL), lambda i, *_: (jnp.maximum(i - np_tiles, 0), 0))]
        out_shape = [jax.ShapeDtypeStruct((N_PROMPT, D_MODEL), F32), jax.ShapeDtypeStruct((N_SAMPLE, D_MODEL), F32)]
    else:
        out_specs = pl.BlockSpec((TM, D_MODEL), tok)
        out_shape = jax.ShapeDtypeStruct((N_TOK, D_MODEL), F32)
    return pl.pallas_call(
        functools.partial(_combine_kernel, split),
        grid_spec=pltpu.PrefetchScalarGridSpec(
            num_scalar_prefetch=2,
            grid=(N_TILES,),
            in_specs=[
                pl.BlockSpec((TM, D_MODEL), tok),
                pl.BlockSpec((TM, TOP_K), tok),
                pl.BlockSpec((TM, 128), tok),
                pl.BlockSpec((1, 1, 6 * D_MODEL), lambda i, c, *_: (c[i], 0, 0)),
                pl.BlockSpec((4, D_MODEL), lambda i, *_: (0, 0)),
                pl.BlockSpec(memory_space=pl.ANY),
            ],
            out_specs=out_specs,
            scratch_shapes=[pltpu.VMEM((2, TILE_SLOTS, D_MODEL), BF16), pltpu.SemaphoreType.DMA((2,))],
        ),
        out_shape=out_shape,
        compiler_params=pltpu.CompilerParams(
            dimension_semantics=("arbitrary",), vmem_limit_bytes=VMEM_LIMIT),
        name="moe_combine",
    )(tile_cond, tile_groups, x1, slot, gate, mod_l, norm_w_l, ys)


def _pos_embed_2d(rows, dim):
    r = jnp.repeat(jnp.arange(rows, dtype=F32), GRID_W)
    col = jnp.tile(jnp.arange(GRID_W, dtype=F32), rows)
    quarter = dim // 4
    omega = 1.0 / (POS_BASE ** (jnp.arange(quarter, dtype=F32) / quarter))

    def emb(pos):
        ang = pos[:, None] * omega[None, :]
        return jnp.concatenate([jnp.sin(ang), jnp.cos(ang)], axis=-1)

    return jnp.concatenate([emb(r), emb(col)], axis=-1)


def _assemble_kernel(xp_ref, xs_ref, pos_ref, o_ref):
    i = pl.program_id(0)

    @pl.when(i < N_PROMPT // TM)
    def _():
        o_ref[...] = xp_ref[...]

    @pl.when(i >= N_PROMPT // TM)
    def _():
        o_ref[...] = xs_ref[...] + pos_ref[...]


def _assemble(xp, xs, pos):
    np_tiles = N_PROMPT // TM
    return pl.pallas_call(
        _assemble_kernel,
        grid=(N_TILES,),
        in_specs=[pl.BlockSpec((TM, D_MODEL), lambda i: (jnp.minimum(i, np_tiles - 1), 0)),
                  pl.BlockSpec((TM, D_MODEL), lambda i: (jnp.maximum(i - np_tiles, 0), 0)),
                  pl.BlockSpec((TM, D_MODEL), lambda i: (jnp.maximum(i - np_tiles, 0) % (DEC_SEQ // TM), 0))],
        out_specs=pl.BlockSpec((TM, D_MODEL), lambda i: (i, 0)),
        out_shape=jax.ShapeDtypeStruct((N_TOK, D_MODEL), F32),
        compiler_params=pltpu.CompilerParams(dimension_semantics=("arbitrary",)),
        name="assemble_tokens",
    )(xp, xs, pos)


def _routing_tables(ri, counts):
    i32 = jnp.int32
    slot = ri[:, TOP_K:2 * TOP_K].reshape(N_TILES, TM, TOP_K)
    cnt = counts.reshape(N_TILES, N_EXPERTS)
    seg = (cnt + SEG_ALIGN - 1) // SEG_ALIGN
    lstart = jnp.cumsum(seg, axis=1) - seg
    tile_groups = jnp.sum(seg, axis=1)
    region = jnp.sum(seg, axis=0)
    nblk = (region + MOE_BLK_GROUPS - 1) // MOE_BLK_GROUPS
    gend = jnp.cumsum(nblk) * MOE_BLK_GROUPS
    gbase = gend - nblk * MOE_BLK_GROUPS
    has = nblk > 0
    edge = (has & (gbase == 0)).astype(i32) + 2 * (has & (gend == gend[-1])).astype(i32)
    cum = jnp.cumsum(seg, axis=0) - seg
    g = jnp.arange(MOE_GROUPS + MOE_AHEAD * MOE_BLK_GROUPS, dtype=i32)
    e_of = jnp.minimum(jnp.sum(gend[None, :] <= g[:, None], axis=1), N_EXPERTS - 1)
    onehot_e = (e_of[:, None] == jnp.arange(N_EXPERTS)[None, :]).astype(F32)
    pick_e = lambda t: jnp.dot(onehot_e, t.astype(F32), precision=lax.Precision.HIGHEST).astype(i32)
    off = g - pick_e(gbase)
    real = (g < gend[-1]) & (off < pick_e(region))
    cum_e = pick_e(cum.T)
    tile_of = jnp.sum(cum_e <= off[:, None], axis=1) - 1
    onehot_t = tile_of[:, None] == jnp.arange(N_TILES)[None, :]
    pick_t = lambda t: jnp.sum(jnp.where(onehot_t, t, 0), axis=1)
    pos = tile_of * MOE_TILE_GROUPS + pick_t(pick_e(lstart.T)) + off - pick_t(cum_e)
    pad_rank = jnp.cumsum((~real).astype(i32)) - 1
    gsrc = jnp.where(real, pos, MOE_ZERO_GROUP).astype(i32)
    gdst = jnp.where(real, pos, MOE_ZERO_GROUP + jnp.minimum(pad_rank, MOE_PAD_GROUPS - 1)).astype(i32)
    slot_t = jnp.concatenate([slot.transpose(0, 2, 1), jnp.full((1, TOP_K, TM), -1, i32)], axis=0)
    return (slot.reshape(N_TOK, TOP_K).astype(i32), slot_t.astype(i32),
            (nblk.astype(i32), gbase.astype(i32), edge.astype(i32), gsrc, gdst), tile_groups.astype(i32))


def kernel(x_prompt, x_sample, state_s5_re, state_s5_im, state_hgrn, c, c_ctx, ada_w, ada_b, norm_w, w_in,
           s5_a_re, s5_a_im, s5_log_dt, s5_b_re, s5_b_im, s5_c_re, s5_c_im, s5_d, s5_w_glu, s5_b_glu,
           hgrn_lb, hgrn_norm_w, w_out, router_w, router_b, exp_w1, exp_b1, exp_w2, exp_b2):
    x = _assemble(x_prompt.reshape(N_PROMPT, D_MODEL), x_sample.reshape(N_SAMPLE, D_MODEL),
                  _pos_embed_2d(DEC_SEQ // GRID_W, D_MODEL))
    cond = jnp.concatenate([c_ctx[None, :], c, jnp.zeros((N_COND - 1 - DEC_BATCH, D_MODEL), F32)], axis=0)
    mod = _modulation(cond, ada_w, ada_b).reshape(DEPTH, N_COND, 1, 6 * D_MODEL)
    tiles = jnp.arange(N_TILES, dtype=jnp.int32)
    tile_cond = jnp.where(tiles < N_PROMPT // TM, 0, 1 + (tiles - N_PROMPT // TM) // (DEC_SEQ // TM)).astype(jnp.int32)
    w_in_b = w_in.astype(BF16)
    w_out_b = w_out.astype(BF16)
    w_glu_b = s5_w_glu.astype(BF16)
    router_hi = router_w.astype(BF16)
    router_parts = jnp.stack([router_hi, (router_w - router_hi.astype(F32)).astype(BF16)])

    s5_w = _s5_weights(s5_a_re, s5_a_im, s5_log_dt, s5_b_re, s5_b_im, s5_c_re, s5_c_im)
    fin_re, fin_im, fin_hg = [], [], []
    n = S5_STATE
    for l in range(DEPTH):
        u, qs, v, gs, lf_f, k_f, lf_b, k_b = _in_proj(l, tile_cond, x, mod[l], norm_w[l], w_in_b[l], hgrn_lb)

        h0 = jnp.concatenate([
            state_s5_re[:, l].transpose(2, 0, 1, 3).reshape(S5_GROUPS, DEC_BATCH, 2 * n),
            state_s5_im[:, l].transpose(2, 0, 1, 3).reshape(S5_GROUPS, DEC_BATCH, 2 * n)], axis=-1)
        h0 = h0.reshape(S5_NLT, S5_GPT, S5_NRB - 1, S5_SEQ_PER_RB, S5_NS).transpose(0, 2, 1, 3, 4)
        ycore, hfin = _s5_scan(l, u, *s5_w, h0)
        hfin = hfin.reshape(S5_GROUPS, BATCH, S5_NS)
        fin_re.append(hfin[:, :, 0:2 * n].reshape(S5_GROUPS, BATCH, 2, n).transpose(1, 2, 0, 3))
        fin_im.append(hfin[:, :, 2 * n:].reshape(S5_GROUPS, BATCH, 2, n).transpose(1, 2, 0, 3))

        s0 = jnp.concatenate([jnp.zeros((BATCH, 2, HGRN_HEADS, HGRN_DV, HGRN_DK), F32),
                              jnp.swapaxes(state_hgrn[:, l], -1, -2)], axis=0)
        o_f, o_b, sfin = _hgrn_scan(qs, v, lf_f, k_f, lf_b, k_b, s0)
        fin_hg.append(jnp.swapaxes(sfin[:BATCH], -1, -2))

        x1, h2, ri, gate, counts = _out_proj(
            tile_cond, x, ycore, u, o_f, o_b, gs, mod[l], norm_w[l], s5_d[l].reshape(1, S5_WIDTH),
            w_glu_b[l], s5_b_glu[l].reshape(1, S5_WIDTH), hgrn_norm_w[l].reshape(1, HGRN_DV),
            w_out_b[l], router_parts[:, l], router_b[l].reshape(1, N_EXPERTS))

        slot, slot_t, groups, tile_groups = _routing_tables(ri, counts)
        xs = _dispatch(slot_t, h2)
        ys = _moe_experts(l, *groups, xs, exp_w1, exp_b1, exp_w2, exp_b2)
        x = _combine(l == DEPTH - 1, tile_cond, tile_groups, x1, slot, gate, mod[l], norm_w[l], ys)

    y_prompt, y_sample = x
    return (y_prompt.reshape(BATCH, SEQ, D_MODEL), y_sample.reshape(DEC_BATCH, DEC_SEQ, D_MODEL),
            jnp.stack(fin_re, axis=1), jnp.stack(fin_im, axis=1), jnp.stack(fin_hg, axis=1))
```
